```python
import math
import jax, jax.numpy as jnp
from jax import lax
import numpy as np

D_MODEL = 1024
BATCH = 8
SEQ = 2048
DEPTH = 1

CHUNK = 64
GM_GROUPS = 8
GM_GROUP_DIM = D_MODEL // GM_GROUPS
GM_WIDTH = GM_GROUPS * GM_GROUP_DIM
GM_BLOCK = 128
DN_HEADS = 8
DN_DK = 128
DN_DV = 128
DN_KEY = DN_HEADS * DN_DK
DN_VAL = DN_HEADS * DN_DV
DN_CONV = 4
MOE_GROUPS = 4
MOE_EXPERTS_PER_GROUP = 8
N_EXPERTS = MOE_GROUPS * MOE_EXPERTS_PER_GROUP
MOE_TOPK = 2
D_EXPERT = D_MODEL // 2
MOE_BLOCK = 128
LN_EPS = 1e-5
RMS_EPS = 1e-6
L2_EPS = 1e-6
DEEPNORM_ALPHA = (2 * DEPTH) ** 0.25
DEEPNORM_BETA = (8 * DEPTH) ** -0.25
IN_SIZES = (2 * GM_WIDTH, 2 * DN_KEY + DN_VAL, DN_VAL, DN_HEADS, DN_HEADS, D_MODEL, D_MODEL)
IN_WIDTH = 2 * GM_WIDTH + 2 * DN_KEY + 2 * DN_VAL + 2 * DN_HEADS + 2 * D_MODEL

kernel_name = "hybrid_gmlp_gdn_hiermoe_deepnorm"


def _layer_norm(x, g, b):
    xf = x.astype(jnp.float32)
    mu = jnp.mean(xf, -1, keepdims=True)
    var = jnp.mean(jnp.square(xf - mu), -1, keepdims=True)
    return ((xf - mu) * lax.rsqrt(var + LN_EPS) * g + b).astype(x.dtype)


def _causal_dwconv(x, w):
    c = x.shape[-1]
    return lax.conv_general_dilated(
        x, w[:, None, :].astype(x.dtype), window_strides=(1,),
        padding=[(DN_CONV - 1, 0)], dimension_numbers=('NWC', 'WIO', 'NWC'),
        feature_group_count=c)


def _spatial_gating(u, v, ln_g, ln_b, w_s, b_s):
    bsz, t, _ = u.shape
    nb = t // GM_BLOCK
    v = _layer_norm(v, ln_g, ln_b).reshape(bsz, nb, GM_BLOCK, GM_GROUPS, GM_GROUP_DIM)
    chunk_id = jnp.arange(GM_BLOCK) // CHUNK
    mask = chunk_id[None, :] <= chunk_id[:, None]
    w = jnp.where(mask[None], w_s, 0).astype(v.dtype)
    s = jnp.einsum('gts,bnsgc->bntgc', w, v) + b_s.T[None, None, :, :, None].astype(v.dtype)
    return u * s.reshape(bsz, t, GM_WIDTH)


def _gated_delta_rule(q, k, v, g, beta):
    bsz, t, nh, dk = q.shape
    dv = v.shape[-1]
    n = t // CHUNK

    def to_chunks(a):
        a = a.reshape((bsz, n, CHUNK, nh) + a.shape[3:])
        return jnp.moveaxis(a, 3, 1)

    q, k, v, g, beta = (to_chunks(a) for a in (q, k, v, g, beta))
    q = q * (dk ** -0.5)
    G = jnp.cumsum(g, axis=-1)
    idx = jnp.arange(CHUNK)
    causal = idx[:, None] >= idx[None, :]
    strict = idx[:, None] > idx[None, :]
    decay = jnp.exp(jnp.where(causal, G[..., :, None] - G[..., None, :], -jnp.inf))
    kk = jnp.einsum('bhnik,bhnjk->bhnij', k, k)
    lmat = jnp.where(strict, beta[..., :, None] * kk * decay, 0.0)
    amat = jnp.eye(CHUNK, dtype=q.dtype) + lmat
    rhs = jnp.concatenate([v * beta[..., None], k * (beta * jnp.exp(G))[..., None]], -1)
    sol = lax.linalg.triangular_solve(amat, rhs, left_side=True, lower=True, unit_diagonal=True)
    u_c, w_c = sol[..., :dv], sol[..., dv:]
    attn = jnp.einsum('bhnik,bhnjk->bhnij', q, k) * decay
    q_dec = q * jnp.exp(G)[..., None]
    g_last = G[..., -1:]
    k_dec = k * jnp.exp(g_last - G)[..., None]
    chunk_decay = jnp.exp(g_last)[..., None]

    def step(state, xs):
        uu, ww, aa, qd, kd, cd = xs
        v_new = uu - jnp.einsum('bhck,bhkv->bhcv', ww, state)
        o = jnp.einsum('bhck,bhkv->bhcv', qd, state) + jnp.einsum('bhij,bhjv->bhiv', aa, v_new)
        state = state * cd + jnp.einsum('bhck,bhcv->bhkv', kd, v_new)
        return state, o

    xs = tuple(jnp.moveaxis(a, 2, 0) for a in (u_c, w_c, attn, q_dec, k_dec, chunk_decay))
    s0 = jnp.zeros((bsz, nh, dk, dv), q.dtype)
    _, o = lax.scan(step, s0, xs)
    return o.transpose(1, 0, 3, 2, 4).reshape(bsz, t, nh, dv)


def _token_mixer(h, w_in, b_in, gm_ln_g, gm_ln_b, gm_w_s, gm_b_s,
                 dn_conv_w, dn_a_log, dn_dt_bias, dn_norm_w, w_pa, w_pb, w_o):
    bsz, t, _ = h.shape
    proj = h @ w_in + b_in
    split_at = list(np.cumsum(IN_SIZES)[:-1])
    gm_uv, dn_qkv, dn_z, dn_a, dn_b, gate_a, gate_b = jnp.split(proj, split_at, axis=-1)

    u, v = jnp.split(jax.nn.gelu(gm_uv, approximate=False), 2, axis=-1)
    y_a = _spatial_gating(u, v, gm_ln_g, gm_ln_b, gm_w_s, gm_b_s)

    qkv = jax.nn.silu(_causal_dwconv(dn_qkv, dn_conv_w)).astype(jnp.float32)
    q, k, vv = jnp.split(qkv, [DN_KEY, 2 * DN_KEY], axis=-1)
    q = q.reshape(bsz, t, DN_HEADS, DN_DK)
    k = k.reshape(bsz, t, DN_HEADS, DN_DK)
    vv = vv.reshape(bsz, t, DN_HEADS, DN_DV)
    q = q * lax.rsqrt(jnp.sum(q * q, -1, keepdims=True) + L2_EPS)
    k = k * lax.rsqrt(jnp.sum(k * k, -1, keepdims=True) + L2_EPS)
    beta = jax.nn.sigmoid(dn_b.astype(jnp.float32))
    g = -jnp.exp(dn_a_log.astype(jnp.float32)) * jax.nn.softplus(
        dn_a.astype(jnp.float32) + dn_dt_bias.astype(jnp.float32))
    o = _gated_delta_rule(q, k, vv, g, beta)
    z = dn_z.astype(jnp.float32).reshape(bsz, t, DN_HEADS, DN_DV)
    o = o * lax.rsqrt(jnp.mean(o * o, -1, keepdims=True) + RMS_EPS) * dn_norm_w * jax.nn.silu(z)
    y_b = o.reshape(bsz, t, DN_VAL).astype(h.dtype)

    merged = jax.nn.sigmoid(gate_a) * (y_a @ w_pa) + jax.nn.sigmoid(gate_b) * (y_b @ w_pb)
    return merged @ w_o


def _hier_moe(x, w_rg, b_rg, w_re, b_re, w1, w3, w2):
    bsz, t, d = x.shape
    n_tok = bsz * t
    xt = x.reshape(n_tok, d)
    pg = jax.nn.softmax((xt @ w_rg + b_rg).astype(jnp.float32), -1)
    grp = jnp.argmax(pg, -1)
    p_grp = jnp.take_along_axis(pg, grp[:, None], -1)
    le = (xt @ w_re + b_re).astype(jnp.float32).reshape(n_tok, MOE_GROUPS, MOE_EXPERTS_PER_GROUP)
    le = jnp.take_along_axis(le, grp[:, None, None], 1)[:, 0]
    top_v, top_i = lax.top_k(le, MOE_TOPK)
    gate = (p_grp * jax.nn.softmax(top_v, -1)).reshape(-1)
    eid = (grp[:, None] * MOE_EXPERTS_PER_GROUP + top_i).reshape(-1).astype(jnp.int32)
    tok = jnp.repeat(jnp.arange(n_tok, dtype=jnp.int32), MOE_TOPK)
    m = n_tok * MOE_TOPK
    order = jnp.argsort(eid)
    e_s, tok_s, gate_s = eid[order], tok[order], gate[order]
    counts = jnp.zeros((N_EXPERTS,), jnp.int32).at[eid].add(1)
    start = jnp.cumsum(counts) - counts
    padded = (counts + MOE_BLOCK - 1) // MOE_BLOCK * MOE_BLOCK
    pad_end = jnp.cumsum(padded)
    pad_start = pad_end - padded
    dest = pad_start[e_s] + (jnp.arange(m, dtype=jnp.int32) - start[e_s])
    n_blocks = (m + N_EXPERTS * (MOE_BLOCK - 1) + MOE_BLOCK - 1) // MOE_BLOCK
    rows = n_blocks * MOE_BLOCK
    x_pad = jnp.zeros((rows, d), x.dtype).at[dest].set(xt[tok_s])
    blk_start = jnp.arange(n_blocks, dtype=jnp.int32) * MOE_BLOCK
    blk_e = jnp.minimum(jnp.sum(pad_end[None, :] <= blk_start[:, None], -1), N_EXPERTS - 1)

    def expert_block(args):
        xb, e = args
        hb = jax.nn.silu(xb @ w1[e]) * (xb @ w3[e])
        return hb @ w2[e]

    y_pad = lax.map(expert_block, (x_pad.reshape(n_blocks, MOE_BLOCK, d), blk_e)).reshape(rows, d)
    y = y_pad[dest] * gate_s[:, None].astype(x.dtype)
    out = jnp.zeros((n_tok, d), x.dtype).at[tok_s].add(y)
    return out.reshape(bsz, t, d)


def setup_inputs(seed: int = 0) -> dict:
    key = jax.random.key(seed)
    ks = list(jax.random.split(key, 48))
    kit = iter(ks)

    def nrm(shape, scale):
        return scale * jax.random.normal(next(kit), shape, jnp.float32)

    def gain(shape):
        return 1.0 + nrm(shape, 0.05)

    L, D = DEPTH, D_MODEL
    s_in = D ** -0.5
    w_in = jnp.concatenate([
        nrm((L, D, 2 * GM_WIDTH), s_in * DEEPNORM_BETA),
        nrm((L, D, 2 * DN_KEY), s_in),
        nrm((L, D, DN_VAL), s_in * DEEPNORM_BETA),
        nrm((L, D, DN_VAL), s_in),
        nrm((L, D, 2 * DN_HEADS), s_in),
        nrm((L, D, 2 * D_MODEL), s_in),
    ], axis=-1)
    a_init = jax.random.uniform(next(kit), (L, DN_HEADS), jnp.float32, 1.0, 16.0)
    log_dt = jax.random.uniform(next(kit), (L, DN_HEADS), jnp.float32, math.log(1e-3), math.log(1e-1))
    dt = jnp.exp(log_dt)
    dt_bias = dt + jnp.log(-jnp.expm1(-dt))
    return {
        "x": jax.random.normal(next(kit), (BATCH, SEQ, D), jnp.float32),
        "ln_in_g": gain((D,)),
        "ln_in_b": nrm((D,), 0.02),
        "w_in": w_in,
        "b_in": nrm((L, IN_WIDTH), 0.01),
        "gm_ln_g": gain((L, GM_WIDTH)),
        "gm_ln_b": nrm((L, GM_WIDTH), 0.02),
        "gm_w_s": nrm((L, GM_GROUPS, GM_BLOCK, GM_BLOCK), 0.5 * GM_BLOCK ** -0.5),
        "gm_b_s": 1.0 + nrm((L, GM_GROUPS, GM_BLOCK), 0.1),
        "dn_conv_w": nrm((L, DN_CONV, 2 * DN_KEY + DN_VAL), DN_CONV ** -0.5),
        "dn_a_log": jnp.log(a_init),
        "dn_dt_bias": dt_bias,
        "dn_norm_w": gain((L, DN_DV)),
        "w_pa": nrm((L, GM_WIDTH, D), GM_WIDTH ** -0.5 * DEEPNORM_BETA),
        "w_pb": nrm((L, DN_VAL, D), DN_VAL ** -0.5 * DEEPNORM_BETA),
        "w_o": nrm((L, D, D), D ** -0.5 * DEEPNORM_BETA),
        "ln1_g": gain((L, D)),
        "ln1_b": nrm((L, D), 0.02),
        "w_rg": nrm((L, D, MOE_GROUPS), s_in),
        "b_rg": nrm((L, MOE_GROUPS), 0.01),
        "w_re": nrm((L, D, N_EXPERTS), s_in),
        "b_re": nrm((L, N_EXPERTS), 0.01),
        "w1": nrm((L, N_EXPERTS, D, D_EXPERT), s_in * DEEPNORM_BETA),
        "w3": nrm((L, N_EXPERTS, D, D_EXPERT), s_in * DEEPNORM_BETA),
        "w2": nrm((L, N_EXPERTS, D_EXPERT, D), D_EXPERT ** -0.5 * DEEPNORM_BETA),
        "ln2_g": gain((L, D)),
        "ln2_b": nrm((L, D), 0.02),
    }


def reference(x, ln_in_g, ln_in_b, w_in, b_in, gm_ln_g, gm_ln_b, gm_w_s, gm_b_s,
              dn_conv_w, dn_a_log, dn_dt_bias, dn_norm_w, w_pa, w_pb, w_o,
              ln1_g, ln1_b, w_rg, b_rg, w_re, b_re, w1, w3, w2, ln2_g, ln2_b):
    h = _layer_norm(x, ln_in_g, ln_in_b)
    for l in range(DEPTH):
        mix = _token_mixer(h, w_in[l], b_in[l], gm_ln_g[l], gm_ln_b[l], gm_w_s[l], gm_b_s[l],
                           dn_conv_w[l], dn_a_log[l], dn_dt_bias[l], dn_norm_w[l],
                           w_pa[l], w_pb[l], w_o[l])
        h = _layer_norm(DEEPNORM_ALPHA * h + mix, ln1_g[l], ln1_b[l])
        ffn = _hier_moe(h, w_rg[l], b_rg[l], w_re[l], b_re[l], w1[l], w3[l], w2[l])
        h = _layer_norm(DEEPNORM_ALPHA * h + ffn, ln2_g[l], ln2_b[l])
    return h
```

```python
import functools
import math

import jax
import jax.numpy as jnp
from jax import lax
from jax.experimental import pallas as pl
from jax.experimental.pallas import tpu as pltpu

D_MODEL = 1024
GM_GROUPS = 8
GM_BLOCK = 128
GM_CHUNK = 64
DN_HEADS = 8
DN_DK = 128
DN_CONV = 4
MOE_GROUPS = 4
MOE_EPG = 8
N_EXPERTS = MOE_GROUPS * MOE_EPG
D_EXPERT = D_MODEL // 2
LN_EPS = 1e-5
RMS_EPS = 1e-6
L2_EPS = 1e-6
DEEPNORM_ALPHA = 2.0 ** 0.25

LANES = 128
PROJ_WIDTH = 8 * D_MODEL + LANES
COL_U, COL_V, COL_Q, COL_K, COL_VV, COL_Z, COL_GA, COL_GB = range(8)
COL_AB = 8 * D_MODEL // LANES

DN_CHUNK = 64
CARRY_ROWS = 8
EXPERT_ROWS = 128
VMEM_LIMIT = 56 * 1024 * 1024

F32 = jnp.float32
BF16 = jnp.bfloat16


def _cparams(*sem):
    return pltpu.CompilerParams(dimension_semantics=sem, vmem_limit_bytes=VMEM_LIMIT)


def _layer_norm(x, g, b):
    mu = jnp.mean(x, -1, keepdims=True)
    xc = x - mu
    var = jnp.mean(xc * xc, -1, keepdims=True)
    return xc * lax.rsqrt(var + LN_EPS) * g + b


def _dot(a, b):
    return jnp.dot(a.astype(BF16), b.astype(BF16), preferred_element_type=F32)


def _dot_nt(a, b):
    return lax.dot_general(a.astype(BF16), b.astype(BF16), (((1,), (1,)), ((), ())),
                           preferred_element_type=F32)


def _inproj_body(x_ref, g_ref, b_ref, w_ref, bias_ref, o_ref, h_scr):
    @pl.when(pl.program_id(1) == 0)
    def _():
        h_scr[...] = _layer_norm(x_ref[...], g_ref[...], b_ref[...]).astype(BF16)

    o_ref[...] = jnp.dot(h_scr[...], w_ref[...], preferred_element_type=F32) + bias_ref[...]


def _inproj(x2, ln_g, ln_b, w, bias, *, tm=512, tn=1664):
    n = x2.shape[0]
    return pl.pallas_call(
        _inproj_body,
        grid=(n // tm, PROJ_WIDTH // tn),
        in_specs=[
            pl.BlockSpec((tm, D_MODEL), lambda i, j: (i, 0)),
            pl.BlockSpec((1, D_MODEL), lambda i, j: (0, 0)),
            pl.BlockSpec((1, D_MODEL), lambda i, j: (0, 0)),
            pl.BlockSpec((D_MODEL, tn), lambda i, j: (0, j)),
            pl.BlockSpec((1, tn), lambda i, j: (0, j)),
        ],
        out_specs=pl.BlockSpec((tm, tn), lambda i, j: (i, j)),
        out_shape=jax.ShapeDtypeStruct((n, PROJ_WIDTH), F32),
        scratch_shapes=[pltpu.VMEM((tm, D_MODEL), BF16)],
        compiler_params=_cparams("parallel", "arbitrary"),
        name="inproj",
    )(x2, ln_g, ln_b, w, bias)


def _gelu(x):
    return 0.5 * x * (1.0 + lax.erf(x * (1.0 / math.sqrt(2.0))))


def _gmlp_body(u_ref, v_ref, lng_ref, lnb_ref, ws_ref, bs_ref, o_ref, *, nblk):
    u = _gelu(u_ref[...])
    v = _layer_norm(_gelu(v_ref[...]), lng_ref[...], lnb_ref[...]).astype(BF16)
    row_chunk = lax.broadcasted_iota(jnp.int32, (GM_BLOCK, GM_BLOCK), 0) // GM_CHUNK
    col_chunk = lax.broadcasted_iota(jnp.int32, (GM_BLOCK, GM_BLOCK), 1) // GM_CHUNK
    causal = col_chunk <= row_chunk
    for g in range(GM_GROUPS):
        cols = slice(g * LANES, (g + 1) * LANES)
        w = jnp.where(causal, ws_ref[g], 0.0).astype(BF16)
        for blk in range(nblk):
            rows = slice(blk * GM_BLOCK, (blk + 1) * GM_BLOCK)
            s = jnp.dot(w, v[rows, cols], preferred_element_type=F32) + bs_ref[g]
            o_ref[rows, cols] = (u[rows, cols] * s).astype(o_ref.dtype)


def _gmlp(proj, ln_g, ln_b, w_s, b_s_full, *, nblk=2):
    n = proj.shape[0]
    rows = nblk * GM_BLOCK
    return pl.pallas_call(
        functools.partial(_gmlp_body, nblk=nblk),
        grid=(n // rows,),
        in_specs=[
            pl.BlockSpec((rows, D_MODEL), lambda i: (i, COL_U)),
            pl.BlockSpec((rows, D_MODEL), lambda i: (i, COL_V)),
            pl.BlockSpec((1, D_MODEL), lambda i: (0, 0)),
            pl.BlockSpec((1, D_MODEL), lambda i: (0, 0)),
            pl.BlockSpec((GM_GROUPS, GM_BLOCK, GM_BLOCK), lambda i: (0, 0, 0)),
            pl.BlockSpec((GM_GROUPS, GM_BLOCK, LANES), lambda i: (0, 0, 0)),
        ],
        out_specs=pl.BlockSpec((rows, D_MODEL), lambda i: (i, 0)),
        out_shape=jax.ShapeDtypeStruct((n, D_MODEL), BF16),
        compiler_params=_cparams("parallel"),
        name="gmlp",
    )(proj, proj, ln_g, ln_b, w_s, b_s_full)


def _cumsum_rows(x):
    n = x.shape[0]
    row = lax.broadcasted_iota(jnp.int32, x.shape, 0)
    shift = 1
    while shift < n:
        x = x + jnp.where(row >= shift, pltpu.roll(x, shift, 0), 0.0)
        shift *= 2
    return x


def _deltanet_body(q_ref, k_ref, v_ref, z_ref, ab_ref, cw_ref, alog_ref, dtb_ref, nw_ref,
                   o_ref, xs_ref, state_ref):
    c = DN_CHUNK
    w3 = 3 * D_MODEL

    @pl.when(pl.program_id(1) == 0)
    def _():
        xs_ref[0:CARRY_ROWS, :] = jnp.zeros((CARRY_ROWS, w3), F32)
        state_ref[...] = jnp.zeros_like(state_ref)

    xs_ref[CARRY_ROWS:, 0:D_MODEL] = q_ref[...]
    xs_ref[CARRY_ROWS:, D_MODEL:2 * D_MODEL] = k_ref[...]
    xs_ref[CARRY_ROWS:, 2 * D_MODEL:] = v_ref[...]
    acc = cw_ref[DN_CONV - 1:DN_CONV, :] * xs_ref[CARRY_ROWS:, :]
    for j in range(DN_CONV - 1):
        off = CARRY_ROWS - (DN_CONV - 1) + j
        acc = acc + cw_ref[j:j + 1, :] * xs_ref[off:off + c, :]
    xs_ref[0:CARRY_ROWS, :] = xs_ref[c:c + CARRY_ROWS, :]
    qkv = acc * jax.nn.sigmoid(acc)

    ab = ab_ref[...]
    xg = ab + dtb_ref[...]
    softplus = jnp.maximum(xg, 0.0) + jnp.log1p(jnp.exp(-jnp.abs(xg)))
    g_all = -jnp.exp(alog_ref[...]) * softplus
    beta_all = jax.nn.sigmoid(ab)
    gcum = _cumsum_rows(g_all)
    gcum_sq = jnp.concatenate([gcum, jnp.zeros((LANES - c, LANES), F32)], axis=0) if c < LANES else gcum
    gcum_t = gcum_sq.T

    ri = lax.broadcasted_iota(jnp.int32, (c, c), 0)
    ci = lax.broadcasted_iota(jnp.int32, (c, c), 1)
    causal = ri >= ci
    strict = ri > ci

    z = z_ref[...]
    for h in range(DN_HEADS):
        cols = slice(h * DN_DK, (h + 1) * DN_DK)
        qh = qkv[:, h * DN_DK:(h + 1) * DN_DK]
        kh = qkv[:, D_MODEL + h * DN_DK:D_MODEL + (h + 1) * DN_DK]
        vh = qkv[:, 2 * D_MODEL + h * DN_DK:2 * D_MODEL + (h + 1) * DN_DK]
        qh = qh * lax.rsqrt(jnp.sum(qh * qh, -1, keepdims=True) + L2_EPS) * (DN_DK ** -0.5)
        kh = kh * lax.rsqrt(jnp.sum(kh * kh, -1, keepdims=True) + L2_EPS)

        g_col = gcum[:, h:h + 1]
        g_row = gcum_t[h:h + 1, 0:c]
        beta = beta_all[:, DN_HEADS + h:DN_HEADS + h + 1]
        g_last = gcum[c - 1:c, h:h + 1]
        decay = jnp.where(causal, jnp.exp(jnp.where(causal, g_col - g_row, 0.0)), 0.0)
        eg = jnp.exp(g_col)

        kq = _dot_nt(jnp.concatenate([kh, qh], axis=0), kh)
        lmat = jnp.where(strict, beta * kq[0:c] * decay, 0.0)
        attn = kq[c:] * decay

        r = jnp.concatenate([vh * beta, kh * (beta * eg)], axis=1)
        p = lmat
        r = r - _dot(p, r)
        span = 2
        while span < c:
            p = _dot(p, p)
            r = r + _dot(p, r)
            span *= 2
        u_c = r[:, 0:DN_DK]
        w_c = r[:, DN_DK:]

        s_h = state_ref[h]
        ws = _dot(jnp.concatenate([w_c, qh * eg], axis=0), s_h)
        v_new = u_c - ws[0:c]
        o = ws[c:] + _dot(attn, v_new)
        k_dec = kh * jnp.exp(g_last - g_col)
        ds = lax.dot_general(k_dec.astype(BF16), v_new.astype(BF16), (((0,), (0,)), ((), ())),
                             preferred_element_type=F32)
        state_ref[h] = s_h * jnp.exp(g_last) + ds

        zh = z[:, cols]
        o = o * lax.rsqrt(jnp.mean(o * o, -1, keepdims=True) + RMS_EPS) * nw_ref[...]
        o_ref[:, cols] = (o * (zh * jax.nn.sigmoid(zh))).astype(o_ref.dtype)


def _deltanet(proj, conv_w, alog_row, dtb_row, norm_w, *, bsz, t):
    c = DN_CHUNK
    nc = t // c
    row = lambda b, n: b * nc + n
    return pl.pallas_call(
        _deltanet_body,
        grid=(bsz, nc),
        in_specs=[
            pl.BlockSpec((c, D_MODEL), lambda b, n: (row(b, n), COL_Q)),
            pl.BlockSpec((c, D_MODEL), lambda b, n: (row(b, n), COL_K)),
            pl.BlockSpec((c, D_MODEL), lambda b, n: (row(b, n), COL_VV)),
            pl.BlockSpec((c, D_MODEL), lambda b, n: (row(b, n), COL_Z)),
            pl.BlockSpec((c, LANES), lambda b, n: (row(b, n), COL_AB)),
            pl.BlockSpec((DN_CONV, 3 * D_MODEL), lambda b, n: (0, 0)),
            pl.BlockSpec((1, LANES), lambda b, n: (0, 0)),
            pl.BlockSpec((1, LANES), lambda b, n: (0, 0)),
            pl.BlockSpec((1, DN_DK), lambda b, n: (0, 0)),
        ],
        out_specs=pl.BlockSpec((c, D_MODEL), lambda b, n: (row(b, n), 0)),
        out_shape=jax.ShapeDtypeStruct((bsz * t, D_MODEL), BF16),
        scratch_shapes=[
            pltpu.VMEM((CARRY_ROWS + c, 3 * D_MODEL), F32),
            pltpu.VMEM((DN_HEADS, DN_DK, DN_DK), F32),
        ],
        compiler_params=_cparams("parallel", "arbitrary"),
        name="deltanet",
    )(proj, proj, proj, proj, proj, conv_w, alog_row, dtb_row, norm_w)


def _merge_body(x_ref, ya_ref, yb_ref, ga_ref, gb_ref, lng_ref, lnb_ref, wpa_ref, wpb_ref, wo_ref,
                l1g_ref, l1b_ref, wr_ref, br_ref, h1_ref, logit_ref):
    h = _layer_norm(x_ref[...], lng_ref[...], lnb_ref[...])
    pa = jnp.dot(ya_ref[...], wpa_ref[...], preferred_element_type=F32)
    pb = jnp.dot(yb_ref[...], wpb_ref[...], preferred_element_type=F32)
    merged = jax.nn.sigmoid(ga_ref[...]) * pa + jax.nn.sigmoid(gb_ref[...]) * pb
    mix = jnp.dot(merged.astype(BF16), wo_ref[...], preferred_element_type=F32)
    h1 = _layer_norm(DEEPNORM_ALPHA * h + mix, l1g_ref[...], l1b_ref[...])
    h1_ref[...] = h1
    logit_ref[...] = jnp.dot(h1, wr_ref[...], preferred_element_type=F32,
                             precision=lax.Precision.HIGHEST) + br_ref[...]


def _merge(x2, ya, yb, proj, ln_g, ln_b, w_pa, w_pb, w_o, l1g, l1b, w_r, b_r, *, tm=256):
    n = x2.shape[0]
    vec = lambda: pl.BlockSpec((1, D_MODEL), lambda i: (0, 0))
    mat = lambda: pl.BlockSpec((D_MODEL, D_MODEL), lambda i: (0, 0))
    return pl.pallas_call(
        _merge_body,
        grid=(n // tm,),
        in_specs=[
            pl.BlockSpec((tm, D_MODEL), lambda i: (i, 0)),
            pl.BlockSpec((tm, D_MODEL), lambda i: (i, 0)),
            pl.BlockSpec((tm, D_MODEL), lambda i: (i, 0)),
            pl.BlockSpec((tm, D_MODEL), lambda i: (i, COL_GA)),
            pl.BlockSpec((tm, D_MODEL), lambda i: (i, COL_GB)),
            vec(), vec(), mat(), mat(), mat(), vec(), vec(),
            pl.BlockSpec((D_MODEL, LANES), lambda i: (0, 0)),
            pl.BlockSpec((1, LANES), lambda i: (0, 0)),
        ],
        out_specs=[
            pl.BlockSpec((tm, D_MODEL), lambda i: (i, 0)),
            pl.BlockSpec((tm, LANES), lambda i: (i, 0)),
        ],
        out_shape=[
            jax.ShapeDtypeStruct((n, D_MODEL), F32),
            jax.ShapeDtypeStruct((n, LANES), F32),
        ],
        compiler_params=_cparams("parallel"),
        name="merge",
    )(x2, ya, yb, proj, proj, ln_g, ln_b, w_pa, w_pb, w_o, l1g, l1b, w_r, b_r)


def _route_body(logit_ref, gate_ref, idx_ref, cnt_ref, carry_ref, *, tm):
    @pl.when(pl.program_id(0) == 0)
    def _():
        carry_ref[...] = jnp.zeros_like(carry_ref)

    lg = logit_ref[...]
    lane_i = lax.broadcasted_iota(jnp.int32, lg.shape, 1)
    lane = lane_i.astype(F32)
    neg = jnp.float32(-jnp.inf)
    big = jnp.float32(1 << 20)

    is_grp = lane_i < MOE_GROUPS
    gl = jnp.where(is_grp, lg, neg)
    gmax = jnp.max(gl, -1, keepdims=True)
    grp = jnp.min(jnp.where(is_grp & (gl == gmax), lane, big), -1, keepdims=True)
    p_grp = 1.0 / jnp.sum(jnp.where(is_grp, jnp.exp(gl - gmax), 0.0), -1, keepdims=True)

    elane = lane - MOE_GROUPS
    in_grp = (elane >= grp * MOE_EPG) & (elane < (grp + 1) * MOE_EPG)
    el = jnp.where(in_grp, lg, neg)
    m1 = jnp.max(el, -1, keepdims=True)
    e1 = jnp.min(jnp.where(in_grp & (el == m1), elane, big), -1, keepdims=True)
    rest = in_grp & (elane != e1)
    el2 = jnp.where(rest, lg, neg)
    m2 = jnp.max(el2, -1, keepdims=True)
    e2 = jnp.min(jnp.where(rest & (el2 == m2), elane, big), -1, keepdims=True)
    t2 = jnp.exp(m2 - m1)
    g1 = p_grp * (1.0 / (1.0 + t2))
    g2 = p_grp * (t2 / (1.0 + t2))

    oh1 = lane == e1
    oh2 = lane == e2
    onehot = jnp.where(oh1 | oh2, 1.0, 0.0)
    ri = lax.broadcasted_iota(jnp.int32, (tm, tm), 0)
    ci = lax.broadcasted_iota(jnp.int32, (tm, tm), 1)
    tri = jnp.where(ci < ri, 1.0, 0.0).astype(BF16)
    before = jnp.dot(tri, onehot.astype(BF16), preferred_element_type=F32) + carry_ref[0:1, :]
    r1 = jnp.sum(jnp.where(oh1, before, 0.0), -1, keepdims=True)
    r2 = jnp.sum(jnp.where(oh2, before, 0.0), -1, keepdims=True)
    total = carry_ref[0:1, :] + jnp.sum(onehot, 0, keepdims=True)
    carry_ref[...] = jnp.broadcast_to(total, carry_ref.shape)
    cnt_ref[...] = jnp.broadcast_to(total, cnt_ref.shape).astype(jnp.int32)

    gate_ref[...] = jnp.where(lane_i == 0, g1, jnp.where(lane_i == 1, g2, 0.0))
    idx_ref[...] = jnp.where(lane_i == 0, e1, jnp.where(lane_i == 1, e2,
                             jnp.where(lane_i == 2, r1, jnp.where(lane_i == 3, r2, 0.0)))).astype(jnp.int32)


def _route(logits, *, tm=256):
    n = logits.shape[0]
    return pl.pallas_call(
        functools.partial(_route_body, tm=tm),
        grid=(n // tm,),
        in_specs=[pl.BlockSpec((tm, LANES), lambda i: (i, 0))],
        out_specs=[
            pl.BlockSpec((tm, LANES), lambda i: (i, 0)),
            pl.BlockSpec((tm, LANES), lambda i: (i, 0)),
            pl.BlockSpec((8, LANES), lambda i: (0, 0)),
        ],
        out_shape=[
            jax.ShapeDtypeStruct((n, LANES), F32),
            jax.ShapeDtypeStruct((n, LANES), jnp.int32),
            jax.ShapeDtypeStruct((8, LANES), jnp.int32),
        ],
        scratch_shapes=[pltpu.VMEM((8, LANES), F32)],
        compiler_params=_cparams("arbitrary"),
        name="route",
    )(logits)


def _dispatch_body(pstart_ref, idx_ref, h_ref, xin_ref, xpad_ref, sem, *, tm):
    del xin_ref

    def row_copy(r, slot):
        dst = pstart_ref[idx_ref[0, 0, 4 * r + slot]] + idx_ref[0, 0, 4 * r + 2 + slot]
        return pltpu.make_async_copy(h_ref.at[pl.ds(r, 1)], xpad_ref.at[pl.ds(dst, 1)], sem)

    def start(r, carry):
        row_copy(r, 0).start()
        row_copy(r, 1).start()
        return carry

    def wait(r, carry):
        row_copy(r, 0).wait()
        row_copy(r, 1).wait()
        return carry

    lax.fori_loop(0, tm, start, 0)
    lax.fori_loop(0, tm, wait, 0)


def _dispatch(pad_start, idx_smem, h1, xpad0, *, tm=256):
    n = h1.shape[0]
    return pl.pallas_call(
        functools.partial(_dispatch_body, tm=tm),
        grid_spec=pltpu.PrefetchScalarGridSpec(
            num_scalar_prefetch=1,
            grid=(n // tm,),
            in_specs=[
                pl.BlockSpec((1, 1, 4 * tm), lambda i, ps: (i, 0, 0), memory_space=pltpu.SMEM),
                pl.BlockSpec((tm, D_MODEL), lambda i, ps: (i, 0)),
                pl.BlockSpec(memory_space=pl.ANY),
            ],
            out_specs=pl.BlockSpec(memory_space=pl.ANY),
            scratch_shapes=[pltpu.SemaphoreType.DMA(())],
        ),
        out_shape=jax.ShapeDtypeStruct(xpad0.shape, xpad0.dtype),
        input_output_aliases={3: 0},
        compiler_params=_cparams("arbitrary"),
        name="dispatch",
    )(pad_start, idx_smem, h1, xpad0)


def _experts_body(blk_e_ref, nused_ref, x_ref, w1_ref, w3_ref, w2_ref, y_ref, w1b, w3b, w2b):
    i = pl.program_id(0)
    prev = blk_e_ref[jnp.maximum(i - 1, 0)]

    @pl.when((i == 0) | (blk_e_ref[i] != prev))
    def _():
        w1b[...] = w1_ref[0].astype(BF16)
        w3b[...] = w3_ref[0].astype(BF16)
        w2b[...] = w2_ref[0].astype(BF16)

    @pl.when(i < nused_ref[0])
    def _():
        xb = x_ref[...].astype(BF16)
        a = jnp.dot(xb, w1b[...], preferred_element_type=F32)
        b = jnp.dot(xb, w3b[...], preferred_element_type=F32)
        hb = (a * jax.nn.sigmoid(a)) * b
        y_ref[...] = jnp.dot(hb.astype(BF16), w2b[...], preferred_element_type=F32)

    @pl.when(i >= nused_ref[0])
    def _():
        y_ref[...] = jnp.zeros_like(y_ref)


def _experts(blk_e, n_used, xpad, w1, w3, w2):
    rows = xpad.shape[0]
    tm = EXPERT_ROWS
    return pl.pallas_call(
        _experts_body,
        grid_spec=pltpu.PrefetchScalarGridSpec(
            num_scalar_prefetch=2,
            grid=(rows // tm,),
            in_specs=[
                pl.BlockSpec((tm, D_MODEL), lambda i, be, nu: (i, 0)),
                pl.BlockSpec((1, D_MODEL, D_EXPERT), lambda i, be, nu: (be[i], 0, 0)),
                pl.BlockSpec((1, D_MODEL, D_EXPERT), lambda i, be, nu: (be[i], 0, 0)),
                pl.BlockSpec((1, D_EXPERT, D_MODEL), lambda i, be, nu: (be[i], 0, 0)),
            ],
            out_specs=pl.BlockSpec((tm, D_MODEL), lambda i, be, nu: (i, 0)),
            scratch_shapes=[
                pltpu.VMEM((D_MODEL, D_EXPERT), BF16),
                pltpu.VMEM((D_MODEL, D_EXPERT), BF16),
                pltpu.VMEM((D_EXPERT, D_MODEL), BF16),
            ],
        ),
        out_shape=jax.ShapeDtypeStruct((rows, D_MODEL), F32),
        compiler_params=_cparams("arbitrary"),
        name="experts",
    )(blk_e, n_used, xpad, w1, w3, w2)


def _combine_body(pstart_ref, idx_ref, h1_ref, gate_ref, l2g_ref, l2b_ref, ypad_ref, o_ref,
                  buf0, buf1, sem, *, tm):
    def row_copy(r, slot, buf):
        src = pstart_ref[idx_ref[0, 0, 4 * r + slot]] + idx_ref[0, 0, 4 * r + 2 + slot]
        return pltpu.make_async_copy(ypad_ref.at[pl.ds(src, 1)], buf.at[pl.ds(r, 1)], sem)

    def start(r, carry):
        row_copy(r, 0, buf0).start()
        row_copy(r, 1, buf1).start()
        return carry

    def wait(r, carry):
        row_copy(r, 0, buf0).wait()
        row_copy(r, 1, buf1).wait()
        return carry

    lax.fori_loop(0, tm, start, 0)
    lax.fori_loop(0, tm, wait, 0)
    gate = gate_ref[...]
    ffn = buf0[...] * gate[:, 0:1] + buf1[...] * gate[:, 1:2]
    o_ref[...] = _layer_norm(DEEPNORM_ALPHA * h1_ref[...] + ffn, l2g_ref[...], l2b_ref[...])


def _combine(pad_start, idx_smem, h1, gates, l2g, l2b, ypad, *, tm=256):
    n = h1.shape[0]
    return pl.pallas_call(
        functools.partial(_combine_body, tm=tm),
        grid_spec=pltpu.PrefetchScalarGridSpec(
            num_scalar_prefetch=1,
            grid=(n // tm,),
            in_specs=[
                pl.BlockSpec((1, 1, 4 * tm), lambda i, ps: (i, 0, 0), memory_space=pltpu.SMEM),
                pl.BlockSpec((tm, D_MODEL), lambda i, ps: (i, 0)),
                pl.BlockSpec((tm, LANES), lambda i, ps: (i, 0)),
                pl.BlockSpec((1, D_MODEL), lambda i, ps: (0, 0)),
                pl.BlockSpec((1, D_MODEL), lambda i, ps: (0, 0)),
                pl.BlockSpec(memory_space=pl.ANY),
            ],
            out_specs=pl.BlockSpec((tm, D_MODEL), lambda i, ps: (i, 0)),
            scratch_shapes=[
                pltpu.VMEM((tm, D_MODEL), F32),
                pltpu.VMEM((tm, D_MODEL), F32),
                pltpu.SemaphoreType.DMA(()),
            ],
        ),
        out_shape=jax.ShapeDtypeStruct((n, D_MODEL), F32),
        compiler_params=_cparams("arbitrary"),
        name="combine",
    )(pad_start, idx_smem, h1, gates, l2g, l2b, ypad)


def _row(v, width=None):
    v = v.reshape(1, -1).astype(F32)
    if width is not None and v.shape[1] < width:
        v = jnp.pad(v, ((0, 0), (0, width - v.shape[1])))
    return v


def kernel(x, ln_in_g, ln_in_b, w_in, b_in, gm_ln_g, gm_ln_b, gm_w_s, gm_b_s, dn_conv_w, dn_a_log,
           dn_dt_bias, dn_norm_w, w_pa, w_pb, w_o, ln1_g, ln1_b, w_rg, b_rg, w_re, b_re, w1, w3, w2,
           ln2_g, ln2_b):
    bsz, t, d = x.shape
    n = bsz * t
    x2 = x.reshape(n, d)
    l = 0

    wi, bi = w_in[l], b_in[l]
    c_ab = 6 * D_MODEL
    pad = LANES - 2 * DN_HEADS
    w_cat = jnp.concatenate([wi[:, :c_ab], wi[:, c_ab + 2 * DN_HEADS:], wi[:, c_ab:c_ab + 2 * DN_HEADS],
                             jnp.zeros((d, pad), F32)], axis=1).astype(BF16)
    b_cat = jnp.concatenate([bi[:c_ab], bi[c_ab + 2 * DN_HEADS:], bi[c_ab:c_ab + 2 * DN_HEADS],
                             jnp.zeros((pad,), F32)]).reshape(1, -1)

    ln_g, ln_b = _row(ln_in_g), _row(ln_in_b)
    proj = _inproj(x2, ln_g, ln_b, w_cat, b_cat)

    bs_full = jnp.broadcast_to(gm_b_s[l][:, :, None], (GM_GROUPS, GM_BLOCK, LANES))
    ya = _gmlp(proj, _row(gm_ln_g[l]), _row(gm_ln_b[l]), gm_w_s[l], bs_full)

    yb = _deltanet(proj, dn_conv_w[l], _row(dn_a_log[l], LANES), _row(dn_dt_bias[l], LANES),
                   _row(dn_norm_w[l]), bsz=bsz, t=t)

    w_r = jnp.concatenate([w_rg[l], w_re[l], jnp.zeros((d, LANES - MOE_GROUPS - N_EXPERTS), F32)], axis=1)
    b_r = _row(jnp.concatenate([b_rg[l], b_re[l]]), LANES)
    h1, logits = _merge(x2, ya, yb, proj, ln_g, ln_b, w_pa[l].astype(BF16), w_pb[l].astype(BF16),
                        w_o[l].astype(BF16), _row(ln1_g[l]), _row(ln1_b[l]), w_r, b_r)

    gates, idx, counts = _route(logits)

    tm_e = EXPERT_ROWS
    cnt = counts[0, :N_EXPERTS]
    padded = (cnt + tm_e - 1) // tm_e * tm_e
    pad_end = jnp.cumsum(padded)
    pad_start = (pad_end - padded).astype(jnp.int32)
    n_blocks = (2 * n + N_EXPERTS * (tm_e - 1) + tm_e - 1) // tm_e
    blk_start = jnp.arange(n_blocks, dtype=jnp.int32) * tm_e
    blk_e = jnp.minimum(jnp.sum(pad_end[None, :] <= blk_start[:, None], -1), N_EXPERTS - 1).astype(jnp.int32)
    n_used = (pad_end[-1] // tm_e).astype(jnp.int32).reshape(1)

    tm_tok = 256
    idx_smem = idx[:, :4].reshape(n // tm_tok, 1, 4 * tm_tok)
    xpad = _dispatch(pad_start, idx_smem, h1, jnp.zeros((n_blocks * tm_e, d), F32), tm=tm_tok)
    ypad = _experts(blk_e, n_used, xpad, w1[l], w3[l], w2[l])
    out = _combine(pad_start, idx_smem, h1, gates, _row(ln2_g[l]), _row(ln2_b[l]), ypad, tm=tm_tok)
    return out.reshape(bsz, t, d)
```

```python
import functools
import math

import jax
import jax.numpy as jnp
from jax import lax
from jax.experimental import pallas as pl
from jax.experimental.pallas import tpu as pltpu

D_MODEL = 1024
GM_GROUPS = 8
GM_BLOCK = 128
GM_CHUNK = 64
DN_HEADS = 8
DN_DK = 128
DN_CONV = 4
MOE_GROUPS = 4
MOE_EPG = 8
N_EXPERTS = MOE_GROUPS * MOE_EPG
D_EXPERT = D_MODEL // 2
LN_EPS = 1e-5
RMS_EPS = 1e-6
L2_EPS = 1e-6
DEEPNORM_ALPHA = 2.0 ** 0.25

LANES = 128
PROJ_WIDTH = 8 * D_MODEL + LANES
COL_U, COL_V, COL_Q, COL_K, COL_VV, COL_Z, COL_GA, COL_GB = range(8)
COL_AB = 8 * D_MODEL // LANES

DN_CHUNK = 64
CARRY_ROWS = 8
EXPERT_ROWS = 128
VMEM_LIMIT = 56 * 1024 * 1024

F32 = jnp.float32
BF16 = jnp.bfloat16


def _cparams(*sem):
    return pltpu.CompilerParams(dimension_semantics=sem, vmem_limit_bytes=VMEM_LIMIT)


def _layer_norm(x, g, b):
    mu = jnp.mean(x, -1, keepdims=True)
    xc = x - mu
    var = jnp.mean(xc * xc, -1, keepdims=True)
    return xc * lax.rsqrt(var + LN_EPS) * g + b


def _dot(a, b):
    return jnp.dot(a.astype(BF16), b.astype(BF16), preferred_element_type=F32)


def _dot_nt(a, b):
    return lax.dot_general(a.astype(BF16), b.astype(BF16), (((1,), (1,)), ((), ())),
                           preferred_element_type=F32)


def _inproj_body(x_ref, g_ref, b_ref, w_ref, bias_ref, o_ref, h_scr):
    @pl.when(pl.program_id(1) == 0)
    def _():
        h_scr[...] = _layer_norm(x_ref[...], g_ref[...], b_ref[...]).astype(BF16)

    o_ref[...] = jnp.dot(h_scr[...], w_ref[...], preferred_element_type=F32) + bias_ref[...]


def _inproj(x2, ln_g, ln_b, w, bias, *, tm=512, tn=1664):
    n = x2.shape[0]
    return pl.pallas_call(
        _inproj_body,
        grid=(n // tm, PROJ_WIDTH // tn),
        in_specs=[
            pl.BlockSpec((tm, D_MODEL), lambda i, j: (i, 0)),
            pl.BlockSpec((1, D_MODEL), lambda i, j: (0, 0)),
            pl.BlockSpec((1, D_MODEL), lambda i, j: (0, 0)),
            pl.BlockSpec((D_MODEL, tn), lambda i, j: (0, j)),
            pl.BlockSpec((1, tn), lambda i, j: (0, j)),
        ],
        out_specs=pl.BlockSpec((tm, tn), lambda i, j: (i, j)),
        out_shape=jax.ShapeDtypeStruct((n, PROJ_WIDTH), F32),
        scratch_shapes=[pltpu.VMEM((tm, D_MODEL), BF16)],
        compiler_params=_cparams("parallel", "arbitrary"),
        name="inproj",
    )(x2, ln_g, ln_b, w, bias)


def _gelu(x):
    return 0.5 * x * (1.0 + lax.erf(x * (1.0 / math.sqrt(2.0))))


def _gmlp_body(u_ref, v_ref, lng_ref, lnb_ref, ws_ref, bs_ref, o_ref, *, nblk):
    u = _gelu(u_ref[...])
    v = _layer_norm(_gelu(v_ref[...]), lng_ref[...], lnb_ref[...]).astype(BF16)
    row_chunk = lax.broadcasted_iota(jnp.int32, (GM_BLOCK, GM_BLOCK), 0) // GM_CHUNK
    col_chunk = lax.broadcasted_iota(jnp.int32, (GM_BLOCK, GM_BLOCK), 1) // GM_CHUNK
    causal = col_chunk <= row_chunk
    for g in range(GM_GROUPS):
        cols = slice(g * LANES, (g + 1) * LANES)
        w = jnp.where(causal, ws_ref[g], 0.0).astype(BF16)
        for blk in range(nblk):
            rows = slice(blk * GM_BLOCK, (blk + 1) * GM_BLOCK)
            s = jnp.dot(w, v[rows, cols], preferred_element_type=F32) + bs_ref[g]
            o_ref[rows, cols] = (u[rows, cols] * s).astype(o_ref.dtype)


def _gmlp(proj, ln_g, ln_b, w_s, b_s_full, *, nblk=2):
    n = proj.shape[0]
    rows = nblk * GM_BLOCK
    return pl.pallas_call(
        functools.partial(_gmlp_body, nblk=nblk),
        grid=(n // rows,),
        in_specs=[
            pl.BlockSpec((rows, D_MODEL), lambda i: (i, COL_U)),
            pl.BlockSpec((rows, D_MODEL), lambda i: (i, COL_V)),
            pl.BlockSpec((1, D_MODEL), lambda i: (0, 0)),
            pl.BlockSpec((1, D_MODEL), lambda i: (0, 0)),
            pl.BlockSpec((GM_GROUPS, GM_BLOCK, GM_BLOCK), lambda i: (0, 0, 0)),
            pl.BlockSpec((GM_GROUPS, GM_BLOCK, LANES), lambda i: (0, 0, 0)),
        ],
        out_specs=pl.BlockSpec((rows, D_MODEL), lambda i: (i, 0)),
        out_shape=jax.ShapeDtypeStruct((n, D_MODEL), BF16),
        compiler_params=_cparams("parallel"),
        name="gmlp",
    )(proj, proj, ln_g, ln_b, w_s, b_s_full)


def _cumsum_rows(x):
    n = x.shape[0]
    row = lax.broadcasted_iota(jnp.int32, x.shape, 0)
    shift = 1
    while shift < n:
        x = x + jnp.where(row >= shift, pltpu.roll(x, shift, 0), 0.0)
        shift *= 2
    return x


def _deltanet_body(q_ref, k_ref, v_ref, z_ref, ab_ref, cw_ref, alog_ref, dtb_ref, nw_ref,
                   o_ref, xs_ref, state_ref, *, nb):
    c = DN_CHUNK
    w3 = 3 * D_MODEL

    @pl.when(pl.program_id(1) == 0)
    def _():
        xs_ref[:, 0:CARRY_ROWS, :] = jnp.zeros((nb, CARRY_ROWS, w3), F32)
        state_ref[...] = jnp.zeros_like(state_ref)

    ri = lax.broadcasted_iota(jnp.int32, (c, c), 0)
    ci = lax.broadcasted_iota(jnp.int32, (c, c), 1)
    causal = ri >= ci
    strict = ri > ci
    heads = range(DN_HEADS)
    hsl = lambda base, h: slice(base + h * DN_DK, base + (h + 1) * DN_DK)

    q, k, v, g_col, beta, g_last, decay = [], [], [], [], [], [], []
    for b in range(nb):
        xs_ref[b, CARRY_ROWS:, 0:D_MODEL] = q_ref[b]
        xs_ref[b, CARRY_ROWS:, D_MODEL:2 * D_MODEL] = k_ref[b]
        xs_ref[b, CARRY_ROWS:, 2 * D_MODEL:] = v_ref[b]
        acc = cw_ref[DN_CONV - 1:DN_CONV, :] * xs_ref[b, CARRY_ROWS:, :]
        for j in range(DN_CONV - 1):
            off = CARRY_ROWS - (DN_CONV - 1) + j
            acc = acc + cw_ref[j:j + 1, :] * xs_ref[b, off:off + c, :]
        xs_ref[b, 0:CARRY_ROWS, :] = xs_ref[b, c:c + CARRY_ROWS, :]
        qkv = acc * jax.nn.sigmoid(acc)

        ab = ab_ref[b]
        xg = ab + dtb_ref[...]
        softplus = jnp.maximum(xg, 0.0) + jnp.log1p(jnp.exp(-jnp.abs(xg)))
        gcum = _cumsum_rows(-jnp.exp(alog_ref[...]) * softplus)
        beta_all = jax.nn.sigmoid(ab)
        gcum_sq = gcum if c == LANES else jnp.concatenate([gcum, jnp.zeros((LANES - c, LANES), F32)], axis=0)
        gcum_t = gcum_sq.T

        for h in heads:
            qh, kh = qkv[:, hsl(0, h)], qkv[:, hsl(D_MODEL, h)]
            q.append(qh * (lax.rsqrt(jnp.sum(qh * qh, -1, keepdims=True) + L2_EPS) * (DN_DK ** -0.5)))
            k.append(kh * lax.rsqrt(jnp.sum(kh * kh, -1, keepdims=True) + L2_EPS))
            v.append(qkv[:, hsl(2 * D_MODEL, h)])
            g_col.append(gcum[:, h:h + 1])
            beta.append(beta_all[:, DN_HEADS + h:DN_HEADS + h + 1])
            g_last.append(gcum[c - 1:c, h:h + 1])
            decay.append(jnp.where(causal, jnp.exp(jnp.where(causal, g_col[-1] - gcum_t[h:h + 1, 0:c], 0.0)), 0.0))

    units = range(nb * DN_HEADS)
    eg = [jnp.exp(g) for g in g_col]
    kb = [x.astype(BF16) for x in k]
    kq = [lax.dot_general(jnp.concatenate([kb[u], q[u].astype(BF16)], axis=0), kb[u],
                          (((1,), (1,)), ((), ())), preferred_element_type=F32) for u in units]
    p = [jnp.where(strict, beta[u] * kq[u][0:c] * decay[u], 0.0).astype(BF16) for u in units]
    attn = [(kq[u][c:] * decay[u]).astype(BF16) for u in units]

    r = [jnp.concatenate([v[u] * beta[u], k[u] * (beta[u] * eg[u])], axis=1) for u in units]
    r = [r[u] - jnp.dot(p[u], r[u].astype(BF16), preferred_element_type=F32) for u in units]
    span = 2
    while span < c:
        p = [jnp.dot(p[u], p[u], preferred_element_type=F32).astype(BF16) for u in units]
        r = [r[u] + jnp.dot(p[u], r[u].astype(BF16), preferred_element_type=F32) for u in units]
        span *= 2

    s_old = [state_ref[u] for u in units]
    ws = [jnp.dot(jnp.concatenate([r[u][:, DN_DK:], q[u] * eg[u]], axis=0).astype(BF16),
                  s_old[u].astype(BF16), preferred_element_type=F32) for u in units]
    v_new = [(r[u][:, 0:DN_DK] - ws[u][0:c]).astype(BF16) for u in units]
    o = [ws[u][c:] + jnp.dot(attn[u], v_new[u], preferred_element_type=F32) for u in units]
    k_dec = [(k[u] * jnp.exp(g_last[u] - g_col[u])).astype(BF16) for u in units]
    ds = [lax.dot_general(k_dec[u], v_new[u], (((0,), (0,)), ((), ())), preferred_element_type=F32)
          for u in units]
    for u in units:
        state_ref[u] = s_old[u] * jnp.exp(g_last[u]) + ds[u]

    for b in range(nb):
        z = z_ref[b]
        for h in heads:
            u = b * DN_HEADS + h
            zh = z[:, hsl(0, h)]
            on = o[u] * lax.rsqrt(jnp.mean(o[u] * o[u], -1, keepdims=True) + RMS_EPS) * nw_ref[...]
            o_ref[b, :, hsl(0, h)] = (on * (zh * jax.nn.sigmoid(zh))).astype(o_ref.dtype)


def _deltanet(proj, conv_w, alog_row, dtb_row, norm_w, *, bsz, t, nb=2):
    c = DN_CHUNK
    proj3 = proj.reshape(bsz, t, PROJ_WIDTH)
    col = lambda j: pl.BlockSpec((nb, c, D_MODEL), lambda bp, n: (bp, n, j))
    yb = pl.pallas_call(
        functools.partial(_deltanet_body, nb=nb),
        grid=(bsz // nb, t // c),
        in_specs=[
            col(COL_Q), col(COL_K), col(COL_VV), col(COL_Z),
            pl.BlockSpec((nb, c, LANES), lambda bp, n: (bp, n, COL_AB)),
            pl.BlockSpec((DN_CONV, 3 * D_MODEL), lambda bp, n: (0, 0)),
            pl.BlockSpec((1, LANES), lambda bp, n: (0, 0)),
            pl.BlockSpec((1, LANES), lambda bp, n: (0, 0)),
            pl.BlockSpec((1, DN_DK), lambda bp, n: (0, 0)),
        ],
        out_specs=pl.BlockSpec((nb, c, D_MODEL), lambda bp, n: (bp, n, 0)),
        out_shape=jax.ShapeDtypeStruct((bsz, t, D_MODEL), BF16),
        scratch_shapes=[
            pltpu.VMEM((nb, CARRY_ROWS + c, 3 * D_MODEL), F32),
            pltpu.VMEM((nb * DN_HEADS, DN_DK, DN_DK), F32),
        ],
        compiler_params=_cparams("parallel", "arbitrary"),
        name="deltanet",
    )(proj3, proj3, proj3, proj3, proj3, conv_w, alog_row, dtb_row, norm_w)
    return yb.reshape(bsz * t, D_MODEL)


def _merge_body(x_ref, ya_ref, yb_ref, ga_ref, gb_ref, lng_ref, lnb_ref, wpa_ref, wpb_ref, wo_ref,
                l1g_ref, l1b_ref, wr_ref, br_ref, h1_ref, logit_ref):
    h = _layer_norm(x_ref[...], lng_ref[...], lnb_ref[...])
    pa = jnp.dot(ya_ref[...], wpa_ref[...], preferred_element_type=F32)
    pb = jnp.dot(yb_ref[...], wpb_ref[...], preferred_element_type=F32)
    merged = jax.nn.sigmoid(ga_ref[...]) * pa + jax.nn.sigmoid(gb_ref[...]) * pb
    mix = jnp.dot(merged.astype(BF16), wo_ref[...], preferred_element_type=F32)
    h1 = _layer_norm(DEEPNORM_ALPHA * h + mix, l1g_ref[...], l1b_ref[...])
    h1_ref[...] = h1
    logit_ref[...] = jnp.dot(h1, wr_ref[...], preferred_element_type=F32,
                             precision=lax.Precision.HIGHEST) + br_ref[...]


def _merge(x2, ya, yb, proj, ln_g, ln_b, w_pa, w_pb, w_o, l1g, l1b, w_r, b_r, *, tm=256):
    n = x2.shape[0]
    vec = lambda: pl.BlockSpec((1, D_MODEL), lambda i: (0, 0))
    mat = lambda: pl.BlockSpec((D_MODEL, D_MODEL), lambda i: (0, 0))
    return pl.pallas_call(
        _merge_body,
        grid=(n // tm,),
        in_specs=[
            pl.BlockSpec((tm, D_MODEL), lambda i: (i, 0)),
            pl.BlockSpec((tm, D_MODEL), lambda i: (i, 0)),
            pl.BlockSpec((tm, D_MODEL), lambda i: (i, 0)),
            pl.BlockSpec((tm, D_MODEL), lambda i: (i, COL_GA)),
            pl.BlockSpec((tm, D_MODEL), lambda i: (i, COL_GB)),
            vec(), vec(), mat(), mat(), mat(), vec(), vec(),
            pl.BlockSpec((D_MODEL, LANES), lambda i: (0, 0)),
            pl.BlockSpec((1, LANES), lambda i: (0, 0)),
        ],
        out_specs=[
            pl.BlockSpec((tm, D_MODEL), lambda i: (i, 0)),
            pl.BlockSpec((tm, LANES), lambda i: (i, 0)),
        ],
        out_shape=[
            jax.ShapeDtypeStruct((n, D_MODEL), F32),
            jax.ShapeDtypeStruct((n, LANES), F32),
        ],
        compiler_params=_cparams("parallel"),
        name="merge",
    )(x2, ya, yb, proj, proj, ln_g, ln_b, w_pa, w_pb, w_o, l1g, l1b, w_r, b_r)


def _route_body(logit_ref, gate_ref, idx_ref, cnt_ref, carry_ref, *, tm):
    @pl.when(pl.program_id(0) == 0)
    def _():
        carry_ref[...] = jnp.zeros_like(carry_ref)

    lg = logit_ref[...]
    lane_i = lax.broadcasted_iota(jnp.int32, lg.shape, 1)
    lane = lane_i.astype(F32)
    neg = jnp.float32(-jnp.inf)
    big = jnp.float32(1 << 20)

    is_grp = lane_i < MOE_GROUPS
    gl = jnp.where(is_grp, lg, neg)
    gmax = jnp.max(gl, -1, keepdims=True)
    grp = jnp.min(jnp.where(is_grp & (gl == gmax), lane, big), -1, keepdims=True)
    p_grp = 1.0 / jnp.sum(jnp.where(is_grp, jnp.exp(gl - gmax), 0.0), -1, keepdims=True)

    elane = lane - MOE_GROUPS
    in_grp = (elane >= grp * MOE_EPG) & (elane < (grp + 1) * MOE_EPG)
    el = jnp.where(in_grp, lg, neg)
    m1 = jnp.max(el, -1, keepdims=True)
    e1 = jnp.min(jnp.where(in_grp & (el == m1), elane, big), -1, keepdims=True)
    rest = in_grp & (elane != e1)
    el2 = jnp.where(rest, lg, neg)
    m2 = jnp.max(el2, -1, keepdims=True)
    e2 = jnp.min(jnp.where(rest & (el2 == m2), elane, big), -1, keepdims=True)
    t2 = jnp.exp(m2 - m1)
    g1 = p_grp * (1.0 / (1.0 + t2))
    g2 = p_grp * (t2 / (1.0 + t2))

    oh1 = lane == e1
    oh2 = lane == e2
    onehot = jnp.where(oh1 | oh2, 1.0, 0.0)
    ri = lax.broadcasted_iota(jnp.int32, (tm, tm), 0)
    ci = lax.broadcasted_iota(jnp.int32, (tm, tm), 1)
    tri = jnp.where(ci < ri, 1.0, 0.0).astype(BF16)
    before = jnp.dot(tri, onehot.astype(BF16), preferred_element_type=F32) + carry_ref[0:1, :]
    r1 = jnp.sum(jnp.where(oh1, before, 0.0), -1, keepdims=True)
    r2 = jnp.sum(jnp.where(oh2, before, 0.0), -1, keepdims=True)
    total = carry_ref[0:1, :] + jnp.sum(onehot, 0, keepdims=True)
    carry_ref[...] = jnp.broadcast_to(total, carry_ref.shape)
    cnt_ref[...] = jnp.broadcast_to(total, cnt_ref.shape).astype(jnp.int32)

    gate_ref[...] = jnp.where(lane_i == 0, g1, jnp.where(lane_i == 1, g2, 0.0))
    idx_ref[...] = jnp.where(lane_i == 0, e1, jnp.where(lane_i == 1, e2,
                             jnp.where(lane_i == 2, r1, jnp.where(lane_i == 3, r2, 0.0)))).astype(jnp.int32)


def _route(logits, *, tm=256):
    n = logits.shape[0]
    return pl.pallas_call(
        functools.partial(_route_body, tm=tm),
        grid=(n // tm,),
        in_specs=[pl.BlockSpec((tm, LANES), lambda i: (i, 0))],
        out_specs=[
            pl.BlockSpec((tm, LANES), lambda i: (i, 0)),
            pl.BlockSpec((tm, LANES), lambda i: (i, 0)),
            pl.BlockSpec((8, LANES), lambda i: (0, 0)),
        ],
        out_shape=[
            jax.ShapeDtypeStruct((n, LANES), F32),
            jax.ShapeDtypeStruct((n, LANES), jnp.int32),
            jax.ShapeDtypeStruct((8, LANES), jnp.int32),
        ],
        scratch_shapes=[pltpu.VMEM((8, LANES), F32)],
        compiler_params=_cparams("arbitrary"),
        name="route",
    )(logits)


def _dispatch_body(pstart_ref, idx_ref, h_ref, xin_ref, xpad_ref, sem, *, tm):
    del xin_ref

    def row_copy(r, slot):
        dst = pstart_ref[idx_ref[0, 0, 4 * r + slot]] + idx_ref[0, 0, 4 * r + 2 + slot]
        return pltpu.make_async_copy(h_ref.at[pl.ds(r, 1)], xpad_ref.at[pl.ds(dst, 1)], sem)

    def start(r, carry):
        row_copy(r, 0).start()
        row_copy(r, 1).start()
        return carry

    def wait(r, carry):
        row_copy(r, 0).wait()
        row_copy(r, 1).wait()
        return carry

    lax.fori_loop(0, tm, start, 0)
    lax.fori_loop(0, tm, wait, 0)


def _dispatch(pad_start, idx_smem, h1, xpad0, *, tm=256):
    n = h1.shape[0]
    return pl.pallas_call(
        functools.partial(_dispatch_body, tm=tm),
        grid_spec=pltpu.PrefetchScalarGridSpec(
            num_scalar_prefetch=1,
            grid=(n // tm,),
            in_specs=[
                pl.BlockSpec((1, 1, 4 * tm), lambda i, ps: (i, 0, 0), memory_space=pltpu.SMEM),
                pl.BlockSpec((tm, D_MODEL), lambda i, ps: (i, 0)),
                pl.BlockSpec(memory_space=pl.ANY),
            ],
            out_specs=pl.BlockSpec(memory_space=pl.ANY),
            scratch_shapes=[pltpu.SemaphoreType.DMA(())],
        ),
        out_shape=jax.ShapeDtypeStruct(xpad0.shape, xpad0.dtype),
        input_output_aliases={3: 0},
        compiler_params=_cparams("arbitrary"),
        name="dispatch",
    )(pad_start, idx_smem, h1, xpad0)


def _experts_body(blk_e_ref, nused_ref, x_ref, w1_ref, w3_ref, w2_ref, y_ref, w1b, w3b, w2b):
    i = pl.program_id(0)
    prev = blk_e_ref[jnp.maximum(i - 1, 0)]

    @pl.when((i == 0) | (blk_e_ref[i] != prev))
    def _():
        w1b[...] = w1_ref[0].astype(BF16)
        w3b[...] = w3_ref[0].astype(BF16)
        w2b[...] = w2_ref[0].astype(BF16)

    @pl.when(i < nused_ref[0])
    def _():
        xb = x_ref[...].astype(BF16)
        a = jnp.dot(xb, w1b[...], preferred_element_type=F32)
        b = jnp.dot(xb, w3b[...], preferred_element_type=F32)
        hb = (a * jax.nn.sigmoid(a)) * b
        y_ref[...] = jnp.dot(hb.astype(BF16), w2b[...], preferred_element_type=F32)

    @pl.when(i >= nused_ref[0])
    def _():
        y_ref[...] = jnp.zeros_like(y_ref)


def _experts(blk_e, n_used, xpad, w1, w3, w2):
    rows = xpad.shape[0]
    tm = EXPERT_ROWS
    return pl.pallas_call(
        _experts_body,
        grid_spec=pltpu.PrefetchScalarGridSpec(
            num_scalar_prefetch=2,
            grid=(rows // tm,),
            in_specs=[
                pl.BlockSpec((tm, D_MODEL), lambda i, be, nu: (i, 0)),
                pl.BlockSpec((1, D_MODEL, D_EXPERT), lambda i, be, nu: (be[i], 0, 0)),
                pl.BlockSpec((1, D_MODEL, D_EXPERT), lambda i, be, nu: (be[i], 0, 0)),
                pl.BlockSpec((1, D_EXPERT, D_MODEL), lambda i, be, nu: (be[i], 0, 0)),
            ],
            out_specs=pl.BlockSpec((tm, D_MODEL), lambda i, be, nu: (i, 0)),
            scratch_shapes=[
                pltpu.VMEM((D_MODEL, D_EXPERT), BF16),
                pltpu.VMEM((D_MODEL, D_EXPERT), BF16),
                pltpu.VMEM((D_EXPERT, D_MODEL), BF16),
            ],
        ),
        out_shape=jax.ShapeDtypeStruct((rows, D_MODEL), F32),
        compiler_params=_cparams("arbitrary"),
        name="experts",
    )(blk_e, n_used, xpad, w1, w3, w2)


def _combine_body(pstart_ref, idx_ref, h1_ref, gate_ref, l2g_ref, l2b_ref, ypad_ref, o_ref,
                  buf0, buf1, sem, *, tm):
    def row_copy(r, slot, buf):
        src = pstart_ref[idx_ref[0, 0, 4 * r + slot]] + idx_ref[0, 0, 4 * r + 2 + slot]
        return pltpu.make_async_copy(ypad_ref.at[pl.ds(src, 1)], buf.at[pl.ds(r, 1)], sem)

    def start(r, carry):
        row_copy(r, 0, buf0).start()
        row_copy(r, 1, buf1).start()
        return carry

    def wait(r, carry):
        row_copy(r, 0, buf0).wait()
        row_copy(r, 1, buf1).wait()
        return carry

    lax.fori_loop(0, tm, start, 0)
    lax.fori_loop(0, tm, wait, 0)
    gate = gate_ref[...]
    ffn = buf0[...] * gate[:, 0:1] + buf1[...] * gate[:, 1:2]
    o_ref[...] = _layer_norm(DEEPNORM_ALPHA * h1_ref[...] + ffn, l2g_ref[...], l2b_ref[...])


def _combine(pad_start, idx_smem, h1, gates, l2g, l2b, ypad, *, tm=256):
    n = h1.shape[0]
    return pl.pallas_call(
        functools.partial(_combine_body, tm=tm),
        grid_spec=pltpu.PrefetchScalarGridSpec(
            num_scalar_prefetch=1,
            grid=(n // tm,),
            in_specs=[
                pl.BlockSpec((1, 1, 4 * tm), lambda i, ps: (i, 0, 0), memory_space=pltpu.SMEM),
                pl.BlockSpec((tm, D_MODEL), lambda i, ps: (i, 0)),
                pl.BlockSpec((tm, LANES), lambda i, ps: (i, 0)),
                pl.BlockSpec((1, D_MODEL), lambda i, ps: (0, 0)),
                pl.BlockSpec((1, D_MODEL), lambda i, ps: (0, 0)),
                pl.BlockSpec(memory_space=pl.ANY),
            ],
            out_specs=pl.BlockSpec((tm, D_MODEL), lambda i, ps: (i, 0)),
            scratch_shapes=[
                pltpu.VMEM((tm, D_MODEL), F32),
                pltpu.VMEM((tm, D_MODEL), F32),
                pltpu.SemaphoreType.DMA(()),
            ],
        ),
        out_shape=jax.ShapeDtypeStruct((n, D_MODEL), F32),
        compiler_params=_cparams("arbitrary"),
        name="combine",
    )(pad_start, idx_smem, h1, gates, l2g, l2b, ypad)


def _row(v, width=None):
    v = v.reshape(1, -1).astype(F32)
    if width is not None and v.shape[1] < width:
        v = jnp.pad(v, ((0, 0), (0, width - v.shape[1])))
    return v


def kernel(x, ln_in_g, ln_in_b, w_in, b_in, gm_ln_g, gm_ln_b, gm_w_s, gm_b_s, dn_conv_w, dn_a_log,
           dn_dt_bias, dn_norm_w, w_pa, w_pb, w_o, ln1_g, ln1_b, w_rg, b_rg, w_re, b_re, w1, w3, w2,
           ln2_g, ln2_b):
    bsz, t, d = x.shape
    n = bsz * t
    x2 = x.reshape(n, d)
    l = 0

    wi, bi = w_in[l], b_in[l]
    c_ab = 6 * D_MODEL
    pad = LANES - 2 * DN_HEADS
    w_cat = jnp.concatenate([wi[:, :c_ab], wi[:, c_ab + 2 * DN_HEADS:], wi[:, c_ab:c_ab + 2 * DN_HEADS],
                             jnp.zeros((d, pad), F32)], axis=1).astype(BF16)
    b_cat = jnp.concatenate([bi[:c_ab], bi[c_ab + 2 * DN_HEADS:], bi[c_ab:c_ab + 2 * DN_HEADS],
                             jnp.zeros((pad,), F32)]).reshape(1, -1)

    ln_g, ln_b = _row(ln_in_g), _row(ln_in_b)
    proj = _inproj(x2, ln_g, ln_b, w_cat, b_cat)

    bs_full = jnp.broadcast_to(gm_b_s[l][:, :, None], (GM_GROUPS, GM_BLOCK, LANES))
    ya = _gmlp(proj, _row(gm_ln_g[l]), _row(gm_ln_b[l]), gm_w_s[l], bs_full)

    yb = _deltanet(proj, dn_conv_w[l], _row(dn_a_log[l], LANES), _row(dn_dt_bias[l], LANES),
                   _row(dn_norm_w[l]), bsz=bsz, t=t)

    w_r = jnp.concatenate([w_rg[l], w_re[l], jnp.zeros((d, LANES - MOE_GROUPS - N_EXPERTS), F32)], axis=1)
    b_r = _row(jnp.concatenate([b_rg[l], b_re[l]]), LANES)
    h1, logits = _merge(x2, ya, yb, proj, ln_g, ln_b, w_pa[l].astype(BF16), w_pb[l].astype(BF16),
                        w_o[l].astype(BF16), _row(ln1_g[l]), _row(ln1_b[l]), w_r, b_r)

    gates, idx, counts = _route(logits)

    tm_e = EXPERT_ROWS
    cnt = counts[0, :N_EXPERTS]
    padded = (cnt + tm_e - 1) // tm_e * tm_e
    pad_end = jnp.cumsum(padded)
    pad_start = (pad_end - padded).astype(jnp.int32)
    n_blocks = (2 * n + N_EXPERTS * (tm_e - 1) + tm_e - 1) // tm_e
    blk_start = jnp.arange(n_blocks, dtype=jnp.int32) * tm_e
    blk_e = jnp.minimum(jnp.sum(pad_end[None, :] <= blk_start[:, None], -1), N_EXPERTS - 1).astype(jnp.int32)
    n_used = (pad_end[-1] // tm_e).astype(jnp.int32).reshape(1)

    tm_tok = 256
    idx_smem = idx[:, :4].reshape(n // tm_tok, 1, 4 * tm_tok)
    xpad = _dispatch(pad_start, idx_smem, h1, jnp.zeros((n_blocks * tm_e, d), F32), tm=tm_tok)
    ypad = _experts(blk_e, n_used, xpad, w1[l], w3[l], w2[l])
    out = _combine(pad_start, idx_smem, h1, gates, _row(ln2_g[l]), _row(ln2_b[l]), ypad, tm=tm_tok)
    return out.reshape(bsz, t, d)
```

```python
import functools
import math

import jax
import jax.numpy as jnp
from jax import lax
from jax.experimental import pallas as pl
from jax.experimental.pallas import tpu as pltpu

D_MODEL = 1024
GM_GROUPS = 8
GM_BLOCK = 128
GM_CHUNK = 64
DN_HEADS = 8
DN_DK = 128
DN_CONV = 4
MOE_GROUPS = 4
MOE_EPG = 8
N_EXPERTS = MOE_GROUPS * MOE_EPG
D_EXPERT = D_MODEL // 2
LN_EPS = 1e-5
RMS_EPS = 1e-6
L2_EPS = 1e-6
DEEPNORM_ALPHA = 2.0 ** 0.25

LANES = 128
PROJ_WIDTH = 8 * D_MODEL + LANES
COL_U, COL_V, COL_Q, COL_K, COL_VV, COL_Z, COL_GA, COL_GB = range(8)
COL_AB = 8 * D_MODEL // LANES

DN_CHUNK = 64
CARRY_ROWS = 8
EXPERT_ROWS = 256
VMEM_LIMIT = 56 * 1024 * 1024
SUBLANES = 8
ROW_CHUNKS = D_MODEL // LANES
assert ROW_CHUNKS == SUBLANES

F32 = jnp.float32
BF16 = jnp.bfloat16


def _cparams(*sem):
    return pltpu.CompilerParams(dimension_semantics=sem, vmem_limit_bytes=VMEM_LIMIT)


def _store_row_tiles(ref, val):
    m = val.shape[0]
    for ch in range(ROW_CHUNKS):
        ref[pl.ds(ch, m, stride=ROW_CHUNKS), :] = val[:, ch * LANES:(ch + 1) * LANES]


def _load_row_tiles(ref, m):
    return jnp.concatenate([ref[pl.ds(ch, m, stride=ROW_CHUNKS), :] for ch in range(ROW_CHUNKS)], axis=1)


def _layer_norm(x, g, b):
    mu = jnp.mean(x, -1, keepdims=True)
    xc = x - mu
    var = jnp.mean(xc * xc, -1, keepdims=True)
    return xc * lax.rsqrt(var + LN_EPS) * g + b


def _dot(a, b):
    return jnp.dot(a.astype(BF16), b.astype(BF16), preferred_element_type=F32)


def _dot_nt(a, b):
    return lax.dot_general(a.astype(BF16), b.astype(BF16), (((1,), (1,)), ((), ())),
                           preferred_element_type=F32)


def _inproj_body(x_ref, g_ref, b_ref, w_ref, bias_ref, o_ref, h_scr):
    @pl.when(pl.program_id(1) == 0)
    def _():
        h_scr[...] = _layer_norm(x_ref[...], g_ref[...], b_ref[...]).astype(BF16)

    o_ref[...] = jnp.dot(h_scr[...], w_ref[...], preferred_element_type=F32) + bias_ref[...]


def _inproj(x2, ln_g, ln_b, w, bias, *, tm=512, tn=1664):
    n = x2.shape[0]
    return pl.pallas_call(
        _inproj_body,
        grid=(n // tm, PROJ_WIDTH // tn),
        in_specs=[
            pl.BlockSpec((tm, D_MODEL), lambda i, j: (i, 0)),
            pl.BlockSpec((1, D_MODEL), lambda i, j: (0, 0)),
            pl.BlockSpec((1, D_MODEL), lambda i, j: (0, 0)),
            pl.BlockSpec((D_MODEL, tn), lambda i, j: (0, j)),
            pl.BlockSpec((1, tn), lambda i, j: (0, j)),
        ],
        out_specs=pl.BlockSpec((tm, tn), lambda i, j: (i, j)),
        out_shape=jax.ShapeDtypeStruct((n, PROJ_WIDTH), F32),
        scratch_shapes=[pltpu.VMEM((tm, D_MODEL), BF16)],
        compiler_params=_cparams("parallel", "arbitrary"),
        name="inproj",
    )(x2, ln_g, ln_b, w, bias)


def _gelu(x):
    return 0.5 * x * (1.0 + lax.erf(x * (1.0 / math.sqrt(2.0))))


def _gmlp_body(u_ref, v_ref, lng_ref, lnb_ref, ws_ref, bs_ref, o_ref, *, nblk):
    u = _gelu(u_ref[...])
    v = _layer_norm(_gelu(v_ref[...]), lng_ref[...], lnb_ref[...]).astype(BF16)
    row_chunk = lax.broadcasted_iota(jnp.int32, (GM_BLOCK, GM_BLOCK), 0) // GM_CHUNK
    col_chunk = lax.broadcasted_iota(jnp.int32, (GM_BLOCK, GM_BLOCK), 1) // GM_CHUNK
    causal = col_chunk <= row_chunk
    for g in range(GM_GROUPS):
        cols = slice(g * LANES, (g + 1) * LANES)
        w = jnp.where(causal, ws_ref[g], 0.0).astype(BF16)
        for blk in range(nblk):
            rows = slice(blk * GM_BLOCK, (blk + 1) * GM_BLOCK)
            s = jnp.dot(w, v[rows, cols], preferred_element_type=F32) + bs_ref[g]
            o_ref[rows, cols] = (u[rows, cols] * s).astype(o_ref.dtype)


def _gmlp(proj, ln_g, ln_b, w_s, b_s_full, *, nblk=2):
    n = proj.shape[0]
    rows = nblk * GM_BLOCK
    return pl.pallas_call(
        functools.partial(_gmlp_body, nblk=nblk),
        grid=(n // rows,),
        in_specs=[
            pl.BlockSpec((rows, D_MODEL), lambda i: (i, COL_U)),
            pl.BlockSpec((rows, D_MODEL), lambda i: (i, COL_V)),
            pl.BlockSpec((1, D_MODEL), lambda i: (0, 0)),
            pl.BlockSpec((1, D_MODEL), lambda i: (0, 0)),
            pl.BlockSpec((GM_GROUPS, GM_BLOCK, GM_BLOCK), lambda i: (0, 0, 0)),
            pl.BlockSpec((GM_GROUPS, GM_BLOCK, LANES), lambda i: (0, 0, 0)),
        ],
        out_specs=pl.BlockSpec((rows, D_MODEL), lambda i: (i, 0)),
        out_shape=jax.ShapeDtypeStruct((n, D_MODEL), BF16),
        compiler_params=_cparams("parallel"),
        name="gmlp",
    )(proj, proj, ln_g, ln_b, w_s, b_s_full)


def _cumsum_rows(x):
    n = x.shape[0]
    row = lax.broadcasted_iota(jnp.int32, x.shape, 0)
    shift = 1
    while shift < n:
        x = x + jnp.where(row >= shift, pltpu.roll(x, shift, 0), 0.0)
        shift *= 2
    return x


def _deltanet_body(q_ref, k_ref, v_ref, z_ref, ab_ref, cw_ref, alog_ref, dtb_ref, nw_ref,
                   o_ref, xs_ref, state_ref, *, nb):
    c = DN_CHUNK
    w3 = 3 * D_MODEL

    @pl.when(pl.program_id(1) == 0)
    def _():
        xs_ref[:, 0:CARRY_ROWS, :] = jnp.zeros((nb, CARRY_ROWS, w3), F32)
        state_ref[...] = jnp.zeros_like(state_ref)

    ri = lax.broadcasted_iota(jnp.int32, (c, c), 0)
    ci = lax.broadcasted_iota(jnp.int32, (c, c), 1)
    causal = ri >= ci
    strict = ri > ci
    heads = range(DN_HEADS)
    hsl = lambda base, h: slice(base + h * DN_DK, base + (h + 1) * DN_DK)

    q, k, v, g_col, beta, g_last, decay = [], [], [], [], [], [], []
    for b in range(nb):
        xs_ref[b, CARRY_ROWS:, 0:D_MODEL] = q_ref[b]
        xs_ref[b, CARRY_ROWS:, D_MODEL:2 * D_MODEL] = k_ref[b]
        xs_ref[b, CARRY_ROWS:, 2 * D_MODEL:] = v_ref[b]
        acc = cw_ref[DN_CONV - 1:DN_CONV, :] * xs_ref[b, CARRY_ROWS:, :]
        for j in range(DN_CONV - 1):
            off = CARRY_ROWS - (DN_CONV - 1) + j
            acc = acc + cw_ref[j:j + 1, :] * xs_ref[b, off:off + c, :]
        xs_ref[b, 0:CARRY_ROWS, :] = xs_ref[b, c:c + CARRY_ROWS, :]
        qkv = acc * jax.nn.sigmoid(acc)

        ab = ab_ref[b]
        xg = ab + dtb_ref[...]
        softplus = jnp.maximum(xg, 0.0) + jnp.log1p(jnp.exp(-jnp.abs(xg)))
        gcum = _cumsum_rows(-jnp.exp(alog_ref[...]) * softplus)
        beta_all = jax.nn.sigmoid(ab)
        gcum_sq = gcum if c == LANES else jnp.concatenate([gcum, jnp.zeros((LANES - c, LANES), F32)], axis=0)
        gcum_t = gcum_sq.T

        for h in heads:
            qh, kh = qkv[:, hsl(0, h)], qkv[:, hsl(D_MODEL, h)]
            q.append(qh * (lax.rsqrt(jnp.sum(qh * qh, -1, keepdims=True) + L2_EPS) * (DN_DK ** -0.5)))
            k.append(kh * lax.rsqrt(jnp.sum(kh * kh, -1, keepdims=True) + L2_EPS))
            v.append(qkv[:, hsl(2 * D_MODEL, h)])
            g_col.append(gcum[:, h:h + 1])
            beta.append(beta_all[:, DN_HEADS + h:DN_HEADS + h + 1])
            g_last.append(gcum[c - 1:c, h:h + 1])
            decay.append(jnp.where(causal, jnp.exp(jnp.where(causal, g_col[-1] - gcum_t[h:h + 1, 0:c], 0.0)), 0.0))

    units = range(nb * DN_HEADS)
    eg = [jnp.exp(g) for g in g_col]
    kb = [x.astype(BF16) for x in k]
    kq = [lax.dot_general(jnp.concatenate([kb[u], q[u].astype(BF16)], axis=0), kb[u],
                          (((1,), (1,)), ((), ())), preferred_element_type=F32) for u in units]
    p = [jnp.where(strict, beta[u] * kq[u][0:c] * decay[u], 0.0).astype(BF16) for u in units]
    attn = [(kq[u][c:] * decay[u]).astype(BF16) for u in units]

    r = [jnp.concatenate([v[u] * beta[u], k[u] * (beta[u] * eg[u])], axis=1) for u in units]
    r = [r[u] - jnp.dot(p[u], r[u].astype(BF16), preferred_element_type=F32) for u in units]
    span = 2
    while span < c:
        p = [jnp.dot(p[u], p[u], preferred_element_type=F32).astype(BF16) for u in units]
        r = [r[u] + jnp.dot(p[u], r[u].astype(BF16), preferred_element_type=F32) for u in units]
        span *= 2

    s_old = [state_ref[u] for u in units]
    ws = [jnp.dot(jnp.concatenate([r[u][:, DN_DK:], q[u] * eg[u]], axis=0).astype(BF16),
                  s_old[u].astype(BF16), preferred_element_type=F32) for u in units]
    v_new = [(r[u][:, 0:DN_DK] - ws[u][0:c]).astype(BF16) for u in units]
    o = [ws[u][c:] + jnp.dot(attn[u], v_new[u], preferred_element_type=F32) for u in units]
    k_dec = [(k[u] * jnp.exp(g_last[u] - g_col[u])).astype(BF16) for u in units]
    ds = [lax.dot_general(k_dec[u], v_new[u], (((0,), (0,)), ((), ())), preferred_element_type=F32)
          for u in units]
    for u in units:
        state_ref[u] = s_old[u] * jnp.exp(g_last[u]) + ds[u]

    for b in range(nb):
        z = z_ref[b]
        for h in heads:
            u = b * DN_HEADS + h
            zh = z[:, hsl(0, h)]
            on = o[u] * lax.rsqrt(jnp.mean(o[u] * o[u], -1, keepdims=True) + RMS_EPS) * nw_ref[...]
            o_ref[b, :, hsl(0, h)] = (on * (zh * jax.nn.sigmoid(zh))).astype(o_ref.dtype)


def _deltanet(proj, conv_w, alog_row, dtb_row, norm_w, *, bsz, t, nb=2):
    c = DN_CHUNK
    proj3 = proj.reshape(bsz, t, PROJ_WIDTH)
    col = lambda j: pl.BlockSpec((nb, c, D_MODEL), lambda bp, n: (bp, n, j))
    yb = pl.pallas_call(
        functools.partial(_deltanet_body, nb=nb),
        grid=(bsz // nb, t // c),
        in_specs=[
            col(COL_Q), col(COL_K), col(COL_VV), col(COL_Z),
            pl.BlockSpec((nb, c, LANES), lambda bp, n: (bp, n, COL_AB)),
            pl.BlockSpec((DN_CONV, 3 * D_MODEL), lambda bp, n: (0, 0)),
            pl.BlockSpec((1, LANES), lambda bp, n: (0, 0)),
            pl.BlockSpec((1, LANES), lambda bp, n: (0, 0)),
            pl.BlockSpec((1, DN_DK), lambda bp, n: (0, 0)),
        ],
        out_specs=pl.BlockSpec((nb, c, D_MODEL), lambda bp, n: (bp, n, 0)),
        out_shape=jax.ShapeDtypeStruct((bsz, t, D_MODEL), BF16),
        scratch_shapes=[
            pltpu.VMEM((nb, CARRY_ROWS + c, 3 * D_MODEL), F32),
            pltpu.VMEM((nb * DN_HEADS, DN_DK, DN_DK), F32),
        ],
        compiler_params=_cparams("parallel", "arbitrary"),
        name="deltanet",
    )(proj3, proj3, proj3, proj3, proj3, conv_w, alog_row, dtb_row, norm_w)
    return yb.reshape(bsz * t, D_MODEL)


def _merge_body(x_ref, ya_ref, yb_ref, ga_ref, gb_ref, lng_ref, lnb_ref, wpa_ref, wpb_ref, wo_ref,
                l1g_ref, l1b_ref, wr_ref, br_ref, h1_ref, logit_ref):
    h = _layer_norm(x_ref[...], lng_ref[...], lnb_ref[...])
    pa = jnp.dot(ya_ref[...], wpa_ref[...], preferred_element_type=F32)
    pb = jnp.dot(yb_ref[...], wpb_ref[...], preferred_element_type=F32)
    merged = jax.nn.sigmoid(ga_ref[...]) * pa + jax.nn.sigmoid(gb_ref[...]) * pb
    mix = jnp.dot(merged.astype(BF16), wo_ref[...], preferred_element_type=F32)
    h1 = _layer_norm(DEEPNORM_ALPHA * h + mix, l1g_ref[...], l1b_ref[...])
    _store_row_tiles(h1_ref, h1)
    logit_ref[...] = jnp.dot(h1, wr_ref[...], preferred_element_type=F32,
                             precision=lax.Precision.HIGHEST) + br_ref[...]


def _merge(x2, ya, yb, proj, ln_g, ln_b, w_pa, w_pb, w_o, l1g, l1b, w_r, b_r, *, tm=256):
    n = x2.shape[0]
    vec = lambda: pl.BlockSpec((1, D_MODEL), lambda i: (0, 0))
    mat = lambda: pl.BlockSpec((D_MODEL, D_MODEL), lambda i: (0, 0))
    return pl.pallas_call(
        _merge_body,
        grid=(n // tm,),
        in_specs=[
            pl.BlockSpec((tm, D_MODEL), lambda i: (i, 0)),
            pl.BlockSpec((tm, D_MODEL), lambda i: (i, 0)),
            pl.BlockSpec((tm, D_MODEL), lambda i: (i, 0)),
            pl.BlockSpec((tm, D_MODEL), lambda i: (i, COL_GA)),
            pl.BlockSpec((tm, D_MODEL), lambda i: (i, COL_GB)),
            vec(), vec(), mat(), mat(), mat(), vec(), vec(),
            pl.BlockSpec((D_MODEL, LANES), lambda i: (0, 0)),
            pl.BlockSpec((1, LANES), lambda i: (0, 0)),
        ],
        out_specs=[
            pl.BlockSpec((tm * ROW_CHUNKS, LANES), lambda i: (i, 0)),
            pl.BlockSpec((tm, LANES), lambda i: (i, 0)),
        ],
        out_shape=[
            jax.ShapeDtypeStruct((n * ROW_CHUNKS, LANES), F32),
            jax.ShapeDtypeStruct((n, LANES), F32),
        ],
        compiler_params=_cparams("parallel"),
        name="merge",
    )(x2, ya, yb, proj, proj, ln_g, ln_b, w_pa, w_pb, w_o, l1g, l1b, w_r, b_r)


def _route_body(logit_ref, gate_ref, idx_ref, cnt_ref, carry_ref, *, tm):
    @pl.when(pl.program_id(0) == 0)
    def _():
        carry_ref[...] = jnp.zeros_like(carry_ref)

    lg = logit_ref[...]
    lane_i = lax.broadcasted_iota(jnp.int32, lg.shape, 1)
    lane = lane_i.astype(F32)
    neg = jnp.float32(-jnp.inf)
    big = jnp.float32(1 << 20)

    is_grp = lane_i < MOE_GROUPS
    gl = jnp.where(is_grp, lg, neg)
    gmax = jnp.max(gl, -1, keepdims=True)
    grp = jnp.min(jnp.where(is_grp & (gl == gmax), lane, big), -1, keepdims=True)
    p_grp = 1.0 / jnp.sum(jnp.where(is_grp, jnp.exp(gl - gmax), 0.0), -1, keepdims=True)

    elane = lane - MOE_GROUPS
    in_grp = (elane >= grp * MOE_EPG) & (elane < (grp + 1) * MOE_EPG)
    el = jnp.where(in_grp, lg, neg)
    m1 = jnp.max(el, -1, keepdims=True)
    e1 = jnp.min(jnp.where(in_grp & (el == m1), elane, big), -1, keepdims=True)
    rest = in_grp & (elane != e1)
    el2 = jnp.where(rest, lg, neg)
    m2 = jnp.max(el2, -1, keepdims=True)
    e2 = jnp.min(jnp.where(rest & (el2 == m2), elane, big), -1, keepdims=True)
    t2 = jnp.exp(m2 - m1)
    g1 = p_grp * (1.0 / (1.0 + t2))
    g2 = p_grp * (t2 / (1.0 + t2))

    oh1 = lane == e1
    oh2 = lane == e2
    onehot = jnp.where(oh1 | oh2, 1.0, 0.0)
    ri = lax.broadcasted_iota(jnp.int32, (tm, tm), 0)
    ci = lax.broadcasted_iota(jnp.int32, (tm, tm), 1)
    tri = jnp.where(ci < ri, 1.0, 0.0).astype(BF16)
    before = jnp.dot(tri, onehot.astype(BF16), preferred_element_type=F32) + carry_ref[0:1, :]
    r1 = jnp.sum(jnp.where(oh1, before, 0.0), -1, keepdims=True)
    r2 = jnp.sum(jnp.where(oh2, before, 0.0), -1, keepdims=True)
    total = carry_ref[0:1, :] + jnp.sum(onehot, 0, keepdims=True)
    carry_ref[...] = jnp.broadcast_to(total, carry_ref.shape)
    cnt_ref[...] = jnp.broadcast_to(total, cnt_ref.shape).astype(jnp.int32)

    gate_ref[...] = jnp.where(lane_i == 0, g1, jnp.where(lane_i == 1, g2, 0.0))
    idx_ref[...] = jnp.where(lane_i == 0, e1, jnp.where(lane_i == 1, e2,
                             jnp.where(lane_i == 2, r1, jnp.where(lane_i == 3, r2, 0.0)))).astype(jnp.int32)


def _route(logits, *, tm=256):
    n = logits.shape[0]
    return pl.pallas_call(
        functools.partial(_route_body, tm=tm),
        grid=(n // tm,),
        in_specs=[pl.BlockSpec((tm, LANES), lambda i: (i, 0))],
        out_specs=[
            pl.BlockSpec((tm, LANES), lambda i: (i, 0)),
            pl.BlockSpec((tm, LANES), lambda i: (i, 0)),
            pl.BlockSpec((8, LANES), lambda i: (0, 0)),
        ],
        out_shape=[
            jax.ShapeDtypeStruct((n, LANES), F32),
            jax.ShapeDtypeStruct((n, LANES), jnp.int32),
            jax.ShapeDtypeStruct((8, LANES), jnp.int32),
        ],
        scratch_shapes=[pltpu.VMEM((8, LANES), F32)],
        compiler_params=_cparams("arbitrary"),
        name="route",
    )(logits)


def _dest_body(idx_ref, ps_ref, o_ref):
    idx = idx_ref[...]
    lane = lax.broadcasted_iota(jnp.int32, idx.shape, 1)
    ps = ps_ref[...].astype(F32)
    d1 = jnp.sum(jnp.where(lane == idx[:, 0:1], ps, 0.0), -1, keepdims=True).astype(jnp.int32) + idx[:, 2:3]
    d2 = jnp.sum(jnp.where(lane == idx[:, 1:2], ps, 0.0), -1, keepdims=True).astype(jnp.int32) + idx[:, 3:4]
    o_ref[...] = jnp.where(lane == 0, d1, jnp.where(lane == 1, d2, 0))


def _dest(idx, pad_start_row, *, tm=1024):
    n = idx.shape[0]
    return pl.pallas_call(
        _dest_body,
        grid=(n // tm,),
        in_specs=[pl.BlockSpec((tm, LANES), lambda i: (i, 0)), pl.BlockSpec((1, LANES), lambda i: (0, 0))],
        out_specs=pl.BlockSpec((tm, LANES), lambda i: (i, 0)),
        out_shape=jax.ShapeDtypeStruct((n, LANES), jnp.int32),
        compiler_params=_cparams("parallel"),
        name="dest",
    )(idx, pad_start_row)


DMA_GROUP = 8


def _row_tile(ref, row):
    return ref.at[pl.ds(pl.multiple_of(row * SUBLANES, SUBLANES), SUBLANES)]


def _dispatch_body(dest_ref, h_ref, xin_ref, xpad_ref, sem, *, tm):
    del xin_ref
    base = pl.program_id(0) * tm

    def row_copy(src_row, dst_row):
        return pltpu.make_async_copy(_row_tile(h_ref, src_row), _row_tile(xpad_ref, dst_row), sem)

    def start(g, carry):
        r0 = g * DMA_GROUP
        dst = [dest_ref[0, 0, 2 * r0 + j] for j in range(2 * DMA_GROUP)]
        for j in range(2 * DMA_GROUP):
            row_copy(base + r0 + j // 2, dst[j]).start()
        return carry

    def wait(g, carry):
        for _ in range(2 * DMA_GROUP):
            row_copy(0, 0).wait()
        return carry

    lax.fori_loop(0, tm // DMA_GROUP, start, 0)
    lax.fori_loop(0, tm // DMA_GROUP, wait, 0)


def _dispatch(dest_smem, h1_rows, xpad0, *, tm):
    n = h1_rows.shape[0] // ROW_CHUNKS
    return pl.pallas_call(
        functools.partial(_dispatch_body, tm=tm),
        grid=(n // tm,),
        in_specs=[
            pl.BlockSpec((1, 1, 2 * tm), lambda i: (i, 0, 0), memory_space=pltpu.SMEM),
            pl.BlockSpec(memory_space=pl.ANY),
            pl.BlockSpec(memory_space=pl.ANY),
        ],
        out_specs=pl.BlockSpec(memory_space=pl.ANY),
        scratch_shapes=[pltpu.SemaphoreType.DMA(())],
        out_shape=jax.ShapeDtypeStruct(xpad0.shape, xpad0.dtype),
        input_output_aliases={2: 0},
        compiler_params=_cparams("arbitrary"),
        name="dispatch",
    )(dest_smem, h1_rows, xpad0)


def _experts_body(blk_e_ref, nused_ref, x_ref, w1_ref, w3_ref, w2_ref, y_ref, w1b, w3b, w2b):
    i = pl.program_id(0)
    prev = blk_e_ref[jnp.maximum(i - 1, 0)]

    @pl.when((i == 0) | (blk_e_ref[i] != prev))
    def _():
        w1b[...] = w1_ref[0].astype(BF16)
        w3b[...] = w3_ref[0].astype(BF16)
        w2b[...] = w2_ref[0].astype(BF16)

    @pl.when(i < nused_ref[0])
    def _():
        xb = _load_row_tiles(x_ref, EXPERT_ROWS).astype(BF16)
        a = jnp.dot(xb, w1b[...], preferred_element_type=F32)
        b = jnp.dot(xb, w3b[...], preferred_element_type=F32)
        hb = (a * jax.nn.sigmoid(a)) * b
        _store_row_tiles(y_ref, jnp.dot(hb.astype(BF16), w2b[...], preferred_element_type=F32))

    @pl.when(i >= nused_ref[0])
    def _():
        y_ref[...] = jnp.zeros_like(y_ref)


def _experts(blk_e, n_used, xpad, w1, w3, w2):
    tm = EXPERT_ROWS * ROW_CHUNKS
    return pl.pallas_call(
        _experts_body,
        grid_spec=pltpu.PrefetchScalarGridSpec(
            num_scalar_prefetch=2,
            grid=(xpad.shape[0] // tm,),
            in_specs=[
                pl.BlockSpec((tm, LANES), lambda i, be, nu: (i, 0)),
                pl.BlockSpec((1, D_MODEL, D_EXPERT), lambda i, be, nu: (be[i], 0, 0)),
                pl.BlockSpec((1, D_MODEL, D_EXPERT), lambda i, be, nu: (be[i], 0, 0)),
                pl.BlockSpec((1, D_EXPERT, D_MODEL), lambda i, be, nu: (be[i], 0, 0)),
            ],
            out_specs=pl.BlockSpec((tm, LANES), lambda i, be, nu: (i, 0)),
            scratch_shapes=[
                pltpu.VMEM((D_MODEL, D_EXPERT), BF16),
                pltpu.VMEM((D_MODEL, D_EXPERT), BF16),
                pltpu.VMEM((D_EXPERT, D_MODEL), BF16),
            ],
        ),
        out_shape=jax.ShapeDtypeStruct(xpad.shape, F32),
        compiler_params=_cparams("arbitrary"),
        name="experts",
    )(blk_e, n_used, xpad, w1, w3, w2)


def _combine_body(dest_ref, h1_ref, gate_ref, l2g_ref, l2b_ref, ypad_ref, o_ref, buf0, buf1, sem, *, tm):
    bufs = (buf0, buf1)

    def row_copy(src_row, r, slot):
        return pltpu.make_async_copy(_row_tile(ypad_ref, src_row), _row_tile(bufs[slot], r), sem)

    def start(g, carry):
        r0 = g * DMA_GROUP
        src = [dest_ref[0, 0, 2 * r0 + j] for j in range(2 * DMA_GROUP)]
        for j in range(2 * DMA_GROUP):
            row_copy(src[j], r0 + j // 2, j % 2).start()
        return carry

    def wait(g, carry):
        for j in range(2 * DMA_GROUP):
            row_copy(0, 0, j % 2).wait()
        return carry

    lax.fori_loop(0, tm // DMA_GROUP, start, 0)
    lax.fori_loop(0, tm // DMA_GROUP, wait, 0)
    gate = gate_ref[...]
    ffn = _load_row_tiles(buf0, tm) * gate[:, 0:1] + _load_row_tiles(buf1, tm) * gate[:, 1:2]
    h1 = _load_row_tiles(h1_ref, tm)
    o_ref[...] = _layer_norm(DEEPNORM_ALPHA * h1 + ffn, l2g_ref[...], l2b_ref[...])


def _combine(dest_smem, h1_rows, gates, l2g, l2b, ypad, *, tm):
    n = h1_rows.shape[0] // ROW_CHUNKS
    return pl.pallas_call(
        functools.partial(_combine_body, tm=tm),
        grid=(n // tm,),
        in_specs=[
            pl.BlockSpec((1, 1, 2 * tm), lambda i: (i, 0, 0), memory_space=pltpu.SMEM),
            pl.BlockSpec((tm * ROW_CHUNKS, LANES), lambda i: (i, 0)),
            pl.BlockSpec((tm, LANES), lambda i: (i, 0)),
            pl.BlockSpec((1, D_MODEL), lambda i: (0, 0)),
            pl.BlockSpec((1, D_MODEL), lambda i: (0, 0)),
            pl.BlockSpec(memory_space=pl.ANY),
        ],
        out_specs=pl.BlockSpec((tm, D_MODEL), lambda i: (i, 0)),
        scratch_shapes=[
            pltpu.VMEM((tm * ROW_CHUNKS, LANES), F32),
            pltpu.VMEM((tm * ROW_CHUNKS, LANES), F32),
            pltpu.SemaphoreType.DMA(()),
        ],
        out_shape=jax.ShapeDtypeStruct((n, D_MODEL), F32),
        compiler_params=_cparams("arbitrary"),
        name="combine",
    )(dest_smem, h1_rows, gates, l2g, l2b, ypad)


def _row(v, width=None):
    v = v.reshape(1, -1).astype(F32)
    if width is not None and v.shape[1] < width:
        v = jnp.pad(v, ((0, 0), (0, width - v.shape[1])))
    return v


def kernel(x, ln_in_g, ln_in_b, w_in, b_in, gm_ln_g, gm_ln_b, gm_w_s, gm_b_s, dn_conv_w, dn_a_log,
           dn_dt_bias, dn_norm_w, w_pa, w_pb, w_o, ln1_g, ln1_b, w_rg, b_rg, w_re, b_re, w1, w3, w2,
           ln2_g, ln2_b):
    bsz, t, d = x.shape
    n = bsz * t
    x2 = x.reshape(n, d)
    l = 0

    wi, bi = w_in[l], b_in[l]
    c_ab = 6 * D_MODEL
    pad = LANES - 2 * DN_HEADS
    w_cat = jnp.concatenate([wi[:, :c_ab], wi[:, c_ab + 2 * DN_HEADS:], wi[:, c_ab:c_ab + 2 * DN_HEADS],
                             jnp.zeros((d, pad), F32)], axis=1).astype(BF16)
    b_cat = jnp.concatenate([bi[:c_ab], bi[c_ab + 2 * DN_HEADS:], bi[c_ab:c_ab + 2 * DN_HEADS],
                             jnp.zeros((pad,), F32)]).reshape(1, -1)

    ln_g, ln_b = _row(ln_in_g), _row(ln_in_b)
    proj = _inproj(x2, ln_g, ln_b, w_cat, b_cat)

    bs_full = jnp.broadcast_to(gm_b_s[l][:, :, None], (GM_GROUPS, GM_BLOCK, LANES))
    ya = _gmlp(proj, _row(gm_ln_g[l]), _row(gm_ln_b[l]), gm_w_s[l], bs_full)

    yb = _deltanet(proj, dn_conv_w[l], _row(dn_a_log[l], LANES), _row(dn_dt_bias[l], LANES),
                   _row(dn_norm_w[l]), bsz=bsz, t=t)

    w_r = jnp.concatenate([w_rg[l], w_re[l], jnp.zeros((d, LANES - MOE_GROUPS - N_EXPERTS), F32)], axis=1)
    b_r = _row(jnp.concatenate([b_rg[l], b_re[l]]), LANES)
    h1, logits = _merge(x2, ya, yb, proj, ln_g, ln_b, w_pa[l].astype(BF16), w_pb[l].astype(BF16),
                        w_o[l].astype(BF16), _row(ln1_g[l]), _row(ln1_b[l]), w_r, b_r)

    gates, idx, counts = _route(logits)

    tm_e = EXPERT_ROWS
    cnt = counts[0, :N_EXPERTS]
    padded = (cnt + tm_e - 1) // tm_e * tm_e
    pad_end = jnp.cumsum(padded)
    pad_start = (pad_end - padded).astype(jnp.int32)
    n_blocks = (2 * n + N_EXPERTS * (tm_e - 1) + tm_e - 1) // tm_e
    blk_start = jnp.arange(n_blocks, dtype=jnp.int32) * tm_e
    blk_e = jnp.minimum(jnp.sum(pad_end[None, :] <= blk_start[:, None], -1), N_EXPERTS - 1).astype(jnp.int32)
    n_used = (pad_end[-1] // tm_e).astype(jnp.int32).reshape(1)

    ps_row = jnp.pad(pad_start, (0, LANES - N_EXPERTS)).reshape(1, LANES)
    dest = _dest(idx, ps_row, tm=min(1024, n))[:, :2]
    tm_d, tm_c = 512, 256
    xpad0 = jnp.zeros((n_blocks * tm_e * ROW_CHUNKS, LANES), F32)
    xpad = _dispatch(dest.reshape(n // tm_d, 1, 2 * tm_d), h1, xpad0, tm=tm_d)
    ypad = _experts(blk_e, n_used, xpad, w1[l], w3[l], w2[l])
    out = _combine(dest.reshape(n // tm_c, 1, 2 * tm_c), h1, gates, _row(ln2_g[l]), _row(ln2_b[l]), ypad,
                   tm=tm_c)
    return out.reshape(bsz, t, d)
```

```python
import functools
import math

import jax
import jax.numpy as jnp
from jax import lax
from jax.experimental import pallas as pl
from jax.experimental.pallas import tpu as pltpu

D_MODEL = 1024
GM_GROUPS = 8
GM_BLOCK = 128
GM_CHUNK = 64
DN_HEADS = 8
DN_DK = 128
DN_CONV = 4
MOE_GROUPS = 4
MOE_EPG = 8
N_EXPERTS = MOE_GROUPS * MOE_EPG
D_EXPERT = D_MODEL // 2
LN_EPS = 1e-5
RMS_EPS = 1e-6
L2_EPS = 1e-6
DEEPNORM_ALPHA = 2.0 ** 0.25

LANES = 128
PROJ_WIDTH = 8 * D_MODEL + LANES
COL_U, COL_V, COL_Q, COL_K, COL_VV, COL_Z, COL_GA, COL_GB = range(8)
COL_AB = 8 * D_MODEL // LANES

DN_CHUNK = 64
CARRY_ROWS = 8
EXPERT_ROWS = 256
VMEM_LIMIT = 56 * 1024 * 1024
SUBLANES = 8
ROW_CHUNKS = D_MODEL // LANES
assert ROW_CHUNKS == SUBLANES

F32 = jnp.float32
BF16 = jnp.bfloat16


def _cparams(*sem):
    return pltpu.CompilerParams(dimension_semantics=sem, vmem_limit_bytes=VMEM_LIMIT)


def _store_row_tiles(ref, val):
    m = val.shape[0]
    for ch in range(ROW_CHUNKS):
        ref[pl.ds(ch, m, stride=ROW_CHUNKS), :] = val[:, ch * LANES:(ch + 1) * LANES]


def _load_row_tiles(ref, m):
    return jnp.concatenate([ref[pl.ds(ch, m, stride=ROW_CHUNKS), :] for ch in range(ROW_CHUNKS)], axis=1)


def _layer_norm(x, g, b):
    mu = jnp.mean(x, -1, keepdims=True)
    xc = x - mu
    var = jnp.mean(xc * xc, -1, keepdims=True)
    return xc * lax.rsqrt(var + LN_EPS) * g + b


def _dot(a, b):
    return jnp.dot(a.astype(BF16), b.astype(BF16), preferred_element_type=F32)


def _dot_nt(a, b):
    return lax.dot_general(a.astype(BF16), b.astype(BF16), (((1,), (1,)), ((), ())),
                           preferred_element_type=F32)


def _inproj_body(x_ref, g_ref, b_ref, w_ref, bias_ref, o_ref, h_scr):
    @pl.when(pl.program_id(1) == 0)
    def _():
        h_scr[...] = _layer_norm(x_ref[...], g_ref[...], b_ref[...]).astype(BF16)

    o_ref[...] = jnp.dot(h_scr[...], w_ref[...], preferred_element_type=F32) + bias_ref[...]


def _inproj(x2, ln_g, ln_b, w, bias, *, tm=512, tn=1664):
    n = x2.shape[0]
    return pl.pallas_call(
        _inproj_body,
        grid=(n // tm, PROJ_WIDTH // tn),
        in_specs=[
            pl.BlockSpec((tm, D_MODEL), lambda i, j: (i, 0)),
            pl.BlockSpec((1, D_MODEL), lambda i, j: (0, 0)),
            pl.BlockSpec((1, D_MODEL), lambda i, j: (0, 0)),
            pl.BlockSpec((D_MODEL, tn), lambda i, j: (0, j)),
            pl.BlockSpec((1, tn), lambda i, j: (0, j)),
        ],
        out_specs=pl.BlockSpec((tm, tn), lambda i, j: (i, j)),
        out_shape=jax.ShapeDtypeStruct((n, PROJ_WIDTH), F32),
        scratch_shapes=[pltpu.VMEM((tm, D_MODEL), BF16)],
        compiler_params=_cparams("parallel", "arbitrary"),
        name="inproj",
    )(x2, ln_g, ln_b, w, bias)


def _gelu(x):
    return 0.5 * x * (1.0 + lax.erf(x * (1.0 / math.sqrt(2.0))))


def _gmlp_body(u_ref, v_ref, lng_ref, lnb_ref, ws_ref, bs_ref, o_ref, *, nblk):
    u = _gelu(u_ref[...])
    v = _layer_norm(_gelu(v_ref[...]), lng_ref[...], lnb_ref[...]).astype(BF16)
    row_chunk = lax.broadcasted_iota(jnp.int32, (GM_BLOCK, GM_BLOCK), 0) // GM_CHUNK
    col_chunk = lax.broadcasted_iota(jnp.int32, (GM_BLOCK, GM_BLOCK), 1) // GM_CHUNK
    causal = col_chunk <= row_chunk
    for g in range(GM_GROUPS):
        cols = slice(g * LANES, (g + 1) * LANES)
        w = jnp.where(causal, ws_ref[g], 0.0).astype(BF16)
        for blk in range(nblk):
            rows = slice(blk * GM_BLOCK, (blk + 1) * GM_BLOCK)
            s = jnp.dot(w, v[rows, cols], preferred_element_type=F32) + bs_ref[g]
            o_ref[rows, cols] = (u[rows, cols] * s).astype(o_ref.dtype)


def _gmlp(proj, ln_g, ln_b, w_s, b_s_full, *, nblk=2):
    n = proj.shape[0]
    rows = nblk * GM_BLOCK
    return pl.pallas_call(
        functools.partial(_gmlp_body, nblk=nblk),
        grid=(n // rows,),
        in_specs=[
            pl.BlockSpec((rows, D_MODEL), lambda i: (i, COL_U)),
            pl.BlockSpec((rows, D_MODEL), lambda i: (i, COL_V)),
            pl.BlockSpec((1, D_MODEL), lambda i: (0, 0)),
            pl.BlockSpec((1, D_MODEL), lambda i: (0, 0)),
            pl.BlockSpec((GM_GROUPS, GM_BLOCK, GM_BLOCK), lambda i: (0, 0, 0)),
            pl.BlockSpec((GM_GROUPS, GM_BLOCK, LANES), lambda i: (0, 0, 0)),
        ],
        out_specs=pl.BlockSpec((rows, D_MODEL), lambda i: (i, 0)),
        out_shape=jax.ShapeDtypeStruct((n, D_MODEL), BF16),
        compiler_params=_cparams("parallel"),
        name="gmlp",
    )(proj, proj, ln_g, ln_b, w_s, b_s_full)


def _cumsum_rows(x):
    n = x.shape[0]
    row = lax.broadcasted_iota(jnp.int32, x.shape, 0)
    shift = 1
    while shift < n:
        x = x + jnp.where(row >= shift, pltpu.roll(x, shift, 0), 0.0)
        shift *= 2
    return x


def _deltanet_body(q_ref, k_ref, v_ref, z_ref, ab_ref, cw_ref, alog_ref, dtb_ref, nw_ref,
                   o_ref, xs_ref, state_ref, *, nb):
    c = DN_CHUNK
    w3 = 3 * D_MODEL

    @pl.when(pl.program_id(1) == 0)
    def _():
        xs_ref[:, 0:CARRY_ROWS, :] = jnp.zeros((nb, CARRY_ROWS, w3), F32)
        state_ref[...] = jnp.zeros_like(state_ref)

    ri = lax.broadcasted_iota(jnp.int32, (c, c), 0)
    ci = lax.broadcasted_iota(jnp.int32, (c, c), 1)
    causal = ri >= ci
    strict = ri > ci
    heads = range(DN_HEADS)
    hsl = lambda base, h: slice(base + h * DN_DK, base + (h + 1) * DN_DK)

    q, k, v, g_col, beta, g_last, decay = [], [], [], [], [], [], []
    for b in range(nb):
        xs_ref[b, CARRY_ROWS:, 0:D_MODEL] = q_ref[b]
        xs_ref[b, CARRY_ROWS:, D_MODEL:2 * D_MODEL] = k_ref[b]
        xs_ref[b, CARRY_ROWS:, 2 * D_MODEL:] = v_ref[b]
        acc = cw_ref[DN_CONV - 1:DN_CONV, :] * xs_ref[b, CARRY_ROWS:, :]
        for j in range(DN_CONV - 1):
            off = CARRY_ROWS - (DN_CONV - 1) + j
            acc = acc + cw_ref[j:j + 1, :] * xs_ref[b, off:off + c, :]
        xs_ref[b, 0:CARRY_ROWS, :] = xs_ref[b, c:c + CARRY_ROWS, :]
        qkv = acc * jax.nn.sigmoid(acc)

        ab = ab_ref[b]
        xg = ab + dtb_ref[...]
        softplus = jnp.maximum(xg, 0.0) + jnp.log1p(jnp.exp(-jnp.abs(xg)))
        gcum = _cumsum_rows(-jnp.exp(alog_ref[...]) * softplus)
        beta_all = jax.nn.sigmoid(ab)
        gcum_sq = gcum if c == LANES else jnp.concatenate([gcum, jnp.zeros((LANES - c, LANES), F32)], axis=0)
        gcum_t = gcum_sq.T

        for h in heads:
            qh, kh = qkv[:, hsl(0, h)], qkv[:, hsl(D_MODEL, h)]
            q.append(qh * (lax.rsqrt(jnp.sum(qh * qh, -1, keepdims=True) + L2_EPS) * (DN_DK ** -0.5)))
            k.append(kh * lax.rsqrt(jnp.sum(kh * kh, -1, keepdims=True) + L2_EPS))
            v.append(qkv[:, hsl(2 * D_MODEL, h)])
            g_col.append(gcum[:, h:h + 1])
            beta.append(beta_all[:, DN_HEADS + h:DN_HEADS + h + 1])
            g_last.append(gcum[c - 1:c, h:h + 1])
            decay.append(jnp.where(causal, jnp.exp(jnp.where(causal, g_col[-1] - gcum_t[h:h + 1, 0:c], 0.0)), 0.0))

    units = range(nb * DN_HEADS)
    eg = [jnp.exp(g) for g in g_col]
    kb = [x.astype(BF16) for x in k]
    kq = [lax.dot_general(jnp.concatenate([kb[u], q[u].astype(BF16)], axis=0), kb[u],
                          (((1,), (1,)), ((), ())), preferred_element_type=F32) for u in units]
    p = [jnp.where(strict, beta[u] * kq[u][0:c] * decay[u], 0.0).astype(BF16) for u in units]
    attn = [(kq[u][c:] * decay[u]).astype(BF16) for u in units]

    r = [jnp.concatenate([v[u] * beta[u], k[u] * (beta[u] * eg[u])], axis=1) for u in units]
    r = [r[u] - jnp.dot(p[u], r[u].astype(BF16), preferred_element_type=F32) for u in units]
    span = 2
    while span < c:
        p = [jnp.dot(p[u], p[u], preferred_element_type=F32).astype(BF16) for u in units]
        r = [r[u] + jnp.dot(p[u], r[u].astype(BF16), preferred_element_type=F32) for u in units]
        span *= 2

    s_old = [state_ref[u] for u in units]
    ws = [jnp.dot(jnp.concatenate([r[u][:, DN_DK:], q[u] * eg[u]], axis=0).astype(BF16),
                  s_old[u].astype(BF16), preferred_element_type=F32) for u in units]
    v_new = [(r[u][:, 0:DN_DK] - ws[u][0:c]).astype(BF16) for u in units]
    o = [ws[u][c:] + jnp.dot(attn[u], v_new[u], preferred_element_type=F32) for u in units]
    k_dec = [(k[u] * jnp.exp(g_last[u] - g_col[u])).astype(BF16) for u in units]
    ds = [lax.dot_general(k_dec[u], v_new[u], (((0,), (0,)), ((), ())), preferred_element_type=F32)
          for u in units]
    for u in units:
        state_ref[u] = s_old[u] * jnp.exp(g_last[u]) + ds[u]

    for b in range(nb):
        z = z_ref[b]
        for h in heads:
            u = b * DN_HEADS + h
            zh = z[:, hsl(0, h)]
            on = o[u] * lax.rsqrt(jnp.mean(o[u] * o[u], -1, keepdims=True) + RMS_EPS) * nw_ref[...]
            o_ref[b, :, hsl(0, h)] = (on * (zh * jax.nn.sigmoid(zh))).astype(o_ref.dtype)


def _deltanet(proj, conv_w, alog_row, dtb_row, norm_w, *, bsz, t, nb=2):
    c = DN_CHUNK
    proj3 = proj.reshape(bsz, t, PROJ_WIDTH)
    col = lambda j: pl.BlockSpec((nb, c, D_MODEL), lambda bp, n: (bp, n, j))
    yb = pl.pallas_call(
        functools.partial(_deltanet_body, nb=nb),
        grid=(bsz // nb, t // c),
        in_specs=[
            col(COL_Q), col(COL_K), col(COL_VV), col(COL_Z),
            pl.BlockSpec((nb, c, LANES), lambda bp, n: (bp, n, COL_AB)),
            pl.BlockSpec((DN_CONV, 3 * D_MODEL), lambda bp, n: (0, 0)),
            pl.BlockSpec((1, LANES), lambda bp, n: (0, 0)),
            pl.BlockSpec((1, LANES), lambda bp, n: (0, 0)),
            pl.BlockSpec((1, DN_DK), lambda bp, n: (0, 0)),
        ],
        out_specs=pl.BlockSpec((nb, c, D_MODEL), lambda bp, n: (bp, n, 0)),
        out_shape=jax.ShapeDtypeStruct((bsz, t, D_MODEL), BF16),
        scratch_shapes=[
            pltpu.VMEM((nb, CARRY_ROWS + c, 3 * D_MODEL), F32),
            pltpu.VMEM((nb * DN_HEADS, DN_DK, DN_DK), F32),
        ],
        compiler_params=_cparams("parallel", "arbitrary"),
        name="deltanet",
    )(proj3, proj3, proj3, proj3, proj3, conv_w, alog_row, dtb_row, norm_w)
    return yb.reshape(bsz * t, D_MODEL)


def _merge_body(x_ref, ya_ref, yb_ref, ga_ref, gb_ref, lng_ref, lnb_ref, wpa_ref, wpb_ref, wo_ref,
                l1g_ref, l1b_ref, wr_ref, br_ref, h1_ref, logit_ref):
    h = _layer_norm(x_ref[...], lng_ref[...], lnb_ref[...])
    pa = jnp.dot(ya_ref[...], wpa_ref[...], preferred_element_type=F32)
    pb = jnp.dot(yb_ref[...], wpb_ref[...], preferred_element_type=F32)
    merged = jax.nn.sigmoid(ga_ref[...]) * pa + jax.nn.sigmoid(gb_ref[...]) * pb
    mix = jnp.dot(merged.astype(BF16), wo_ref[...], preferred_element_type=F32)
    h1 = _layer_norm(DEEPNORM_ALPHA * h + mix, l1g_ref[...], l1b_ref[...])
    _store_row_tiles(h1_ref, h1)
    logit_ref[...] = jnp.dot(h1, wr_ref[...], preferred_element_type=F32,
                             precision=lax.Precision.HIGHEST) + br_ref[...]


def _merge(x2, ya, yb, proj, ln_g, ln_b, w_pa, w_pb, w_o, l1g, l1b, w_r, b_r, *, tm=256):
    n = x2.shape[0]
    vec = lambda: pl.BlockSpec((1, D_MODEL), lambda i: (0, 0))
    mat = lambda: pl.BlockSpec((D_MODEL, D_MODEL), lambda i: (0, 0))
    return pl.pallas_call(
        _merge_body,
        grid=(n // tm,),
        in_specs=[
            pl.BlockSpec((tm, D_MODEL), lambda i: (i, 0)),
            pl.BlockSpec((tm, D_MODEL), lambda i: (i, 0)),
            pl.BlockSpec((tm, D_MODEL), lambda i: (i, 0)),
            pl.BlockSpec((tm, D_MODEL), lambda i: (i, COL_GA)),
            pl.BlockSpec((tm, D_MODEL), lambda i: (i, COL_GB)),
            vec(), vec(), mat(), mat(), mat(), vec(), vec(),
            pl.BlockSpec((D_MODEL, LANES), lambda i: (0, 0)),
            pl.BlockSpec((1, LANES), lambda i: (0, 0)),
        ],
        out_specs=[
            pl.BlockSpec((tm * ROW_CHUNKS, LANES), lambda i: (i, 0)),
            pl.BlockSpec((tm, LANES), lambda i: (i, 0)),
        ],
        out_shape=[
            jax.ShapeDtypeStruct((n * ROW_CHUNKS, LANES), F32),
            jax.ShapeDtypeStruct((n, LANES), F32),
        ],
        compiler_params=_cparams("parallel"),
        name="merge",
    )(x2, ya, yb, proj, proj, ln_g, ln_b, w_pa, w_pb, w_o, l1g, l1b, w_r, b_r)


def _route_body(logit_ref, gate_ref, idx_ref, cnt_ref, carry_ref, *, tm):
    @pl.when(pl.program_id(0) == 0)
    def _():
        carry_ref[...] = jnp.zeros_like(carry_ref)

    lg = logit_ref[...]
    lane_i = lax.broadcasted_iota(jnp.int32, lg.shape, 1)
    lane = lane_i.astype(F32)
    neg = jnp.float32(-jnp.inf)
    big = jnp.float32(1 << 20)

    is_grp = lane_i < MOE_GROUPS
    gl = jnp.where(is_grp, lg, neg)
    gmax = jnp.max(gl, -1, keepdims=True)
    grp = jnp.min(jnp.where(is_grp & (gl == gmax), lane, big), -1, keepdims=True)
    p_grp = 1.0 / jnp.sum(jnp.where(is_grp, jnp.exp(gl - gmax), 0.0), -1, keepdims=True)

    elane = lane - MOE_GROUPS
    in_grp = (elane >= grp * MOE_EPG) & (elane < (grp + 1) * MOE_EPG)
    el = jnp.where(in_grp, lg, neg)
    m1 = jnp.max(el, -1, keepdims=True)
    e1 = jnp.min(jnp.where(in_grp & (el == m1), elane, big), -1, keepdims=True)
    rest = in_grp & (elane != e1)
    el2 = jnp.where(rest, lg, neg)
    m2 = jnp.max(el2, -1, keepdims=True)
    e2 = jnp.min(jnp.where(rest & (el2 == m2), elane, big), -1, keepdims=True)
    t2 = jnp.exp(m2 - m1)
    g1 = p_grp * (1.0 / (1.0 + t2))
    g2 = p_grp * (t2 / (1.0 + t2))

    oh1 = lane == e1
    oh2 = lane == e2
    onehot = jnp.where(oh1 | oh2, 1.0, 0.0)
    ri = lax.broadcasted_iota(jnp.int32, (tm, tm), 0)
    ci = lax.broadcasted_iota(jnp.int32, (tm, tm), 1)
    tri = jnp.where(ci < ri, 1.0, 0.0).astype(BF16)
    before = jnp.dot(tri, onehot.astype(BF16), preferred_element_type=F32) + carry_ref[0:1, :]
    r1 = jnp.sum(jnp.where(oh1, before, 0.0), -1, keepdims=True)
    r2 = jnp.sum(jnp.where(oh2, before, 0.0), -1, keepdims=True)
    total = carry_ref[0:1, :] + jnp.sum(onehot, 0, keepdims=True)
    carry_ref[...] = jnp.broadcast_to(total, carry_ref.shape)
    cnt_ref[...] = jnp.broadcast_to(total, cnt_ref.shape).astype(jnp.int32)

    gate_ref[...] = jnp.where(lane_i == 0, g1, jnp.where(lane_i == 1, g2, 0.0))
    idx_ref[...] = jnp.where(lane_i == 0, e1, jnp.where(lane_i == 1, e2,
                             jnp.where(lane_i == 2, r1, jnp.where(lane_i == 3, r2, 0.0)))).astype(jnp.int32)


def _route(logits, *, tm=256):
    n = logits.shape[0]
    return pl.pallas_call(
        functools.partial(_route_body, tm=tm),
        grid=(n // tm,),
        in_specs=[pl.BlockSpec((tm, LANES), lambda i: (i, 0))],
        out_specs=[
            pl.BlockSpec((tm, LANES), lambda i: (i, 0)),
            pl.BlockSpec((tm, LANES), lambda i: (i, 0)),
            pl.BlockSpec((8, LANES), lambda i: (0, 0)),
        ],
        out_shape=[
            jax.ShapeDtypeStruct((n, LANES), F32),
            jax.ShapeDtypeStruct((n, LANES), jnp.int32),
            jax.ShapeDtypeStruct((8, LANES), jnp.int32),
        ],
        scratch_shapes=[pltpu.VMEM((8, LANES), F32)],
        compiler_params=_cparams("arbitrary"),
        name="route",
    )(logits)


def _dest_body(idx_ref, ps_ref, o_ref):
    idx = idx_ref[...]
    lane = lax.broadcasted_iota(jnp.int32, idx.shape, 1)
    ps = ps_ref[...].astype(F32)
    d1 = jnp.sum(jnp.where(lane == idx[:, 0:1], ps, 0.0), -1, keepdims=True).astype(jnp.int32) + idx[:, 2:3]
    d2 = jnp.sum(jnp.where(lane == idx[:, 1:2], ps, 0.0), -1, keepdims=True).astype(jnp.int32) + idx[:, 3:4]
    o_ref[...] = jnp.where(lane == 0, d1, jnp.where(lane == 1, d2, 0))


def _dest(idx, pad_start_row, *, tm=1024):
    n = idx.shape[0]
    return pl.pallas_call(
        _dest_body,
        grid=(n // tm,),
        in_specs=[pl.BlockSpec((tm, LANES), lambda i: (i, 0)), pl.BlockSpec((1, LANES), lambda i: (0, 0))],
        out_specs=pl.BlockSpec((tm, LANES), lambda i: (i, 0)),
        out_shape=jax.ShapeDtypeStruct((n, LANES), jnp.int32),
        compiler_params=_cparams("parallel"),
        name="dest",
    )(idx, pad_start_row)


DMA_GROUP = 8


def _row_tile(ref, row):
    return ref.at[pl.ds(pl.multiple_of(row * SUBLANES, SUBLANES), SUBLANES)]


def _dispatch_body(dest_ref, h_ref, xin_ref, xpad_ref, sem, *, tm):
    del xin_ref

    def row_copy(src_row, dst_row):
        return pltpu.make_async_copy(_row_tile(h_ref, src_row), _row_tile(xpad_ref, dst_row), sem)

    def start(g, carry):
        r0 = g * DMA_GROUP
        dst = [dest_ref[0, 0, 2 * r0 + j] for j in range(2 * DMA_GROUP)]
        for j in range(2 * DMA_GROUP):
            row_copy(r0 + j // 2, dst[j]).start()
        return carry

    def wait(g, carry):
        for _ in range(2 * DMA_GROUP):
            row_copy(0, 0).wait()
        return carry

    lax.fori_loop(0, tm // DMA_GROUP, start, 0)
    lax.fori_loop(0, tm // DMA_GROUP, wait, 0)


def _dispatch(dest_smem, h1_rows, xpad0, *, tm):
    n = h1_rows.shape[0] // ROW_CHUNKS
    return pl.pallas_call(
        functools.partial(_dispatch_body, tm=tm),
        grid=(n // tm,),
        in_specs=[
            pl.BlockSpec((1, 1, 2 * tm), lambda i: (i, 0, 0), memory_space=pltpu.SMEM),
            pl.BlockSpec((tm * ROW_CHUNKS, LANES), lambda i: (i, 0)),
            pl.BlockSpec(memory_space=pl.ANY),
        ],
        out_specs=pl.BlockSpec(memory_space=pl.ANY),
        scratch_shapes=[pltpu.SemaphoreType.DMA(())],
        out_shape=jax.ShapeDtypeStruct(xpad0.shape, xpad0.dtype),
        input_output_aliases={2: 0},
        compiler_params=_cparams("arbitrary"),
        name="dispatch",
    )(dest_smem, h1_rows, xpad0)


def _experts_body(blk_e_ref, nused_ref, x_ref, w1_ref, w3_ref, w2_ref, y_ref, w1b, w3b, w2b):
    i = pl.program_id(0)
    prev = blk_e_ref[jnp.maximum(i - 1, 0)]

    @pl.when((i == 0) | (blk_e_ref[i] != prev))
    def _():
        w1b[...] = w1_ref[0].astype(BF16)
        w3b[...] = w3_ref[0].astype(BF16)
        w2b[...] = w2_ref[0].astype(BF16)

    @pl.when(i < nused_ref[0])
    def _():
        xb = _load_row_tiles(x_ref, EXPERT_ROWS).astype(BF16)
        a = jnp.dot(xb, w1b[...], preferred_element_type=F32)
        b = jnp.dot(xb, w3b[...], preferred_element_type=F32)
        hb = (a * jax.nn.sigmoid(a)) * b
        _store_row_tiles(y_ref, jnp.dot(hb.astype(BF16), w2b[...], preferred_element_type=F32))

    @pl.when(i >= nused_ref[0])
    def _():
        y_ref[...] = jnp.zeros_like(y_ref)


def _experts(blk_e, n_used, xpad, w1, w3, w2):
    tm = EXPERT_ROWS * ROW_CHUNKS
    return pl.pallas_call(
        _experts_body,
        grid_spec=pltpu.PrefetchScalarGridSpec(
            num_scalar_prefetch=2,
            grid=(xpad.shape[0] // tm,),
            in_specs=[
                pl.BlockSpec((tm, LANES), lambda i, be, nu: (i, 0)),
                pl.BlockSpec((1, D_MODEL, D_EXPERT), lambda i, be, nu: (be[i], 0, 0)),
                pl.BlockSpec((1, D_MODEL, D_EXPERT), lambda i, be, nu: (be[i], 0, 0)),
                pl.BlockSpec((1, D_EXPERT, D_MODEL), lambda i, be, nu: (be[i], 0, 0)),
            ],
            out_specs=pl.BlockSpec((tm, LANES), lambda i, be, nu: (i, 0)),
            scratch_shapes=[
                pltpu.VMEM((D_MODEL, D_EXPERT), BF16),
                pltpu.VMEM((D_MODEL, D_EXPERT), BF16),
                pltpu.VMEM((D_EXPERT, D_MODEL), BF16),
            ],
        ),
        out_shape=jax.ShapeDtypeStruct(xpad.shape, F32),
        compiler_params=_cparams("arbitrary"),
        name="experts",
    )(blk_e, n_used, xpad, w1, w3, w2)


def _combine_body(dest_ref, h1_ref, gate_ref, l2g_ref, l2b_ref, ypad_ref, o_ref, buf0, buf1, sem, *, tm):
    bufs = (buf0, buf1)

    def row_copy(src_row, r, slot):
        return pltpu.make_async_copy(_row_tile(ypad_ref, src_row), _row_tile(bufs[slot], r), sem)

    def start(g, carry):
        r0 = g * DMA_GROUP
        src = [dest_ref[0, 0, 2 * r0 + j] for j in range(2 * DMA_GROUP)]
        for j in range(2 * DMA_GROUP):
            row_copy(src[j], r0 + j // 2, j % 2).start()
        return carry

    def wait(g, carry):
        for j in range(2 * DMA_GROUP):
            row_copy(0, 0, j % 2).wait()
        return carry

    lax.fori_loop(0, tm // DMA_GROUP, start, 0)
    lax.fori_loop(0, tm // DMA_GROUP, wait, 0)
    gate = gate_ref[...]
    ffn = _load_row_tiles(buf0, tm) * gate[:, 0:1] + _load_row_tiles(buf1, tm) * gate[:, 1:2]
    h1 = _load_row_tiles(h1_ref, tm)
    o_ref[...] = _layer_norm(DEEPNORM_ALPHA * h1 + ffn, l2g_ref[...], l2b_ref[...])


def _combine(dest_smem, h1_rows, gates, l2g, l2b, ypad, *, tm):
    n = h1_rows.shape[0] // ROW_CHUNKS
    return pl.pallas_call(
        functools.partial(_combine_body, tm=tm),
        grid=(n // tm,),
        in_specs=[
            pl.BlockSpec((1, 1, 2 * tm), lambda i: (i, 0, 0), memory_space=pltpu.SMEM),
            pl.BlockSpec((tm * ROW_CHUNKS, LANES), lambda i: (i, 0)),
            pl.BlockSpec((tm, LANES), lambda i: (i, 0)),
            pl.BlockSpec((1, D_MODEL), lambda i: (0, 0)),
            pl.BlockSpec((1, D_MODEL), lambda i: (0, 0)),
            pl.BlockSpec(memory_space=pl.ANY),
        ],
        out_specs=pl.BlockSpec((tm, D_MODEL), lambda i: (i, 0)),
        scratch_shapes=[
            pltpu.VMEM((tm * ROW_CHUNKS, LANES), F32),
            pltpu.VMEM((tm * ROW_CHUNKS, LANES), F32),
            pltpu.SemaphoreType.DMA(()),
        ],
        out_shape=jax.ShapeDtypeStruct((n, D_MODEL), F32),
        compiler_params=_cparams("arbitrary"),
        name="combine",
    )(dest_smem, h1_rows, gates, l2g, l2b, ypad)


def _row(v, width=None):
    v = v.reshape(1, -1).astype(F32)
    if width is not None and v.shape[1] < width:
        v = jnp.pad(v, ((0, 0), (0, width - v.shape[1])))
    return v


def kernel(x, ln_in_g, ln_in_b, w_in, b_in, gm_ln_g, gm_ln_b, gm_w_s, gm_b_s, dn_conv_w, dn_a_log,
           dn_dt_bias, dn_norm_w, w_pa, w_pb, w_o, ln1_g, ln1_b, w_rg, b_rg, w_re, b_re, w1, w3, w2,
           ln2_g, ln2_b):
    bsz, t, d = x.shape
    n = bsz * t
    x2 = x.reshape(n, d)
    l = 0

    wi, bi = w_in[l], b_in[l]
    c_ab = 6 * D_MODEL
    pad = LANES - 2 * DN_HEADS
    w_cat = jnp.concatenate([wi[:, :c_ab], wi[:, c_ab + 2 * DN_HEADS:], wi[:, c_ab:c_ab + 2 * DN_HEADS],
                             jnp.zeros((d, pad), F32)], axis=1).astype(BF16)
    b_cat = jnp.concatenate([bi[:c_ab], bi[c_ab + 2 * DN_HEADS:], bi[c_ab:c_ab + 2 * DN_HEADS],
                             jnp.zeros((pad,), F32)]).reshape(1, -1)

    ln_g, ln_b = _row(ln_in_g), _row(ln_in_b)
    proj = _inproj(x2, ln_g, ln_b, w_cat, b_cat)

    bs_full = jnp.broadcast_to(gm_b_s[l][:, :, None], (GM_GROUPS, GM_BLOCK, LANES))
    ya = _gmlp(proj, _row(gm_ln_g[l]), _row(gm_ln_b[l]), gm_w_s[l], bs_full)

    yb = _deltanet(proj, dn_conv_w[l], _row(dn_a_log[l], LANES), _row(dn_dt_bias[l], LANES),
                   _row(dn_norm_w[l]), bsz=bsz, t=t)

    w_r = jnp.concatenate([w_rg[l], w_re[l], jnp.zeros((d, LANES - MOE_GROUPS - N_EXPERTS), F32)], axis=1)
    b_r = _row(jnp.concatenate([b_rg[l], b_re[l]]), LANES)
    h1, logits = _merge(x2, ya, yb, proj, ln_g, ln_b, w_pa[l].astype(BF16), w_pb[l].astype(BF16),
                        w_o[l].astype(BF16), _row(ln1_g[l]), _row(ln1_b[l]), w_r, b_r)

    gates, idx, counts = _route(logits)

    tm_e = EXPERT_ROWS
    cnt = counts[0, :N_EXPERTS]
    padded = (cnt + tm_e - 1) // tm_e * tm_e
    pad_end = jnp.cumsum(padded)
    pad_start = (pad_end - padded).astype(jnp.int32)
    n_blocks = (2 * n + N_EXPERTS * (tm_e - 1) + tm_e - 1) // tm_e
    blk_start = jnp.arange(n_blocks, dtype=jnp.int32) * tm_e
    blk_e = jnp.minimum(jnp.sum(pad_end[None, :] <= blk_start[:, None], -1), N_EXPERTS - 1).astype(jnp.int32)
    n_used = (pad_end[-1] // tm_e).astype(jnp.int32).reshape(1)

    ps_row = jnp.pad(pad_start, (0, LANES - N_EXPERTS)).reshape(1, LANES)
    dest = _dest(idx, ps_row, tm=min(1024, n))[:, :2]
    tm_d, tm_c = 512, 256
    xpad0 = jnp.zeros((n_blocks * tm_e * ROW_CHUNKS, LANES), F32)
    xpad = _dispatch(dest.reshape(n // tm_d, 1, 2 * tm_d), h1, xpad0, tm=tm_d)
    ypad = _experts(blk_e, n_used, xpad, w1[l], w3[l], w2[l])
    out = _combine(dest.reshape(n // tm_c, 1, 2 * tm_c), h1, gates, _row(ln2_g[l]), _row(ln2_b[l]), ypad,
                   tm=tm_c)
    return out.reshape(bsz, t, d)
```

```python
import functools
import math

import jax
import jax.numpy as jnp
from jax import lax
from jax.experimental import pallas as pl
from jax.experimental.pallas import tpu as pltpu

D_MODEL = 1024
GM_GROUPS = 8
GM_BLOCK = 128
GM_CHUNK = 64
DN_HEADS = 8
DN_DK = 128
DN_CONV = 4
MOE_GROUPS = 4
MOE_EPG = 8
N_EXPERTS = MOE_GROUPS * MOE_EPG
D_EXPERT = D_MODEL // 2
LN_EPS = 1e-5
RMS_EPS = 1e-6
L2_EPS = 1e-6
DEEPNORM_ALPHA = 2.0 ** 0.25

LANES = 128
COL_U, COL_V, COL_Q, COL_K, COL_VV, COL_Z, COL_GA, COL_GB = range(8)
N_PROJ_BLOCKS = 8
N_FRONT_BLOCKS = 6
PROJ_WIDTH = N_PROJ_BLOCKS * D_MODEL

DN_CHUNK = 64
CARRY_ROWS = 8
EXPERT_ROWS = 256
VMEM_LIMIT = 56 * 1024 * 1024
SUBLANES = 8
ROW_CHUNKS = D_MODEL // LANES
assert ROW_CHUNKS == SUBLANES

F32 = jnp.float32
BF16 = jnp.bfloat16


def _cparams(*sem):
    return pltpu.CompilerParams(dimension_semantics=sem, vmem_limit_bytes=VMEM_LIMIT)


def _store_row_tiles(ref, val):
    m = val.shape[0]
    for ch in range(ROW_CHUNKS):
        ref[pl.ds(ch, m, stride=ROW_CHUNKS), :] = val[:, ch * LANES:(ch + 1) * LANES]


def _load_row_tiles(ref, m):
    return jnp.concatenate([ref[pl.ds(ch, m, stride=ROW_CHUNKS), :] for ch in range(ROW_CHUNKS)], axis=1)


def _layer_norm(x, g, b):
    mu = jnp.mean(x, -1, keepdims=True)
    xc = x - mu
    var = jnp.mean(xc * xc, -1, keepdims=True)
    return xc * lax.rsqrt(var + LN_EPS) * g + b


def _dot(a, b):
    return jnp.dot(a.astype(BF16), b.astype(BF16), preferred_element_type=F32)


def _dot_nt(a, b):
    return lax.dot_general(a.astype(BF16), b.astype(BF16), (((1,), (1,)), ((), ())),
                           preferred_element_type=F32)


def _ln_ab_body(x_ref, g_ref, b_ref, wab_ref, bab_ref, h_ref, ab_ref):
    h = _layer_norm(x_ref[...], g_ref[...], b_ref[...]).astype(BF16)
    h_ref[...] = h
    ab_ref[...] = jnp.dot(h, wab_ref[...], preferred_element_type=F32) + bab_ref[...]


def _ln_ab(x2, ln_g, ln_b, w_ab, b_ab, *, tm=512):
    n = x2.shape[0]
    return pl.pallas_call(
        _ln_ab_body,
        grid=(n // tm,),
        in_specs=[
            pl.BlockSpec((tm, D_MODEL), lambda i: (i, 0)),
            pl.BlockSpec((1, D_MODEL), lambda i: (0, 0)),
            pl.BlockSpec((1, D_MODEL), lambda i: (0, 0)),
            pl.BlockSpec((D_MODEL, LANES), lambda i: (0, 0)),
            pl.BlockSpec((1, LANES), lambda i: (0, 0)),
        ],
        out_specs=[pl.BlockSpec((tm, D_MODEL), lambda i: (i, 0)), pl.BlockSpec((tm, LANES), lambda i: (i, 0))],
        out_shape=[jax.ShapeDtypeStruct((n, D_MODEL), BF16), jax.ShapeDtypeStruct((n, LANES), F32)],
        compiler_params=_cparams("parallel"),
        name="ln_ab",
    )(x2, ln_g, ln_b, w_ab, b_ab)


def _inproj_body(h_ref, wa_ref, wb_ref, bias_ref, o_ref, w_scr):
    j = pl.program_id(0)

    @pl.when((pl.program_id(1) == 0) & (j < N_FRONT_BLOCKS))
    def _():
        w_scr[...] = wa_ref[...].astype(BF16)

    @pl.when((pl.program_id(1) == 0) & (j >= N_FRONT_BLOCKS))
    def _():
        w_scr[...] = wb_ref[...].astype(BF16)

    o_ref[...] = (jnp.dot(h_ref[...], w_scr[...], preferred_element_type=F32) + bias_ref[...]).astype(o_ref.dtype)


def _inproj(h, w_in, w_gates, bias, *, tm=512):
    n = h.shape[0]
    return pl.pallas_call(
        _inproj_body,
        grid=(N_PROJ_BLOCKS, n // tm),
        in_specs=[
            pl.BlockSpec((tm, D_MODEL), lambda j, i: (i, 0)),
            pl.BlockSpec((D_MODEL, D_MODEL), lambda j, i: (0, jnp.minimum(j, N_FRONT_BLOCKS - 1))),
            pl.BlockSpec((D_MODEL, D_MODEL), lambda j, i: (0, jnp.maximum(j - N_FRONT_BLOCKS, 0))),
            pl.BlockSpec((1, D_MODEL), lambda j, i: (0, j)),
        ],
        out_specs=pl.BlockSpec((tm, D_MODEL), lambda j, i: (i, j)),
        out_shape=jax.ShapeDtypeStruct((n, N_PROJ_BLOCKS * D_MODEL), BF16),
        scratch_shapes=[pltpu.VMEM((D_MODEL, D_MODEL), BF16)],
        compiler_params=_cparams("arbitrary", "arbitrary"),
        name="inproj",
    )(h, w_in, w_gates, bias)


def _gelu(x):
    return 0.5 * x * (1.0 + lax.erf(x * (1.0 / math.sqrt(2.0))))


def _gmlp_body(u_ref, v_ref, lng_ref, lnb_ref, ws_ref, bs_ref, o_ref, *, nblk):
    u = _gelu(u_ref[...].astype(F32))
    v = _layer_norm(_gelu(v_ref[...].astype(F32)), lng_ref[...], lnb_ref[...]).astype(BF16)
    row_chunk = lax.broadcasted_iota(jnp.int32, (GM_BLOCK, GM_BLOCK), 0) // GM_CHUNK
    col_chunk = lax.broadcasted_iota(jnp.int32, (GM_BLOCK, GM_BLOCK), 1) // GM_CHUNK
    causal = col_chunk <= row_chunk
    for g in range(GM_GROUPS):
        cols = slice(g * LANES, (g + 1) * LANES)
        w = jnp.where(causal, ws_ref[g], 0.0).astype(BF16)
        for blk in range(nblk):
            rows = slice(blk * GM_BLOCK, (blk + 1) * GM_BLOCK)
            s = jnp.dot(w, v[rows, cols], preferred_element_type=F32) + bs_ref[g]
            o_ref[rows, cols] = (u[rows, cols] * s).astype(o_ref.dtype)


def _gmlp(proj, ln_g, ln_b, w_s, b_s_full, *, nblk=2):
    n = proj.shape[0]
    rows = nblk * GM_BLOCK
    return pl.pallas_call(
        functools.partial(_gmlp_body, nblk=nblk),
        grid=(n // rows,),
        in_specs=[
            pl.BlockSpec((rows, D_MODEL), lambda i: (i, COL_U)),
            pl.BlockSpec((rows, D_MODEL), lambda i: (i, COL_V)),
            pl.BlockSpec((1, D_MODEL), lambda i: (0, 0)),
            pl.BlockSpec((1, D_MODEL), lambda i: (0, 0)),
            pl.BlockSpec((GM_GROUPS, GM_BLOCK, GM_BLOCK), lambda i: (0, 0, 0)),
            pl.BlockSpec((GM_GROUPS, GM_BLOCK, LANES), lambda i: (0, 0, 0)),
        ],
        out_specs=pl.BlockSpec((rows, D_MODEL), lambda i: (i, 0)),
        out_shape=jax.ShapeDtypeStruct((n, D_MODEL), BF16),
        compiler_params=_cparams("parallel"),
        name="gmlp",
    )(proj, proj, ln_g, ln_b, w_s, b_s_full)


def _cumsum_rows(x):
    n = x.shape[0]
    row = lax.broadcasted_iota(jnp.int32, x.shape, 0)
    shift = 1
    while shift < n:
        x = x + jnp.where(row >= shift, pltpu.roll(x, shift, 0), 0.0)
        shift *= 2
    return x


def _deltanet_body(q_ref, k_ref, v_ref, z_ref, ab_ref, cw_ref, alog_ref, dtb_ref, nw_ref,
                   o_ref, xs_ref, state_ref, *, nb):
    c = DN_CHUNK
    w3 = 3 * D_MODEL

    @pl.when(pl.program_id(1) == 0)
    def _():
        xs_ref[:, 0:CARRY_ROWS, :] = jnp.zeros((nb, CARRY_ROWS, w3), F32)
        state_ref[...] = jnp.zeros_like(state_ref)

    ri = lax.broadcasted_iota(jnp.int32, (c, c), 0)
    ci = lax.broadcasted_iota(jnp.int32, (c, c), 1)
    causal = ri >= ci
    strict = ri > ci
    heads = range(DN_HEADS)
    hsl = lambda base, h: slice(base + h * DN_DK, base + (h + 1) * DN_DK)

    q, k, v, g_col, beta, g_last, decay = [], [], [], [], [], [], []
    for b in range(nb):
        xs_ref[b, CARRY_ROWS:, 0:D_MODEL] = q_ref[b].astype(F32)
        xs_ref[b, CARRY_ROWS:, D_MODEL:2 * D_MODEL] = k_ref[b].astype(F32)
        xs_ref[b, CARRY_ROWS:, 2 * D_MODEL:] = v_ref[b].astype(F32)
        acc = cw_ref[DN_CONV - 1:DN_CONV, :] * xs_ref[b, CARRY_ROWS:, :]
        for j in range(DN_CONV - 1):
            off = CARRY_ROWS - (DN_CONV - 1) + j
            acc = acc + cw_ref[j:j + 1, :] * xs_ref[b, off:off + c, :]
        xs_ref[b, 0:CARRY_ROWS, :] = xs_ref[b, c:c + CARRY_ROWS, :]
        qkv = acc * jax.nn.sigmoid(acc)

        ab = ab_ref[b]
        xg = ab + dtb_ref[...]
        softplus = jnp.maximum(xg, 0.0) + jnp.log1p(jnp.exp(-jnp.abs(xg)))
        gcum = _cumsum_rows(-jnp.exp(alog_ref[...]) * softplus)
        beta_all = jax.nn.sigmoid(ab)
        gcum_sq = gcum if c == LANES else jnp.concatenate([gcum, jnp.zeros((LANES - c, LANES), F32)], axis=0)
        gcum_t = gcum_sq.T

        for h in heads:
            qh, kh = qkv[:, hsl(0, h)], qkv[:, hsl(D_MODEL, h)]
            q.append(qh * (lax.rsqrt(jnp.sum(qh * qh, -1, keepdims=True) + L2_EPS) * (DN_DK ** -0.5)))
            k.append(kh * lax.rsqrt(jnp.sum(kh * kh, -1, keepdims=True) + L2_EPS))
            v.append(qkv[:, hsl(2 * D_MODEL, h)])
            g_col.append(gcum[:, h:h + 1])
            beta.append(beta_all[:, DN_HEADS + h:DN_HEADS + h + 1])
            g_last.append(gcum[c - 1:c, h:h + 1])
            decay.append(jnp.where(causal, jnp.exp(jnp.where(causal, g_col[-1] - gcum_t[h:h + 1, 0:c], 0.0)), 0.0))

    units = range(nb * DN_HEADS)
    eg = [jnp.exp(g) for g in g_col]
    kb = [x.astype(BF16) for x in k]
    kq = [lax.dot_general(jnp.concatenate([kb[u], q[u].astype(BF16)], axis=0), kb[u],
                          (((1,), (1,)), ((), ())), preferred_element_type=F32) for u in units]
    p = [jnp.where(strict, beta[u] * kq[u][0:c] * decay[u], 0.0).astype(BF16) for u in units]
    attn = [(kq[u][c:] * decay[u]).astype(BF16) for u in units]

    r = [jnp.concatenate([v[u] * beta[u], k[u] * (beta[u] * eg[u])], axis=1) for u in units]
    r = [r[u] - jnp.dot(p[u], r[u].astype(BF16), preferred_element_type=F32) for u in units]
    span = 2
    while span < c:
        p = [jnp.dot(p[u], p[u], preferred_element_type=F32).astype(BF16) for u in units]
        r = [r[u] + jnp.dot(p[u], r[u].astype(BF16), preferred_element_type=F32) for u in units]
        span *= 2

    s_old = [state_ref[u] for u in units]
    ws = [jnp.dot(jnp.concatenate([r[u][:, DN_DK:], q[u] * eg[u]], axis=0).astype(BF16),
                  s_old[u].astype(BF16), preferred_element_type=F32) for u in units]
    v_new = [(r[u][:, 0:DN_DK] - ws[u][0:c]).astype(BF16) for u in units]
    o = [ws[u][c:] + jnp.dot(attn[u], v_new[u], preferred_element_type=F32) for u in units]
    k_dec = [(k[u] * jnp.exp(g_last[u] - g_col[u])).astype(BF16) for u in units]
    ds = [lax.dot_general(k_dec[u], v_new[u], (((0,), (0,)), ((), ())), preferred_element_type=F32)
          for u in units]
    for u in units:
        state_ref[u] = s_old[u] * jnp.exp(g_last[u]) + ds[u]

    for b in range(nb):
        z = z_ref[b].astype(F32)
        for h in heads:
            u = b * DN_HEADS + h
            zh = z[:, hsl(0, h)]
            on = o[u] * lax.rsqrt(jnp.mean(o[u] * o[u], -1, keepdims=True) + RMS_EPS) * nw_ref[...]
            o_ref[b, :, hsl(0, h)] = (on * (zh * jax.nn.sigmoid(zh))).astype(o_ref.dtype)


def _deltanet(proj, ab, conv_w, alog_row, dtb_row, norm_w, *, bsz, t, nb=2):
    c = DN_CHUNK
    proj3 = proj.reshape(bsz, t, PROJ_WIDTH)
    ab3 = ab.reshape(bsz, t, LANES)
    col = lambda j: pl.BlockSpec((nb, c, D_MODEL), lambda bp, n: (bp, n, j))
    yb = pl.pallas_call(
        functools.partial(_deltanet_body, nb=nb),
        grid=(bsz // nb, t // c),
        in_specs=[
            col(COL_Q), col(COL_K), col(COL_VV), col(COL_Z),
            pl.BlockSpec((nb, c, LANES), lambda bp, n: (bp, n, 0)),
            pl.BlockSpec((DN_CONV, 3 * D_MODEL), lambda bp, n: (0, 0)),
            pl.BlockSpec((1, LANES), lambda bp, n: (0, 0)),
            pl.BlockSpec((1, LANES), lambda bp, n: (0, 0)),
            pl.BlockSpec((1, DN_DK), lambda bp, n: (0, 0)),
        ],
        out_specs=pl.BlockSpec((nb, c, D_MODEL), lambda bp, n: (bp, n, 0)),
        out_shape=jax.ShapeDtypeStruct((bsz, t, D_MODEL), BF16),
        scratch_shapes=[
            pltpu.VMEM((nb, CARRY_ROWS + c, 3 * D_MODEL), F32),
            pltpu.VMEM((nb * DN_HEADS, DN_DK, DN_DK), F32),
        ],
        compiler_params=_cparams("parallel", "arbitrary"),
        name="deltanet",
    )(proj3, proj3, proj3, proj3, ab3, conv_w, alog_row, dtb_row, norm_w)
    return yb.reshape(bsz * t, D_MODEL)


def _merge_body(x_ref, ya_ref, yb_ref, ga_ref, gb_ref, lng_ref, lnb_ref, wpa_ref, wpb_ref, wo_ref,
                l1g_ref, l1b_ref, wr_ref, wrlo_ref, br_ref, h1_ref, logit_ref):
    h = _layer_norm(x_ref[...], lng_ref[...], lnb_ref[...])
    pa = jnp.dot(ya_ref[...], wpa_ref[...], preferred_element_type=F32)
    pb = jnp.dot(yb_ref[...], wpb_ref[...], preferred_element_type=F32)
    merged = jax.nn.sigmoid(ga_ref[...].astype(F32)) * pa + jax.nn.sigmoid(gb_ref[...].astype(F32)) * pb
    mix = jnp.dot(merged.astype(BF16), wo_ref[...], preferred_element_type=F32)
    h1 = _layer_norm(DEEPNORM_ALPHA * h + mix, l1g_ref[...], l1b_ref[...])
    _store_row_tiles(h1_ref, h1)
    tm = h1.shape[0]
    hi = h1.astype(BF16)
    lo = (h1 - hi.astype(F32)).astype(BF16)
    hw = jnp.dot(jnp.concatenate([hi, lo], axis=0), wr_ref[...], preferred_element_type=F32)
    logit_ref[...] = (hw[0:tm] + hw[tm:] + jnp.dot(hi, wrlo_ref[...], preferred_element_type=F32)
                      + br_ref[...])


def _merge(x2, ya, yb, proj, ln_g, ln_b, w_pa, w_pb, w_o, l1g, l1b, w_r, w_r_lo, b_r, *, tm=256):
    n = x2.shape[0]
    vec = lambda: pl.BlockSpec((1, D_MODEL), lambda i: (0, 0))
    mat = lambda: pl.BlockSpec((D_MODEL, D_MODEL), lambda i: (0, 0))
    return pl.pallas_call(
        _merge_body,
        grid=(n // tm,),
        in_specs=[
            pl.BlockSpec((tm, D_MODEL), lambda i: (i, 0)),
            pl.BlockSpec((tm, D_MODEL), lambda i: (i, 0)),
            pl.BlockSpec((tm, D_MODEL), lambda i: (i, 0)),
            pl.BlockSpec((tm, D_MODEL), lambda i: (i, COL_GA)),
            pl.BlockSpec((tm, D_MODEL), lambda i: (i, COL_GB)),
            vec(), vec(), mat(), mat(), mat(), vec(), vec(),
            pl.BlockSpec((D_MODEL, LANES), lambda i: (0, 0)),
            pl.BlockSpec((D_MODEL, LANES), lambda i: (0, 0)),
            pl.BlockSpec((1, LANES), lambda i: (0, 0)),
        ],
        out_specs=[
            pl.BlockSpec((tm * ROW_CHUNKS, LANES), lambda i: (i, 0)),
            pl.BlockSpec((tm, LANES), lambda i: (i, 0)),
        ],
        out_shape=[
            jax.ShapeDtypeStruct((n * ROW_CHUNKS, LANES), F32),
            jax.ShapeDtypeStruct((n, LANES), F32),
        ],
        compiler_params=_cparams("parallel"),
        name="merge",
    )(x2, ya, yb, proj, proj, ln_g, ln_b, w_pa, w_pb, w_o, l1g, l1b, w_r, w_r_lo, b_r)


def _route_body(logit_ref, gate_ref, idx_ref, cnt_ref, carry_ref, *, tm):
    @pl.when(pl.program_id(0) == 0)
    def _():
        carry_ref[...] = jnp.zeros_like(carry_ref)

    lg = logit_ref[...]
    lane_i = lax.broadcasted_iota(jnp.int32, lg.shape, 1)
    lane = lane_i.astype(F32)
    neg = jnp.float32(-jnp.inf)
    big = jnp.float32(1 << 20)

    is_grp = lane_i < MOE_GROUPS
    gl = jnp.where(is_grp, lg, neg)
    gmax = jnp.max(gl, -1, keepdims=True)
    grp = jnp.min(jnp.where(is_grp & (gl == gmax), lane, big), -1, keepdims=True)
    p_grp = 1.0 / jnp.sum(jnp.where(is_grp, jnp.exp(gl - gmax), 0.0), -1, keepdims=True)

    elane = lane - MOE_GROUPS
    in_grp = (elane >= grp * MOE_EPG) & (elane < (grp + 1) * MOE_EPG)
    el = jnp.where(in_grp, lg, neg)
    m1 = jnp.max(el, -1, keepdims=True)
    e1 = jnp.min(jnp.where(in_grp & (el == m1), elane, big), -1, keepdims=True)
    rest = in_grp & (elane != e1)
    el2 = jnp.where(rest, lg, neg)
    m2 = jnp.max(el2, -1, keepdims=True)
    e2 = jnp.min(jnp.where(rest & (el2 == m2), elane, big), -1, keepdims=True)
    t2 = jnp.exp(m2 - m1)
    g1 = p_grp * (1.0 / (1.0 + t2))
    g2 = p_grp * (t2 / (1.0 + t2))

    oh1 = lane == e1
    oh2 = lane == e2
    onehot = jnp.where(oh1 | oh2, 1.0, 0.0)
    ri = lax.broadcasted_iota(jnp.int32, (tm, tm), 0)
    ci = lax.broadcasted_iota(jnp.int32, (tm, tm), 1)
    tri = jnp.where(ci < ri, 1.0, 0.0).astype(BF16)
    before = jnp.dot(tri, onehot.astype(BF16), preferred_element_type=F32) + carry_ref[0:1, :]
    r1 = jnp.sum(jnp.where(oh1, before, 0.0), -1, keepdims=True)
    r2 = jnp.sum(jnp.where(oh2, before, 0.0), -1, keepdims=True)
    total = carry_ref[0:1, :] + jnp.sum(onehot, 0, keepdims=True)
    carry_ref[...] = jnp.broadcast_to(total, carry_ref.shape)
    cnt_ref[...] = jnp.broadcast_to(total, cnt_ref.shape).astype(jnp.int32)

    gate_ref[...] = jnp.where(lane_i == 0, g1, jnp.where(lane_i == 1, g2, 0.0))
    idx_ref[...] = jnp.where(lane_i == 0, e1, jnp.where(lane_i == 1, e2,
                             jnp.where(lane_i == 2, r1, jnp.where(lane_i == 3, r2, 0.0)))).astype(jnp.int32)


def _route(logits, *, tm=256):
    n = logits.shape[0]
    return pl.pallas_call(
        functools.partial(_route_body, tm=tm),
        grid=(n // tm,),
        in_specs=[pl.BlockSpec((tm, LANES), lambda i: (i, 0))],
        out_specs=[
            pl.BlockSpec((tm, LANES), lambda i: (i, 0)),
            pl.BlockSpec((tm, LANES), lambda i: (i, 0)),
            pl.BlockSpec((8, LANES), lambda i: (0, 0)),
        ],
        out_shape=[
            jax.ShapeDtypeStruct((n, LANES), F32),
            jax.ShapeDtypeStruct((n, LANES), jnp.int32),
            jax.ShapeDtypeStruct((8, LANES), jnp.int32),
        ],
        scratch_shapes=[pltpu.VMEM((8, LANES), F32)],
        compiler_params=_cparams("arbitrary"),
        name="route",
    )(logits)


def _dest_body(idx_ref, ps_ref, o_ref):
    idx = idx_ref[...]
    lane = lax.broadcasted_iota(jnp.int32, idx.shape, 1)
    ps = ps_ref[...].astype(F32)
    d1 = jnp.sum(jnp.where(lane == idx[:, 0:1], ps, 0.0), -1, keepdims=True).astype(jnp.int32) + idx[:, 2:3]
    d2 = jnp.sum(jnp.where(lane == idx[:, 1:2], ps, 0.0), -1, keepdims=True).astype(jnp.int32) + idx[:, 3:4]
    o_ref[...] = jnp.where(lane == 0, d1, jnp.where(lane == 1, d2, 0))


def _dest(idx, pad_start_row, *, tm=1024):
    n = idx.shape[0]
    return pl.pallas_call(
        _dest_body,
        grid=(n // tm,),
        in_specs=[pl.BlockSpec((tm, LANES), lambda i: (i, 0)), pl.BlockSpec((1, LANES), lambda i: (0, 0))],
        out_specs=pl.BlockSpec((tm, LANES), lambda i: (i, 0)),
        out_shape=jax.ShapeDtypeStruct((n, LANES), jnp.int32),
        compiler_params=_cparams("parallel"),
        name="dest",
    )(idx, pad_start_row)


DMA_GROUP = 8


def _row_tile(ref, row):
    return ref.at[pl.ds(pl.multiple_of(row * SUBLANES, SUBLANES), SUBLANES)]


def _dispatch_body(dest_ref, h_ref, xin_ref, xpad_ref, sem, *, tm):
    del xin_ref

    def row_copy(src_row, dst_row):
        return pltpu.make_async_copy(_row_tile(h_ref, src_row), _row_tile(xpad_ref, dst_row), sem)

    def start(g, carry):
        r0 = g * DMA_GROUP
        dst = [dest_ref[0, 0, 2 * r0 + j] for j in range(2 * DMA_GROUP)]
        for j in range(2 * DMA_GROUP):
            row_copy(r0 + j // 2, dst[j]).start()
        return carry

    def wait(g, carry):
        for _ in range(2 * DMA_GROUP):
            row_copy(0, 0).wait()
        return carry

    lax.fori_loop(0, tm // DMA_GROUP, start, 0)
    lax.fori_loop(0, tm // DMA_GROUP, wait, 0)


def _dispatch(dest_smem, h1_rows, xpad0, *, tm):
    n = h1_rows.shape[0] // ROW_CHUNKS
    return pl.pallas_call(
        functools.partial(_dispatch_body, tm=tm),
        grid=(n // tm,),
        in_specs=[
            pl.BlockSpec((1, 1, 2 * tm), lambda i: (i, 0, 0), memory_space=pltpu.SMEM),
            pl.BlockSpec((tm * ROW_CHUNKS, LANES), lambda i: (i, 0)),
            pl.BlockSpec(memory_space=pl.ANY),
        ],
        out_specs=pl.BlockSpec(memory_space=pl.ANY),
        scratch_shapes=[pltpu.SemaphoreType.DMA(())],
        out_shape=jax.ShapeDtypeStruct(xpad0.shape, xpad0.dtype),
        input_output_aliases={2: 0},
        compiler_params=_cparams("arbitrary"),
        name="dispatch",
    )(dest_smem, h1_rows, xpad0)


def _experts_body(blk_e_ref, nused_ref, x_ref, w1_ref, w3_ref, w2_ref, y_ref, w1b, w3b, w2b):
    i = pl.program_id(0)
    prev = blk_e_ref[jnp.maximum(i - 1, 0)]

    @pl.when((i == 0) | (blk_e_ref[i] != prev))
    def _():
        w1b[...] = w1_ref[0].astype(BF16)
        w3b[...] = w3_ref[0].astype(BF16)
        w2b[...] = w2_ref[0].astype(BF16)

    @pl.when(i < nused_ref[0])
    def _():
        xb = _load_row_tiles(x_ref, EXPERT_ROWS).astype(BF16)
        a = jnp.dot(xb, w1b[...], preferred_element_type=F32)
        b = jnp.dot(xb, w3b[...], preferred_element_type=F32)
        hb = (a * jax.nn.sigmoid(a)) * b
        _store_row_tiles(y_ref, jnp.dot(hb.astype(BF16), w2b[...], preferred_element_type=F32))

    @pl.when(i >= nused_ref[0])
    def _():
        y_ref[...] = jnp.zeros_like(y_ref)


def _experts(blk_e, n_used, xpad, w1, w3, w2):
    tm = EXPERT_ROWS * ROW_CHUNKS
    return pl.pallas_call(
        _experts_body,
        grid_spec=pltpu.PrefetchScalarGridSpec(
            num_scalar_prefetch=2,
            grid=(xpad.shape[0] // tm,),
            in_specs=[
                pl.BlockSpec((tm, LANES), lambda i, be, nu: (i, 0)),
                pl.BlockSpec((1, D_MODEL, D_EXPERT), lambda i, be, nu: (be[i], 0, 0)),
                pl.BlockSpec((1, D_MODEL, D_EXPERT), lambda i, be, nu: (be[i], 0, 0)),
                pl.BlockSpec((1, D_EXPERT, D_MODEL), lambda i, be, nu: (be[i], 0, 0)),
            ],
            out_specs=pl.BlockSpec((tm, LANES), lambda i, be, nu: (i, 0)),
            scratch_shapes=[
                pltpu.VMEM((D_MODEL, D_EXPERT), BF16),
                pltpu.VMEM((D_MODEL, D_EXPERT), BF16),
                pltpu.VMEM((D_EXPERT, D_MODEL), BF16),
            ],
        ),
        out_shape=jax.ShapeDtypeStruct(xpad.shape, F32),
        compiler_params=_cparams("arbitrary"),
        name="experts",
    )(blk_e, n_used, xpad, w1, w3, w2)


def _combine_body(dest_ref, h1_ref, gate_ref, l2g_ref, l2b_ref, ypad_ref, o_ref, buf0, buf1, sem, *, tm):
    bufs = (buf0, buf1)

    def row_copy(src_row, r, slot):
        return pltpu.make_async_copy(_row_tile(ypad_ref, src_row), _row_tile(bufs[slot], r), sem)

    def start(g, carry):
        r0 = g * DMA_GROUP
        src = [dest_ref[0, 0, 2 * r0 + j] for j in range(2 * DMA_GROUP)]
        for j in range(2 * DMA_GROUP):
            row_copy(src[j], r0 + j // 2, j % 2).start()
        return carry

    def wait(g, carry):
        for j in range(2 * DMA_GROUP):
            row_copy(0, 0, j % 2).wait()
        return carry

    lax.fori_loop(0, tm // DMA_GROUP, start, 0)
    lax.fori_loop(0, tm // DMA_GROUP, wait, 0)
    gate = gate_ref[...]
    ffn = _load_row_tiles(buf0, tm) * gate[:, 0:1] + _load_row_tiles(buf1, tm) * gate[:, 1:2]
    h1 = _load_row_tiles(h1_ref, tm)
    o_ref[...] = _layer_norm(DEEPNORM_ALPHA * h1 + ffn, l2g_ref[...], l2b_ref[...])


def _combine(dest_smem, h1_rows, gates, l2g, l2b, ypad, *, tm):
    n = h1_rows.shape[0] // ROW_CHUNKS
    return pl.pallas_call(
        functools.partial(_combine_body, tm=tm),
        grid=(n // tm,),
        in_specs=[
            pl.BlockSpec((1, 1, 2 * tm), lambda i: (i, 0, 0), memory_space=pltpu.SMEM),
            pl.BlockSpec((tm * ROW_CHUNKS, LANES), lambda i: (i, 0)),
            pl.BlockSpec((tm, LANES), lambda i: (i, 0)),
            pl.BlockSpec((1, D_MODEL), lambda i: (0, 0)),
            pl.BlockSpec((1, D_MODEL), lambda i: (0, 0)),
            pl.BlockSpec(memory_space=pl.ANY),
        ],
        out_specs=pl.BlockSpec((tm, D_MODEL), lambda i: (i, 0)),
        scratch_shapes=[
            pltpu.VMEM((tm * ROW_CHUNKS, LANES), F32),
            pltpu.VMEM((tm * ROW_CHUNKS, LANES), F32),
            pltpu.SemaphoreType.DMA(()),
        ],
        out_shape=jax.ShapeDtypeStruct((n, D_MODEL), F32),
        compiler_params=_cparams("arbitrary"),
        name="combine",
    )(dest_smem, h1_rows, gates, l2g, l2b, ypad)


def _row(v, width=None):
    v = v.reshape(1, -1).astype(F32)
    if width is not None and v.shape[1] < width:
        v = jnp.pad(v, ((0, 0), (0, width - v.shape[1])))
    return v


def kernel(x, ln_in_g, ln_in_b, w_in, b_in, gm_ln_g, gm_ln_b, gm_w_s, gm_b_s, dn_conv_w, dn_a_log,
           dn_dt_bias, dn_norm_w, w_pa, w_pb, w_o, ln1_g, ln1_b, w_rg, b_rg, w_re, b_re, w1, w3, w2,
           ln2_g, ln2_b):
    bsz, t, d = x.shape
    n = bsz * t
    x2 = x.reshape(n, d)
    l = 0

    wi, bi = w_in[l], b_in[l]
    c_ab = N_FRONT_BLOCKS * D_MODEL
    c_gate = c_ab + 2 * DN_HEADS
    w_ab = jnp.pad(wi[:, c_ab:c_gate], ((0, 0), (0, LANES - 2 * DN_HEADS))).astype(BF16)
    b_ab = _row(bi[c_ab:c_gate], LANES)
    w_gates = wi[:, c_gate:]
    b_proj = jnp.concatenate([bi[:c_ab], bi[c_gate:]]).reshape(1, PROJ_WIDTH)

    ln_g, ln_b = _row(ln_in_g), _row(ln_in_b)
    h, ab = _ln_ab(x2, ln_g, ln_b, w_ab, b_ab)
    proj = _inproj(h, wi, w_gates, b_proj)

    bs_full = jnp.broadcast_to(gm_b_s[l][:, :, None], (GM_GROUPS, GM_BLOCK, LANES))
    ya = _gmlp(proj, _row(gm_ln_g[l]), _row(gm_ln_b[l]), gm_w_s[l], bs_full)

    yb = _deltanet(proj, ab, dn_conv_w[l], _row(dn_a_log[l], LANES), _row(dn_dt_bias[l], LANES),
                   _row(dn_norm_w[l]), bsz=bsz, t=t)

    w_r = jnp.concatenate([w_rg[l], w_re[l], jnp.zeros((d, LANES - MOE_GROUPS - N_EXPERTS), F32)], axis=1)
    w_r_hi = w_r.astype(BF16)
    w_r_lo = (w_r - w_r_hi.astype(F32)).astype(BF16)
    b_r = _row(jnp.concatenate([b_rg[l], b_re[l]]), LANES)
    h1, logits = _merge(x2, ya, yb, proj, ln_g, ln_b, w_pa[l].astype(BF16), w_pb[l].astype(BF16),
                        w_o[l].astype(BF16), _row(ln1_g[l]), _row(ln1_b[l]), w_r_hi, w_r_lo, b_r)

    gates, idx, counts = _route(logits)

    tm_e = EXPERT_ROWS
    cnt = counts[0, :N_EXPERTS]
    padded = (cnt + tm_e - 1) // tm_e * tm_e
    pad_end = jnp.cumsum(padded)
    pad_start = (pad_end - padded).astype(jnp.int32)
    n_blocks = (2 * n + N_EXPERTS * (tm_e - 1) + tm_e - 1) // tm_e
    blk_start = jnp.arange(n_blocks, dtype=jnp.int32) * tm_e
    blk_e = jnp.minimum(jnp.sum(pad_end[None, :] <= blk_start[:, None], -1), N_EXPERTS - 1).astype(jnp.int32)
    n_used = (pad_end[-1] // tm_e).astype(jnp.int32).reshape(1)

    ps_row = jnp.pad(pad_start, (0, LANES - N_EXPERTS)).reshape(1, LANES)
    dest = _dest(idx, ps_row, tm=min(1024, n))[:, :2]
    tm_d, tm_c = 512, 256
    xpad0 = jnp.zeros((n_blocks * tm_e * ROW_CHUNKS, LANES), F32)
    xpad = _dispatch(dest.reshape(n // tm_d, 1, 2 * tm_d), h1, xpad0, tm=tm_d)
    ypad = _experts(blk_e, n_used, xpad, w1[l], w3[l], w2[l])
    out = _combine(dest.reshape(n // tm_c, 1, 2 * tm_c), h1, gates, _row(ln2_g[l]), _row(ln2_b[l]), ypad,
                   tm=tm_c)
    return out.reshape(bsz, t, d)
```

```python
import functools
import math

import jax
import jax.numpy as jnp
from jax import lax
from jax.experimental import pallas as pl
from jax.experimental.pallas import tpu as pltpu

D_MODEL = 1024
GM_GROUPS = 8
GM_BLOCK = 128
GM_CHUNK = 64
DN_HEADS = 8
DN_DK = 128
DN_CONV = 4
MOE_GROUPS = 4
MOE_EPG = 8
N_EXPERTS = MOE_GROUPS * MOE_EPG
D_EXPERT = D_MODEL // 2
LN_EPS = 1e-5
RMS_EPS = 1e-6
L2_EPS = 1e-6
DEEPNORM_ALPHA = 2.0 ** 0.25

LANES = 128
COL_U, COL_V, COL_Q, COL_K, COL_VV, COL_Z, COL_GA, COL_GB = range(8)
N_PROJ_BLOCKS = 8
N_FRONT_BLOCKS = 6
PROJ_WIDTH = N_PROJ_BLOCKS * D_MODEL

DN_CHUNK = 64
CARRY_ROWS = 8
EXPERT_ROWS = 256
VMEM_LIMIT = 56 * 1024 * 1024
SUBLANES = 8
ROW_CHUNKS = D_MODEL // LANES
assert ROW_CHUNKS == SUBLANES

F32 = jnp.float32
BF16 = jnp.bfloat16


def _cparams(*sem):
    return pltpu.CompilerParams(dimension_semantics=sem, vmem_limit_bytes=VMEM_LIMIT)


def _store_row_tiles(ref, val):
    m = val.shape[0]
    for ch in range(ROW_CHUNKS):
        ref[pl.ds(ch, m, stride=ROW_CHUNKS), :] = val[:, ch * LANES:(ch + 1) * LANES]


def _load_row_tiles(ref, m):
    return jnp.concatenate([ref[pl.ds(ch, m, stride=ROW_CHUNKS), :] for ch in range(ROW_CHUNKS)], axis=1)


def _layer_norm(x, g, b):
    mu = jnp.mean(x, -1, keepdims=True)
    xc = x - mu
    var = jnp.mean(xc * xc, -1, keepdims=True)
    return xc * lax.rsqrt(var + LN_EPS) * g + b


def _dot(a, b):
    return jnp.dot(a.astype(BF16), b.astype(BF16), preferred_element_type=F32)


def _dot_nt(a, b):
    return lax.dot_general(a.astype(BF16), b.astype(BF16), (((1,), (1,)), ((), ())),
                           preferred_element_type=F32)


def _ln_ab_body(x_ref, g_ref, b_ref, wab_ref, bab_ref, h_ref, ab_ref):
    h = _layer_norm(x_ref[...], g_ref[...], b_ref[...]).astype(BF16)
    h_ref[...] = h
    ab_ref[...] = jnp.dot(h, wab_ref[...], preferred_element_type=F32) + bab_ref[...]


def _ln_ab(x2, ln_g, ln_b, w_ab, b_ab, *, tm=512):
    n = x2.shape[0]
    return pl.pallas_call(
        _ln_ab_body,
        grid=(n // tm,),
        in_specs=[
            pl.BlockSpec((tm, D_MODEL), lambda i: (i, 0)),
            pl.BlockSpec((1, D_MODEL), lambda i: (0, 0)),
            pl.BlockSpec((1, D_MODEL), lambda i: (0, 0)),
            pl.BlockSpec((D_MODEL, LANES), lambda i: (0, 0)),
            pl.BlockSpec((1, LANES), lambda i: (0, 0)),
        ],
        out_specs=[pl.BlockSpec((tm, D_MODEL), lambda i: (i, 0)), pl.BlockSpec((tm, LANES), lambda i: (i, 0))],
        out_shape=[jax.ShapeDtypeStruct((n, D_MODEL), BF16), jax.ShapeDtypeStruct((n, LANES), F32)],
        compiler_params=_cparams("parallel"),
        name="ln_ab",
    )(x2, ln_g, ln_b, w_ab, b_ab)


def _inproj_body(h_ref, wa_ref, wb_ref, bias_ref, o_ref, w_scr):
    j = pl.program_id(0)

    @pl.when((pl.program_id(1) == 0) & (j < N_FRONT_BLOCKS))
    def _():
        w_scr[...] = wa_ref[...].astype(BF16)

    @pl.when((pl.program_id(1) == 0) & (j >= N_FRONT_BLOCKS))
    def _():
        w_scr[...] = wb_ref[...].astype(BF16)

    o_ref[...] = (jnp.dot(h_ref[...], w_scr[...], preferred_element_type=F32) + bias_ref[...]).astype(o_ref.dtype)


def _inproj(h, w_in, w_gates, bias, *, tm=512):
    n = h.shape[0]
    return pl.pallas_call(
        _inproj_body,
        grid=(N_PROJ_BLOCKS, n // tm),
        in_specs=[
            pl.BlockSpec((tm, D_MODEL), lambda j, i: (i, 0)),
            pl.BlockSpec((D_MODEL, D_MODEL), lambda j, i: (0, jnp.minimum(j, N_FRONT_BLOCKS - 1))),
            pl.BlockSpec((D_MODEL, D_MODEL), lambda j, i: (0, jnp.maximum(j - N_FRONT_BLOCKS, 0))),
            pl.BlockSpec((1, D_MODEL), lambda j, i: (0, j)),
        ],
        out_specs=pl.BlockSpec((tm, D_MODEL), lambda j, i: (i, j)),
        out_shape=jax.ShapeDtypeStruct((n, N_PROJ_BLOCKS * D_MODEL), BF16),
        scratch_shapes=[pltpu.VMEM((D_MODEL, D_MODEL), BF16)],
        compiler_params=_cparams("arbitrary", "arbitrary"),
        name="inproj",
    )(h, w_in, w_gates, bias)


def _gelu(x):
    return 0.5 * x * (1.0 + lax.erf(x * (1.0 / math.sqrt(2.0))))


def _gmlp_body(u_ref, v_ref, lng_ref, lnb_ref, ws_ref, bs_ref, o_ref, *, nblk):
    u = _gelu(u_ref[...].astype(F32))
    v = _layer_norm(_gelu(v_ref[...].astype(F32)), lng_ref[...], lnb_ref[...]).astype(BF16)
    row_chunk = lax.broadcasted_iota(jnp.int32, (GM_BLOCK, GM_BLOCK), 0) // GM_CHUNK
    col_chunk = lax.broadcasted_iota(jnp.int32, (GM_BLOCK, GM_BLOCK), 1) // GM_CHUNK
    causal = col_chunk <= row_chunk
    for g in range(GM_GROUPS):
        cols = slice(g * LANES, (g + 1) * LANES)
        w = jnp.where(causal, ws_ref[g], 0.0).astype(BF16)
        for blk in range(nblk):
            rows = slice(blk * GM_BLOCK, (blk + 1) * GM_BLOCK)
            s = jnp.dot(w, v[rows, cols], preferred_element_type=F32) + bs_ref[g]
            o_ref[rows, cols] = (u[rows, cols] * s).astype(o_ref.dtype)


def _gmlp(proj, ln_g, ln_b, w_s, b_s_full, *, nblk=2):
    n = proj.shape[0]
    rows = nblk * GM_BLOCK
    return pl.pallas_call(
        functools.partial(_gmlp_body, nblk=nblk),
        grid=(n // rows,),
        in_specs=[
            pl.BlockSpec((rows, D_MODEL), lambda i: (i, COL_U)),
            pl.BlockSpec((rows, D_MODEL), lambda i: (i, COL_V)),
            pl.BlockSpec((1, D_MODEL), lambda i: (0, 0)),
            pl.BlockSpec((1, D_MODEL), lambda i: (0, 0)),
            pl.BlockSpec((GM_GROUPS, GM_BLOCK, GM_BLOCK), lambda i: (0, 0, 0)),
            pl.BlockSpec((GM_GROUPS, GM_BLOCK, LANES), lambda i: (0, 0, 0)),
        ],
        out_specs=pl.BlockSpec((rows, D_MODEL), lambda i: (i, 0)),
        out_shape=jax.ShapeDtypeStruct((n, D_MODEL), BF16),
        compiler_params=_cparams("parallel"),
        name="gmlp",
    )(proj, proj, ln_g, ln_b, w_s, b_s_full)


def _cumsum_rows(x):
    n = x.shape[0]
    row = lax.broadcasted_iota(jnp.int32, x.shape, 0)
    shift = 1
    while shift < n:
        x = x + jnp.where(row >= shift, pltpu.roll(x, shift, 0), 0.0)
        shift *= 2
    return x


def _deltanet_body(q_ref, k_ref, v_ref, z_ref, ab_ref, cw_ref, alog_ref, dtb_ref, nw_ref,
                   o_ref, xs_ref, state_ref, *, nb):
    c = DN_CHUNK
    w3 = 3 * D_MODEL

    @pl.when(pl.program_id(1) == 0)
    def _():
        xs_ref[:, 0:CARRY_ROWS, :] = jnp.zeros((nb, CARRY_ROWS, w3), F32)
        state_ref[...] = jnp.zeros_like(state_ref)

    ri = lax.broadcasted_iota(jnp.int32, (c, c), 0)
    ci = lax.broadcasted_iota(jnp.int32, (c, c), 1)
    causal = ri >= ci
    strict = ri > ci
    heads = range(DN_HEADS)
    hsl = lambda base, h: slice(base + h * DN_DK, base + (h + 1) * DN_DK)

    q, k, v, g_col, beta, g_last, decay = [], [], [], [], [], [], []
    for b in range(nb):
        xs_ref[b, CARRY_ROWS:, 0:D_MODEL] = q_ref[b].astype(F32)
        xs_ref[b, CARRY_ROWS:, D_MODEL:2 * D_MODEL] = k_ref[b].astype(F32)
        xs_ref[b, CARRY_ROWS:, 2 * D_MODEL:] = v_ref[b].astype(F32)
        acc = cw_ref[DN_CONV - 1:DN_CONV, :] * xs_ref[b, CARRY_ROWS:, :]
        for j in range(DN_CONV - 1):
            off = CARRY_ROWS - (DN_CONV - 1) + j
            acc = acc + cw_ref[j:j + 1, :] * xs_ref[b, off:off + c, :]
        xs_ref[b, 0:CARRY_ROWS, :] = xs_ref[b, c:c + CARRY_ROWS, :]
        qkv = acc * jax.nn.sigmoid(acc)

        ab = ab_ref[b]
        xg = ab + dtb_ref[...]
        softplus = jnp.maximum(xg, 0.0) + jnp.log1p(jnp.exp(-jnp.abs(xg)))
        gcum = _cumsum_rows(-jnp.exp(alog_ref[...]) * softplus)
        beta_all = jax.nn.sigmoid(ab)
        gcum_sq = gcum if c == LANES else jnp.concatenate([gcum, jnp.zeros((LANES - c, LANES), F32)], axis=0)
        gcum_t = gcum_sq.T

        for h in heads:
            qh, kh = qkv[:, hsl(0, h)], qkv[:, hsl(D_MODEL, h)]
            q.append(qh * (lax.rsqrt(jnp.sum(qh * qh, -1, keepdims=True) + L2_EPS) * (DN_DK ** -0.5)))
            k.append(kh * lax.rsqrt(jnp.sum(kh * kh, -1, keepdims=True) + L2_EPS))
            v.append(qkv[:, hsl(2 * D_MODEL, h)])
            g_col.append(gcum[:, h:h + 1])
            beta.append(beta_all[:, DN_HEADS + h:DN_HEADS + h + 1])
            g_last.append(gcum[c - 1:c, h:h + 1])
            decay.append(jnp.where(causal, jnp.exp(jnp.where(causal, g_col[-1] - gcum_t[h:h + 1, 0:c], 0.0)), 0.0))

    units = range(nb * DN_HEADS)
    eg = [jnp.exp(g) for g in g_col]
    kb = [x.astype(BF16) for x in k]
    kq = [lax.dot_general(jnp.concatenate([kb[u], q[u].astype(BF16)], axis=0), kb[u],
                          (((1,), (1,)), ((), ())), preferred_element_type=F32) for u in units]
    p = [jnp.where(strict, beta[u] * kq[u][0:c] * decay[u], 0.0).astype(BF16) for u in units]
    attn = [(kq[u][c:] * decay[u]).astype(BF16) for u in units]

    r = [jnp.concatenate([v[u] * beta[u], k[u] * (beta[u] * eg[u])], axis=1) for u in units]
    r = [r[u] - jnp.dot(p[u], r[u].astype(BF16), preferred_element_type=F32) for u in units]
    span = 2
    while span < c:
        p = [jnp.dot(p[u], p[u], preferred_element_type=F32).astype(BF16) for u in units]
        r = [r[u] + jnp.dot(p[u], r[u].astype(BF16), preferred_element_type=F32) for u in units]
        span *= 2

    s_old = [state_ref[u] for u in units]
    ws = [jnp.dot(jnp.concatenate([r[u][:, DN_DK:], q[u] * eg[u]], axis=0).astype(BF16),
                  s_old[u].astype(BF16), preferred_element_type=F32) for u in units]
    v_new = [(r[u][:, 0:DN_DK] - ws[u][0:c]).astype(BF16) for u in units]
    o = [ws[u][c:] + jnp.dot(attn[u], v_new[u], preferred_element_type=F32) for u in units]
    k_dec = [(k[u] * jnp.exp(g_last[u] - g_col[u])).astype(BF16) for u in units]
    ds = [lax.dot_general(k_dec[u], v_new[u], (((0,), (0,)), ((), ())), preferred_element_type=F32)
          for u in units]
    for u in units:
        state_ref[u] = s_old[u] * jnp.exp(g_last[u]) + ds[u]

    for b in range(nb):
        z = z_ref[b].astype(F32)
        for h in heads:
            u = b * DN_HEADS + h
            zh = z[:, hsl(0, h)]
            on = o[u] * lax.rsqrt(jnp.mean(o[u] * o[u], -1, keepdims=True) + RMS_EPS) * nw_ref[...]
            o_ref[b, :, hsl(0, h)] = (on * (zh * jax.nn.sigmoid(zh))).astype(o_ref.dtype)


def _deltanet(proj, ab, conv_w, alog_row, dtb_row, norm_w, *, bsz, t, nb=2):
    c = DN_CHUNK
    proj3 = proj.reshape(bsz, t, PROJ_WIDTH)
    ab3 = ab.reshape(bsz, t, LANES)
    col = lambda j: pl.BlockSpec((nb, c, D_MODEL), lambda bp, n: (bp, n, j))
    yb = pl.pallas_call(
        functools.partial(_deltanet_body, nb=nb),
        grid=(bsz // nb, t // c),
        in_specs=[
            col(COL_Q), col(COL_K), col(COL_VV), col(COL_Z),
            pl.BlockSpec((nb, c, LANES), lambda bp, n: (bp, n, 0)),
            pl.BlockSpec((DN_CONV, 3 * D_MODEL), lambda bp, n: (0, 0)),
            pl.BlockSpec((1, LANES), lambda bp, n: (0, 0)),
            pl.BlockSpec((1, LANES), lambda bp, n: (0, 0)),
            pl.BlockSpec((1, DN_DK), lambda bp, n: (0, 0)),
        ],
        out_specs=pl.BlockSpec((nb, c, D_MODEL), lambda bp, n: (bp, n, 0)),
        out_shape=jax.ShapeDtypeStruct((bsz, t, D_MODEL), BF16),
        scratch_shapes=[
            pltpu.VMEM((nb, CARRY_ROWS + c, 3 * D_MODEL), F32),
            pltpu.VMEM((nb * DN_HEADS, DN_DK, DN_DK), F32),
        ],
        compiler_params=_cparams("parallel", "arbitrary"),
        name="deltanet",
    )(proj3, proj3, proj3, proj3, ab3, conv_w, alog_row, dtb_row, norm_w)
    return yb.reshape(bsz * t, D_MODEL)


def _merge_body(x_ref, ya_ref, yb_ref, ga_ref, gb_ref, lng_ref, lnb_ref, wpa_ref, wpb_ref, wo_ref,
                l1g_ref, l1b_ref, wr_ref, wrlo_ref, br_ref, h1_ref, logit_ref):
    h = _layer_norm(x_ref[...], lng_ref[...], lnb_ref[...])
    pa = jnp.dot(ya_ref[...], wpa_ref[...], preferred_element_type=F32)
    pb = jnp.dot(yb_ref[...], wpb_ref[...], preferred_element_type=F32)
    merged = jax.nn.sigmoid(ga_ref[...].astype(F32)) * pa + jax.nn.sigmoid(gb_ref[...].astype(F32)) * pb
    mix = jnp.dot(merged.astype(BF16), wo_ref[...], preferred_element_type=F32)
    h1 = _layer_norm(DEEPNORM_ALPHA * h + mix, l1g_ref[...], l1b_ref[...])
    _store_row_tiles(h1_ref, h1)
    tm = h1.shape[0]
    hi = h1.astype(BF16)
    lo = (h1 - hi.astype(F32)).astype(BF16)
    hw = jnp.dot(jnp.concatenate([hi, lo], axis=0), wr_ref[...], preferred_element_type=F32)
    logit_ref[...] = (hw[0:tm] + hw[tm:] + jnp.dot(hi, wrlo_ref[...], preferred_element_type=F32)
                      + br_ref[...])


def _merge(x2, ya, yb, proj, ln_g, ln_b, w_pa, w_pb, w_o, l1g, l1b, w_r, w_r_lo, b_r, *, tm=256):
    n = x2.shape[0]
    vec = lambda: pl.BlockSpec((1, D_MODEL), lambda i: (0, 0))
    mat = lambda: pl.BlockSpec((D_MODEL, D_MODEL), lambda i: (0, 0))
    return pl.pallas_call(
        _merge_body,
        grid=(n // tm,),
        in_specs=[
            pl.BlockSpec((tm, D_MODEL), lambda i: (i, 0)),
            pl.BlockSpec((tm, D_MODEL), lambda i: (i, 0)),
            pl.BlockSpec((tm, D_MODEL), lambda i: (i, 0)),
            pl.BlockSpec((tm, D_MODEL), lambda i: (i, COL_GA)),
            pl.BlockSpec((tm, D_MODEL), lambda i: (i, COL_GB)),
            vec(), vec(), mat(), mat(), mat(), vec(), vec(),
            pl.BlockSpec((D_MODEL, LANES), lambda i: (0, 0)),
            pl.BlockSpec((D_MODEL, LANES), lambda i: (0, 0)),
            pl.BlockSpec((1, LANES), lambda i: (0, 0)),
        ],
        out_specs=[
            pl.BlockSpec((tm * ROW_CHUNKS, LANES), lambda i: (i, 0)),
            pl.BlockSpec((tm, LANES), lambda i: (i, 0)),
        ],
        out_shape=[
            jax.ShapeDtypeStruct((n * ROW_CHUNKS, LANES), F32),
            jax.ShapeDtypeStruct((n, LANES), F32),
        ],
        compiler_params=_cparams("parallel"),
        name="merge",
    )(x2, ya, yb, proj, proj, ln_g, ln_b, w_pa, w_pb, w_o, l1g, l1b, w_r, w_r_lo, b_r)


def _route_body(logit_ref, gate_ref, idx_ref, cnt_ref, carry_ref, *, tm):
    @pl.when(pl.program_id(0) == 0)
    def _():
        carry_ref[...] = jnp.zeros_like(carry_ref)

    lg = logit_ref[...]
    lane_i = lax.broadcasted_iota(jnp.int32, lg.shape, 1)
    lane = lane_i.astype(F32)
    neg = jnp.float32(-jnp.inf)
    big = jnp.float32(1 << 20)

    is_grp = lane_i < MOE_GROUPS
    gl = jnp.where(is_grp, lg, neg)
    gmax = jnp.max(gl, -1, keepdims=True)
    grp = jnp.min(jnp.where(is_grp & (gl == gmax), lane, big), -1, keepdims=True)
    p_grp = 1.0 / jnp.sum(jnp.where(is_grp, jnp.exp(gl - gmax), 0.0), -1, keepdims=True)

    elane = lane - MOE_GROUPS
    in_grp = (elane >= grp * MOE_EPG) & (elane < (grp + 1) * MOE_EPG)
    el = jnp.where(in_grp, lg, neg)
    m1 = jnp.max(el, -1, keepdims=True)
    e1 = jnp.min(jnp.where(in_grp & (el == m1), elane, big), -1, keepdims=True)
    rest = in_grp & (elane != e1)
    el2 = jnp.where(rest, lg, neg)
    m2 = jnp.max(el2, -1, keepdims=True)
    e2 = jnp.min(jnp.where(rest & (el2 == m2), elane, big), -1, keepdims=True)
    t2 = jnp.exp(m2 - m1)
    g1 = p_grp * (1.0 / (1.0 + t2))
    g2 = p_grp * (t2 / (1.0 + t2))

    oh1 = lane == e1
    oh2 = lane == e2
    onehot = jnp.where(oh1 | oh2, 1.0, 0.0)
    ri = lax.broadcasted_iota(jnp.int32, (tm, tm), 0)
    ci = lax.broadcasted_iota(jnp.int32, (tm, tm), 1)
    tri = jnp.where(ci < ri, 1.0, 0.0).astype(BF16)
    before = jnp.dot(tri, onehot.astype(BF16), preferred_element_type=F32) + carry_ref[0:1, :]
    r1 = jnp.sum(jnp.where(oh1, before, 0.0), -1, keepdims=True)
    r2 = jnp.sum(jnp.where(oh2, before, 0.0), -1, keepdims=True)
    total = carry_ref[0:1, :] + jnp.sum(onehot, 0, keepdims=True)
    carry_ref[...] = jnp.broadcast_to(total, carry_ref.shape)
    cnt_ref[...] = jnp.broadcast_to(total, cnt_ref.shape).astype(jnp.int32)

    gate_ref[...] = jnp.where(lane_i == 0, g1, jnp.where(lane_i == 1, g2, 0.0))
    idx_ref[...] = jnp.where(lane_i == 0, e1, jnp.where(lane_i == 1, e2,
                             jnp.where(lane_i == 2, r1, jnp.where(lane_i == 3, r2, 0.0)))).astype(jnp.int32)


def _route(logits, *, tm=256):
    n = logits.shape[0]
    return pl.pallas_call(
        functools.partial(_route_body, tm=tm),
        grid=(n // tm,),
        in_specs=[pl.BlockSpec((tm, LANES), lambda i: (i, 0))],
        out_specs=[
            pl.BlockSpec((tm, LANES), lambda i: (i, 0)),
            pl.BlockSpec((tm, LANES), lambda i: (i, 0)),
            pl.BlockSpec((8, LANES), lambda i: (0, 0)),
        ],
        out_shape=[
            jax.ShapeDtypeStruct((n, LANES), F32),
            jax.ShapeDtypeStruct((n, LANES), jnp.int32),
            jax.ShapeDtypeStruct((8, LANES), jnp.int32),
        ],
        scratch_shapes=[pltpu.VMEM((8, LANES), F32)],
        compiler_params=_cparams("arbitrary"),
        name="route",
    )(logits)


def _dest_body(idx_ref, ps_ref, o_ref):
    idx = idx_ref[...]
    lane = lax.broadcasted_iota(jnp.int32, idx.shape, 1)
    ps = ps_ref[...].astype(F32)
    d1 = jnp.sum(jnp.where(lane == idx[:, 0:1], ps, 0.0), -1, keepdims=True).astype(jnp.int32) + idx[:, 2:3]
    d2 = jnp.sum(jnp.where(lane == idx[:, 1:2], ps, 0.0), -1, keepdims=True).astype(jnp.int32) + idx[:, 3:4]
    o_ref[...] = jnp.where(lane == 0, d1, jnp.where(lane == 1, d2, 0))


def _dest(idx, pad_start_row, *, tm=1024):
    n = idx.shape[0]
    return pl.pallas_call(
        _dest_body,
        grid=(n // tm,),
        in_specs=[pl.BlockSpec((tm, LANES), lambda i: (i, 0)), pl.BlockSpec((1, LANES), lambda i: (0, 0))],
        out_specs=pl.BlockSpec((tm, LANES), lambda i: (i, 0)),
        out_shape=jax.ShapeDtypeStruct((n, LANES), jnp.int32),
        compiler_params=_cparams("parallel"),
        name="dest",
    )(idx, pad_start_row)


DMA_GROUP = 8


def _row_tile(ref, row):
    return ref.at[pl.ds(pl.multiple_of(row * SUBLANES, SUBLANES), SUBLANES)]


def _dispatch_body(dest_ref, h_ref, xin_ref, xpad_ref, sem, *, tm):
    del xin_ref

    def row_copy(src_row, dst_row):
        return pltpu.make_async_copy(_row_tile(h_ref, src_row), _row_tile(xpad_ref, dst_row), sem)

    def start(g, carry):
        r0 = g * DMA_GROUP
        dst = [dest_ref[0, 0, 2 * r0 + j] for j in range(2 * DMA_GROUP)]
        for j in range(2 * DMA_GROUP):
            row_copy(r0 + j // 2, dst[j]).start()
        return carry

    def wait(g, carry):
        for _ in range(2 * DMA_GROUP):
            row_copy(0, 0).wait()
        return carry

    lax.fori_loop(0, tm // DMA_GROUP, start, 0)
    lax.fori_loop(0, tm // DMA_GROUP, wait, 0)


def _dispatch(dest_smem, h1_rows, xpad0, *, tm):
    n = h1_rows.shape[0] // ROW_CHUNKS
    return pl.pallas_call(
        functools.partial(_dispatch_body, tm=tm),
        grid=(n // tm,),
        in_specs=[
            pl.BlockSpec((1, 1, 2 * tm), lambda i: (i, 0, 0), memory_space=pltpu.SMEM),
            pl.BlockSpec((tm * ROW_CHUNKS, LANES), lambda i: (i, 0)),
            pl.BlockSpec(memory_space=pl.ANY),
        ],
        out_specs=pl.BlockSpec(memory_space=pl.ANY),
        scratch_shapes=[pltpu.SemaphoreType.DMA(())],
        out_shape=jax.ShapeDtypeStruct(xpad0.shape, xpad0.dtype),
        input_output_aliases={2: 0},
        compiler_params=_cparams("arbitrary"),
        name="dispatch",
    )(dest_smem, h1_rows, xpad0)


def _experts_body(blk_e_ref, nused_ref, x_ref, w1_ref, w3_ref, w2_ref, y_ref, w1b, w3b, w2b):
    i = pl.program_id(0)
    prev = blk_e_ref[jnp.maximum(i - 1, 0)]

    @pl.when((i == 0) | (blk_e_ref[i] != prev))
    def _():
        w1b[...] = w1_ref[0].astype(BF16)
        w3b[...] = w3_ref[0].astype(BF16)
        w2b[...] = w2_ref[0].astype(BF16)

    @pl.when(i < nused_ref[0])
    def _():
        xb = _load_row_tiles(x_ref, EXPERT_ROWS).astype(BF16)
        a = jnp.dot(xb, w1b[...], preferred_element_type=F32)
        b = jnp.dot(xb, w3b[...], preferred_element_type=F32)
        hb = (a * jax.nn.sigmoid(a)) * b
        _store_row_tiles(y_ref, jnp.dot(hb.astype(BF16), w2b[...], preferred_element_type=F32))

    @pl.when(i >= nused_ref[0])
    def _():
        y_ref[...] = jnp.zeros_like(y_ref)


def _experts(blk_e, n_used, xpad, w1, w3, w2):
    tm = EXPERT_ROWS * ROW_CHUNKS
    return pl.pallas_call(
        _experts_body,
        grid_spec=pltpu.PrefetchScalarGridSpec(
            num_scalar_prefetch=2,
            grid=(xpad.shape[0] // tm,),
            in_specs=[
                pl.BlockSpec((tm, LANES), lambda i, be, nu: (i, 0)),
                pl.BlockSpec((1, D_MODEL, D_EXPERT), lambda i, be, nu: (be[i], 0, 0)),
                pl.BlockSpec((1, D_MODEL, D_EXPERT), lambda i, be, nu: (be[i], 0, 0)),
                pl.BlockSpec((1, D_EXPERT, D_MODEL), lambda i, be, nu: (be[i], 0, 0)),
            ],
            out_specs=pl.BlockSpec((tm, LANES), lambda i, be, nu: (i, 0)),
            scratch_shapes=[
                pltpu.VMEM((D_MODEL, D_EXPERT), BF16),
                pltpu.VMEM((D_MODEL, D_EXPERT), BF16),
                pltpu.VMEM((D_EXPERT, D_MODEL), BF16),
            ],
        ),
        out_shape=jax.ShapeDtypeStruct(xpad.shape, F32),
        compiler_params=_cparams("arbitrary"),
        name="experts",
    )(blk_e, n_used, xpad, w1, w3, w2)


def _combine_body(dest_ref, dest_next_ref, h1_ref, gate_ref, l2g_ref, l2b_ref, ypad_ref, o_ref, buf, sem, *, tm):
    i = pl.program_id(0)
    half = i % 2

    def row_copy(src_row, r, slot, hf):
        return pltpu.make_async_copy(_row_tile(ypad_ref, src_row), _row_tile(buf.at[2 * hf + slot], r), sem.at[hf])

    def gather(idx_ref, hf):
        def start(g, carry):
            r0 = g * DMA_GROUP
            src = [idx_ref[0, 0, 2 * r0 + j] for j in range(2 * DMA_GROUP)]
            for j in range(2 * DMA_GROUP):
                row_copy(src[j], r0 + j // 2, j % 2, hf).start()
            return carry
        lax.fori_loop(0, tm // DMA_GROUP, start, 0)

    @pl.when(i == 0)
    def _():
        gather(dest_ref, 0)

    @pl.when(i + 1 < pl.num_programs(0))
    def _():
        gather(dest_next_ref, 1 - half)

    def wait(g, carry):
        for j in range(2 * DMA_GROUP):
            row_copy(0, 0, j % 2, half).wait()
        return carry

    lax.fori_loop(0, tm // DMA_GROUP, wait, 0)
    gate = gate_ref[...]
    ffn = (_load_row_tiles(buf.at[2 * half], tm) * gate[:, 0:1]
           + _load_row_tiles(buf.at[2 * half + 1], tm) * gate[:, 1:2])
    h1 = _load_row_tiles(h1_ref, tm)
    o_ref[...] = _layer_norm(DEEPNORM_ALPHA * h1 + ffn, l2g_ref[...], l2b_ref[...])


def _combine(dest_smem, h1_rows, gates, l2g, l2b, ypad, *, tm):
    n = h1_rows.shape[0] // ROW_CHUNKS
    steps = n // tm
    return pl.pallas_call(
        functools.partial(_combine_body, tm=tm),
        grid=(steps,),
        in_specs=[
            pl.BlockSpec((1, 1, 2 * tm), lambda i: (i, 0, 0), memory_space=pltpu.SMEM),
            pl.BlockSpec((1, 1, 2 * tm), lambda i: (jnp.minimum(i + 1, steps - 1), 0, 0), memory_space=pltpu.SMEM),
            pl.BlockSpec((tm * ROW_CHUNKS, LANES), lambda i: (i, 0)),
            pl.BlockSpec((tm, LANES), lambda i: (i, 0)),
            pl.BlockSpec((1, D_MODEL), lambda i: (0, 0)),
            pl.BlockSpec((1, D_MODEL), lambda i: (0, 0)),
            pl.BlockSpec(memory_space=pl.ANY),
        ],
        out_specs=pl.BlockSpec((tm, D_MODEL), lambda i: (i, 0)),
        scratch_shapes=[
            pltpu.VMEM((4, tm * ROW_CHUNKS, LANES), F32),
            pltpu.SemaphoreType.DMA((2,)),
        ],
        out_shape=jax.ShapeDtypeStruct((n, D_MODEL), F32),
        compiler_params=_cparams("arbitrary"),
        name="combine",
    )(dest_smem, dest_smem, h1_rows, gates, l2g, l2b, ypad)


def _row(v, width=None):
    v = v.reshape(1, -1).astype(F32)
    if width is not None and v.shape[1] < width:
        v = jnp.pad(v, ((0, 0), (0, width - v.shape[1])))
    return v


def kernel(x, ln_in_g, ln_in_b, w_in, b_in, gm_ln_g, gm_ln_b, gm_w_s, gm_b_s, dn_conv_w, dn_a_log,
           dn_dt_bias, dn_norm_w, w_pa, w_pb, w_o, ln1_g, ln1_b, w_rg, b_rg, w_re, b_re, w1, w3, w2,
           ln2_g, ln2_b):
    bsz, t, d = x.shape
    n = bsz * t
    x2 = x.reshape(n, d)
    l = 0

    wi, bi = w_in[l], b_in[l]
    c_ab = N_FRONT_BLOCKS * D_MODEL
    c_gate = c_ab + 2 * DN_HEADS
    w_ab = jnp.pad(wi[:, c_ab:c_gate], ((0, 0), (0, LANES - 2 * DN_HEADS))).astype(BF16)
    b_ab = _row(bi[c_ab:c_gate], LANES)
    w_gates = wi[:, c_gate:]
    b_proj = jnp.concatenate([bi[:c_ab], bi[c_gate:]]).reshape(1, PROJ_WIDTH)

    ln_g, ln_b = _row(ln_in_g), _row(ln_in_b)
    h, ab = _ln_ab(x2, ln_g, ln_b, w_ab, b_ab)
    proj = _inproj(h, wi, w_gates, b_proj)

    bs_full = jnp.broadcast_to(gm_b_s[l][:, :, None], (GM_GROUPS, GM_BLOCK, LANES))
    ya = _gmlp(proj, _row(gm_ln_g[l]), _row(gm_ln_b[l]), gm_w_s[l], bs_full)

    yb = _deltanet(proj, ab, dn_conv_w[l], _row(dn_a_log[l], LANES), _row(dn_dt_bias[l], LANES),
                   _row(dn_norm_w[l]), bsz=bsz, t=t)

    w_r = jnp.concatenate([w_rg[l], w_re[l], jnp.zeros((d, LANES - MOE_GROUPS - N_EXPERTS), F32)], axis=1)
    w_r_hi = w_r.astype(BF16)
    w_r_lo = (w_r - w_r_hi.astype(F32)).astype(BF16)
    b_r = _row(jnp.concatenate([b_rg[l], b_re[l]]), LANES)
    h1, logits = _merge(x2, ya, yb, proj, ln_g, ln_b, w_pa[l].astype(BF16), w_pb[l].astype(BF16),
                        w_o[l].astype(BF16), _row(ln1_g[l]), _row(ln1_b[l]), w_r_hi, w_r_lo, b_r)

    gates, idx, counts = _route(logits)

    tm_e = EXPERT_ROWS
    cnt = counts[0, :N_EXPERTS]
    padded = (cnt + tm_e - 1) // tm_e * tm_e
    pad_end = jnp.cumsum(padded)
    pad_start = (pad_end - padded).astype(jnp.int32)
    n_blocks = (2 * n + N_EXPERTS * (tm_e - 1) + tm_e - 1) // tm_e
    blk_start = jnp.arange(n_blocks, dtype=jnp.int32) * tm_e
    blk_e = jnp.minimum(jnp.sum(pad_end[None, :] <= blk_start[:, None], -1), N_EXPERTS - 1).astype(jnp.int32)
    n_used = (pad_end[-1] // tm_e).astype(jnp.int32).reshape(1)

    ps_row = jnp.pad(pad_start, (0, LANES - N_EXPERTS)).reshape(1, LANES)
    dest = _dest(idx, ps_row, tm=min(1024, n))[:, :2]
    tm_d, tm_c = min(1024, n), 256
    xpad0 = jnp.zeros((n_blocks * tm_e * ROW_CHUNKS, LANES), F32)
    xpad = _dispatch(dest.reshape(n // tm_d, 1, 2 * tm_d), h1, xpad0, tm=tm_d)
    ypad = _experts(blk_e, n_used, xpad, w1[l], w3[l], w2[l])
    out = _combine(dest.reshape(n // tm_c, 1, 2 * tm_c), h1, gates, _row(ln2_g[l]), _row(ln2_b[l]), ypad,
                   tm=tm_c)
    return out.reshape(bsz, t, d)
```

```python
import functools
import math

import jax
import jax.numpy as jnp
from jax import lax
from jax.experimental import pallas as pl
from jax.experimental.pallas import tpu as pltpu

D_MODEL = 1024
GM_GROUPS = 8
GM_BLOCK = 128
GM_CHUNK = 64
DN_HEADS = 8
DN_DK = 128
DN_CONV = 4
MOE_GROUPS = 4
MOE_EPG = 8
N_EXPERTS = MOE_GROUPS * MOE_EPG
D_EXPERT = D_MODEL // 2
LN_EPS = 1e-5
RMS_EPS = 1e-6
L2_EPS = 1e-6
DEEPNORM_ALPHA = 2.0 ** 0.25

LANES = 128
COL_U, COL_V, COL_Q, COL_K, COL_VV, COL_Z, COL_GA, COL_GB = range(8)
N_PROJ_BLOCKS = 8
N_FRONT_BLOCKS = 6
PROJ_WIDTH = N_PROJ_BLOCKS * D_MODEL

DN_CHUNK = 64
CARRY_ROWS = 16
EXPERT_ROWS = 256
VMEM_LIMIT = 56 * 1024 * 1024
SUBLANES = 8
ROW_CHUNKS = D_MODEL // LANES
assert ROW_CHUNKS == SUBLANES

F32 = jnp.float32
BF16 = jnp.bfloat16


def _cparams(*sem):
    return pltpu.CompilerParams(dimension_semantics=sem, vmem_limit_bytes=VMEM_LIMIT)


def _store_row_tiles(ref, val):
    m = val.shape[0]
    for ch in range(ROW_CHUNKS):
        ref[pl.ds(ch, m, stride=ROW_CHUNKS), :] = val[:, ch * LANES:(ch + 1) * LANES]


def _load_row_tiles(ref, m):
    return jnp.concatenate([ref[pl.ds(ch, m, stride=ROW_CHUNKS), :] for ch in range(ROW_CHUNKS)], axis=1)


def _layer_norm(x, g, b):
    mu = jnp.mean(x, -1, keepdims=True)
    xc = x - mu
    var = jnp.mean(xc * xc, -1, keepdims=True)
    return xc * lax.rsqrt(var + LN_EPS) * g + b


def _dot(a, b):
    return jnp.dot(a.astype(BF16), b.astype(BF16), preferred_element_type=F32)


def _dot_nt(a, b):
    return lax.dot_general(a.astype(BF16), b.astype(BF16), (((1,), (1,)), ((), ())),
                           preferred_element_type=F32)


def _ln_ab_body(x_ref, g_ref, b_ref, wab_ref, bab_ref, h_ref, ab_ref):
    h = _layer_norm(x_ref[...], g_ref[...], b_ref[...]).astype(BF16)
    h_ref[...] = h
    ab_ref[...] = jnp.dot(h, wab_ref[...], preferred_element_type=F32) + bab_ref[...]


def _ln_ab(x2, ln_g, ln_b, w_ab, b_ab, *, tm=512):
    n = x2.shape[0]
    return pl.pallas_call(
        _ln_ab_body,
        grid=(n // tm,),
        in_specs=[
            pl.BlockSpec((tm, D_MODEL), lambda i: (i, 0)),
            pl.BlockSpec((1, D_MODEL), lambda i: (0, 0)),
            pl.BlockSpec((1, D_MODEL), lambda i: (0, 0)),
            pl.BlockSpec((D_MODEL, LANES), lambda i: (0, 0)),
            pl.BlockSpec((1, LANES), lambda i: (0, 0)),
        ],
        out_specs=[pl.BlockSpec((tm, D_MODEL), lambda i: (i, 0)), pl.BlockSpec((tm, LANES), lambda i: (i, 0))],
        out_shape=[jax.ShapeDtypeStruct((n, D_MODEL), BF16), jax.ShapeDtypeStruct((n, LANES), F32)],
        compiler_params=_cparams("parallel"),
        name="ln_ab",
    )(x2, ln_g, ln_b, w_ab, b_ab)


def _inproj_body(h_ref, wa_ref, wb_ref, bias_ref, o_ref, w_scr):
    j = pl.program_id(0)

    @pl.when((pl.program_id(1) == 0) & (j < N_FRONT_BLOCKS))
    def _():
        w_scr[...] = wa_ref[...].astype(BF16)

    @pl.when((pl.program_id(1) == 0) & (j >= N_FRONT_BLOCKS))
    def _():
        w_scr[...] = wb_ref[...].astype(BF16)

    o_ref[...] = (jnp.dot(h_ref[...], w_scr[...], preferred_element_type=F32) + bias_ref[...]).astype(o_ref.dtype)


def _inproj(h, w_in, w_gates, bias, *, tm=512):
    n = h.shape[0]
    return pl.pallas_call(
        _inproj_body,
        grid=(N_PROJ_BLOCKS, n // tm),
        in_specs=[
            pl.BlockSpec((tm, D_MODEL), lambda j, i: (i, 0)),
            pl.BlockSpec((D_MODEL, D_MODEL), lambda j, i: (0, jnp.minimum(j, N_FRONT_BLOCKS - 1))),
            pl.BlockSpec((D_MODEL, D_MODEL), lambda j, i: (0, jnp.maximum(j - N_FRONT_BLOCKS, 0))),
            pl.BlockSpec((1, D_MODEL), lambda j, i: (0, j)),
        ],
        out_specs=pl.BlockSpec((tm, D_MODEL), lambda j, i: (i, j)),
        out_shape=jax.ShapeDtypeStruct((n, N_PROJ_BLOCKS * D_MODEL), BF16),
        scratch_shapes=[pltpu.VMEM((D_MODEL, D_MODEL), BF16)],
        compiler_params=_cparams("arbitrary", "arbitrary"),
        name="inproj",
    )(h, w_in, w_gates, bias)


def _gelu(x):
    return 0.5 * x * (1.0 + lax.erf(x * (1.0 / math.sqrt(2.0))))


def _gmlp_body(u_ref, v_ref, lng_ref, lnb_ref, ws_ref, bs_ref, o_ref, *, nblk):
    u = _gelu(u_ref[...].astype(F32))
    v = _layer_norm(_gelu(v_ref[...].astype(F32)), lng_ref[...], lnb_ref[...]).astype(BF16)
    row_chunk = lax.broadcasted_iota(jnp.int32, (GM_BLOCK, GM_BLOCK), 0) // GM_CHUNK
    col_chunk = lax.broadcasted_iota(jnp.int32, (GM_BLOCK, GM_BLOCK), 1) // GM_CHUNK
    causal = col_chunk <= row_chunk
    for g in range(GM_GROUPS):
        cols = slice(g * LANES, (g + 1) * LANES)
        w = jnp.where(causal, ws_ref[g], 0.0).astype(BF16)
        for blk in range(nblk):
            rows = slice(blk * GM_BLOCK, (blk + 1) * GM_BLOCK)
            s = jnp.dot(w, v[rows, cols], preferred_element_type=F32) + bs_ref[g]
            o_ref[rows, cols] = (u[rows, cols] * s).astype(o_ref.dtype)


def _gmlp(proj, ln_g, ln_b, w_s, b_s_full, *, nblk=2):
    n = proj.shape[0]
    rows = nblk * GM_BLOCK
    return pl.pallas_call(
        functools.partial(_gmlp_body, nblk=nblk),
        grid=(n // rows,),
        in_specs=[
            pl.BlockSpec((rows, D_MODEL), lambda i: (i, COL_U)),
            pl.BlockSpec((rows, D_MODEL), lambda i: (i, COL_V)),
            pl.BlockSpec((1, D_MODEL), lambda i: (0, 0)),
            pl.BlockSpec((1, D_MODEL), lambda i: (0, 0)),
            pl.BlockSpec((GM_GROUPS, GM_BLOCK, GM_BLOCK), lambda i: (0, 0, 0)),
            pl.BlockSpec((GM_GROUPS, GM_BLOCK, LANES), lambda i: (0, 0, 0)),
        ],
        out_specs=pl.BlockSpec((rows, D_MODEL), lambda i: (i, 0)),
        out_shape=jax.ShapeDtypeStruct((n, D_MODEL), BF16),
        compiler_params=_cparams("parallel"),
        name="gmlp",
    )(proj, proj, ln_g, ln_b, w_s, b_s_full)


def _cumsum_rows(x):
    n = x.shape[0]
    row = lax.broadcasted_iota(jnp.int32, x.shape, 0)
    shift = 1
    while shift < n:
        x = x + jnp.where(row >= shift, pltpu.roll(x, shift, 0), 0.0)
        shift *= 2
    return x


def _deltanet_body(q_ref, k_ref, v_ref, z_ref, ab_ref, cw_ref, alog_ref, dtb_ref, nw_ref, o_ref,
                   xs_ref, qkv_ref, state_ref, kq_ref, m_ref, r0_ref, qd_ref, kd_ref, cd_ref, zg_ref, *, nb):
    c = DN_CHUNK
    w3 = 3 * D_MODEL
    heads = range(DN_HEADS)
    units = range(nb * DN_HEADS)
    hsl = lambda base, h: slice(base + h * DN_DK, base + (h + 1) * DN_DK)

    @pl.when(pl.program_id(1) == 0)
    def _():
        xs_ref[:, 0:CARRY_ROWS, :] = jnp.zeros((nb, CARRY_ROWS, w3), BF16)
        for ref in (qkv_ref, state_ref, kq_ref, m_ref, r0_ref, qd_ref, kd_ref, cd_ref, zg_ref):
            ref[...] = jnp.zeros_like(ref)

    kq_in = [kq_ref[u] for u in units]
    kq = [lax.dot_general(kq_in[u], kq_in[u], (((1,), (1,)), ((), ())), preferred_element_type=F32)
          for u in units]
    pm = [kq[u] * m_ref[u] for u in units]
    attn = [x[c:, 0:c].astype(BF16) for x in pm]

    lane = lax.broadcasted_iota(jnp.int32, (c, 2 * c), 1)
    right = lane >= c
    eye_right = jnp.where(lane - c == lax.broadcasted_iota(jnp.int32, (c, 2 * c), 0), 1.0, 0.0)
    pa = [x[0:c] + eye_right for x in pm]
    span = 1
    while span < c:
        pab = [x.astype(BF16) for x in pa]
        pa = [jnp.dot(pab[u][:, 0:c], pab[u], preferred_element_type=F32) + jnp.where(right, pa[u], 0.0)
              for u in units]
        span *= 2
    r = [jnp.dot(pa[u].astype(BF16), r0_ref[u], preferred_element_type=F32) for u in units]

    s_old = [state_ref[u] for u in units]
    ws = [jnp.dot(jnp.concatenate([r[u][:, DN_DK:].astype(BF16), qd_ref[u]], axis=0),
                  s_old[u].astype(BF16), preferred_element_type=F32) for u in units]
    v_new = [(r[u][:, 0:DN_DK] - ws[u][0:c]).astype(BF16) for u in units]
    o = [ws[u][c:] + jnp.dot(attn[u], v_new[u], preferred_element_type=F32) for u in units]
    ds = [lax.dot_general(kd_ref[u], v_new[u], (((0,), (0,)), ((), ())), preferred_element_type=F32)
          for u in units]
    for u in units:
        state_ref[u] = s_old[u] * cd_ref[u][0:1, :] + ds[u]
    for b in range(nb):
        for h in heads:
            u = b * DN_HEADS + h
            on = o[u] * lax.rsqrt(jnp.mean(o[u] * o[u], -1, keepdims=True) + RMS_EPS)
            o_ref[b, :, hsl(0, h)] = (on * zg_ref[b, :, hsl(0, h)]).astype(o_ref.dtype)

    ri = lax.broadcasted_iota(jnp.int32, (c, c), 0)
    ci = lax.broadcasted_iota(jnp.int32, (c, c), 1)
    causal = ri >= ci
    strict = ri > ci
    for b in range(nb):
        qkv = qkv_ref[b]
        ab = ab_ref[b]
        xg = ab + dtb_ref[...]
        softplus = jnp.maximum(xg, 0.0) + jnp.log1p(jnp.exp(-jnp.abs(xg)))
        gcum = _cumsum_rows(-jnp.exp(alog_ref[...]) * softplus)
        beta_all = jax.nn.sigmoid(ab)
        gcum_sq = gcum if c == LANES else jnp.concatenate([gcum, jnp.zeros((LANES - c, LANES), F32)], axis=0)
        gcum_t = gcum_sq.T

        z = z_ref[b].astype(F32)
        zg_ref[b] = z * jax.nn.sigmoid(z) * jnp.concatenate([nw_ref[...]] * DN_HEADS, axis=1)

        for h in heads:
            u = b * DN_HEADS + h
            qh, kh, vh = qkv[:, hsl(0, h)], qkv[:, hsl(D_MODEL, h)], qkv[:, hsl(2 * D_MODEL, h)]
            qh = qh * (lax.rsqrt(jnp.sum(qh * qh, -1, keepdims=True) + L2_EPS) * (DN_DK ** -0.5))
            kh = kh * lax.rsqrt(jnp.sum(kh * kh, -1, keepdims=True) + L2_EPS)
            g_col = gcum[:, h:h + 1]
            beta = beta_all[:, DN_HEADS + h:DN_HEADS + h + 1]
            g_last = gcum[c - 1:c, h:h + 1]
            decay = jnp.where(causal, jnp.exp(jnp.where(causal, g_col - gcum_t[h:h + 1, 0:c], 0.0)), 0.0)
            eg = jnp.exp(g_col)
            kq_ref[u, 0:c, :] = kh.astype(BF16)
            kq_ref[u, c:, :] = qh.astype(BF16)
            m_ref[u, 0:c, 0:c] = jnp.where(strict, -beta * decay, 0.0)
            m_ref[u, c:, 0:c] = decay
            r0_ref[u, c:, 0:DN_DK] = (vh * beta).astype(BF16)
            r0_ref[u, c:, DN_DK:] = (kh * (beta * eg)).astype(BF16)
            qd_ref[u] = (qh * eg).astype(BF16)
            kd_ref[u] = (kh * jnp.exp(g_last - g_col)).astype(BF16)
            cd_ref[u] = jnp.broadcast_to(jnp.exp(g_last), (SUBLANES, DN_DK))

    nsh = DN_CONV - 1
    srow = lax.broadcasted_iota(jnp.int32, (nsh * c, CARRY_ROWS + c), 0)
    scol = lax.broadcasted_iota(jnp.int32, (nsh * c, CARRY_ROWS + c), 1)
    shift_mat = jnp.where(scol == srow % c + srow // c + (CARRY_ROWS - nsh), 1.0, 0.0).astype(BF16)
    for b in range(nb):
        xs_ref[b, CARRY_ROWS:, 0:D_MODEL] = q_ref[b]
        xs_ref[b, CARRY_ROWS:, D_MODEL:2 * D_MODEL] = k_ref[b]
        xs_ref[b, CARRY_ROWS:, 2 * D_MODEL:] = v_ref[b]
        window = xs_ref[b]
        shifted = jnp.dot(shift_mat, window, preferred_element_type=F32)
        acc = cw_ref[nsh:DN_CONV, :] * window[CARRY_ROWS:].astype(F32)
        for j in range(nsh):
            acc = acc + cw_ref[j:j + 1, :] * shifted[j * c:(j + 1) * c]
        xs_ref[b, 0:CARRY_ROWS, :] = xs_ref[b, c:c + CARRY_ROWS, :]
        qkv_ref[b] = acc * jax.nn.sigmoid(acc)


def _deltanet(proj, ab, conv_w, alog_row, dtb_row, norm_w, *, bsz, t, nb=2):
    c = DN_CHUNK
    nc = t // c
    nu = nb * DN_HEADS
    proj3 = proj.reshape(bsz, t, PROJ_WIDTH)
    ab3 = ab.reshape(bsz, t, LANES)
    conv_chunk = lambda n: jnp.minimum(n, nc - 1)
    prep_chunk = lambda n: jnp.clip(n - 1, 0, nc - 1)
    out_chunk = lambda n: jnp.maximum(n - 2, 0)
    col = lambda j, chunk: pl.BlockSpec((nb, c, D_MODEL), lambda bp, n: (bp, chunk(n), j))
    yb = pl.pallas_call(
        functools.partial(_deltanet_body, nb=nb),
        grid=(bsz // nb, nc + 2),
        in_specs=[
            col(COL_Q, conv_chunk), col(COL_K, conv_chunk), col(COL_VV, conv_chunk), col(COL_Z, prep_chunk),
            pl.BlockSpec((nb, c, LANES), lambda bp, n: (bp, prep_chunk(n), 0)),
            pl.BlockSpec((DN_CONV, 3 * D_MODEL), lambda bp, n: (0, 0)),
            pl.BlockSpec((1, LANES), lambda bp, n: (0, 0)),
            pl.BlockSpec((1, LANES), lambda bp, n: (0, 0)),
            pl.BlockSpec((1, DN_DK), lambda bp, n: (0, 0)),
        ],
        out_specs=pl.BlockSpec((nb, c, D_MODEL), lambda bp, n: (bp, out_chunk(n), 0)),
        out_shape=jax.ShapeDtypeStruct((bsz, t, D_MODEL), BF16),
        scratch_shapes=[
            pltpu.VMEM((nb, CARRY_ROWS + c, 3 * D_MODEL), BF16),
            pltpu.VMEM((nb, c, 3 * D_MODEL), F32),
            pltpu.VMEM((nu, DN_DK, DN_DK), F32),
            pltpu.VMEM((nu, 2 * c, DN_DK), BF16),
            pltpu.VMEM((nu, 2 * c, 2 * c), F32),
            pltpu.VMEM((nu, 2 * c, 2 * DN_DK), BF16),
            pltpu.VMEM((nu, c, DN_DK), BF16),
            pltpu.VMEM((nu, c, DN_DK), BF16),
            pltpu.VMEM((nu, SUBLANES, DN_DK), F32),
            pltpu.VMEM((nb, c, D_MODEL), F32),
        ],
        compiler_params=_cparams("parallel", "arbitrary"),
        name="deltanet",
    )(proj3, proj3, proj3, proj3, ab3, conv_w, alog_row, dtb_row, norm_w)
    return yb.reshape(bsz * t, D_MODEL)


def _merge_body(x_ref, ya_ref, yb_ref, ga_ref, gb_ref, lng_ref, lnb_ref, wpa_ref, wpb_ref, wo_ref,
                l1g_ref, l1b_ref, wr_ref, wrlo_ref, br_ref, h1_ref, logit_ref):
    h = _layer_norm(x_ref[...], lng_ref[...], lnb_ref[...])
    pa = jnp.dot(ya_ref[...], wpa_ref[...], preferred_element_type=F32)
    pb = jnp.dot(yb_ref[...], wpb_ref[...], preferred_element_type=F32)
    merged = jax.nn.sigmoid(ga_ref[...].astype(F32)) * pa + jax.nn.sigmoid(gb_ref[...].astype(F32)) * pb
    mix = jnp.dot(merged.astype(BF16), wo_ref[...], preferred_element_type=F32)
    h1 = _layer_norm(DEEPNORM_ALPHA * h + mix, l1g_ref[...], l1b_ref[...])
    _store_row_tiles(h1_ref, h1)
    tm = h1.shape[0]
    hi = h1.astype(BF16)
    lo = (h1 - hi.astype(F32)).astype(BF16)
    hw = jnp.dot(jnp.concatenate([hi, lo], axis=0), wr_ref[...], preferred_element_type=F32)
    logit_ref[...] = (hw[0:tm] + hw[tm:] + jnp.dot(hi, wrlo_ref[...], preferred_element_type=F32)
                      + br_ref[...])


def _merge(x2, ya, yb, proj, ln_g, ln_b, w_pa, w_pb, w_o, l1g, l1b, w_r, w_r_lo, b_r, *, tm=256):
    n = x2.shape[0]
    vec = lambda: pl.BlockSpec((1, D_MODEL), lambda i: (0, 0))
    mat = lambda: pl.BlockSpec((D_MODEL, D_MODEL), lambda i: (0, 0))
    return pl.pallas_call(
        _merge_body,
        grid=(n // tm,),
        in_specs=[
            pl.BlockSpec((tm, D_MODEL), lambda i: (i, 0)),
            pl.BlockSpec((tm, D_MODEL), lambda i: (i, 0)),
            pl.BlockSpec((tm, D_MODEL), lambda i: (i, 0)),
            pl.BlockSpec((tm, D_MODEL), lambda i: (i, COL_GA)),
            pl.BlockSpec((tm, D_MODEL), lambda i: (i, COL_GB)),
            vec(), vec(), mat(), mat(), mat(), vec(), vec(),
            pl.BlockSpec((D_MODEL, LANES), lambda i: (0, 0)),
            pl.BlockSpec((D_MODEL, LANES), lambda i: (0, 0)),
            pl.BlockSpec((1, LANES), lambda i: (0, 0)),
        ],
        out_specs=[
            pl.BlockSpec((tm * ROW_CHUNKS, LANES), lambda i: (i, 0)),
            pl.BlockSpec((tm, LANES), lambda i: (i, 0)),
        ],
        out_shape=[
            jax.ShapeDtypeStruct((n * ROW_CHUNKS, LANES), F32),
            jax.ShapeDtypeStruct((n, LANES), F32),
        ],
        compiler_params=_cparams("parallel"),
        name="merge",
    )(x2, ya, yb, proj, proj, ln_g, ln_b, w_pa, w_pb, w_o, l1g, l1b, w_r, w_r_lo, b_r)


def _route_body(logit_ref, gate_ref, idx_ref, cnt_ref, carry_ref, *, tm):
    @pl.when(pl.program_id(0) == 0)
    def _():
        carry_ref[...] = jnp.zeros_like(carry_ref)

    lg = logit_ref[...]
    lane_i = lax.broadcasted_iota(jnp.int32, lg.shape, 1)
    lane = lane_i.astype(F32)
    neg = jnp.float32(-jnp.inf)
    big = jnp.float32(1 << 20)

    is_grp = lane_i < MOE_GROUPS
    gl = jnp.where(is_grp, lg, neg)
    gmax = jnp.max(gl, -1, keepdims=True)
    grp = jnp.min(jnp.where(is_grp & (gl == gmax), lane, big), -1, keepdims=True)
    p_grp = 1.0 / jnp.sum(jnp.where(is_grp, jnp.exp(gl - gmax), 0.0), -1, keepdims=True)

    elane = lane - MOE_GROUPS
    in_grp = (elane >= grp * MOE_EPG) & (elane < (grp + 1) * MOE_EPG)
    el = jnp.where(in_grp, lg, neg)
    m1 = jnp.max(el, -1, keepdims=True)
    e1 = jnp.min(jnp.where(in_grp & (el == m1), elane, big), -1, keepdims=True)
    rest = in_grp & (elane != e1)
    el2 = jnp.where(rest, lg, neg)
    m2 = jnp.max(el2, -1, keepdims=True)
    e2 = jnp.min(jnp.where(rest & (el2 == m2), elane, big), -1, keepdims=True)
    t2 = jnp.exp(m2 - m1)
    g1 = p_grp * (1.0 / (1.0 + t2))
    g2 = p_grp * (t2 / (1.0 + t2))

    oh1 = lane == e1
    oh2 = lane == e2
    onehot = jnp.where(oh1 | oh2, 1.0, 0.0)
    ri = lax.broadcasted_iota(jnp.int32, (tm, tm), 0)
    ci = lax.broadcasted_iota(jnp.int32, (tm, tm), 1)
    tri = jnp.where(ci < ri, 1.0, 0.0).astype(BF16)
    before = jnp.dot(tri, onehot.astype(BF16), preferred_element_type=F32) + carry_ref[0:1, :]
    r1 = jnp.sum(jnp.where(oh1, before, 0.0), -1, keepdims=True)
    r2 = jnp.sum(jnp.where(oh2, before, 0.0), -1, keepdims=True)
    total = carry_ref[0:1, :] + jnp.sum(onehot, 0, keepdims=True)
    carry_ref[...] = jnp.broadcast_to(total, carry_ref.shape)
    cnt_ref[...] = jnp.broadcast_to(total, cnt_ref.shape).astype(jnp.int32)

    gate_ref[...] = jnp.where(lane_i == 0, g1, jnp.where(lane_i == 1, g2, 0.0))
    idx_ref[...] = jnp.where(lane_i == 0, e1, jnp.where(lane_i == 1, e2,
                             jnp.where(lane_i == 2, r1, jnp.where(lane_i == 3, r2, 0.0)))).astype(jnp.int32)


def _route(logits, *, tm=256):
    n = logits.shape[0]
    return pl.pallas_call(
        functools.partial(_route_body, tm=tm),
        grid=(n // tm,),
        in_specs=[pl.BlockSpec((tm, LANES), lambda i: (i, 0))],
        out_specs=[
            pl.BlockSpec((tm, LANES), lambda i: (i, 0)),
            pl.BlockSpec((tm, LANES), lambda i: (i, 0)),
            pl.BlockSpec((8, LANES), lambda i: (0, 0)),
        ],
        out_shape=[
            jax.ShapeDtypeStruct((n, LANES), F32),
            jax.ShapeDtypeStruct((n, LANES), jnp.int32),
            jax.ShapeDtypeStruct((8, LANES), jnp.int32),
        ],
        scratch_shapes=[pltpu.VMEM((8, LANES), F32)],
        compiler_params=_cparams("arbitrary"),
        name="route",
    )(logits)


def _dest_body(idx_ref, ps_ref, o_ref):
    idx = idx_ref[...]
    lane = lax.broadcasted_iota(jnp.int32, idx.shape, 1)
    ps = ps_ref[...].astype(F32)
    d1 = jnp.sum(jnp.where(lane == idx[:, 0:1], ps, 0.0), -1, keepdims=True).astype(jnp.int32) + idx[:, 2:3]
    d2 = jnp.sum(jnp.where(lane == idx[:, 1:2], ps, 0.0), -1, keepdims=True).astype(jnp.int32) + idx[:, 3:4]
    o_ref[...] = jnp.where(lane == 0, d1, jnp.where(lane == 1, d2, 0))


def _dest(idx, pad_start_row, *, tm=1024):
    n = idx.shape[0]
    return pl.pallas_call(
        _dest_body,
        grid=(n // tm,),
        in_specs=[pl.BlockSpec((tm, LANES), lambda i: (i, 0)), pl.BlockSpec((1, LANES), lambda i: (0, 0))],
        out_specs=pl.BlockSpec((tm, LANES), lambda i: (i, 0)),
        out_shape=jax.ShapeDtypeStruct((n, LANES), jnp.int32),
        compiler_params=_cparams("parallel"),
        name="dest",
    )(idx, pad_start_row)


DMA_GROUP = 8


def _row_tile(ref, row):
    return ref.at[pl.ds(pl.multiple_of(row * SUBLANES, SUBLANES), SUBLANES)]


def _dispatch_body(dest_ref, h_ref, xin_ref, xpad_ref, sem, *, tm):
    del xin_ref

    def row_copy(src_row, dst_row):
        return pltpu.make_async_copy(_row_tile(h_ref, src_row), _row_tile(xpad_ref, dst_row), sem)

    def start(g, carry):
        r0 = g * DMA_GROUP
        dst = [dest_ref[0, 0, 2 * r0 + j] for j in range(2 * DMA_GROUP)]
        for j in range(2 * DMA_GROUP):
            row_copy(r0 + j // 2, dst[j]).start()
        return carry

    def wait(g, carry):
        for _ in range(2 * DMA_GROUP):
            row_copy(0, 0).wait()
        return carry

    lax.fori_loop(0, tm // DMA_GROUP, start, 0)
    lax.fori_loop(0, tm // DMA_GROUP, wait, 0)


def _dispatch(dest_smem, h1_rows, xpad0, *, tm):
    n = h1_rows.shape[0] // ROW_CHUNKS
    return pl.pallas_call(
        functools.partial(_dispatch_body, tm=tm),
        grid=(n // tm,),
        in_specs=[
            pl.BlockSpec((1, 1, 2 * tm), lambda i: (i, 0, 0), memory_space=pltpu.SMEM),
            pl.BlockSpec((tm * ROW_CHUNKS, LANES), lambda i: (i, 0)),
            pl.BlockSpec(memory_space=pl.ANY),
        ],
        out_specs=pl.BlockSpec(memory_space=pl.ANY),
        scratch_shapes=[pltpu.SemaphoreType.DMA(())],
        out_shape=jax.ShapeDtypeStruct(xpad0.shape, xpad0.dtype),
        input_output_aliases={2: 0},
        compiler_params=_cparams("arbitrary"),
        name="dispatch",
    )(dest_smem, h1_rows, xpad0)


def _experts_body(blk_e_ref, nused_ref, x_ref, w1_ref, w3_ref, w2_ref, y_ref, w1b, w3b, w2b):
    i = pl.program_id(0)
    prev = blk_e_ref[jnp.maximum(i - 1, 0)]

    @pl.when((i == 0) | (blk_e_ref[i] != prev))
    def _():
        w1b[...] = w1_ref[0].astype(BF16)
        w3b[...] = w3_ref[0].astype(BF16)
        w2b[...] = w2_ref[0].astype(BF16)

    @pl.when(i < nused_ref[0])
    def _():
        xb = _load_row_tiles(x_ref, EXPERT_ROWS).astype(BF16)
        a = jnp.dot(xb, w1b[...], preferred_element_type=F32)
        b = jnp.dot(xb, w3b[...], preferred_element_type=F32)
        hb = (a * jax.nn.sigmoid(a)) * b
        _store_row_tiles(y_ref, jnp.dot(hb.astype(BF16), w2b[...], preferred_element_type=F32))

    @pl.when(i >= nused_ref[0])
    def _():
        y_ref[...] = jnp.zeros_like(y_ref)


def _experts(blk_e, n_used, xpad, w1, w3, w2):
    tm = EXPERT_ROWS * ROW_CHUNKS
    return pl.pallas_call(
        _experts_body,
        grid_spec=pltpu.PrefetchScalarGridSpec(
            num_scalar_prefetch=2,
            grid=(xpad.shape[0] // tm,),
            in_specs=[
                pl.BlockSpec((tm, LANES), lambda i, be, nu: (i, 0)),
                pl.BlockSpec((1, D_MODEL, D_EXPERT), lambda i, be, nu: (be[i], 0, 0)),
                pl.BlockSpec((1, D_MODEL, D_EXPERT), lambda i, be, nu: (be[i], 0, 0)),
                pl.BlockSpec((1, D_EXPERT, D_MODEL), lambda i, be, nu: (be[i], 0, 0)),
            ],
            out_specs=pl.BlockSpec((tm, LANES), lambda i, be, nu: (i, 0)),
            scratch_shapes=[
                pltpu.VMEM((D_MODEL, D_EXPERT), BF16),
                pltpu.VMEM((D_MODEL, D_EXPERT), BF16),
                pltpu.VMEM((D_EXPERT, D_MODEL), BF16),
            ],
        ),
        out_shape=jax.ShapeDtypeStruct(xpad.shape, F32),
        compiler_params=_cparams("arbitrary"),
        name="experts",
    )(blk_e, n_used, xpad, w1, w3, w2)


def _combine_body(dest_ref, dest_next_ref, h1_ref, gate_ref, l2g_ref, l2b_ref, ypad_ref, o_ref, buf, sem, *, tm):
    i = pl.program_id(0)
    half = i % 2

    def row_copy(src_row, r, slot, hf):
        return pltpu.make_async_copy(_row_tile(ypad_ref, src_row), _row_tile(buf.at[2 * hf + slot], r), sem.at[hf])

    def gather(idx_ref, hf):
        def start(g, carry):
            r0 = g * DMA_GROUP
            src = [idx_ref[0, 0, 2 * r0 + j] for j in range(2 * DMA_GROUP)]
            for j in range(2 * DMA_GROUP):
                row_copy(src[j], r0 + j // 2, j % 2, hf).start()
            return carry
        lax.fori_loop(0, tm // DMA_GROUP, start, 0)

    @pl.when(i == 0)
    def _():
        gather(dest_ref, 0)

    @pl.when(i + 1 < pl.num_programs(0))
    def _():
        gather(dest_next_ref, 1 - half)

    def wait(g, carry):
        for j in range(2 * DMA_GROUP):
            row_copy(0, 0, j % 2, half).wait()
        return carry

    lax.fori_loop(0, tm // DMA_GROUP, wait, 0)
    gate = gate_ref[...]
    ffn = (_load_row_tiles(buf.at[2 * half], tm) * gate[:, 0:1]
           + _load_row_tiles(buf.at[2 * half + 1], tm) * gate[:, 1:2])
    h1 = _load_row_tiles(h1_ref, tm)
    o_ref[...] = _layer_norm(DEEPNORM_ALPHA * h1 + ffn, l2g_ref[...], l2b_ref[...])


def _combine(dest_smem, h1_rows, gates, l2g, l2b, ypad, *, tm):
    n = h1_rows.shape[0] // ROW_CHUNKS
    steps = n // tm
    return pl.pallas_call(
        functools.partial(_combine_body, tm=tm),
        grid=(steps,),
        in_specs=[
            pl.BlockSpec((1, 1, 2 * tm), lambda i: (i, 0, 0), memory_space=pltpu.SMEM),
            pl.BlockSpec((1, 1, 2 * tm), lambda i: (jnp.minimum(i + 1, steps - 1), 0, 0), memory_space=pltpu.SMEM),
            pl.BlockSpec((tm * ROW_CHUNKS, LANES), lambda i: (i, 0)),
            pl.BlockSpec((tm, LANES), lambda i: (i, 0)),
            pl.BlockSpec((1, D_MODEL), lambda i: (0, 0)),
            pl.BlockSpec((1, D_MODEL), lambda i: (0, 0)),
            pl.BlockSpec(memory_space=pl.ANY),
        ],
        out_specs=pl.BlockSpec((tm, D_MODEL), lambda i: (i, 0)),
        scratch_shapes=[
            pltpu.VMEM((4, tm * ROW_CHUNKS, LANES), F32),
            pltpu.SemaphoreType.DMA((2,)),
        ],
        out_shape=jax.ShapeDtypeStruct((n, D_MODEL), F32),
        compiler_params=_cparams("arbitrary"),
        name="combine",
    )(dest_smem, dest_smem, h1_rows, gates, l2g, l2b, ypad)


def _row(v, width=None):
    v = v.reshape(1, -1).astype(F32)
    if width is not None and v.shape[1] < width:
        v = jnp.pad(v, ((0, 0), (0, width - v.shape[1])))
    return v


def kernel(x, ln_in_g, ln_in_b, w_in, b_in, gm_ln_g, gm_ln_b, gm_w_s, gm_b_s, dn_conv_w, dn_a_log,
           dn_dt_bias, dn_norm_w, w_pa, w_pb, w_o, ln1_g, ln1_b, w_rg, b_rg, w_re, b_re, w1, w3, w2,
           ln2_g, ln2_b):
    bsz, t, d = x.shape
    n = bsz * t
    x2 = x.reshape(n, d)
    l = 0

    wi, bi = w_in[l], b_in[l]
    c_ab = N_FRONT_BLOCKS * D_MODEL
    c_gate = c_ab + 2 * DN_HEADS
    w_ab = jnp.pad(wi[:, c_ab:c_gate], ((0, 0), (0, LANES - 2 * DN_HEADS))).astype(BF16)
    b_ab = _row(bi[c_ab:c_gate], LANES)
    w_gates = wi[:, c_gate:]
    b_proj = jnp.concatenate([bi[:c_ab], bi[c_gate:]]).reshape(1, PROJ_WIDTH)

    ln_g, ln_b = _row(ln_in_g), _row(ln_in_b)
    h, ab = _ln_ab(x2, ln_g, ln_b, w_ab, b_ab)
    proj = _inproj(h, wi, w_gates, b_proj)

    bs_full = jnp.broadcast_to(gm_b_s[l][:, :, None], (GM_GROUPS, GM_BLOCK, LANES))
    ya = _gmlp(proj, _row(gm_ln_g[l]), _row(gm_ln_b[l]), gm_w_s[l], bs_full)

    yb = _deltanet(proj, ab, dn_conv_w[l], _row(dn_a_log[l], LANES), _row(dn_dt_bias[l], LANES),
                   _row(dn_norm_w[l]), bsz=bsz, t=t)

    w_r = jnp.concatenate([w_rg[l], w_re[l], jnp.zeros((d, LANES - MOE_GROUPS - N_EXPERTS), F32)], axis=1)
    w_r_hi = w_r.astype(BF16)
    w_r_lo = (w_r - w_r_hi.astype(F32)).astype(BF16)
    b_r = _row(jnp.concatenate([b_rg[l], b_re[l]]), LANES)
    h1, logits = _merge(x2, ya, yb, proj, ln_g, ln_b, w_pa[l].astype(BF16), w_pb[l].astype(BF16),
                        w_o[l].astype(BF16), _row(ln1_g[l]), _row(ln1_b[l]), w_r_hi, w_r_lo, b_r)

    gates, idx, counts = _route(logits)

    tm_e = EXPERT_ROWS
    cnt = counts[0, :N_EXPERTS]
    padded = (cnt + tm_e - 1) // tm_e * tm_e
    pad_end = jnp.cumsum(padded)
    pad_start = (pad_end - padded).astype(jnp.int32)
    n_blocks = (2 * n + N_EXPERTS * (tm_e - 1) + tm_e - 1) // tm_e
    blk_start = jnp.arange(n_blocks, dtype=jnp.int32) * tm_e
    blk_e = jnp.minimum(jnp.sum(pad_end[None, :] <= blk_start[:, None], -1), N_EXPERTS - 1).astype(jnp.int32)
    n_used = (pad_end[-1] // tm_e).astype(jnp.int32).reshape(1)

    ps_row = jnp.pad(pad_start, (0, LANES - N_EXPERTS)).reshape(1, LANES)
    dest = _dest(idx, ps_row, tm=min(1024, n))[:, :2]
    tm_d, tm_c = min(1024, n), 256
    xpad0 = jnp.zeros((n_blocks * tm_e * ROW_CHUNKS, LANES), F32)
    xpad = _dispatch(dest.reshape(n // tm_d, 1, 2 * tm_d), h1, xpad0, tm=tm_d)
    ypad = _experts(blk_e, n_used, xpad, w1[l], w3[l], w2[l])
    out = _combine(dest.reshape(n // tm_c, 1, 2 * tm_c), h1, gates, _row(ln2_g[l]), _row(ln2_b[l]), ypad,
                   tm=tm_c)
    return out.reshape(bsz, t, d)
```

```python
import functools
import math

import jax
import jax.numpy as jnp
from jax import lax
from jax.experimental import pallas as pl
from jax.experimental.pallas import tpu as pltpu

D_MODEL = 1024
GM_GROUPS = 8
GM_BLOCK = 128
GM_CHUNK = 64
DN_HEADS = 8
DN_DK = 128
DN_CONV = 4
MOE_GROUPS = 4
MOE_EPG = 8
N_EXPERTS = MOE_GROUPS * MOE_EPG
D_EXPERT = D_MODEL // 2
LN_EPS = 1e-5
RMS_EPS = 1e-6
L2_EPS = 1e-6
DEEPNORM_ALPHA = 2.0 ** 0.25

LANES = 128
COL_U, COL_V, COL_Q, COL_K, COL_VV, COL_Z, COL_GA, COL_GB = range(8)
N_PROJ_BLOCKS = 8
N_FRONT_BLOCKS = 6
PROJ_WIDTH = N_PROJ_BLOCKS * D_MODEL

DN_CHUNK = 64
CARRY_ROWS = 16
EXPERT_ROWS = 256
VMEM_LIMIT = 56 * 1024 * 1024
SUBLANES = 8
ROW_CHUNKS = D_MODEL // LANES
assert ROW_CHUNKS == SUBLANES

F32 = jnp.float32
BF16 = jnp.bfloat16


def _cparams(*sem):
    return pltpu.CompilerParams(dimension_semantics=sem, vmem_limit_bytes=VMEM_LIMIT)


def _store_row_tiles(ref, val):
    m = val.shape[0]
    for ch in range(ROW_CHUNKS):
        ref[pl.ds(ch, m, stride=ROW_CHUNKS), :] = val[:, ch * LANES:(ch + 1) * LANES]


def _load_row_tiles(ref, m):
    return jnp.concatenate([ref[pl.ds(ch, m, stride=ROW_CHUNKS), :] for ch in range(ROW_CHUNKS)], axis=1)


def _layer_norm(x, g, b):
    mu = jnp.mean(x, -1, keepdims=True)
    xc = x - mu
    var = jnp.mean(xc * xc, -1, keepdims=True)
    return xc * lax.rsqrt(var + LN_EPS) * g + b


def _dot(a, b):
    return jnp.dot(a.astype(BF16), b.astype(BF16), preferred_element_type=F32)


def _dot_nt(a, b):
    return lax.dot_general(a.astype(BF16), b.astype(BF16), (((1,), (1,)), ((), ())),
                           preferred_element_type=F32)


def _ln_ab_body(x_ref, g_ref, b_ref, wab_ref, bab_ref, h_ref, ab_ref):
    h = _layer_norm(x_ref[...], g_ref[...], b_ref[...]).astype(BF16)
    h_ref[...] = h
    ab_ref[...] = jnp.dot(h, wab_ref[...], preferred_element_type=F32) + bab_ref[...]


def _ln_ab(x2, ln_g, ln_b, w_ab, b_ab, *, tm=512):
    n = x2.shape[0]
    return pl.pallas_call(
        _ln_ab_body,
        grid=(n // tm,),
        in_specs=[
            pl.BlockSpec((tm, D_MODEL), lambda i: (i, 0)),
            pl.BlockSpec((1, D_MODEL), lambda i: (0, 0)),
            pl.BlockSpec((1, D_MODEL), lambda i: (0, 0)),
            pl.BlockSpec((D_MODEL, LANES), lambda i: (0, 0)),
            pl.BlockSpec((1, LANES), lambda i: (0, 0)),
        ],
        out_specs=[pl.BlockSpec((tm, D_MODEL), lambda i: (i, 0)), pl.BlockSpec((tm, LANES), lambda i: (i, 0))],
        out_shape=[jax.ShapeDtypeStruct((n, D_MODEL), BF16), jax.ShapeDtypeStruct((n, LANES), F32)],
        compiler_params=_cparams("parallel"),
        name="ln_ab",
    )(x2, ln_g, ln_b, w_ab, b_ab)


def _inproj_body(h_ref, wa_ref, wb_ref, bias_ref, o_ref, w_scr):
    j = pl.program_id(0)

    @pl.when((pl.program_id(1) == 0) & (j < N_FRONT_BLOCKS))
    def _():
        w_scr[...] = wa_ref[...].astype(BF16)

    @pl.when((pl.program_id(1) == 0) & (j >= N_FRONT_BLOCKS))
    def _():
        w_scr[...] = wb_ref[...].astype(BF16)

    o_ref[...] = (jnp.dot(h_ref[...], w_scr[...], preferred_element_type=F32) + bias_ref[...]).astype(o_ref.dtype)


def _inproj(h, w_in, w_gates, bias, *, tm=1024):
    n = h.shape[0]
    return pl.pallas_call(
        _inproj_body,
        grid=(N_PROJ_BLOCKS, n // tm),
        in_specs=[
            pl.BlockSpec((tm, D_MODEL), lambda j, i: (i, 0)),
            pl.BlockSpec((D_MODEL, D_MODEL), lambda j, i: (0, jnp.minimum(j, N_FRONT_BLOCKS - 1))),
            pl.BlockSpec((D_MODEL, D_MODEL), lambda j, i: (0, jnp.maximum(j - N_FRONT_BLOCKS, 0))),
            pl.BlockSpec((1, D_MODEL), lambda j, i: (0, j)),
        ],
        out_specs=pl.BlockSpec((tm, D_MODEL), lambda j, i: (i, j)),
        out_shape=jax.ShapeDtypeStruct((n, N_PROJ_BLOCKS * D_MODEL), BF16),
        scratch_shapes=[pltpu.VMEM((D_MODEL, D_MODEL), BF16)],
        compiler_params=_cparams("arbitrary", "arbitrary"),
        name="inproj",
    )(h, w_in, w_gates, bias)


def _gelu(x):
    return 0.5 * x * (1.0 + lax.erf(x * (1.0 / math.sqrt(2.0))))


def _gmlp_body(u_ref, v_ref, lng_ref, lnb_ref, ws_ref, bs_ref, o_ref, *, nblk):
    u = _gelu(u_ref[...].astype(F32))
    v = _layer_norm(_gelu(v_ref[...].astype(F32)), lng_ref[...], lnb_ref[...]).astype(BF16)
    row_chunk = lax.broadcasted_iota(jnp.int32, (GM_BLOCK, GM_BLOCK), 0) // GM_CHUNK
    col_chunk = lax.broadcasted_iota(jnp.int32, (GM_BLOCK, GM_BLOCK), 1) // GM_CHUNK
    causal = col_chunk <= row_chunk
    for g in range(GM_GROUPS):
        cols = slice(g * LANES, (g + 1) * LANES)
        w = jnp.where(causal, ws_ref[g], 0.0).astype(BF16)
        for blk in range(nblk):
            rows = slice(blk * GM_BLOCK, (blk + 1) * GM_BLOCK)
            s = jnp.dot(w, v[rows, cols], preferred_element_type=F32) + bs_ref[g]
            o_ref[rows, cols] = (u[rows, cols] * s).astype(o_ref.dtype)


def _gmlp(proj, ln_g, ln_b, w_s, b_s_full, *, nblk=2):
    n = proj.shape[0]
    rows = nblk * GM_BLOCK
    return pl.pallas_call(
        functools.partial(_gmlp_body, nblk=nblk),
        grid=(n // rows,),
        in_specs=[
            pl.BlockSpec((rows, D_MODEL), lambda i: (i, COL_U)),
            pl.BlockSpec((rows, D_MODEL), lambda i: (i, COL_V)),
            pl.BlockSpec((1, D_MODEL), lambda i: (0, 0)),
            pl.BlockSpec((1, D_MODEL), lambda i: (0, 0)),
            pl.BlockSpec((GM_GROUPS, GM_BLOCK, GM_BLOCK), lambda i: (0, 0, 0)),
            pl.BlockSpec((GM_GROUPS, GM_BLOCK, LANES), lambda i: (0, 0, 0)),
        ],
        out_specs=pl.BlockSpec((rows, D_MODEL), lambda i: (i, 0)),
        out_shape=jax.ShapeDtypeStruct((n, D_MODEL), BF16),
        compiler_params=_cparams("parallel"),
        name="gmlp",
    )(proj, proj, ln_g, ln_b, w_s, b_s_full)


def _cumsum_rows(x):
    n = x.shape[0]
    row = lax.broadcasted_iota(jnp.int32, x.shape, 0)
    shift = 1
    while shift < n:
        x = x + jnp.where(row >= shift, pltpu.roll(x, shift, 0), 0.0)
        shift *= 2
    return x


def _deltanet_body(q_ref, k_ref, v_ref, z_ref, ab_ref, cw_ref, alog_ref, dtb_ref, nw_ref, o_ref,
                   xs_ref, qkv_ref, state_ref, kq_ref, m_ref, r0_ref, qd_ref, kd_ref, cd_ref, zg_ref, *, nb):
    c = DN_CHUNK
    w3 = 3 * D_MODEL
    heads = range(DN_HEADS)
    units = range(nb * DN_HEADS)
    hsl = lambda base, h: slice(base + h * DN_DK, base + (h + 1) * DN_DK)

    @pl.when(pl.program_id(1) == 0)
    def _():
        xs_ref[:, 0:CARRY_ROWS, :] = jnp.zeros((nb, CARRY_ROWS, w3), BF16)
        for ref in (qkv_ref, state_ref, kq_ref, m_ref, r0_ref, qd_ref, kd_ref, cd_ref, zg_ref):
            ref[...] = jnp.zeros_like(ref)

    kq_in = [kq_ref[u] for u in units]
    kq = [lax.dot_general(kq_in[u], kq_in[u], (((1,), (1,)), ((), ())), preferred_element_type=F32)
          for u in units]
    pm = [kq[u] * m_ref[u] for u in units]
    attn = [x[c:, 0:c].astype(BF16) for x in pm]

    lane = lax.broadcasted_iota(jnp.int32, (c, 2 * c), 1)
    right = lane >= c
    eye_right = jnp.where(lane - c == lax.broadcasted_iota(jnp.int32, (c, 2 * c), 0), 1.0, 0.0)
    pa = [x[0:c] + eye_right for x in pm]
    span = 1
    while span < c:
        pab = [x.astype(BF16) for x in pa]
        pa = [jnp.dot(pab[u][:, 0:c], pab[u], preferred_element_type=F32) + jnp.where(right, pa[u], 0.0)
              for u in units]
        span *= 2
    r = [jnp.dot(pa[u].astype(BF16), r0_ref[u], preferred_element_type=F32) for u in units]

    s_old = [state_ref[u] for u in units]
    ws = [jnp.dot(jnp.concatenate([r[u][:, DN_DK:].astype(BF16), qd_ref[u]], axis=0),
                  s_old[u].astype(BF16), preferred_element_type=F32) for u in units]
    v_new = [(r[u][:, 0:DN_DK] - ws[u][0:c]).astype(BF16) for u in units]
    o = [ws[u][c:] + jnp.dot(attn[u], v_new[u], preferred_element_type=F32) for u in units]
    ds = [lax.dot_general(kd_ref[u], v_new[u], (((0,), (0,)), ((), ())), preferred_element_type=F32)
          for u in units]
    for u in units:
        state_ref[u] = s_old[u] * cd_ref[u][0:1, :] + ds[u]
    for b in range(nb):
        for h in heads:
            u = b * DN_HEADS + h
            on = o[u] * lax.rsqrt(jnp.mean(o[u] * o[u], -1, keepdims=True) + RMS_EPS)
            o_ref[b, :, hsl(0, h)] = (on * zg_ref[b, :, hsl(0, h)]).astype(o_ref.dtype)

    ri = lax.broadcasted_iota(jnp.int32, (c, c), 0)
    ci = lax.broadcasted_iota(jnp.int32, (c, c), 1)
    causal = ri >= ci
    strict = ri > ci
    for b in range(nb):
        qkv = qkv_ref[b]
        ab = ab_ref[b]
        xg = ab + dtb_ref[...]
        softplus = jnp.maximum(xg, 0.0) + jnp.log1p(jnp.exp(-jnp.abs(xg)))
        gcum = _cumsum_rows(-jnp.exp(alog_ref[...]) * softplus)
        beta_all = jax.nn.sigmoid(ab)
        gcum_sq = gcum if c == LANES else jnp.concatenate([gcum, jnp.zeros((LANES - c, LANES), F32)], axis=0)
        gcum_t = gcum_sq.T

        z = z_ref[b].astype(F32)
        zg_ref[b] = z * jax.nn.sigmoid(z) * jnp.concatenate([nw_ref[...]] * DN_HEADS, axis=1)

        for h in heads:
            u = b * DN_HEADS + h
            qh, kh, vh = qkv[:, hsl(0, h)], qkv[:, hsl(D_MODEL, h)], qkv[:, hsl(2 * D_MODEL, h)]
            qh = qh * (lax.rsqrt(jnp.sum(qh * qh, -1, keepdims=True) + L2_EPS) * (DN_DK ** -0.5))
            kh = kh * lax.rsqrt(jnp.sum(kh * kh, -1, keepdims=True) + L2_EPS)
            g_col = gcum[:, h:h + 1]
            beta = beta_all[:, DN_HEADS + h:DN_HEADS + h + 1]
            g_last = gcum[c - 1:c, h:h + 1]
            decay = jnp.where(causal, jnp.exp(jnp.where(causal, g_col - gcum_t[h:h + 1, 0:c], 0.0)), 0.0)
            eg = jnp.exp(g_col)
            kq_ref[u, 0:c, :] = kh.astype(BF16)
            kq_ref[u, c:, :] = qh.astype(BF16)
            m_ref[u, 0:c, 0:c] = jnp.where(strict, -beta * decay, 0.0)
            m_ref[u, c:, 0:c] = decay
            r0_ref[u, c:, 0:DN_DK] = (vh * beta).astype(BF16)
            r0_ref[u, c:, DN_DK:] = (kh * (beta * eg)).astype(BF16)
            qd_ref[u] = (qh * eg).astype(BF16)
            kd_ref[u] = (kh * jnp.exp(g_last - g_col)).astype(BF16)
            cd_ref[u] = jnp.broadcast_to(jnp.exp(g_last), (SUBLANES, DN_DK))

    nsh = DN_CONV - 1
    srow = lax.broadcasted_iota(jnp.int32, (nsh * c, CARRY_ROWS + c), 0)
    scol = lax.broadcasted_iota(jnp.int32, (nsh * c, CARRY_ROWS + c), 1)
    shift_mat = jnp.where(scol == srow % c + srow // c + (CARRY_ROWS - nsh), 1.0, 0.0).astype(BF16)
    for b in range(nb):
        xs_ref[b, CARRY_ROWS:, 0:D_MODEL] = q_ref[b]
        xs_ref[b, CARRY_ROWS:, D_MODEL:2 * D_MODEL] = k_ref[b]
        xs_ref[b, CARRY_ROWS:, 2 * D_MODEL:] = v_ref[b]
        window = xs_ref[b]
        shifted = jnp.dot(shift_mat, window, preferred_element_type=F32)
        acc = cw_ref[nsh:DN_CONV, :] * window[CARRY_ROWS:].astype(F32)
        for j in range(nsh):
            acc = acc + cw_ref[j:j + 1, :] * shifted[j * c:(j + 1) * c]
        xs_ref[b, 0:CARRY_ROWS, :] = xs_ref[b, c:c + CARRY_ROWS, :]
        qkv_ref[b] = acc * jax.nn.sigmoid(acc)


def _deltanet(proj, ab, conv_w, alog_row, dtb_row, norm_w, *, bsz, t, nb=4):
    c = DN_CHUNK
    nc = t // c
    nu = nb * DN_HEADS
    proj3 = proj.reshape(bsz, t, PROJ_WIDTH)
    ab3 = ab.reshape(bsz, t, LANES)
    conv_chunk = lambda n: jnp.minimum(n, nc - 1)
    prep_chunk = lambda n: jnp.clip(n - 1, 0, nc - 1)
    out_chunk = lambda n: jnp.maximum(n - 2, 0)
    col = lambda j, chunk: pl.BlockSpec((nb, c, D_MODEL), lambda bp, n: (bp, chunk(n), j))
    yb = pl.pallas_call(
        functools.partial(_deltanet_body, nb=nb),
        grid=(bsz // nb, nc + 2),
        in_specs=[
            col(COL_Q, conv_chunk), col(COL_K, conv_chunk), col(COL_VV, conv_chunk), col(COL_Z, prep_chunk),
            pl.BlockSpec((nb, c, LANES), lambda bp, n: (bp, prep_chunk(n), 0)),
            pl.BlockSpec((DN_CONV, 3 * D_MODEL), lambda bp, n: (0, 0)),
            pl.BlockSpec((1, LANES), lambda bp, n: (0, 0)),
            pl.BlockSpec((1, LANES), lambda bp, n: (0, 0)),
            pl.BlockSpec((1, DN_DK), lambda bp, n: (0, 0)),
        ],
        out_specs=pl.BlockSpec((nb, c, D_MODEL), lambda bp, n: (bp, out_chunk(n), 0)),
        out_shape=jax.ShapeDtypeStruct((bsz, t, D_MODEL), BF16),
        scratch_shapes=[
            pltpu.VMEM((nb, CARRY_ROWS + c, 3 * D_MODEL), BF16),
            pltpu.VMEM((nb, c, 3 * D_MODEL), F32),
            pltpu.VMEM((nu, DN_DK, DN_DK), F32),
            pltpu.VMEM((nu, 2 * c, DN_DK), BF16),
            pltpu.VMEM((nu, 2 * c, 2 * c), F32),
            pltpu.VMEM((nu, 2 * c, 2 * DN_DK), BF16),
            pltpu.VMEM((nu, c, DN_DK), BF16),
            pltpu.VMEM((nu, c, DN_DK), BF16),
            pltpu.VMEM((nu, SUBLANES, DN_DK), F32),
            pltpu.VMEM((nb, c, D_MODEL), F32),
        ],
        compiler_params=_cparams("parallel", "arbitrary"),
        name="deltanet",
    )(proj3, proj3, proj3, proj3, ab3, conv_w, alog_row, dtb_row, norm_w)
    return yb.reshape(bsz * t, D_MODEL)


def _merge_body(x_ref, ya_ref, yb_ref, ga_ref, gb_ref, lng_ref, lnb_ref, wpa_ref, wpb_ref, wo_ref,
                l1g_ref, l1b_ref, wr_ref, wrlo_ref, br_ref, h1_ref, logit_ref):
    h = _layer_norm(x_ref[...], lng_ref[...], lnb_ref[...])
    pa = jnp.dot(ya_ref[...], wpa_ref[...], preferred_element_type=F32)
    pb = jnp.dot(yb_ref[...], wpb_ref[...], preferred_element_type=F32)
    merged = jax.nn.sigmoid(ga_ref[...].astype(F32)) * pa + jax.nn.sigmoid(gb_ref[...].astype(F32)) * pb
    mix = jnp.dot(merged.astype(BF16), wo_ref[...], preferred_element_type=F32)
    h1 = _layer_norm(DEEPNORM_ALPHA * h + mix, l1g_ref[...], l1b_ref[...])
    _store_row_tiles(h1_ref, h1)
    tm = h1.shape[0]
    hi = h1.astype(BF16)
    lo = (h1 - hi.astype(F32)).astype(BF16)
    hw = jnp.dot(jnp.concatenate([hi, lo], axis=0), wr_ref[...], preferred_element_type=F32)
    logit_ref[...] = (hw[0:tm] + hw[tm:] + jnp.dot(hi, wrlo_ref[...], preferred_element_type=F32)
                      + br_ref[...])


def _merge(x2, ya, yb, proj, ln_g, ln_b, w_pa, w_pb, w_o, l1g, l1b, w_r, w_r_lo, b_r, *, tm=512):
    n = x2.shape[0]
    vec = lambda: pl.BlockSpec((1, D_MODEL), lambda i: (0, 0))
    mat = lambda: pl.BlockSpec((D_MODEL, D_MODEL), lambda i: (0, 0))
    return pl.pallas_call(
        _merge_body,
        grid=(n // tm,),
        in_specs=[
            pl.BlockSpec((tm, D_MODEL), lambda i: (i, 0)),
            pl.BlockSpec((tm, D_MODEL), lambda i: (i, 0)),
            pl.BlockSpec((tm, D_MODEL), lambda i: (i, 0)),
            pl.BlockSpec((tm, D_MODEL), lambda i: (i, COL_GA)),
            pl.BlockSpec((tm, D_MODEL), lambda i: (i, COL_GB)),
            vec(), vec(), mat(), mat(), mat(), vec(), vec(),
            pl.BlockSpec((D_MODEL, LANES), lambda i: (0, 0)),
            pl.BlockSpec((D_MODEL, LANES), lambda i: (0, 0)),
            pl.BlockSpec((1, LANES), lambda i: (0, 0)),
        ],
        out_specs=[
            pl.BlockSpec((tm * ROW_CHUNKS, LANES), lambda i: (i, 0)),
            pl.BlockSpec((tm, LANES), lambda i: (i, 0)),
        ],
        out_shape=[
            jax.ShapeDtypeStruct((n * ROW_CHUNKS, LANES), F32),
            jax.ShapeDtypeStruct((n, LANES), F32),
        ],
        compiler_params=_cparams("parallel"),
        name="merge",
    )(x2, ya, yb, proj, proj, ln_g, ln_b, w_pa, w_pb, w_o, l1g, l1b, w_r, w_r_lo, b_r)


def _route_body(logit_ref, gate_ref, idx_ref, cnt_ref, carry_ref, *, tm):
    @pl.when(pl.program_id(0) == 0)
    def _():
        carry_ref[...] = jnp.zeros_like(carry_ref)

    lg = logit_ref[...]
    lane_i = lax.broadcasted_iota(jnp.int32, lg.shape, 1)
    lane = lane_i.astype(F32)
    neg = jnp.float32(-jnp.inf)
    big = jnp.float32(1 << 20)

    is_grp = lane_i < MOE_GROUPS
    gl = jnp.where(is_grp, lg, neg)
    gmax = jnp.max(gl, -1, keepdims=True)
    grp = jnp.min(jnp.where(is_grp & (gl == gmax), lane, big), -1, keepdims=True)
    p_grp = 1.0 / jnp.sum(jnp.where(is_grp, jnp.exp(gl - gmax), 0.0), -1, keepdims=True)

    elane = lane - MOE_GROUPS
    in_grp = (elane >= grp * MOE_EPG) & (elane < (grp + 1) * MOE_EPG)
    el = jnp.where(in_grp, lg, neg)
    m1 = jnp.max(el, -1, keepdims=True)
    e1 = jnp.min(jnp.where(in_grp & (el == m1), elane, big), -1, keepdims=True)
    rest = in_grp & (elane != e1)
    el2 = jnp.where(rest, lg, neg)
    m2 = jnp.max(el2, -1, keepdims=True)
    e2 = jnp.min(jnp.where(rest & (el2 == m2), elane, big), -1, keepdims=True)
    t2 = jnp.exp(m2 - m1)
    g1 = p_grp * (1.0 / (1.0 + t2))
    g2 = p_grp * (t2 / (1.0 + t2))

    oh1 = lane == e1
    oh2 = lane == e2
    onehot = jnp.where(oh1 | oh2, 1.0, 0.0)
    ri = lax.broadcasted_iota(jnp.int32, (tm, tm), 0)
    ci = lax.broadcasted_iota(jnp.int32, (tm, tm), 1)
    tri = jnp.where(ci < ri, 1.0, 0.0).astype(BF16)
    before = jnp.dot(tri, onehot.astype(BF16), preferred_element_type=F32) + carry_ref[0:1, :]
    r1 = jnp.sum(jnp.where(oh1, before, 0.0), -1, keepdims=True)
    r2 = jnp.sum(jnp.where(oh2, before, 0.0), -1, keepdims=True)
    total = carry_ref[0:1, :] + jnp.sum(onehot, 0, keepdims=True)
    carry_ref[...] = jnp.broadcast_to(total, carry_ref.shape)
    cnt_ref[...] = jnp.broadcast_to(total, cnt_ref.shape).astype(jnp.int32)

    gate_ref[...] = jnp.where(lane_i == 0, g1, jnp.where(lane_i == 1, g2, 0.0))
    idx_ref[...] = jnp.where(lane_i == 0, e1, jnp.where(lane_i == 1, e2,
                             jnp.where(lane_i == 2, r1, jnp.where(lane_i == 3, r2, 0.0)))).astype(jnp.int32)


def _route(logits, *, tm=1024):
    n = logits.shape[0]
    return pl.pallas_call(
        functools.partial(_route_body, tm=tm),
        grid=(n // tm,),
        in_specs=[pl.BlockSpec((tm, LANES), lambda i: (i, 0))],
        out_specs=[
            pl.BlockSpec((tm, LANES), lambda i: (i, 0)),
            pl.BlockSpec((tm, LANES), lambda i: (i, 0)),
            pl.BlockSpec((8, LANES), lambda i: (0, 0)),
        ],
        out_shape=[
            jax.ShapeDtypeStruct((n, LANES), F32),
            jax.ShapeDtypeStruct((n, LANES), jnp.int32),
            jax.ShapeDtypeStruct((8, LANES), jnp.int32),
        ],
        scratch_shapes=[pltpu.VMEM((8, LANES), F32)],
        compiler_params=_cparams("arbitrary"),
        name="route",
    )(logits)


def _dest_body(idx_ref, ps_ref, o_ref):
    idx = idx_ref[...]
    lane = lax.broadcasted_iota(jnp.int32, idx.shape, 1)
    ps = ps_ref[...].astype(F32)
    d1 = jnp.sum(jnp.where(lane == idx[:, 0:1], ps, 0.0), -1, keepdims=True).astype(jnp.int32) + idx[:, 2:3]
    d2 = jnp.sum(jnp.where(lane == idx[:, 1:2], ps, 0.0), -1, keepdims=True).astype(jnp.int32) + idx[:, 3:4]
    o_ref[...] = jnp.where(lane == 0, d1, jnp.where(lane == 1, d2, 0))


def _dest(idx, pad_start_row, *, tm=1024):
    n = idx.shape[0]
    return pl.pallas_call(
        _dest_body,
        grid=(n // tm,),
        in_specs=[pl.BlockSpec((tm, LANES), lambda i: (i, 0)), pl.BlockSpec((1, LANES), lambda i: (0, 0))],
        out_specs=pl.BlockSpec((tm, LANES), lambda i: (i, 0)),
        out_shape=jax.ShapeDtypeStruct((n, LANES), jnp.int32),
        compiler_params=_cparams("parallel"),
        name="dest",
    )(idx, pad_start_row)


DMA_GROUP = 8


def _row_tile(ref, row):
    return ref.at[pl.ds(pl.multiple_of(row * SUBLANES, SUBLANES), SUBLANES)]


def _dispatch_body(dest_ref, h_ref, xin_ref, xpad_ref, sem, *, tm):
    del xin_ref

    def row_copy(src_row, dst_row):
        return pltpu.make_async_copy(_row_tile(h_ref, src_row), _row_tile(xpad_ref, dst_row), sem)

    def start(g, carry):
        r0 = g * DMA_GROUP
        dst = [dest_ref[0, 0, 2 * r0 + j] for j in range(2 * DMA_GROUP)]
        for j in range(2 * DMA_GROUP):
            row_copy(r0 + j // 2, dst[j]).start()
        return carry

    def wait(g, carry):
        for _ in range(2 * DMA_GROUP):
            row_copy(0, 0).wait()
        return carry

    lax.fori_loop(0, tm // DMA_GROUP, start, 0)
    lax.fori_loop(0, tm // DMA_GROUP, wait, 0)


def _dispatch(dest_smem, h1_rows, xpad0, *, tm):
    n = h1_rows.shape[0] // ROW_CHUNKS
    return pl.pallas_call(
        functools.partial(_dispatch_body, tm=tm),
        grid=(n // tm,),
        in_specs=[
            pl.BlockSpec((1, 1, 2 * tm), lambda i: (i, 0, 0), memory_space=pltpu.SMEM),
            pl.BlockSpec((tm * ROW_CHUNKS, LANES), lambda i: (i, 0)),
            pl.BlockSpec(memory_space=pl.ANY),
        ],
        out_specs=pl.BlockSpec(memory_space=pl.ANY),
        scratch_shapes=[pltpu.SemaphoreType.DMA(())],
        out_shape=jax.ShapeDtypeStruct(xpad0.shape, xpad0.dtype),
        input_output_aliases={2: 0},
        compiler_params=_cparams("arbitrary"),
        name="dispatch",
    )(dest_smem, h1_rows, xpad0)


def _experts_body(blk_e_ref, nused_ref, x_ref, w1_ref, w3_ref, w2_ref, y_ref, w1b, w3b, w2b):
    i = pl.program_id(0)
    prev = blk_e_ref[jnp.maximum(i - 1, 0)]

    @pl.when((i == 0) | (blk_e_ref[i] != prev))
    def _():
        w1b[...] = w1_ref[0].astype(BF16)
        w3b[...] = w3_ref[0].astype(BF16)
        w2b[...] = w2_ref[0].astype(BF16)

    @pl.when(i < nused_ref[0])
    def _():
        xb = _load_row_tiles(x_ref, EXPERT_ROWS).astype(BF16)
        a = jnp.dot(xb, w1b[...], preferred_element_type=F32)
        b = jnp.dot(xb, w3b[...], preferred_element_type=F32)
        hb = (a * jax.nn.sigmoid(a)) * b
        _store_row_tiles(y_ref, jnp.dot(hb.astype(BF16), w2b[...], preferred_element_type=F32))

    @pl.when(i >= nused_ref[0])
    def _():
        y_ref[...] = jnp.zeros_like(y_ref)


def _experts(blk_e, n_used, xpad, w1, w3, w2):
    tm = EXPERT_ROWS * ROW_CHUNKS
    return pl.pallas_call(
        _experts_body,
        grid_spec=pltpu.PrefetchScalarGridSpec(
            num_scalar_prefetch=2,
            grid=(xpad.shape[0] // tm,),
            in_specs=[
                pl.BlockSpec((tm, LANES), lambda i, be, nu: (i, 0)),
                pl.BlockSpec((1, D_MODEL, D_EXPERT), lambda i, be, nu: (be[i], 0, 0)),
                pl.BlockSpec((1, D_MODEL, D_EXPERT), lambda i, be, nu: (be[i], 0, 0)),
                pl.BlockSpec((1, D_EXPERT, D_MODEL), lambda i, be, nu: (be[i], 0, 0)),
            ],
            out_specs=pl.BlockSpec((tm, LANES), lambda i, be, nu: (i, 0)),
            scratch_shapes=[
                pltpu.VMEM((D_MODEL, D_EXPERT), BF16),
                pltpu.VMEM((D_MODEL, D_EXPERT), BF16),
                pltpu.VMEM((D_EXPERT, D_MODEL), BF16),
            ],
        ),
        out_shape=jax.ShapeDtypeStruct(xpad.shape, F32),
        compiler_params=_cparams("arbitrary"),
        name="experts",
    )(blk_e, n_used, xpad, w1, w3, w2)


def _combine_body(dest_ref, dest_next_ref, h1_ref, gate_ref, l2g_ref, l2b_ref, ypad_ref, o_ref, buf, sem, *, tm):
    i = pl.program_id(0)
    half = i % 2

    def row_copy(src_row, r, slot, hf):
        return pltpu.make_async_copy(_row_tile(ypad_ref, src_row), _row_tile(buf.at[2 * hf + slot], r), sem.at[hf])

    def gather(idx_ref, hf):
        def start(g, carry):
            r0 = g * DMA_GROUP
            src = [idx_ref[0, 0, 2 * r0 + j] for j in range(2 * DMA_GROUP)]
            for j in range(2 * DMA_GROUP):
                row_copy(src[j], r0 + j // 2, j % 2, hf).start()
            return carry
        lax.fori_loop(0, tm // DMA_GROUP, start, 0)

    @pl.when(i == 0)
    def _():
        gather(dest_ref, 0)

    @pl.when(i + 1 < pl.num_programs(0))
    def _():
        gather(dest_next_ref, 1 - half)

    def wait(g, carry):
        for j in range(2 * DMA_GROUP):
            row_copy(0, 0, j % 2, half).wait()
        return carry

    lax.fori_loop(0, tm // DMA_GROUP, wait, 0)
    gate = gate_ref[...]
    ffn = (_load_row_tiles(buf.at[2 * half], tm) * gate[:, 0:1]
           + _load_row_tiles(buf.at[2 * half + 1], tm) * gate[:, 1:2])
    h1 = _load_row_tiles(h1_ref, tm)
    o_ref[...] = _layer_norm(DEEPNORM_ALPHA * h1 + ffn, l2g_ref[...], l2b_ref[...])


def _combine(dest_smem, h1_rows, gates, l2g, l2b, ypad, *, tm):
    n = h1_rows.shape[0] // ROW_CHUNKS
    steps = n // tm
    return pl.pallas_call(
        functools.partial(_combine_body, tm=tm),
        grid=(steps,),
        in_specs=[
            pl.BlockSpec((1, 1, 2 * tm), lambda i: (i, 0, 0), memory_space=pltpu.SMEM),
            pl.BlockSpec((1, 1, 2 * tm), lambda i: (jnp.minimum(i + 1, steps - 1), 0, 0), memory_space=pltpu.SMEM),
            pl.BlockSpec((tm * ROW_CHUNKS, LANES), lambda i: (i, 0)),
            pl.BlockSpec((tm, LANES), lambda i: (i, 0)),
            pl.BlockSpec((1, D_MODEL), lambda i: (0, 0)),
            pl.BlockSpec((1, D_MODEL), lambda i: (0, 0)),
            pl.BlockSpec(memory_space=pl.ANY),
        ],
        out_specs=pl.BlockSpec((tm, D_MODEL), lambda i: (i, 0)),
        scratch_shapes=[
            pltpu.VMEM((4, tm * ROW_CHUNKS, LANES), F32),
            pltpu.SemaphoreType.DMA((2,)),
        ],
        out_shape=jax.ShapeDtypeStruct((n, D_MODEL), F32),
        compiler_params=_cparams("arbitrary"),
        name="combine",
    )(dest_smem, dest_smem, h1_rows, gates, l2g, l2b, ypad)


def _row(v, width=None):
    v = v.reshape(1, -1).astype(F32)
    if width is not None and v.shape[1] < width:
        v = jnp.pad(v, ((0, 0), (0, width - v.shape[1])))
    return v


def kernel(x, ln_in_g, ln_in_b, w_in, b_in, gm_ln_g, gm_ln_b, gm_w_s, gm_b_s, dn_conv_w, dn_a_log,
           dn_dt_bias, dn_norm_w, w_pa, w_pb, w_o, ln1_g, ln1_b, w_rg, b_rg, w_re, b_re, w1, w3, w2,
           ln2_g, ln2_b):
    bsz, t, d = x.shape
    n = bsz * t
    x2 = x.reshape(n, d)
    l = 0

    wi, bi = w_in[l], b_in[l]
    c_ab = N_FRONT_BLOCKS * D_MODEL
    c_gate = c_ab + 2 * DN_HEADS
    w_ab = jnp.pad(wi[:, c_ab:c_gate], ((0, 0), (0, LANES - 2 * DN_HEADS))).astype(BF16)
    b_ab = _row(bi[c_ab:c_gate], LANES)
    w_gates = wi[:, c_gate:]
    b_proj = jnp.concatenate([bi[:c_ab], bi[c_gate:]]).reshape(1, PROJ_WIDTH)

    ln_g, ln_b = _row(ln_in_g), _row(ln_in_b)
    h, ab = _ln_ab(x2, ln_g, ln_b, w_ab, b_ab)
    proj = _inproj(h, wi, w_gates, b_proj, tm=min(1024, n))

    bs_full = jnp.broadcast_to(gm_b_s[l][:, :, None], (GM_GROUPS, GM_BLOCK, LANES))
    ya = _gmlp(proj, _row(gm_ln_g[l]), _row(gm_ln_b[l]), gm_w_s[l], bs_full)

    yb = _deltanet(proj, ab, dn_conv_w[l], _row(dn_a_log[l], LANES), _row(dn_dt_bias[l], LANES),
                   _row(dn_norm_w[l]), bsz=bsz, t=t, nb=min(4, bsz))

    w_r = jnp.concatenate([w_rg[l], w_re[l], jnp.zeros((d, LANES - MOE_GROUPS - N_EXPERTS), F32)], axis=1)
    w_r_hi = w_r.astype(BF16)
    w_r_lo = (w_r - w_r_hi.astype(F32)).astype(BF16)
    b_r = _row(jnp.concatenate([b_rg[l], b_re[l]]), LANES)
    h1, logits = _merge(x2, ya, yb, proj, ln_g, ln_b, w_pa[l].astype(BF16), w_pb[l].astype(BF16),
                        w_o[l].astype(BF16), _row(ln1_g[l]), _row(ln1_b[l]), w_r_hi, w_r_lo, b_r)

    gates, idx, counts = _route(logits, tm=min(1024, n))

    tm_e = EXPERT_ROWS
    cnt = counts[0, :N_EXPERTS]
    padded = (cnt + tm_e - 1) // tm_e * tm_e
    pad_end = jnp.cumsum(padded)
    pad_start = (pad_end - padded).astype(jnp.int32)
    n_blocks = (2 * n + N_EXPERTS * (tm_e - 1) + tm_e - 1) // tm_e
    blk_start = jnp.arange(n_blocks, dtype=jnp.int32) * tm_e
    blk_e = jnp.minimum(jnp.sum(pad_end[None, :] <= blk_start[:, None], -1), N_EXPERTS - 1).astype(jnp.int32)
    n_used = (pad_end[-1] // tm_e).astype(jnp.int32).reshape(1)

    ps_row = jnp.pad(pad_start, (0, LANES - N_EXPERTS)).reshape(1, LANES)
    dest = _dest(idx, ps_row, tm=min(1024, n))[:, :2]
    tm_d, tm_c = min(1024, n), 256
    xpad0 = jnp.zeros((n_blocks * tm_e * ROW_CHUNKS, LANES), F32)
    xpad = _dispatch(dest.reshape(n // tm_d, 1, 2 * tm_d), h1, xpad0, tm=tm_d)
    ypad = _experts(blk_e, n_used, xpad, w1[l], w3[l], w2[l])
    out = _combine(dest.reshape(n // tm_c, 1, 2 * tm_c), h1, gates, _row(ln2_g[l]), _row(ln2_b[l]), ypad,
                   tm=tm_c)
    return out.reshape(bsz, t, d)
```

```python
import functools
import math

import jax
import jax.numpy as jnp
from jax import lax
from jax.experimental import pallas as pl
from jax.experimental.pallas import tpu as pltpu

D_MODEL = 1024
GM_GROUPS = 8
GM_BLOCK = 128
GM_CHUNK = 64
DN_HEADS = 8
DN_DK = 128
DN_CONV = 4
MOE_GROUPS = 4
MOE_EPG = 8
N_EXPERTS = MOE_GROUPS * MOE_EPG
D_EXPERT = D_MODEL // 2
LN_EPS = 1e-5
RMS_EPS = 1e-6
L2_EPS = 1e-6
DEEPNORM_ALPHA = 2.0 ** 0.25

LANES = 128
COL_U, COL_V, COL_Q, COL_K, COL_VV, COL_Z, COL_GA, COL_GB = range(8)
N_PROJ_BLOCKS = 8
N_FRONT_BLOCKS = 6
PROJ_WIDTH = N_PROJ_BLOCKS * D_MODEL

DN_CHUNK = 64
CARRY_ROWS = 16
EXPERT_ROWS = 512
VMEM_LIMIT = 56 * 1024 * 1024
SUBLANES = 8
ROW_CHUNKS = D_MODEL // LANES
assert ROW_CHUNKS == SUBLANES

F32 = jnp.float32
BF16 = jnp.bfloat16


def _cparams(*sem):
    return pltpu.CompilerParams(dimension_semantics=sem, vmem_limit_bytes=VMEM_LIMIT)


def _store_row_tiles(ref, val):
    m = val.shape[0]
    for ch in range(ROW_CHUNKS):
        ref[pl.ds(ch, m, stride=ROW_CHUNKS), :] = val[:, ch * LANES:(ch + 1) * LANES]


def _load_row_tiles(ref, m):
    return jnp.concatenate([ref[pl.ds(ch, m, stride=ROW_CHUNKS), :] for ch in range(ROW_CHUNKS)], axis=1)


def _layer_norm(x, g, b):
    mu = jnp.mean(x, -1, keepdims=True)
    xc = x - mu
    var = jnp.mean(xc * xc, -1, keepdims=True)
    return xc * lax.rsqrt(var + LN_EPS) * g + b


def _dot(a, b):
    return jnp.dot(a.astype(BF16), b.astype(BF16), preferred_element_type=F32)


def _dot_nt(a, b):
    return lax.dot_general(a.astype(BF16), b.astype(BF16), (((1,), (1,)), ((), ())),
                           preferred_element_type=F32)


def _ln_ab_body(x_ref, g_ref, b_ref, wab_ref, bab_ref, h_ref, ab_ref):
    h = _layer_norm(x_ref[...], g_ref[...], b_ref[...]).astype(BF16)
    h_ref[...] = h
    ab_ref[...] = jnp.dot(h, wab_ref[...], preferred_element_type=F32) + bab_ref[...]


def _ln_ab(x2, ln_g, ln_b, w_ab, b_ab, *, tm):
    n = x2.shape[0]
    return pl.pallas_call(
        _ln_ab_body,
        grid=(n // tm,),
        in_specs=[
            pl.BlockSpec((tm, D_MODEL), lambda i: (i, 0)),
            pl.BlockSpec((1, D_MODEL), lambda i: (0, 0)),
            pl.BlockSpec((1, D_MODEL), lambda i: (0, 0)),
            pl.BlockSpec((D_MODEL, LANES), lambda i: (0, 0)),
            pl.BlockSpec((1, LANES), lambda i: (0, 0)),
        ],
        out_specs=[pl.BlockSpec((tm, D_MODEL), lambda i: (i, 0)), pl.BlockSpec((tm, LANES), lambda i: (i, 0))],
        out_shape=[jax.ShapeDtypeStruct((n, D_MODEL), BF16), jax.ShapeDtypeStruct((n, LANES), F32)],
        compiler_params=_cparams("parallel"),
        name="ln_ab",
    )(x2, ln_g, ln_b, w_ab, b_ab)


def _inproj_body(h_ref, wa_ref, wb_ref, bias_ref, o_ref, w_scr):
    j = pl.program_id(0)

    @pl.when((pl.program_id(1) == 0) & (j < N_FRONT_BLOCKS))
    def _():
        w_scr[...] = wa_ref[...].astype(BF16)

    @pl.when((pl.program_id(1) == 0) & (j >= N_FRONT_BLOCKS))
    def _():
        w_scr[...] = wb_ref[...].astype(BF16)

    o_ref[...] = (jnp.dot(h_ref[...], w_scr[...], preferred_element_type=F32) + bias_ref[...]).astype(o_ref.dtype)


def _inproj(h, w_in, w_gates, bias, *, tm=1024):
    n = h.shape[0]
    return pl.pallas_call(
        _inproj_body,
        grid=(N_PROJ_BLOCKS, n // tm),
        in_specs=[
            pl.BlockSpec((tm, D_MODEL), lambda j, i: (i, 0)),
            pl.BlockSpec((D_MODEL, D_MODEL), lambda j, i: (0, jnp.minimum(j, N_FRONT_BLOCKS - 1))),
            pl.BlockSpec((D_MODEL, D_MODEL), lambda j, i: (0, jnp.maximum(j - N_FRONT_BLOCKS, 0))),
            pl.BlockSpec((1, D_MODEL), lambda j, i: (0, j)),
        ],
        out_specs=pl.BlockSpec((tm, D_MODEL), lambda j, i: (i, j)),
        out_shape=jax.ShapeDtypeStruct((n, N_PROJ_BLOCKS * D_MODEL), BF16),
        scratch_shapes=[pltpu.VMEM((D_MODEL, D_MODEL), BF16)],
        compiler_params=_cparams("arbitrary", "arbitrary"),
        name="inproj",
    )(h, w_in, w_gates, bias)


def _gelu(x):
    return 0.5 * x * (1.0 + lax.erf(x * (1.0 / math.sqrt(2.0))))


def _gmlp_body(u_ref, v_ref, lng_ref, lnb_ref, ws_ref, bs_ref, o_ref, *, nblk):
    u = _gelu(u_ref[...].astype(F32))
    v = _layer_norm(_gelu(v_ref[...].astype(F32)), lng_ref[...], lnb_ref[...]).astype(BF16)
    row_chunk = lax.broadcasted_iota(jnp.int32, (GM_BLOCK, GM_BLOCK), 0) // GM_CHUNK
    col_chunk = lax.broadcasted_iota(jnp.int32, (GM_BLOCK, GM_BLOCK), 1) // GM_CHUNK
    causal = col_chunk <= row_chunk
    for g in range(GM_GROUPS):
        cols = slice(g * LANES, (g + 1) * LANES)
        w = jnp.where(causal, ws_ref[g], 0.0).astype(BF16)
        for blk in range(nblk):
            rows = slice(blk * GM_BLOCK, (blk + 1) * GM_BLOCK)
            s = jnp.dot(w, v[rows, cols], preferred_element_type=F32) + bs_ref[g]
            o_ref[rows, cols] = (u[rows, cols] * s).astype(o_ref.dtype)


def _gmlp(proj, ln_g, ln_b, w_s, b_s_full, *, nblk=4):
    n = proj.shape[0]
    rows = nblk * GM_BLOCK
    return pl.pallas_call(
        functools.partial(_gmlp_body, nblk=nblk),
        grid=(n // rows,),
        in_specs=[
            pl.BlockSpec((rows, D_MODEL), lambda i: (i, COL_U)),
            pl.BlockSpec((rows, D_MODEL), lambda i: (i, COL_V)),
            pl.BlockSpec((1, D_MODEL), lambda i: (0, 0)),
            pl.BlockSpec((1, D_MODEL), lambda i: (0, 0)),
            pl.BlockSpec((GM_GROUPS, GM_BLOCK, GM_BLOCK), lambda i: (0, 0, 0)),
            pl.BlockSpec((GM_GROUPS, GM_BLOCK, LANES), lambda i: (0, 0, 0)),
        ],
        out_specs=pl.BlockSpec((rows, D_MODEL), lambda i: (i, 0)),
        out_shape=jax.ShapeDtypeStruct((n, D_MODEL), BF16),
        compiler_params=_cparams("parallel"),
        name="gmlp",
    )(proj, proj, ln_g, ln_b, w_s, b_s_full)


def _cumsum_rows(x):
    n = x.shape[0]
    row = lax.broadcasted_iota(jnp.int32, x.shape, 0)
    shift = 1
    while shift < n:
        x = x + jnp.where(row >= shift, pltpu.roll(x, shift, 0), 0.0)
        shift *= 2
    return x


def _deltanet_body(q_ref, k_ref, v_ref, z_ref, ab_ref, cw_ref, alog_ref, dtb_ref, nw_ref, o_ref,
                   xs_ref, qkv_ref, state_ref, kq_ref, m_ref, r0_ref, qd_ref, kd_ref, cd_ref, zg_ref, *, nb):
    c = DN_CHUNK
    w3 = 3 * D_MODEL
    heads = range(DN_HEADS)
    units = range(nb * DN_HEADS)
    hsl = lambda base, h: slice(base + h * DN_DK, base + (h + 1) * DN_DK)

    @pl.when(pl.program_id(1) == 0)
    def _():
        xs_ref[:, 0:CARRY_ROWS, :] = jnp.zeros((nb, CARRY_ROWS, w3), BF16)
        for ref in (qkv_ref, state_ref, kq_ref, m_ref, r0_ref, qd_ref, kd_ref, cd_ref, zg_ref):
            ref[...] = jnp.zeros_like(ref)

    kq_in = [kq_ref[u] for u in units]
    kq = [lax.dot_general(kq_in[u], kq_in[u], (((1,), (1,)), ((), ())), preferred_element_type=F32)
          for u in units]
    pm = [kq[u] * m_ref[u] for u in units]
    attn = [x[c:, 0:c].astype(BF16) for x in pm]

    lane = lax.broadcasted_iota(jnp.int32, (c, 2 * c), 1)
    right = lane >= c
    eye_right = jnp.where(lane - c == lax.broadcasted_iota(jnp.int32, (c, 2 * c), 0), 1.0, 0.0)
    pa = [x[0:c] + eye_right for x in pm]
    span = 1
    while span < c:
        pab = [x.astype(BF16) for x in pa]
        pa = [jnp.dot(pab[u][:, 0:c], pab[u], preferred_element_type=F32) + jnp.where(right, pa[u], 0.0)
              for u in units]
        span *= 2
    r = [jnp.dot(pa[u].astype(BF16), r0_ref[u], preferred_element_type=F32) for u in units]

    s_old = [state_ref[u] for u in units]
    ws = [jnp.dot(jnp.concatenate([r[u][:, DN_DK:].astype(BF16), qd_ref[u]], axis=0),
                  s_old[u].astype(BF16), preferred_element_type=F32) for u in units]
    v_new = [(r[u][:, 0:DN_DK] - ws[u][0:c]).astype(BF16) for u in units]
    o = [ws[u][c:] + jnp.dot(attn[u], v_new[u], preferred_element_type=F32) for u in units]
    ds = [lax.dot_general(kd_ref[u], v_new[u], (((0,), (0,)), ((), ())), preferred_element_type=F32)
          for u in units]
    for u in units:
        state_ref[u] = s_old[u] * cd_ref[u][0:1, :] + ds[u]
    for b in range(nb):
        for h in heads:
            u = b * DN_HEADS + h
            on = o[u] * lax.rsqrt(jnp.mean(o[u] * o[u], -1, keepdims=True) + RMS_EPS)
            o_ref[b, :, hsl(0, h)] = (on * zg_ref[b, :, hsl(0, h)]).astype(o_ref.dtype)

    ri = lax.broadcasted_iota(jnp.int32, (c, c), 0)
    ci = lax.broadcasted_iota(jnp.int32, (c, c), 1)
    causal = ri >= ci
    strict = ri > ci
    for b in range(nb):
        qkv = qkv_ref[b]
        ab = ab_ref[b]
        xg = ab + dtb_ref[...]
        softplus = jnp.maximum(xg, 0.0) + jnp.log1p(jnp.exp(-jnp.abs(xg)))
        gcum = _cumsum_rows(-jnp.exp(alog_ref[...]) * softplus)
        beta_all = jax.nn.sigmoid(ab)
        gcum_sq = gcum if c == LANES else jnp.concatenate([gcum, jnp.zeros((LANES - c, LANES), F32)], axis=0)
        gcum_t = gcum_sq.T

        z = z_ref[b].astype(F32)
        zg_ref[b] = z * jax.nn.sigmoid(z) * jnp.concatenate([nw_ref[...]] * DN_HEADS, axis=1)

        for h in heads:
            u = b * DN_HEADS + h
            qh, kh, vh = qkv[:, hsl(0, h)], qkv[:, hsl(D_MODEL, h)], qkv[:, hsl(2 * D_MODEL, h)]
            qh = qh * (lax.rsqrt(jnp.sum(qh * qh, -1, keepdims=True) + L2_EPS) * (DN_DK ** -0.5))
            kh = kh * lax.rsqrt(jnp.sum(kh * kh, -1, keepdims=True) + L2_EPS)
            g_col = gcum[:, h:h + 1]
            beta = beta_all[:, DN_HEADS + h:DN_HEADS + h + 1]
            g_last = gcum[c - 1:c, h:h + 1]
            decay = jnp.where(causal, jnp.exp(jnp.where(causal, g_col - gcum_t[h:h + 1, 0:c], 0.0)), 0.0)
            eg = jnp.exp(g_col)
            kq_ref[u, 0:c, :] = kh.astype(BF16)
            kq_ref[u, c:, :] = qh.astype(BF16)
            m_ref[u, 0:c, 0:c] = jnp.where(strict, -beta * decay, 0.0)
            m_ref[u, c:, 0:c] = decay
            r0_ref[u, c:, 0:DN_DK] = (vh * beta).astype(BF16)
            r0_ref[u, c:, DN_DK:] = (kh * (beta * eg)).astype(BF16)
            qd_ref[u] = (qh * eg).astype(BF16)
            kd_ref[u] = (kh * jnp.exp(g_last - g_col)).astype(BF16)
            cd_ref[u] = jnp.broadcast_to(jnp.exp(g_last), (SUBLANES, DN_DK))

    nsh = DN_CONV - 1
    srow = lax.broadcasted_iota(jnp.int32, (nsh * c, CARRY_ROWS + c), 0)
    scol = lax.broadcasted_iota(jnp.int32, (nsh * c, CARRY_ROWS + c), 1)
    shift_mat = jnp.where(scol == srow % c + srow // c + (CARRY_ROWS - nsh), 1.0, 0.0).astype(BF16)
    for b in range(nb):
        xs_ref[b, CARRY_ROWS:, 0:D_MODEL] = q_ref[b]
        xs_ref[b, CARRY_ROWS:, D_MODEL:2 * D_MODEL] = k_ref[b]
        xs_ref[b, CARRY_ROWS:, 2 * D_MODEL:] = v_ref[b]
        window = xs_ref[b]
        shifted = jnp.dot(shift_mat, window, preferred_element_type=F32)
        acc = cw_ref[nsh:DN_CONV, :] * window[CARRY_ROWS:].astype(F32)
        for j in range(nsh):
            acc = acc + cw_ref[j:j + 1, :] * shifted[j * c:(j + 1) * c]
        xs_ref[b, 0:CARRY_ROWS, :] = xs_ref[b, c:c + CARRY_ROWS, :]
        qkv_ref[b] = acc * jax.nn.sigmoid(acc)


def _deltanet(proj, ab, conv_w, alog_row, dtb_row, norm_w, *, bsz, t, nb=4):
    c = DN_CHUNK
    nc = t // c
    nu = nb * DN_HEADS
    proj3 = proj.reshape(bsz, t, PROJ_WIDTH)
    ab3 = ab.reshape(bsz, t, LANES)
    conv_chunk = lambda n: jnp.minimum(n, nc - 1)
    prep_chunk = lambda n: jnp.clip(n - 1, 0, nc - 1)
    out_chunk = lambda n: jnp.maximum(n - 2, 0)
    col = lambda j, chunk: pl.BlockSpec((nb, c, D_MODEL), lambda bp, n: (bp, chunk(n), j))
    yb = pl.pallas_call(
        functools.partial(_deltanet_body, nb=nb),
        grid=(bsz // nb, nc + 2),
        in_specs=[
            col(COL_Q, conv_chunk), col(COL_K, conv_chunk), col(COL_VV, conv_chunk), col(COL_Z, prep_chunk),
            pl.BlockSpec((nb, c, LANES), lambda bp, n: (bp, prep_chunk(n), 0)),
            pl.BlockSpec((DN_CONV, 3 * D_MODEL), lambda bp, n: (0, 0)),
            pl.BlockSpec((1, LANES), lambda bp, n: (0, 0)),
            pl.BlockSpec((1, LANES), lambda bp, n: (0, 0)),
            pl.BlockSpec((1, DN_DK), lambda bp, n: (0, 0)),
        ],
        out_specs=pl.BlockSpec((nb, c, D_MODEL), lambda bp, n: (bp, out_chunk(n), 0)),
        out_shape=jax.ShapeDtypeStruct((bsz, t, D_MODEL), BF16),
        scratch_shapes=[
            pltpu.VMEM((nb, CARRY_ROWS + c, 3 * D_MODEL), BF16),
            pltpu.VMEM((nb, c, 3 * D_MODEL), F32),
            pltpu.VMEM((nu, DN_DK, DN_DK), F32),
            pltpu.VMEM((nu, 2 * c, DN_DK), BF16),
            pltpu.VMEM((nu, 2 * c, 2 * c), F32),
            pltpu.VMEM((nu, 2 * c, 2 * DN_DK), BF16),
            pltpu.VMEM((nu, c, DN_DK), BF16),
            pltpu.VMEM((nu, c, DN_DK), BF16),
            pltpu.VMEM((nu, SUBLANES, DN_DK), F32),
            pltpu.VMEM((nb, c, D_MODEL), F32),
        ],
        compiler_params=_cparams("parallel", "arbitrary"),
        name="deltanet",
    )(proj3, proj3, proj3, proj3, ab3, conv_w, alog_row, dtb_row, norm_w)
    return yb.reshape(bsz * t, D_MODEL)


def _merge_body(x_ref, ya_ref, yb_ref, ga_ref, gb_ref, lng_ref, lnb_ref, wpa_ref, wpb_ref, wo_ref,
                l1g_ref, l1b_ref, wr_ref, wrlo_ref, br_ref, h1_ref, logit_ref):
    h = _layer_norm(x_ref[...], lng_ref[...], lnb_ref[...])
    pa = jnp.dot(ya_ref[...], wpa_ref[...], preferred_element_type=F32)
    pb = jnp.dot(yb_ref[...], wpb_ref[...], preferred_element_type=F32)
    merged = jax.nn.sigmoid(ga_ref[...].astype(F32)) * pa + jax.nn.sigmoid(gb_ref[...].astype(F32)) * pb
    mix = jnp.dot(merged.astype(BF16), wo_ref[...], preferred_element_type=F32)
    h1 = _layer_norm(DEEPNORM_ALPHA * h + mix, l1g_ref[...], l1b_ref[...])
    _store_row_tiles(h1_ref, h1)
    tm = h1.shape[0]
    hi = h1.astype(BF16)
    lo = (h1 - hi.astype(F32)).astype(BF16)
    hw = jnp.dot(jnp.concatenate([hi, lo], axis=0), wr_ref[...], preferred_element_type=F32)
    logit_ref[...] = (hw[0:tm] + hw[tm:] + jnp.dot(hi, wrlo_ref[...], preferred_element_type=F32)
                      + br_ref[...])


def _merge(x2, ya, yb, proj, ln_g, ln_b, w_pa, w_pb, w_o, l1g, l1b, w_r, w_r_lo, b_r, *, tm=512):
    n = x2.shape[0]
    vec = lambda: pl.BlockSpec((1, D_MODEL), lambda i: (0, 0))
    mat = lambda: pl.BlockSpec((D_MODEL, D_MODEL), lambda i: (0, 0))
    return pl.pallas_call(
        _merge_body,
        grid=(n // tm,),
        in_specs=[
            pl.BlockSpec((tm, D_MODEL), lambda i: (i, 0)),
            pl.BlockSpec((tm, D_MODEL), lambda i: (i, 0)),
            pl.BlockSpec((tm, D_MODEL), lambda i: (i, 0)),
            pl.BlockSpec((tm, D_MODEL), lambda i: (i, COL_GA)),
            pl.BlockSpec((tm, D_MODEL), lambda i: (i, COL_GB)),
            vec(), vec(), mat(), mat(), mat(), vec(), vec(),
            pl.BlockSpec((D_MODEL, LANES), lambda i: (0, 0)),
            pl.BlockSpec((D_MODEL, LANES), lambda i: (0, 0)),
            pl.BlockSpec((1, LANES), lambda i: (0, 0)),
        ],
        out_specs=[
            pl.BlockSpec((tm * ROW_CHUNKS, LANES), lambda i: (i, 0)),
            pl.BlockSpec((tm, LANES), lambda i: (i, 0)),
        ],
        out_shape=[
            jax.ShapeDtypeStruct((n * ROW_CHUNKS, LANES), F32),
            jax.ShapeDtypeStruct((n, LANES), F32),
        ],
        compiler_params=_cparams("parallel"),
        name="merge",
    )(x2, ya, yb, proj, proj, ln_g, ln_b, w_pa, w_pb, w_o, l1g, l1b, w_r, w_r_lo, b_r)


def _route_body(logit_ref, gate_ref, idx_ref, cnt_ref, carry_ref, *, tm):
    @pl.when(pl.program_id(0) == 0)
    def _():
        carry_ref[...] = jnp.zeros_like(carry_ref)

    lg = logit_ref[...]
    lane_i = lax.broadcasted_iota(jnp.int32, lg.shape, 1)
    lane = lane_i.astype(F32)
    neg = jnp.float32(-jnp.inf)
    big = jnp.float32(1 << 20)

    is_grp = lane_i < MOE_GROUPS
    gl = jnp.where(is_grp, lg, neg)
    gmax = jnp.max(gl, -1, keepdims=True)
    grp = jnp.min(jnp.where(is_grp & (gl == gmax), lane, big), -1, keepdims=True)
    p_grp = 1.0 / jnp.sum(jnp.where(is_grp, jnp.exp(gl - gmax), 0.0), -1, keepdims=True)

    elane = lane - MOE_GROUPS
    in_grp = (elane >= grp * MOE_EPG) & (elane < (grp + 1) * MOE_EPG)
    el = jnp.where(in_grp, lg, neg)
    m1 = jnp.max(el, -1, keepdims=True)
    e1 = jnp.min(jnp.where(in_grp & (el == m1), elane, big), -1, keepdims=True)
    rest = in_grp & (elane != e1)
    el2 = jnp.where(rest, lg, neg)
    m2 = jnp.max(el2, -1, keepdims=True)
    e2 = jnp.min(jnp.where(rest & (el2 == m2), elane, big), -1, keepdims=True)
    t2 = jnp.exp(m2 - m1)
    g1 = p_grp * (1.0 / (1.0 + t2))
    g2 = p_grp * (t2 / (1.0 + t2))

    oh1 = lane == e1
    oh2 = lane == e2
    onehot = jnp.where(oh1 | oh2, 1.0, 0.0)
    ri = lax.broadcasted_iota(jnp.int32, (tm, tm), 0)
    ci = lax.broadcasted_iota(jnp.int32, (tm, tm), 1)
    tri = jnp.where(ci < ri, 1.0, 0.0).astype(BF16)
    before = jnp.dot(tri, onehot.astype(BF16), preferred_element_type=F32) + carry_ref[0:1, :]
    r1 = jnp.sum(jnp.where(oh1, before, 0.0), -1, keepdims=True)
    r2 = jnp.sum(jnp.where(oh2, before, 0.0), -1, keepdims=True)
    total = carry_ref[0:1, :] + jnp.sum(onehot, 0, keepdims=True)
    carry_ref[...] = jnp.broadcast_to(total, carry_ref.shape)
    cnt_ref[...] = jnp.broadcast_to(total, cnt_ref.shape).astype(jnp.int32)

    gate_ref[...] = jnp.where(lane_i == 0, g1, jnp.where(lane_i == 1, g2, 0.0))
    idx_ref[...] = jnp.where(lane_i == 0, e1, jnp.where(lane_i == 1, e2,
                             jnp.where(lane_i == 2, r1, jnp.where(lane_i == 3, r2, 0.0)))).astype(jnp.int32)


def _route(logits, *, tm=1024):
    n = logits.shape[0]
    return pl.pallas_call(
        functools.partial(_route_body, tm=tm),
        grid=(n // tm,),
        in_specs=[pl.BlockSpec((tm, LANES), lambda i: (i, 0))],
        out_specs=[
            pl.BlockSpec((tm, LANES), lambda i: (i, 0)),
            pl.BlockSpec((tm, LANES), lambda i: (i, 0)),
            pl.BlockSpec((8, LANES), lambda i: (0, 0)),
        ],
        out_shape=[
            jax.ShapeDtypeStruct((n, LANES), F32),
            jax.ShapeDtypeStruct((n, LANES), jnp.int32),
            jax.ShapeDtypeStruct((8, LANES), jnp.int32),
        ],
        scratch_shapes=[pltpu.VMEM((8, LANES), F32)],
        compiler_params=_cparams("arbitrary"),
        name="route",
    )(logits)


def _dest_body(idx_ref, ps_ref, o_ref):
    idx = idx_ref[...]
    lane = lax.broadcasted_iota(jnp.int32, idx.shape, 1)
    ps = ps_ref[...].astype(F32)
    d1 = jnp.sum(jnp.where(lane == idx[:, 0:1], ps, 0.0), -1, keepdims=True).astype(jnp.int32) + idx[:, 2:3]
    d2 = jnp.sum(jnp.where(lane == idx[:, 1:2], ps, 0.0), -1, keepdims=True).astype(jnp.int32) + idx[:, 3:4]
    o_ref[...] = jnp.where(lane == 0, d1, jnp.where(lane == 1, d2, 0))


def _dest(idx, pad_start_row, *, tm=1024):
    n = idx.shape[0]
    return pl.pallas_call(
        _dest_body,
        grid=(n // tm,),
        in_specs=[pl.BlockSpec((tm, LANES), lambda i: (i, 0)), pl.BlockSpec((1, LANES), lambda i: (0, 0))],
        out_specs=pl.BlockSpec((tm, LANES), lambda i: (i, 0)),
        out_shape=jax.ShapeDtypeStruct((n, LANES), jnp.int32),
        compiler_params=_cparams("parallel"),
        name="dest",
    )(idx, pad_start_row)


DMA_GROUP = 8


def _row_tile(ref, row):
    return ref.at[pl.ds(pl.multiple_of(row * SUBLANES, SUBLANES), SUBLANES)]


def _dispatch_body(dest_ref, h_ref, xin_ref, xpad_ref, sem, *, tm):
    del xin_ref

    def row_copy(src_row, dst_row):
        return pltpu.make_async_copy(_row_tile(h_ref, src_row), _row_tile(xpad_ref, dst_row), sem)

    def start(g, carry):
        r0 = g * DMA_GROUP
        dst = [dest_ref[0, 0, 2 * r0 + j] for j in range(2 * DMA_GROUP)]
        for j in range(2 * DMA_GROUP):
            row_copy(r0 + j // 2, dst[j]).start()
        return carry

    def wait(g, carry):
        for _ in range(2 * DMA_GROUP):
            row_copy(0, 0).wait()
        return carry

    lax.fori_loop(0, tm // DMA_GROUP, start, 0)
    lax.fori_loop(0, tm // DMA_GROUP, wait, 0)


def _dispatch(dest_smem, h1_rows, xpad0, *, tm):
    n = h1_rows.shape[0] // ROW_CHUNKS
    return pl.pallas_call(
        functools.partial(_dispatch_body, tm=tm),
        grid=(n // tm,),
        in_specs=[
            pl.BlockSpec((1, 1, 2 * tm), lambda i: (i, 0, 0), memory_space=pltpu.SMEM),
            pl.BlockSpec((tm * ROW_CHUNKS, LANES), lambda i: (i, 0)),
            pl.BlockSpec(memory_space=pl.ANY),
        ],
        out_specs=pl.BlockSpec(memory_space=pl.ANY),
        scratch_shapes=[pltpu.SemaphoreType.DMA(())],
        out_shape=jax.ShapeDtypeStruct(xpad0.shape, xpad0.dtype),
        input_output_aliases={2: 0},
        compiler_params=_cparams("arbitrary"),
        name="dispatch",
    )(dest_smem, h1_rows, xpad0)


def _experts_body(blk_e_ref, nused_ref, x_ref, w1_ref, w3_ref, w2_ref, y_ref, w1b, w3b, w2b):
    i = pl.program_id(0)
    prev = blk_e_ref[jnp.maximum(i - 1, 0)]

    @pl.when((i == 0) | (blk_e_ref[i] != prev))
    def _():
        w1b[...] = w1_ref[0].astype(BF16)
        w3b[...] = w3_ref[0].astype(BF16)
        w2b[...] = w2_ref[0].astype(BF16)

    @pl.when(i < nused_ref[0])
    def _():
        xb = _load_row_tiles(x_ref, EXPERT_ROWS).astype(BF16)
        a = jnp.dot(xb, w1b[...], preferred_element_type=F32)
        b = jnp.dot(xb, w3b[...], preferred_element_type=F32)
        hb = (a * jax.nn.sigmoid(a)) * b
        _store_row_tiles(y_ref, jnp.dot(hb.astype(BF16), w2b[...], preferred_element_type=F32))

    @pl.when(i >= nused_ref[0])
    def _():
        y_ref[...] = jnp.zeros_like(y_ref)


def _experts(blk_e, n_used, xpad, w1, w3, w2):
    tm = EXPERT_ROWS * ROW_CHUNKS
    return pl.pallas_call(
        _experts_body,
        grid_spec=pltpu.PrefetchScalarGridSpec(
            num_scalar_prefetch=2,
            grid=(xpad.shape[0] // tm,),
            in_specs=[
                pl.BlockSpec((tm, LANES), lambda i, be, nu: (i, 0)),
                pl.BlockSpec((1, D_MODEL, D_EXPERT), lambda i, be, nu: (be[i], 0, 0)),
                pl.BlockSpec((1, D_MODEL, D_EXPERT), lambda i, be, nu: (be[i], 0, 0)),
                pl.BlockSpec((1, D_EXPERT, D_MODEL), lambda i, be, nu: (be[i], 0, 0)),
            ],
            out_specs=pl.BlockSpec((tm, LANES), lambda i, be, nu: (i, 0)),
            scratch_shapes=[
                pltpu.VMEM((D_MODEL, D_EXPERT), BF16),
                pltpu.VMEM((D_MODEL, D_EXPERT), BF16),
                pltpu.VMEM((D_EXPERT, D_MODEL), BF16),
            ],
        ),
        out_shape=jax.ShapeDtypeStruct(xpad.shape, F32),
        compiler_params=_cparams("arbitrary"),
        name="experts",
    )(blk_e, n_used, xpad, w1, w3, w2)


def _combine_body(dest_ref, dest_next_ref, h1_ref, gate_ref, l2g_ref, l2b_ref, ypad_ref, o_ref, buf, sem, *, tm):
    i = pl.program_id(0)
    half = i % 2

    def row_copy(src_row, r, slot, hf):
        return pltpu.make_async_copy(_row_tile(ypad_ref, src_row), _row_tile(buf.at[2 * hf + slot], r), sem.at[hf])

    def gather(idx_ref, hf):
        def start(g, carry):
            r0 = g * DMA_GROUP
            src = [idx_ref[0, 0, 2 * r0 + j] for j in range(2 * DMA_GROUP)]
            for j in range(2 * DMA_GROUP):
                row_copy(src[j], r0 + j // 2, j % 2, hf).start()
            return carry
        lax.fori_loop(0, tm // DMA_GROUP, start, 0)

    @pl.when(i == 0)
    def _():
        gather(dest_ref, 0)

    @pl.when(i + 1 < pl.num_programs(0))
    def _():
        gather(dest_next_ref, 1 - half)

    def wait(g, carry):
        for j in range(2 * DMA_GROUP):
            row_copy(0, 0, j % 2, half).wait()
        return carry

    lax.fori_loop(0, tm // DMA_GROUP, wait, 0)
    gate = gate_ref[...]
    ffn = (_load_row_tiles(buf.at[2 * half], tm) * gate[:, 0:1]
           + _load_row_tiles(buf.at[2 * half + 1], tm) * gate[:, 1:2])
    h1 = _load_row_tiles(h1_ref, tm)
    o_ref[...] = _layer_norm(DEEPNORM_ALPHA * h1 + ffn, l2g_ref[...], l2b_ref[...])


def _combine(dest_smem, h1_rows, gates, l2g, l2b, ypad, *, tm):
    n = h1_rows.shape[0] // ROW_CHUNKS
    steps = n // tm
    return pl.pallas_call(
        functools.partial(_combine_body, tm=tm),
        grid=(steps,),
        in_specs=[
            pl.BlockSpec((1, 1, 2 * tm), lambda i: (i, 0, 0), memory_space=pltpu.SMEM),
            pl.BlockSpec((1, 1, 2 * tm), lambda i: (jnp.minimum(i + 1, steps - 1), 0, 0), memory_space=pltpu.SMEM),
            pl.BlockSpec((tm * ROW_CHUNKS, LANES), lambda i: (i, 0)),
            pl.BlockSpec((tm, LANES), lambda i: (i, 0)),
            pl.BlockSpec((1, D_MODEL), lambda i: (0, 0)),
            pl.BlockSpec((1, D_MODEL), lambda i: (0, 0)),
            pl.BlockSpec(memory_space=pl.ANY),
        ],
        out_specs=pl.BlockSpec((tm, D_MODEL), lambda i: (i, 0)),
        scratch_shapes=[
            pltpu.VMEM((4, tm * ROW_CHUNKS, LANES), F32),
            pltpu.SemaphoreType.DMA((2,)),
        ],
        out_shape=jax.ShapeDtypeStruct((n, D_MODEL), F32),
        compiler_params=_cparams("arbitrary"),
        name="combine",
    )(dest_smem, dest_smem, h1_rows, gates, l2g, l2b, ypad)


def _row(v, width=None):
    v = v.reshape(1, -1).astype(F32)
    if width is not None and v.shape[1] < width:
        v = jnp.pad(v, ((0, 0), (0, width - v.shape[1])))
    return v


def kernel(x, ln_in_g, ln_in_b, w_in, b_in, gm_ln_g, gm_ln_b, gm_w_s, gm_b_s, dn_conv_w, dn_a_log,
           dn_dt_bias, dn_norm_w, w_pa, w_pb, w_o, ln1_g, ln1_b, w_rg, b_rg, w_re, b_re, w1, w3, w2,
           ln2_g, ln2_b):
    bsz, t, d = x.shape
    n = bsz * t
    x2 = x.reshape(n, d)
    l = 0

    wi, bi = w_in[l], b_in[l]
    c_ab = N_FRONT_BLOCKS * D_MODEL
    c_gate = c_ab + 2 * DN_HEADS
    w_ab = jnp.pad(wi[:, c_ab:c_gate], ((0, 0), (0, LANES - 2 * DN_HEADS))).astype(BF16)
    b_ab = _row(bi[c_ab:c_gate], LANES)
    w_gates = wi[:, c_gate:]
    b_proj = jnp.concatenate([bi[:c_ab], bi[c_gate:]]).reshape(1, PROJ_WIDTH)

    ln_g, ln_b = _row(ln_in_g), _row(ln_in_b)
    h, ab = _ln_ab(x2, ln_g, ln_b, w_ab, b_ab, tm=min(1024, n))
    proj = _inproj(h, wi, w_gates, b_proj, tm=min(2048, n))

    bs_full = jnp.broadcast_to(gm_b_s[l][:, :, None], (GM_GROUPS, GM_BLOCK, LANES))
    ya = _gmlp(proj, _row(gm_ln_g[l]), _row(gm_ln_b[l]), gm_w_s[l], bs_full)

    yb = _deltanet(proj, ab, dn_conv_w[l], _row(dn_a_log[l], LANES), _row(dn_dt_bias[l], LANES),
                   _row(dn_norm_w[l]), bsz=bsz, t=t, nb=min(4, bsz))

    w_r = jnp.concatenate([w_rg[l], w_re[l], jnp.zeros((d, LANES - MOE_GROUPS - N_EXPERTS), F32)], axis=1)
    w_r_hi = w_r.astype(BF16)
    w_r_lo = (w_r - w_r_hi.astype(F32)).astype(BF16)
    b_r = _row(jnp.concatenate([b_rg[l], b_re[l]]), LANES)
    h1, logits = _merge(x2, ya, yb, proj, ln_g, ln_b, w_pa[l].astype(BF16), w_pb[l].astype(BF16),
                        w_o[l].astype(BF16), _row(ln1_g[l]), _row(ln1_b[l]), w_r_hi, w_r_lo, b_r,
                        tm=min(1024, n))

    gates, idx, counts = _route(logits, tm=min(1024, n))

    tm_e = EXPERT_ROWS
    cnt = counts[0, :N_EXPERTS]
    padded = (cnt + tm_e - 1) // tm_e * tm_e
    pad_end = jnp.cumsum(padded)
    pad_start = (pad_end - padded).astype(jnp.int32)
    n_blocks = (2 * n + N_EXPERTS * (tm_e - 1) + tm_e - 1) // tm_e
    blk_start = jnp.arange(n_blocks, dtype=jnp.int32) * tm_e
    blk_e = jnp.minimum(jnp.sum(pad_end[None, :] <= blk_start[:, None], -1), N_EXPERTS - 1).astype(jnp.int32)
    n_used = (pad_end[-1] // tm_e).astype(jnp.int32).reshape(1)

    ps_row = jnp.pad(pad_start, (0, LANES - N_EXPERTS)).reshape(1, LANES)
    dest = _dest(idx, ps_row, tm=min(1024, n))[:, :2]
    tm_d, tm_c = min(1024, n), min(512, n // 2)
    xpad0 = jnp.zeros((n_blocks * tm_e * ROW_CHUNKS, LANES), F32)
    xpad = _dispatch(dest.reshape(n // tm_d, 1, 2 * tm_d), h1, xpad0, tm=tm_d)
    ypad = _experts(blk_e, n_used, xpad, w1[l], w3[l], w2[l])
    out = _combine(dest.reshape(n // tm_c, 1, 2 * tm_c), h1, gates, _row(ln2_g[l]), _row(ln2_b[l]), ypad,
                   tm=tm_c)
    return out.reshape(bsz, t, d)
```

```python
import functools
import math

import jax
import jax.numpy as jnp
from jax import lax
from jax.experimental import pallas as pl
from jax.experimental.pallas import tpu as pltpu

D_MODEL = 1024
GM_GROUPS = 8
GM_BLOCK = 128
GM_CHUNK = 64
DN_HEADS = 8
DN_DK = 128
DN_CONV = 4
MOE_GROUPS = 4
MOE_EPG = 8
N_EXPERTS = MOE_GROUPS * MOE_EPG
D_EXPERT = D_MODEL // 2
LN_EPS = 1e-5
RMS_EPS = 1e-6
L2_EPS = 1e-6
DEEPNORM_ALPHA = 2.0 ** 0.25

LANES = 128
COL_U, COL_V, COL_Q, COL_K, COL_VV, COL_Z, COL_GA, COL_GB = range(8)
N_PROJ_BLOCKS = 8
N_FRONT_BLOCKS = 6
PROJ_WIDTH = N_PROJ_BLOCKS * D_MODEL

DN_CHUNK = 64
CARRY_ROWS = 16
EXPERT_ROWS = 512
VMEM_LIMIT = 56 * 1024 * 1024
SUBLANES = 8
ROW_WORDS = D_MODEL // 2
ROW_CHUNKS = ROW_WORDS // LANES

F32 = jnp.float32
BF16 = jnp.bfloat16


def _cparams(*sem):
    return pltpu.CompilerParams(dimension_semantics=sem, vmem_limit_bytes=VMEM_LIMIT)


def _store_row_tiles(ref, val):
    m = val.shape[0]
    hi = pltpu.bitcast(val[:, :ROW_WORDS].astype(BF16).astype(F32), jnp.uint32)
    lo = pltpu.bitcast(val[:, ROW_WORDS:].astype(BF16).astype(F32), jnp.uint32)
    words = hi | (lo >> 16)
    for ch in range(ROW_CHUNKS):
        ref[pl.ds(ch, m, stride=ROW_CHUNKS), :] = words[:, ch * LANES:(ch + 1) * LANES]


def _load_row_tiles(ref, m):
    words = jnp.concatenate([ref[pl.ds(ch, m, stride=ROW_CHUNKS), :] for ch in range(ROW_CHUNKS)], axis=1)
    hi = pltpu.bitcast(words & jnp.uint32(0xFFFF0000), F32)
    lo = pltpu.bitcast(words << 16, F32)
    return jnp.concatenate([hi, lo], axis=1).astype(BF16)


def _layer_norm(x, g, b):
    mu = jnp.mean(x, -1, keepdims=True)
    xc = x - mu
    var = jnp.mean(xc * xc, -1, keepdims=True)
    return xc * lax.rsqrt(var + LN_EPS) * g + b


def _dot(a, b):
    return jnp.dot(a.astype(BF16), b.astype(BF16), preferred_element_type=F32)


def _dot_nt(a, b):
    return lax.dot_general(a.astype(BF16), b.astype(BF16), (((1,), (1,)), ((), ())),
                           preferred_element_type=F32)


def _ln_ab_body(x_ref, g_ref, b_ref, wab_ref, bab_ref, h_ref, ab_ref):
    h = _layer_norm(x_ref[...], g_ref[...], b_ref[...]).astype(BF16)
    h_ref[...] = h
    ab_ref[...] = jnp.dot(h, wab_ref[...], preferred_element_type=F32) + bab_ref[...]


def _ln_ab(x2, ln_g, ln_b, w_ab, b_ab, *, tm):
    n = x2.shape[0]
    return pl.pallas_call(
        _ln_ab_body,
        grid=(n // tm,),
        in_specs=[
            pl.BlockSpec((tm, D_MODEL), lambda i: (i, 0)),
            pl.BlockSpec((1, D_MODEL), lambda i: (0, 0)),
            pl.BlockSpec((1, D_MODEL), lambda i: (0, 0)),
            pl.BlockSpec((D_MODEL, LANES), lambda i: (0, 0)),
            pl.BlockSpec((1, LANES), lambda i: (0, 0)),
        ],
        out_specs=[pl.BlockSpec((tm, D_MODEL), lambda i: (i, 0)), pl.BlockSpec((tm, LANES), lambda i: (i, 0))],
        out_shape=[jax.ShapeDtypeStruct((n, D_MODEL), BF16), jax.ShapeDtypeStruct((n, LANES), F32)],
        compiler_params=_cparams("parallel"),
        name="ln_ab",
    )(x2, ln_g, ln_b, w_ab, b_ab)


def _inproj_body(h_ref, wa_ref, wb_ref, bias_ref, o_ref, w_scr):
    j = pl.program_id(0)

    @pl.when((pl.program_id(1) == 0) & (j < N_FRONT_BLOCKS))
    def _():
        w_scr[...] = wa_ref[...].astype(BF16)

    @pl.when((pl.program_id(1) == 0) & (j >= N_FRONT_BLOCKS))
    def _():
        w_scr[...] = wb_ref[...].astype(BF16)

    o_ref[...] = (jnp.dot(h_ref[...], w_scr[...], preferred_element_type=F32) + bias_ref[...]).astype(o_ref.dtype)


def _inproj(h, w_in, w_gates, bias, *, tm=1024):
    n = h.shape[0]
    return pl.pallas_call(
        _inproj_body,
        grid=(N_PROJ_BLOCKS, n // tm),
        in_specs=[
            pl.BlockSpec((tm, D_MODEL), lambda j, i: (i, 0)),
            pl.BlockSpec((D_MODEL, D_MODEL), lambda j, i: (0, jnp.minimum(j, N_FRONT_BLOCKS - 1))),
            pl.BlockSpec((D_MODEL, D_MODEL), lambda j, i: (0, jnp.maximum(j - N_FRONT_BLOCKS, 0))),
            pl.BlockSpec((1, D_MODEL), lambda j, i: (0, j)),
        ],
        out_specs=pl.BlockSpec((tm, D_MODEL), lambda j, i: (i, j)),
        out_shape=jax.ShapeDtypeStruct((n, N_PROJ_BLOCKS * D_MODEL), BF16),
        scratch_shapes=[pltpu.VMEM((D_MODEL, D_MODEL), BF16)],
        compiler_params=_cparams("arbitrary", "arbitrary"),
        name="inproj",
    )(h, w_in, w_gates, bias)


def _gelu(x):
    return 0.5 * x * (1.0 + lax.erf(x * (1.0 / math.sqrt(2.0))))


def _gmlp_body(u_ref, v_ref, lng_ref, lnb_ref, ws_ref, bs_ref, o_ref, *, nblk):
    u = _gelu(u_ref[...].astype(F32))
    v = _layer_norm(_gelu(v_ref[...].astype(F32)), lng_ref[...], lnb_ref[...]).astype(BF16)
    row_chunk = lax.broadcasted_iota(jnp.int32, (GM_BLOCK, GM_BLOCK), 0) // GM_CHUNK
    col_chunk = lax.broadcasted_iota(jnp.int32, (GM_BLOCK, GM_BLOCK), 1) // GM_CHUNK
    causal = col_chunk <= row_chunk
    for g in range(GM_GROUPS):
        cols = slice(g * LANES, (g + 1) * LANES)
        w = jnp.where(causal, ws_ref[g], 0.0).astype(BF16)
        for blk in range(nblk):
            rows = slice(blk * GM_BLOCK, (blk + 1) * GM_BLOCK)
            s = jnp.dot(w, v[rows, cols], preferred_element_type=F32) + bs_ref[g]
            o_ref[rows, cols] = (u[rows, cols] * s).astype(o_ref.dtype)


def _gmlp(proj, ln_g, ln_b, w_s, b_s_full, *, nblk=4):
    n = proj.shape[0]
    rows = nblk * GM_BLOCK
    return pl.pallas_call(
        functools.partial(_gmlp_body, nblk=nblk),
        grid=(n // rows,),
        in_specs=[
            pl.BlockSpec((rows, D_MODEL), lambda i: (i, COL_U)),
            pl.BlockSpec((rows, D_MODEL), lambda i: (i, COL_V)),
            pl.BlockSpec((1, D_MODEL), lambda i: (0, 0)),
            pl.BlockSpec((1, D_MODEL), lambda i: (0, 0)),
            pl.BlockSpec((GM_GROUPS, GM_BLOCK, GM_BLOCK), lambda i: (0, 0, 0)),
            pl.BlockSpec((GM_GROUPS, GM_BLOCK, LANES), lambda i: (0, 0, 0)),
        ],
        out_specs=pl.BlockSpec((rows, D_MODEL), lambda i: (i, 0)),
        out_shape=jax.ShapeDtypeStruct((n, D_MODEL), BF16),
        compiler_params=_cparams("parallel"),
        name="gmlp",
    )(proj, proj, ln_g, ln_b, w_s, b_s_full)


def _cumsum_rows(x):
    n = x.shape[0]
    row = lax.broadcasted_iota(jnp.int32, x.shape, 0)
    shift = 1
    while shift < n:
        x = x + jnp.where(row >= shift, pltpu.roll(x, shift, 0), 0.0)
        shift *= 2
    return x


def _deltanet_body(q_ref, k_ref, v_ref, z_ref, ab_ref, cw_ref, alog_ref, dtb_ref, nw_ref, o_ref,
                   xs_ref, qkv_ref, state_ref, kq_ref, m_ref, r0_ref, qd_ref, kd_ref, cd_ref, zg_ref, *, nb):
    c = DN_CHUNK
    w3 = 3 * D_MODEL
    heads = range(DN_HEADS)
    units = range(nb * DN_HEADS)
    hsl = lambda base, h: slice(base + h * DN_DK, base + (h + 1) * DN_DK)

    @pl.when(pl.program_id(1) == 0)
    def _():
        xs_ref[:, 0:CARRY_ROWS, :] = jnp.zeros((nb, CARRY_ROWS, w3), BF16)
        for ref in (qkv_ref, state_ref, kq_ref, m_ref, r0_ref, qd_ref, kd_ref, cd_ref, zg_ref):
            ref[...] = jnp.zeros_like(ref)

    kq_in = [kq_ref[u] for u in units]
    kq = [lax.dot_general(kq_in[u], kq_in[u], (((1,), (1,)), ((), ())), preferred_element_type=F32)
          for u in units]
    pm = [kq[u] * m_ref[u] for u in units]
    attn = [x[c:, 0:c].astype(BF16) for x in pm]

    lane = lax.broadcasted_iota(jnp.int32, (c, 2 * c), 1)
    right = lane >= c
    eye_right = jnp.where(lane - c == lax.broadcasted_iota(jnp.int32, (c, 2 * c), 0), 1.0, 0.0)
    pa = [x[0:c] + eye_right for x in pm]
    span = 1
    while span < c:
        pab = [x.astype(BF16) for x in pa]
        pa = [jnp.dot(pab[u][:, 0:c], pab[u], preferred_element_type=F32) + jnp.where(right, pa[u], 0.0)
              for u in units]
        span *= 2
    r = [jnp.dot(pa[u].astype(BF16), r0_ref[u], preferred_element_type=F32) for u in units]

    s_old = [state_ref[u] for u in units]
    ws = [jnp.dot(jnp.concatenate([r[u][:, DN_DK:].astype(BF16), qd_ref[u]], axis=0),
                  s_old[u].astype(BF16), preferred_element_type=F32) for u in units]
    v_new = [(r[u][:, 0:DN_DK] - ws[u][0:c]).astype(BF16) for u in units]
    o = [ws[u][c:] + jnp.dot(attn[u], v_new[u], preferred_element_type=F32) for u in units]
    ds = [lax.dot_general(kd_ref[u], v_new[u], (((0,), (0,)), ((), ())), preferred_element_type=F32)
          for u in units]
    for u in units:
        state_ref[u] = s_old[u] * cd_ref[u][0:1, :] + ds[u]
    for b in range(nb):
        for h in heads:
            u = b * DN_HEADS + h
            on = o[u] * lax.rsqrt(jnp.mean(o[u] * o[u], -1, keepdims=True) + RMS_EPS)
            o_ref[b, :, hsl(0, h)] = (on * zg_ref[b, :, hsl(0, h)]).astype(o_ref.dtype)

    ri = lax.broadcasted_iota(jnp.int32, (c, c), 0)
    ci = lax.broadcasted_iota(jnp.int32, (c, c), 1)
    causal = ri >= ci
    strict = ri > ci
    for b in range(nb):
        qkv = qkv_ref[b]
        ab = ab_ref[b]
        xg = ab + dtb_ref[...]
        softplus = jnp.maximum(xg, 0.0) + jnp.log1p(jnp.exp(-jnp.abs(xg)))
        gcum = _cumsum_rows(-jnp.exp(alog_ref[...]) * softplus)
        beta_all = jax.nn.sigmoid(ab)
        gcum_sq = gcum if c == LANES else jnp.concatenate([gcum, jnp.zeros((LANES - c, LANES), F32)], axis=0)
        gcum_t = gcum_sq.T

        z = z_ref[b].astype(F32)
        zg_ref[b] = z * jax.nn.sigmoid(z) * jnp.concatenate([nw_ref[...]] * DN_HEADS, axis=1)

        for h in heads:
            u = b * DN_HEADS + h
            qh, kh, vh = qkv[:, hsl(0, h)], qkv[:, hsl(D_MODEL, h)], qkv[:, hsl(2 * D_MODEL, h)]
            qh = qh * (lax.rsqrt(jnp.sum(qh * qh, -1, keepdims=True) + L2_EPS) * (DN_DK ** -0.5))
            kh = kh * lax.rsqrt(jnp.sum(kh * kh, -1, keepdims=True) + L2_EPS)
            g_col = gcum[:, h:h + 1]
            beta = beta_all[:, DN_HEADS + h:DN_HEADS + h + 1]
            g_last = gcum[c - 1:c, h:h + 1]
            decay = jnp.where(causal, jnp.exp(jnp.where(causal, g_col - gcum_t[h:h + 1, 0:c], 0.0)), 0.0)
            eg = jnp.exp(g_col)
            kq_ref[u, 0:c, :] = kh.astype(BF16)
            kq_ref[u, c:, :] = qh.astype(BF16)
            m_ref[u, 0:c, 0:c] = jnp.where(strict, -beta * decay, 0.0)
            m_ref[u, c:, 0:c] = decay
            r0_ref[u, c:, 0:DN_DK] = (vh * beta).astype(BF16)
            r0_ref[u, c:, DN_DK:] = (kh * (beta * eg)).astype(BF16)
            qd_ref[u] = (qh * eg).astype(BF16)
            kd_ref[u] = (kh * jnp.exp(g_last - g_col)).astype(BF16)
            cd_ref[u] = jnp.broadcast_to(jnp.exp(g_last), (SUBLANES, DN_DK))

    nsh = DN_CONV - 1
    srow = lax.broadcasted_iota(jnp.int32, (nsh * c, CARRY_ROWS + c), 0)
    scol = lax.broadcasted_iota(jnp.int32, (nsh * c, CARRY_ROWS + c), 1)
    shift_mat = jnp.where(scol == srow % c + srow // c + (CARRY_ROWS - nsh), 1.0, 0.0).astype(BF16)
    for b in range(nb):
        xs_ref[b, CARRY_ROWS:, 0:D_MODEL] = q_ref[b]
        xs_ref[b, CARRY_ROWS:, D_MODEL:2 * D_MODEL] = k_ref[b]
        xs_ref[b, CARRY_ROWS:, 2 * D_MODEL:] = v_ref[b]
        window = xs_ref[b]
        shifted = jnp.dot(shift_mat, window, preferred_element_type=F32)
        acc = cw_ref[nsh:DN_CONV, :] * window[CARRY_ROWS:].astype(F32)
        for j in range(nsh):
            acc = acc + cw_ref[j:j + 1, :] * shifted[j * c:(j + 1) * c]
        xs_ref[b, 0:CARRY_ROWS, :] = xs_ref[b, c:c + CARRY_ROWS, :]
        qkv_ref[b] = acc * jax.nn.sigmoid(acc)


def _deltanet(proj, ab, conv_w, alog_row, dtb_row, norm_w, *, bsz, t, nb=4):
    c = DN_CHUNK
    nc = t // c
    nu = nb * DN_HEADS
    proj3 = proj.reshape(bsz, t, PROJ_WIDTH)
    ab3 = ab.reshape(bsz, t, LANES)
    conv_chunk = lambda n: jnp.minimum(n, nc - 1)
    prep_chunk = lambda n: jnp.clip(n - 1, 0, nc - 1)
    out_chunk = lambda n: jnp.maximum(n - 2, 0)
    col = lambda j, chunk: pl.BlockSpec((nb, c, D_MODEL), lambda bp, n: (bp, chunk(n), j))
    yb = pl.pallas_call(
        functools.partial(_deltanet_body, nb=nb),
        grid=(bsz // nb, nc + 2),
        in_specs=[
            col(COL_Q, conv_chunk), col(COL_K, conv_chunk), col(COL_VV, conv_chunk), col(COL_Z, prep_chunk),
            pl.BlockSpec((nb, c, LANES), lambda bp, n: (bp, prep_chunk(n), 0)),
            pl.BlockSpec((DN_CONV, 3 * D_MODEL), lambda bp, n: (0, 0)),
            pl.BlockSpec((1, LANES), lambda bp, n: (0, 0)),
            pl.BlockSpec((1, LANES), lambda bp, n: (0, 0)),
            pl.BlockSpec((1, DN_DK), lambda bp, n: (0, 0)),
        ],
        out_specs=pl.BlockSpec((nb, c, D_MODEL), lambda bp, n: (bp, out_chunk(n), 0)),
        out_shape=jax.ShapeDtypeStruct((bsz, t, D_MODEL), BF16),
        scratch_shapes=[
            pltpu.VMEM((nb, CARRY_ROWS + c, 3 * D_MODEL), BF16),
            pltpu.VMEM((nb, c, 3 * D_MODEL), F32),
            pltpu.VMEM((nu, DN_DK, DN_DK), F32),
            pltpu.VMEM((nu, 2 * c, DN_DK), BF16),
            pltpu.VMEM((nu, 2 * c, 2 * c), F32),
            pltpu.VMEM((nu, 2 * c, 2 * DN_DK), BF16),
            pltpu.VMEM((nu, c, DN_DK), BF16),
            pltpu.VMEM((nu, c, DN_DK), BF16),
            pltpu.VMEM((nu, SUBLANES, DN_DK), F32),
            pltpu.VMEM((nb, c, D_MODEL), F32),
        ],
        compiler_params=_cparams("parallel", "arbitrary"),
        name="deltanet",
    )(proj3, proj3, proj3, proj3, ab3, conv_w, alog_row, dtb_row, norm_w)
    return yb.reshape(bsz * t, D_MODEL)


def _merge_body(x_ref, ya_ref, yb_ref, ga_ref, gb_ref, lng_ref, lnb_ref, wpa_ref, wpb_ref, wo_ref,
                l1g_ref, l1b_ref, wr_ref, wrlo_ref, br_ref, h1_ref, h1_rows_ref, logit_ref):
    h = _layer_norm(x_ref[...], lng_ref[...], lnb_ref[...])
    pa = jnp.dot(ya_ref[...], wpa_ref[...], preferred_element_type=F32)
    pb = jnp.dot(yb_ref[...], wpb_ref[...], preferred_element_type=F32)
    merged = jax.nn.sigmoid(ga_ref[...].astype(F32)) * pa + jax.nn.sigmoid(gb_ref[...].astype(F32)) * pb
    mix = jnp.dot(merged.astype(BF16), wo_ref[...], preferred_element_type=F32)
    h1 = _layer_norm(DEEPNORM_ALPHA * h + mix, l1g_ref[...], l1b_ref[...])
    h1_ref[...] = h1
    _store_row_tiles(h1_rows_ref, h1)
    tm = h1.shape[0]
    hi = h1.astype(BF16)
    lo = (h1 - hi.astype(F32)).astype(BF16)
    hw = jnp.dot(jnp.concatenate([hi, lo], axis=0), wr_ref[...], preferred_element_type=F32)
    logit_ref[...] = (hw[0:tm] + hw[tm:] + jnp.dot(hi, wrlo_ref[...], preferred_element_type=F32)
                      + br_ref[...])


def _merge(x2, ya, yb, proj, ln_g, ln_b, w_pa, w_pb, w_o, l1g, l1b, w_r, w_r_lo, b_r, *, tm=512):
    n = x2.shape[0]
    vec = lambda: pl.BlockSpec((1, D_MODEL), lambda i: (0, 0))
    mat = lambda: pl.BlockSpec((D_MODEL, D_MODEL), lambda i: (0, 0))
    return pl.pallas_call(
        _merge_body,
        grid=(n // tm,),
        in_specs=[
            pl.BlockSpec((tm, D_MODEL), lambda i: (i, 0)),
            pl.BlockSpec((tm, D_MODEL), lambda i: (i, 0)),
            pl.BlockSpec((tm, D_MODEL), lambda i: (i, 0)),
            pl.BlockSpec((tm, D_MODEL), lambda i: (i, COL_GA)),
            pl.BlockSpec((tm, D_MODEL), lambda i: (i, COL_GB)),
            vec(), vec(), mat(), mat(), mat(), vec(), vec(),
            pl.BlockSpec((D_MODEL, LANES), lambda i: (0, 0)),
            pl.BlockSpec((D_MODEL, LANES), lambda i: (0, 0)),
            pl.BlockSpec((1, LANES), lambda i: (0, 0)),
        ],
        out_specs=[
            pl.BlockSpec((tm, D_MODEL), lambda i: (i, 0)),
            pl.BlockSpec((tm * ROW_CHUNKS, LANES), lambda i: (i, 0)),
            pl.BlockSpec((tm, LANES), lambda i: (i, 0)),
        ],
        out_shape=[
            jax.ShapeDtypeStruct((n, D_MODEL), F32),
            jax.ShapeDtypeStruct((n * ROW_CHUNKS, LANES), jnp.uint32),
            jax.ShapeDtypeStruct((n, LANES), F32),
        ],
        compiler_params=_cparams("parallel"),
        name="merge",
    )(x2, ya, yb, proj, proj, ln_g, ln_b, w_pa, w_pb, w_o, l1g, l1b, w_r, w_r_lo, b_r)


def _route_body(logit_ref, gate_ref, idx_ref, cnt_ref, carry_ref, *, tm):
    @pl.when(pl.program_id(0) == 0)
    def _():
        carry_ref[...] = jnp.zeros_like(carry_ref)

    lg = logit_ref[...]
    lane_i = lax.broadcasted_iota(jnp.int32, lg.shape, 1)
    lane = lane_i.astype(F32)
    neg = jnp.float32(-jnp.inf)
    big = jnp.float32(1 << 20)

    is_grp = lane_i < MOE_GROUPS
    gl = jnp.where(is_grp, lg, neg)
    gmax = jnp.max(gl, -1, keepdims=True)
    grp = jnp.min(jnp.where(is_grp & (gl == gmax), lane, big), -1, keepdims=True)
    p_grp = 1.0 / jnp.sum(jnp.where(is_grp, jnp.exp(gl - gmax), 0.0), -1, keepdims=True)

    elane = lane - MOE_GROUPS
    in_grp = (elane >= grp * MOE_EPG) & (elane < (grp + 1) * MOE_EPG)
    el = jnp.where(in_grp, lg, neg)
    m1 = jnp.max(el, -1, keepdims=True)
    e1 = jnp.min(jnp.where(in_grp & (el == m1), elane, big), -1, keepdims=True)
    rest = in_grp & (elane != e1)
    el2 = jnp.where(rest, lg, neg)
    m2 = jnp.max(el2, -1, keepdims=True)
    e2 = jnp.min(jnp.where(rest & (el2 == m2), elane, big), -1, keepdims=True)
    t2 = jnp.exp(m2 - m1)
    g1 = p_grp * (1.0 / (1.0 + t2))
    g2 = p_grp * (t2 / (1.0 + t2))

    oh1 = lane == e1
    oh2 = lane == e2
    onehot = jnp.where(oh1 | oh2, 1.0, 0.0)
    ri = lax.broadcasted_iota(jnp.int32, (tm, tm), 0)
    ci = lax.broadcasted_iota(jnp.int32, (tm, tm), 1)
    tri = jnp.where(ci < ri, 1.0, 0.0).astype(BF16)
    before = jnp.dot(tri, onehot.astype(BF16), preferred_element_type=F32) + carry_ref[0:1, :]
    r1 = jnp.sum(jnp.where(oh1, before, 0.0), -1, keepdims=True)
    r2 = jnp.sum(jnp.where(oh2, before, 0.0), -1, keepdims=True)
    total = carry_ref[0:1, :] + jnp.sum(onehot, 0, keepdims=True)
    carry_ref[...] = jnp.broadcast_to(total, carry_ref.shape)
    cnt_ref[...] = jnp.broadcast_to(total, cnt_ref.shape).astype(jnp.int32)

    gate_ref[...] = jnp.where(lane_i == 0, g1, jnp.where(lane_i == 1, g2, 0.0))
    idx_ref[...] = jnp.where(lane_i == 0, e1, jnp.where(lane_i == 1, e2,
                             jnp.where(lane_i == 2, r1, jnp.where(lane_i == 3, r2, 0.0)))).astype(jnp.int32)


def _route(logits, *, tm=1024):
    n = logits.shape[0]
    return pl.pallas_call(
        functools.partial(_route_body, tm=tm),
        grid=(n // tm,),
        in_specs=[pl.BlockSpec((tm, LANES), lambda i: (i, 0))],
        out_specs=[
            pl.BlockSpec((tm, LANES), lambda i: (i, 0)),
            pl.BlockSpec((tm, LANES), lambda i: (i, 0)),
            pl.BlockSpec((8, LANES), lambda i: (0, 0)),
        ],
        out_shape=[
            jax.ShapeDtypeStruct((n, LANES), F32),
            jax.ShapeDtypeStruct((n, LANES), jnp.int32),
            jax.ShapeDtypeStruct((8, LANES), jnp.int32),
        ],
        scratch_shapes=[pltpu.VMEM((8, LANES), F32)],
        compiler_params=_cparams("arbitrary"),
        name="route",
    )(logits)


def _dest_body(idx_ref, ps_ref, o_ref):
    idx = idx_ref[...]
    lane = lax.broadcasted_iota(jnp.int32, idx.shape, 1)
    ps = ps_ref[...].astype(F32)
    d1 = jnp.sum(jnp.where(lane == idx[:, 0:1], ps, 0.0), -1, keepdims=True).astype(jnp.int32) + idx[:, 2:3]
    d2 = jnp.sum(jnp.where(lane == idx[:, 1:2], ps, 0.0), -1, keepdims=True).astype(jnp.int32) + idx[:, 3:4]
    o_ref[...] = jnp.where(lane == 0, d1, jnp.where(lane == 1, d2, 0))


def _dest(idx, pad_start_row, *, tm=1024):
    n = idx.shape[0]
    return pl.pallas_call(
        _dest_body,
        grid=(n // tm,),
        in_specs=[pl.BlockSpec((tm, LANES), lambda i: (i, 0)), pl.BlockSpec((1, LANES), lambda i: (0, 0))],
        out_specs=pl.BlockSpec((tm, LANES), lambda i: (i, 0)),
        out_shape=jax.ShapeDtypeStruct((n, LANES), jnp.int32),
        compiler_params=_cparams("parallel"),
        name="dest",
    )(idx, pad_start_row)


DMA_GROUP = 8


def _row_tile(ref, row):
    return ref.at[pl.ds(pl.multiple_of(row * ROW_CHUNKS, ROW_CHUNKS), ROW_CHUNKS)]


def _dispatch_body(dest_ref, h_ref, xin_ref, xpad_ref, sem, *, tm):
    del xin_ref

    def row_copy(src_row, dst_row):
        return pltpu.make_async_copy(_row_tile(h_ref, src_row), _row_tile(xpad_ref, dst_row), sem)

    def start(g, carry):
        r0 = g * DMA_GROUP
        dst = [dest_ref[0, 0, 2 * r0 + j] for j in range(2 * DMA_GROUP)]
        for j in range(2 * DMA_GROUP):
            row_copy(r0 + j // 2, dst[j]).start()
        return carry

    def wait(g, carry):
        for _ in range(2 * DMA_GROUP):
            row_copy(0, 0).wait()
        return carry

    lax.fori_loop(0, tm // DMA_GROUP, start, 0)
    lax.fori_loop(0, tm // DMA_GROUP, wait, 0)


def _dispatch(dest_smem, h1_rows, xpad0, *, tm):
    n = h1_rows.shape[0] // ROW_CHUNKS
    return pl.pallas_call(
        functools.partial(_dispatch_body, tm=tm),
        grid=(n // tm,),
        in_specs=[
            pl.BlockSpec((1, 1, 2 * tm), lambda i: (i, 0, 0), memory_space=pltpu.SMEM),
            pl.BlockSpec((tm * ROW_CHUNKS, LANES), lambda i: (i, 0)),
            pl.BlockSpec(memory_space=pl.ANY),
        ],
        out_specs=pl.BlockSpec(memory_space=pl.ANY),
        scratch_shapes=[pltpu.SemaphoreType.DMA(())],
        out_shape=jax.ShapeDtypeStruct(xpad0.shape, xpad0.dtype),
        input_output_aliases={2: 0},
        compiler_params=_cparams("arbitrary"),
        name="dispatch",
    )(dest_smem, h1_rows, xpad0)


def _experts_body(blk_e_ref, nused_ref, x_ref, w1_ref, w3_ref, w2_ref, y_ref, w1b, w3b, w2b):
    i = pl.program_id(0)
    prev = blk_e_ref[jnp.maximum(i - 1, 0)]

    @pl.when((i == 0) | (blk_e_ref[i] != prev))
    def _():
        w1b[...] = w1_ref[0].astype(BF16)
        w3b[...] = w3_ref[0].astype(BF16)
        w2b[...] = w2_ref[0].astype(BF16)

    @pl.when(i < nused_ref[0])
    def _():
        xb = _load_row_tiles(x_ref, EXPERT_ROWS)
        a = jnp.dot(xb, w1b[...], preferred_element_type=F32)
        b = jnp.dot(xb, w3b[...], preferred_element_type=F32)
        hb = (a * jax.nn.sigmoid(a)) * b
        _store_row_tiles(y_ref, jnp.dot(hb.astype(BF16), w2b[...], preferred_element_type=F32))

    @pl.when(i >= nused_ref[0])
    def _():
        y_ref[...] = jnp.zeros_like(y_ref)


def _experts(blk_e, n_used, xpad, w1, w3, w2):
    tm = EXPERT_ROWS * ROW_CHUNKS
    return pl.pallas_call(
        _experts_body,
        grid_spec=pltpu.PrefetchScalarGridSpec(
            num_scalar_prefetch=2,
            grid=(xpad.shape[0] // tm,),
            in_specs=[
                pl.BlockSpec((tm, LANES), lambda i, be, nu: (jnp.minimum(i, jnp.maximum(nu[0] - 1, 0)), 0)),
                pl.BlockSpec((1, D_MODEL, D_EXPERT), lambda i, be, nu: (be[i], 0, 0)),
                pl.BlockSpec((1, D_MODEL, D_EXPERT), lambda i, be, nu: (be[i], 0, 0)),
                pl.BlockSpec((1, D_EXPERT, D_MODEL), lambda i, be, nu: (be[i], 0, 0)),
            ],
            out_specs=pl.BlockSpec((tm, LANES), lambda i, be, nu: (i, 0)),
            scratch_shapes=[
                pltpu.VMEM((D_MODEL, D_EXPERT), BF16),
                pltpu.VMEM((D_MODEL, D_EXPERT), BF16),
                pltpu.VMEM((D_EXPERT, D_MODEL), BF16),
            ],
        ),
        out_shape=jax.ShapeDtypeStruct(xpad.shape, xpad.dtype),
        compiler_params=_cparams("arbitrary"),
        name="experts",
    )(blk_e, n_used, xpad, w1, w3, w2)


def _combine_body(dest_ref, dest_next_ref, h1_ref, gate_ref, l2g_ref, l2b_ref, ypad_ref, o_ref, buf, sem, *, tm):
    i = pl.program_id(0)
    half = i % 2

    def row_copy(src_row, r, slot, hf):
        return pltpu.make_async_copy(_row_tile(ypad_ref, src_row), _row_tile(buf.at[2 * hf + slot], r), sem.at[hf])

    def gather(idx_ref, hf):
        def start(g, carry):
            r0 = g * DMA_GROUP
            src = [idx_ref[0, 0, 2 * r0 + j] for j in range(2 * DMA_GROUP)]
            for j in range(2 * DMA_GROUP):
                row_copy(src[j], r0 + j // 2, j % 2, hf).start()
            return carry
        lax.fori_loop(0, tm // DMA_GROUP, start, 0)

    @pl.when(i == 0)
    def _():
        gather(dest_ref, 0)

    @pl.when(i + 1 < pl.num_programs(0))
    def _():
        gather(dest_next_ref, 1 - half)

    def wait(g, carry):
        for j in range(2 * DMA_GROUP):
            row_copy(0, 0, j % 2, half).wait()
        return carry

    lax.fori_loop(0, tm // DMA_GROUP, wait, 0)
    gate = gate_ref[...]
    ffn = (_load_row_tiles(buf.at[2 * half], tm).astype(F32) * gate[:, 0:1]
           + _load_row_tiles(buf.at[2 * half + 1], tm).astype(F32) * gate[:, 1:2])
    o_ref[...] = _layer_norm(DEEPNORM_ALPHA * h1_ref[...] + ffn, l2g_ref[...], l2b_ref[...])


def _combine(dest_smem, h1, gates, l2g, l2b, ypad, *, tm):
    n = h1.shape[0]
    steps = n // tm
    return pl.pallas_call(
        functools.partial(_combine_body, tm=tm),
        grid=(steps,),
        in_specs=[
            pl.BlockSpec((1, 1, 2 * tm), lambda i: (i, 0, 0), memory_space=pltpu.SMEM),
            pl.BlockSpec((1, 1, 2 * tm), lambda i: (jnp.minimum(i + 1, steps - 1), 0, 0), memory_space=pltpu.SMEM),
            pl.BlockSpec((tm, D_MODEL), lambda i: (i, 0)),
            pl.BlockSpec((tm, LANES), lambda i: (i, 0)),
            pl.BlockSpec((1, D_MODEL), lambda i: (0, 0)),
            pl.BlockSpec((1, D_MODEL), lambda i: (0, 0)),
            pl.BlockSpec(memory_space=pl.ANY),
        ],
        out_specs=pl.BlockSpec((tm, D_MODEL), lambda i: (i, 0)),
        scratch_shapes=[
            pltpu.VMEM((4, tm * ROW_CHUNKS, LANES), jnp.uint32),
            pltpu.SemaphoreType.DMA((2,)),
        ],
        out_shape=jax.ShapeDtypeStruct((n, D_MODEL), F32),
        compiler_params=_cparams("arbitrary"),
        name="combine",
    )(dest_smem, dest_smem, h1, gates, l2g, l2b, ypad)


def _row(v, width=None):
    v = v.reshape(1, -1).astype(F32)
    if width is not None and v.shape[1] < width:
        v = jnp.pad(v, ((0, 0), (0, width - v.shape[1])))
    return v


def kernel(x, ln_in_g, ln_in_b, w_in, b_in, gm_ln_g, gm_ln_b, gm_w_s, gm_b_s, dn_conv_w, dn_a_log,
           dn_dt_bias, dn_norm_w, w_pa, w_pb, w_o, ln1_g, ln1_b, w_rg, b_rg, w_re, b_re, w1, w3, w2,
           ln2_g, ln2_b):
    bsz, t, d = x.shape
    n = bsz * t
    x2 = x.reshape(n, d)
    l = 0

    wi, bi = w_in[l], b_in[l]
    c_ab = N_FRONT_BLOCKS * D_MODEL
    c_gate = c_ab + 2 * DN_HEADS
    w_ab = jnp.pad(wi[:, c_ab:c_gate], ((0, 0), (0, LANES - 2 * DN_HEADS))).astype(BF16)
    b_ab = _row(bi[c_ab:c_gate], LANES)
    w_gates = wi[:, c_gate:]
    b_proj = jnp.concatenate([bi[:c_ab], bi[c_gate:]]).reshape(1, PROJ_WIDTH)

    ln_g, ln_b = _row(ln_in_g), _row(ln_in_b)
    h, ab = _ln_ab(x2, ln_g, ln_b, w_ab, b_ab, tm=min(1024, n))
    proj = _inproj(h, wi, w_gates, b_proj, tm=min(2048, n))

    bs_full = jnp.broadcast_to(gm_b_s[l][:, :, None], (GM_GROUPS, GM_BLOCK, LANES))
    ya = _gmlp(proj, _row(gm_ln_g[l]), _row(gm_ln_b[l]), gm_w_s[l], bs_full)

    yb = _deltanet(proj, ab, dn_conv_w[l], _row(dn_a_log[l], LANES), _row(dn_dt_bias[l], LANES),
                   _row(dn_norm_w[l]), bsz=bsz, t=t, nb=min(4, bsz))

    w_r = jnp.concatenate([w_rg[l], w_re[l], jnp.zeros((d, LANES - MOE_GROUPS - N_EXPERTS), F32)], axis=1)
    w_r_hi = w_r.astype(BF16)
    w_r_lo = (w_r - w_r_hi.astype(F32)).astype(BF16)
    b_r = _row(jnp.concatenate([b_rg[l], b_re[l]]), LANES)
    h1, h1_rows, logits = _merge(x2, ya, yb, proj, ln_g, ln_b, w_pa[l].astype(BF16), w_pb[l].astype(BF16),
                        w_o[l].astype(BF16), _row(ln1_g[l]), _row(ln1_b[l]), w_r_hi, w_r_lo, b_r,
                        tm=min(1024, n))

    gates, idx, counts = _route(logits, tm=min(1024, n))

    tm_e = EXPERT_ROWS
    cnt = counts[0, :N_EXPERTS]
    padded = (cnt + tm_e - 1) // tm_e * tm_e
    pad_end = jnp.cumsum(padded)
    pad_start = (pad_end - padded).astype(jnp.int32)
    n_blocks = (2 * n + N_EXPERTS * (tm_e - 1) + tm_e - 1) // tm_e
    blk_start = jnp.arange(n_blocks, dtype=jnp.int32) * tm_e
    blk_e = jnp.minimum(jnp.sum(pad_end[None, :] <= blk_start[:, None], -1), N_EXPERTS - 1).astype(jnp.int32)
    n_used = (pad_end[-1] // tm_e).astype(jnp.int32).reshape(1)

    ps_row = jnp.pad(pad_start, (0, LANES - N_EXPERTS)).reshape(1, LANES)
    dest = _dest(idx, ps_row, tm=min(1024, n))[:, :2]
    tm_d, tm_c = min(1024, n), 256
    xpad0 = jnp.zeros((n_blocks * tm_e * ROW_CHUNKS, LANES), jnp.uint32)
    xpad = _dispatch(dest.reshape(n // tm_d, 1, 2 * tm_d), h1_rows, xpad0, tm=tm_d)
    ypad = _experts(blk_e, n_used, xpad, w1[l], w3[l], w2[l])
    out = _combine(dest.reshape(n // tm_c, 1, 2 * tm_c), h1, gates, _row(ln2_g[l]), _row(ln2_b[l]), ypad,
                   tm=tm_c)
    return out.reshape(bsz, t, d)
```

```python
import functools
import math

import jax
import jax.numpy as jnp
from jax import lax
from jax.experimental import pallas as pl
from jax.experimental.pallas import tpu as pltpu

D_MODEL = 1024
GM_GROUPS = 8
GM_BLOCK = 128
GM_CHUNK = 64
DN_HEADS = 8
DN_DK = 128
DN_CONV = 4
MOE_GROUPS = 4
MOE_EPG = 8
N_EXPERTS = MOE_GROUPS * MOE_EPG
D_EXPERT = D_MODEL // 2
LN_EPS = 1e-5
RMS_EPS = 1e-6
L2_EPS = 1e-6
DEEPNORM_ALPHA = 2.0 ** 0.25

LANES = 128
COL_U, COL_V, COL_Q, COL_K, COL_VV, COL_Z, COL_GA, COL_GB = range(8)
N_PROJ_BLOCKS = 8
N_FRONT_BLOCKS = 6
PROJ_WIDTH = N_PROJ_BLOCKS * D_MODEL

DN_CHUNK = 64
CARRY_ROWS = 16
EXPERT_ROWS = 512
VMEM_LIMIT = 56 * 1024 * 1024
SUBLANES = 8
ROW_WORDS = D_MODEL // 2
ROW_CHUNKS = ROW_WORDS // LANES

F32 = jnp.float32
BF16 = jnp.bfloat16


def _cparams(*sem):
    return pltpu.CompilerParams(dimension_semantics=sem, vmem_limit_bytes=VMEM_LIMIT)


def _store_row_tiles(ref, val):
    m = val.shape[0]
    hi = pltpu.bitcast(val[:, :ROW_WORDS].astype(BF16).astype(F32), jnp.uint32)
    lo = pltpu.bitcast(val[:, ROW_WORDS:].astype(BF16).astype(F32), jnp.uint32)
    words = hi | (lo >> 16)
    for ch in range(ROW_CHUNKS):
        ref[pl.ds(ch, m, stride=ROW_CHUNKS), :] = words[:, ch * LANES:(ch + 1) * LANES]


def _load_row_tiles(ref, m):
    words = jnp.concatenate([ref[pl.ds(ch, m, stride=ROW_CHUNKS), :] for ch in range(ROW_CHUNKS)], axis=1)
    hi = pltpu.bitcast(words & jnp.uint32(0xFFFF0000), F32)
    lo = pltpu.bitcast(words << 16, F32)
    return jnp.concatenate([hi, lo], axis=1).astype(BF16)


def _layer_norm(x, g, b):
    mu = jnp.mean(x, -1, keepdims=True)
    xc = x - mu
    var = jnp.mean(xc * xc, -1, keepdims=True)
    return xc * lax.rsqrt(var + LN_EPS) * g + b


def _dot(a, b):
    return jnp.dot(a.astype(BF16), b.astype(BF16), preferred_element_type=F32)


def _dot_nt(a, b):
    return lax.dot_general(a.astype(BF16), b.astype(BF16), (((1,), (1,)), ((), ())),
                           preferred_element_type=F32)


def _ln_ab_body(x_ref, g_ref, b_ref, wab_ref, bab_ref, h_ref, ab_ref):
    h = _layer_norm(x_ref[...], g_ref[...], b_ref[...]).astype(BF16)
    h_ref[...] = h
    ab_ref[...] = jnp.dot(h, wab_ref[...], preferred_element_type=F32) + bab_ref[...]


def _ln_ab(x2, ln_g, ln_b, w_ab, b_ab, *, tm):
    n = x2.shape[0]
    return pl.pallas_call(
        _ln_ab_body,
        grid=(n // tm,),
        in_specs=[
            pl.BlockSpec((tm, D_MODEL), lambda i: (i, 0)),
            pl.BlockSpec((1, D_MODEL), lambda i: (0, 0)),
            pl.BlockSpec((1, D_MODEL), lambda i: (0, 0)),
            pl.BlockSpec((D_MODEL, LANES), lambda i: (0, 0)),
            pl.BlockSpec((1, LANES), lambda i: (0, 0)),
        ],
        out_specs=[pl.BlockSpec((tm, D_MODEL), lambda i: (i, 0)), pl.BlockSpec((tm, LANES), lambda i: (i, 0))],
        out_shape=[jax.ShapeDtypeStruct((n, D_MODEL), BF16), jax.ShapeDtypeStruct((n, LANES), F32)],
        compiler_params=_cparams("parallel"),
        name="ln_ab",
    )(x2, ln_g, ln_b, w_ab, b_ab)


def _inproj_body(h_ref, wa_ref, wb_ref, bias_ref, o_ref, w_scr):
    j = pl.program_id(0)

    @pl.when((pl.program_id(1) == 0) & (j < N_FRONT_BLOCKS))
    def _():
        w_scr[...] = wa_ref[...].astype(BF16)

    @pl.when((pl.program_id(1) == 0) & (j >= N_FRONT_BLOCKS))
    def _():
        w_scr[...] = wb_ref[...].astype(BF16)

    o_ref[...] = (jnp.dot(h_ref[...], w_scr[...], preferred_element_type=F32) + bias_ref[...]).astype(o_ref.dtype)


def _inproj(h, w_in, w_gates, bias, *, tm=1024):
    n = h.shape[0]
    return pl.pallas_call(
        _inproj_body,
        grid=(N_PROJ_BLOCKS, n // tm),
        in_specs=[
            pl.BlockSpec((tm, D_MODEL), lambda j, i: (i, 0)),
            pl.BlockSpec((D_MODEL, D_MODEL), lambda j, i: (0, jnp.minimum(j, N_FRONT_BLOCKS - 1))),
            pl.BlockSpec((D_MODEL, D_MODEL), lambda j, i: (0, jnp.maximum(j - N_FRONT_BLOCKS, 0))),
            pl.BlockSpec((1, D_MODEL), lambda j, i: (0, j)),
        ],
        out_specs=pl.BlockSpec((tm, D_MODEL), lambda j, i: (i, j)),
        out_shape=jax.ShapeDtypeStruct((n, N_PROJ_BLOCKS * D_MODEL), BF16),
        scratch_shapes=[pltpu.VMEM((D_MODEL, D_MODEL), BF16)],
        compiler_params=_cparams("arbitrary", "arbitrary"),
        name="inproj",
    )(h, w_in, w_gates, bias)


def _gelu(x):
    return 0.5 * x * (1.0 + lax.erf(x * (1.0 / math.sqrt(2.0))))


def _gmlp_body(u_ref, v_ref, lng_ref, lnb_ref, ws_ref, bs_ref, o_ref, *, nblk):
    u = _gelu(u_ref[...].astype(F32))
    v = _layer_norm(_gelu(v_ref[...].astype(F32)), lng_ref[...], lnb_ref[...]).astype(BF16)
    row_chunk = lax.broadcasted_iota(jnp.int32, (GM_BLOCK, GM_BLOCK), 0) // GM_CHUNK
    col_chunk = lax.broadcasted_iota(jnp.int32, (GM_BLOCK, GM_BLOCK), 1) // GM_CHUNK
    causal = col_chunk <= row_chunk
    for g in range(GM_GROUPS):
        cols = slice(g * LANES, (g + 1) * LANES)
        w = jnp.where(causal, ws_ref[g], 0.0).astype(BF16)
        for blk in range(nblk):
            rows = slice(blk * GM_BLOCK, (blk + 1) * GM_BLOCK)
            s = jnp.dot(w, v[rows, cols], preferred_element_type=F32) + bs_ref[g]
            o_ref[rows, cols] = (u[rows, cols] * s).astype(o_ref.dtype)


def _gmlp(proj, ln_g, ln_b, w_s, b_s_full, *, nblk=4):
    n = proj.shape[0]
    rows = nblk * GM_BLOCK
    return pl.pallas_call(
        functools.partial(_gmlp_body, nblk=nblk),
        grid=(n // rows,),
        in_specs=[
            pl.BlockSpec((rows, D_MODEL), lambda i: (i, COL_U)),
            pl.BlockSpec((rows, D_MODEL), lambda i: (i, COL_V)),
            pl.BlockSpec((1, D_MODEL), lambda i: (0, 0)),
            pl.BlockSpec((1, D_MODEL), lambda i: (0, 0)),
            pl.BlockSpec((GM_GROUPS, GM_BLOCK, GM_BLOCK), lambda i: (0, 0, 0)),
            pl.BlockSpec((GM_GROUPS, GM_BLOCK, LANES), lambda i: (0, 0, 0)),
        ],
        out_specs=pl.BlockSpec((rows, D_MODEL), lambda i: (i, 0)),
        out_shape=jax.ShapeDtypeStruct((n, D_MODEL), BF16),
        compiler_params=_cparams("parallel"),
        name="gmlp",
    )(proj, proj, ln_g, ln_b, w_s, b_s_full)


def _cumsum_rows(x):
    n = x.shape[0]
    row = lax.broadcasted_iota(jnp.int32, x.shape, 0)
    shift = 1
    while shift < n:
        x = x + jnp.where(row >= shift, pltpu.roll(x, shift, 0), 0.0)
        shift *= 2
    return x


def _deltanet_body(q_ref, k_ref, v_ref, z_ref, ab_ref, cw_ref, alog_ref, dtb_ref, nw_ref, o_ref,
                   xs_ref, qkv_ref, state_ref, kq_ref, m_ref, r0_ref, qd_ref, kd_ref, cd_ref, zg_ref, *, nb):
    c = DN_CHUNK
    w3 = 3 * D_MODEL
    heads = range(DN_HEADS)
    units = range(nb * DN_HEADS)
    hsl = lambda base, h: slice(base + h * DN_DK, base + (h + 1) * DN_DK)

    @pl.when(pl.program_id(1) == 0)
    def _():
        xs_ref[:, 0:CARRY_ROWS, :] = jnp.zeros((nb, CARRY_ROWS, w3), BF16)
        for ref in (qkv_ref, state_ref, kq_ref, m_ref, r0_ref, qd_ref, kd_ref, cd_ref, zg_ref):
            ref[...] = jnp.zeros_like(ref)

    kq_in = [kq_ref[u] for u in units]
    kq = [lax.dot_general(kq_in[u], kq_in[u], (((1,), (1,)), ((), ())), preferred_element_type=F32)
          for u in units]
    pm = [kq[u] * m_ref[u] for u in units]
    attn = [x[c:, 0:c].astype(BF16) for x in pm]

    lane = lax.broadcasted_iota(jnp.int32, (c, 2 * c), 1)
    right = lane >= c
    eye_right = jnp.where(lane - c == lax.broadcasted_iota(jnp.int32, (c, 2 * c), 0), 1.0, 0.0)
    pa = [x[0:c] + eye_right for x in pm]
    span = 1
    while span < c:
        pab = [x.astype(BF16) for x in pa]
        pa = [jnp.dot(pab[u][:, 0:c], pab[u], preferred_element_type=F32) + jnp.where(right, pa[u], 0.0)
              for u in units]
        span *= 2
    r = [jnp.dot(pa[u].astype(BF16), r0_ref[u], preferred_element_type=F32) for u in units]

    s_old = [state_ref[u] for u in units]
    ws = [jnp.dot(jnp.concatenate([r[u][:, DN_DK:].astype(BF16), qd_ref[u]], axis=0),
                  s_old[u].astype(BF16), preferred_element_type=F32) for u in units]
    v_new = [(r[u][:, 0:DN_DK] - ws[u][0:c]).astype(BF16) for u in units]
    o = [ws[u][c:] + jnp.dot(attn[u], v_new[u], preferred_element_type=F32) for u in units]
    ds = [lax.dot_general(kd_ref[u], v_new[u], (((0,), (0,)), ((), ())), preferred_element_type=F32)
          for u in units]
    for u in units:
        state_ref[u] = s_old[u] * cd_ref[u][0:1, :] + ds[u]
    for b in range(nb):
        for h in heads:
            u = b * DN_HEADS + h
            on = o[u] * lax.rsqrt(jnp.mean(o[u] * o[u], -1, keepdims=True) + RMS_EPS)
            o_ref[b, :, hsl(0, h)] = (on * zg_ref[b, :, hsl(0, h)]).astype(o_ref.dtype)

    ri = lax.broadcasted_iota(jnp.int32, (c, c), 0)
    ci = lax.broadcasted_iota(jnp.int32, (c, c), 1)
    causal = ri >= ci
    strict = ri > ci
    for b in range(nb):
        qkv = qkv_ref[b]
        ab = ab_ref[b]
        xg = ab + dtb_ref[...]
        softplus = jnp.maximum(xg, 0.0) + jnp.log1p(jnp.exp(-jnp.abs(xg)))
        gcum = _cumsum_rows(-jnp.exp(alog_ref[...]) * softplus)
        beta_all = jax.nn.sigmoid(ab)
        gcum_sq = gcum if c == LANES else jnp.concatenate([gcum, jnp.zeros((LANES - c, LANES), F32)], axis=0)
        gcum_t = gcum_sq.T

        z = z_ref[b].astype(F32)
        zg_ref[b] = z * jax.nn.sigmoid(z) * jnp.concatenate([nw_ref[...]] * DN_HEADS, axis=1)

        for h in heads:
            u = b * DN_HEADS + h
            qh, kh, vh = qkv[:, hsl(0, h)], qkv[:, hsl(D_MODEL, h)], qkv[:, hsl(2 * D_MODEL, h)]
            qh = qh * (lax.rsqrt(jnp.sum(qh * qh, -1, keepdims=True) + L2_EPS) * (DN_DK ** -0.5))
            kh = kh * lax.rsqrt(jnp.sum(kh * kh, -1, keepdims=True) + L2_EPS)
            g_col = gcum[:, h:h + 1]
            beta = beta_all[:, DN_HEADS + h:DN_HEADS + h + 1]
            g_last = gcum[c - 1:c, h:h + 1]
            decay = jnp.where(causal, jnp.exp(jnp.where(causal, g_col - gcum_t[h:h + 1, 0:c], 0.0)), 0.0)
            eg = jnp.exp(g_col)
            kq_ref[u, 0:c, :] = kh.astype(BF16)
            kq_ref[u, c:, :] = qh.astype(BF16)
            m_ref[u, 0:c, 0:c] = jnp.where(strict, -beta * decay, 0.0)
            m_ref[u, c:, 0:c] = decay
            r0_ref[u, c:, 0:DN_DK] = (vh * beta).astype(BF16)
            r0_ref[u, c:, DN_DK:] = (kh * (beta * eg)).astype(BF16)
            qd_ref[u] = (qh * eg).astype(BF16)
            kd_ref[u] = (kh * jnp.exp(g_last - g_col)).astype(BF16)
            cd_ref[u] = jnp.broadcast_to(jnp.exp(g_last), (SUBLANES, DN_DK))

    nsh = DN_CONV - 1
    srow = lax.broadcasted_iota(jnp.int32, (nsh * c, CARRY_ROWS + c), 0)
    scol = lax.broadcasted_iota(jnp.int32, (nsh * c, CARRY_ROWS + c), 1)
    shift_mat = jnp.where(scol == srow % c + srow // c + (CARRY_ROWS - nsh), 1.0, 0.0).astype(BF16)
    for b in range(nb):
        xs_ref[b, CARRY_ROWS:, 0:D_MODEL] = q_ref[b]
        xs_ref[b, CARRY_ROWS:, D_MODEL:2 * D_MODEL] = k_ref[b]
        xs_ref[b, CARRY_ROWS:, 2 * D_MODEL:] = v_ref[b]
        window = xs_ref[b]
        shifted = jnp.dot(shift_mat, window, preferred_element_type=F32)
        acc = cw_ref[nsh:DN_CONV, :] * window[CARRY_ROWS:].astype(F32)
        for j in range(nsh):
            acc = acc + cw_ref[j:j + 1, :] * shifted[j * c:(j + 1) * c]
        xs_ref[b, 0:CARRY_ROWS, :] = xs_ref[b, c:c + CARRY_ROWS, :]
        qkv_ref[b] = acc * jax.nn.sigmoid(acc)


def _deltanet(proj, ab, conv_w, alog_row, dtb_row, norm_w, *, bsz, t, nb=4):
    c = DN_CHUNK
    nc = t // c
    nu = nb * DN_HEADS
    proj3 = proj.reshape(bsz, t, PROJ_WIDTH)
    ab3 = ab.reshape(bsz, t, LANES)
    conv_chunk = lambda n: jnp.minimum(n, nc - 1)
    prep_chunk = lambda n: jnp.clip(n - 1, 0, nc - 1)
    out_chunk = lambda n: jnp.maximum(n - 2, 0)
    col = lambda j, chunk: pl.BlockSpec((nb, c, D_MODEL), lambda bp, n: (bp, chunk(n), j))
    yb = pl.pallas_call(
        functools.partial(_deltanet_body, nb=nb),
        grid=(bsz // nb, nc + 2),
        in_specs=[
            col(COL_Q, conv_chunk), col(COL_K, conv_chunk), col(COL_VV, conv_chunk), col(COL_Z, prep_chunk),
            pl.BlockSpec((nb, c, LANES), lambda bp, n: (bp, prep_chunk(n), 0)),
            pl.BlockSpec((DN_CONV, 3 * D_MODEL), lambda bp, n: (0, 0)),
            pl.BlockSpec((1, LANES), lambda bp, n: (0, 0)),
            pl.BlockSpec((1, LANES), lambda bp, n: (0, 0)),
            pl.BlockSpec((1, DN_DK), lambda bp, n: (0, 0)),
        ],
        out_specs=pl.BlockSpec((nb, c, D_MODEL), lambda bp, n: (bp, out_chunk(n), 0)),
        out_shape=jax.ShapeDtypeStruct((bsz, t, D_MODEL), BF16),
        scratch_shapes=[
            pltpu.VMEM((nb, CARRY_ROWS + c, 3 * D_MODEL), BF16),
            pltpu.VMEM((nb, c, 3 * D_MODEL), F32),
            pltpu.VMEM((nu, DN_DK, DN_DK), F32),
            pltpu.VMEM((nu, 2 * c, DN_DK), BF16),
            pltpu.VMEM((nu, 2 * c, 2 * c), F32),
            pltpu.VMEM((nu, 2 * c, 2 * DN_DK), BF16),
            pltpu.VMEM((nu, c, DN_DK), BF16),
            pltpu.VMEM((nu, c, DN_DK), BF16),
            pltpu.VMEM((nu, SUBLANES, DN_DK), F32),
            pltpu.VMEM((nb, c, D_MODEL), F32),
        ],
        compiler_params=_cparams("parallel", "arbitrary"),
        name="deltanet",
    )(proj3, proj3, proj3, proj3, ab3, conv_w, alog_row, dtb_row, norm_w)
    return yb.reshape(bsz * t, D_MODEL)


def _merge_body(x_ref, ya_ref, yb_ref, ga_ref, gb_ref, lng_ref, lnb_ref, wpa_ref, wpb_ref, wo_ref,
                l1g_ref, l1b_ref, wr_ref, wrlo_ref, br_ref, h1_ref, h1_rows_ref, logit_ref):
    h = _layer_norm(x_ref[...], lng_ref[...], lnb_ref[...])
    pa = jnp.dot(ya_ref[...], wpa_ref[...], preferred_element_type=F32)
    pb = jnp.dot(yb_ref[...], wpb_ref[...], preferred_element_type=F32)
    merged = jax.nn.sigmoid(ga_ref[...].astype(F32)) * pa + jax.nn.sigmoid(gb_ref[...].astype(F32)) * pb
    mix = jnp.dot(merged.astype(BF16), wo_ref[...], preferred_element_type=F32)
    h1 = _layer_norm(DEEPNORM_ALPHA * h + mix, l1g_ref[...], l1b_ref[...])
    h1_ref[...] = h1
    _store_row_tiles(h1_rows_ref, h1)
    tm = h1.shape[0]
    hi = h1.astype(BF16)
    lo = (h1 - hi.astype(F32)).astype(BF16)
    hw = jnp.dot(jnp.concatenate([hi, lo], axis=0), wr_ref[...], preferred_element_type=F32)
    logit_ref[...] = (hw[0:tm] + hw[tm:] + jnp.dot(hi, wrlo_ref[...], preferred_element_type=F32)
                      + br_ref[...])


def _merge(x2, ya, yb, proj, ln_g, ln_b, w_pa, w_pb, w_o, l1g, l1b, w_r, w_r_lo, b_r, *, tm=512):
    n = x2.shape[0]
    vec = lambda: pl.BlockSpec((1, D_MODEL), lambda i: (0, 0))
    mat = lambda: pl.BlockSpec((D_MODEL, D_MODEL), lambda i: (0, 0))
    return pl.pallas_call(
        _merge_body,
        grid=(n // tm,),
        in_specs=[
            pl.BlockSpec((tm, D_MODEL), lambda i: (i, 0)),
            pl.BlockSpec((tm, D_MODEL), lambda i: (i, 0)),
            pl.BlockSpec((tm, D_MODEL), lambda i: (i, 0)),
            pl.BlockSpec((tm, D_MODEL), lambda i: (i, COL_GA)),
            pl.BlockSpec((tm, D_MODEL), lambda i: (i, COL_GB)),
            vec(), vec(), mat(), mat(), mat(), vec(), vec(),
            pl.BlockSpec((D_MODEL, LANES), lambda i: (0, 0)),
            pl.BlockSpec((D_MODEL, LANES), lambda i: (0, 0)),
            pl.BlockSpec((1, LANES), lambda i: (0, 0)),
        ],
        out_specs=[
            pl.BlockSpec((tm, D_MODEL), lambda i: (i, 0)),
            pl.BlockSpec((tm * ROW_CHUNKS, LANES), lambda i: (i, 0)),
            pl.BlockSpec((tm, LANES), lambda i: (i, 0)),
        ],
        out_shape=[
            jax.ShapeDtypeStruct((n, D_MODEL), F32),
            jax.ShapeDtypeStruct((n * ROW_CHUNKS, LANES), jnp.uint32),
            jax.ShapeDtypeStruct((n, LANES), F32),
        ],
        compiler_params=_cparams("parallel"),
        name="merge",
    )(x2, ya, yb, proj, proj, ln_g, ln_b, w_pa, w_pb, w_o, l1g, l1b, w_r, w_r_lo, b_r)


def _route_body(logit_ref, gate_ref, idx_ref, cnt_ref, carry_ref, *, tm):
    @pl.when(pl.program_id(0) == 0)
    def _():
        carry_ref[...] = jnp.zeros_like(carry_ref)

    lg = logit_ref[...]
    lane_i = lax.broadcasted_iota(jnp.int32, lg.shape, 1)
    lane = lane_i.astype(F32)
    neg = jnp.float32(-jnp.inf)
    big = jnp.float32(1 << 20)

    is_grp = lane_i < MOE_GROUPS
    gl = jnp.where(is_grp, lg, neg)
    gmax = jnp.max(gl, -1, keepdims=True)
    grp = jnp.min(jnp.where(is_grp & (gl == gmax), lane, big), -1, keepdims=True)
    p_grp = 1.0 / jnp.sum(jnp.where(is_grp, jnp.exp(gl - gmax), 0.0), -1, keepdims=True)

    elane = lane - MOE_GROUPS
    in_grp = (elane >= grp * MOE_EPG) & (elane < (grp + 1) * MOE_EPG)
    el = jnp.where(in_grp, lg, neg)
    m1 = jnp.max(el, -1, keepdims=True)
    e1 = jnp.min(jnp.where(in_grp & (el == m1), elane, big), -1, keepdims=True)
    rest = in_grp & (elane != e1)
    el2 = jnp.where(rest, lg, neg)
    m2 = jnp.max(el2, -1, keepdims=True)
    e2 = jnp.min(jnp.where(rest & (el2 == m2), elane, big), -1, keepdims=True)
    t2 = jnp.exp(m2 - m1)
    g1 = p_grp * (1.0 / (1.0 + t2))
    g2 = p_grp * (t2 / (1.0 + t2))

    oh1 = lane == e1
    oh2 = lane == e2
    onehot = jnp.where(oh1 | oh2, 1.0, 0.0)
    ri = lax.broadcasted_iota(jnp.int32, (tm, tm), 0)
    ci = lax.broadcasted_iota(jnp.int32, (tm, tm), 1)
    tri = jnp.where(ci < ri, 1.0, 0.0).astype(BF16)
    before = jnp.dot(tri, onehot.astype(BF16), preferred_element_type=F32) + carry_ref[0:1, :]
    r1 = jnp.sum(jnp.where(oh1, before, 0.0), -1, keepdims=True)
    r2 = jnp.sum(jnp.where(oh2, before, 0.0), -1, keepdims=True)
    total = carry_ref[0:1, :] + jnp.sum(onehot, 0, keepdims=True)
    carry_ref[...] = jnp.broadcast_to(total, carry_ref.shape)
    cnt_ref[...] = jnp.broadcast_to(total, cnt_ref.shape).astype(jnp.int32)

    gate_ref[...] = jnp.where(lane_i == 0, g1, jnp.where(lane_i == 1, g2, 0.0))
    idx_ref[...] = jnp.where(lane_i == 0, e1, jnp.where(lane_i == 1, e2,
                             jnp.where(lane_i == 2, r1, jnp.where(lane_i == 3, r2, 0.0)))).astype(jnp.int32)


def _route(logits, *, tm=1024):
    n = logits.shape[0]
    return pl.pallas_call(
        functools.partial(_route_body, tm=tm),
        grid=(n // tm,),
        in_specs=[pl.BlockSpec((tm, LANES), lambda i: (i, 0))],
        out_specs=[
            pl.BlockSpec((tm, LANES), lambda i: (i, 0)),
            pl.BlockSpec((tm, LANES), lambda i: (i, 0)),
            pl.BlockSpec((8, LANES), lambda i: (0, 0)),
        ],
        out_shape=[
            jax.ShapeDtypeStruct((n, LANES), F32),
            jax.ShapeDtypeStruct((n, LANES), jnp.int32),
            jax.ShapeDtypeStruct((8, LANES), jnp.int32),
        ],
        scratch_shapes=[pltpu.VMEM((8, LANES), F32)],
        compiler_params=_cparams("arbitrary"),
        name="route",
    )(logits)


def _dest_body(idx_ref, ps_ref, o_ref):
    idx = idx_ref[...]
    lane = lax.broadcasted_iota(jnp.int32, idx.shape, 1)
    ps = ps_ref[...].astype(F32)
    d1 = jnp.sum(jnp.where(lane == idx[:, 0:1], ps, 0.0), -1, keepdims=True).astype(jnp.int32) + idx[:, 2:3]
    d2 = jnp.sum(jnp.where(lane == idx[:, 1:2], ps, 0.0), -1, keepdims=True).astype(jnp.int32) + idx[:, 3:4]
    o_ref[...] = jnp.where(lane == 0, d1, jnp.where(lane == 1, d2, 0))


def _dest(idx, pad_start_row, *, tm=1024):
    n = idx.shape[0]
    return pl.pallas_call(
        _dest_body,
        grid=(n // tm,),
        in_specs=[pl.BlockSpec((tm, LANES), lambda i: (i, 0)), pl.BlockSpec((1, LANES), lambda i: (0, 0))],
        out_specs=pl.BlockSpec((tm, LANES), lambda i: (i, 0)),
        out_shape=jax.ShapeDtypeStruct((n, LANES), jnp.int32),
        compiler_params=_cparams("parallel"),
        name="dest",
    )(idx, pad_start_row)


DMA_GROUP = 8


def _row_tile(ref, row):
    start = row * ROW_CHUNKS
    return ref.at[pl.ds(start if isinstance(row, int) else pl.multiple_of(start, ROW_CHUNKS), ROW_CHUNKS)]


def _dispatch_body(dest_ref, h_ref, xin_ref, xpad_ref, sem, *, tm):
    del xin_ref

    def row_copy(src_row, dst_row):
        return pltpu.make_async_copy(_row_tile(h_ref, src_row), _row_tile(xpad_ref, dst_row), sem)

    for r0 in range(0, tm, DMA_GROUP):
        dst = [dest_ref[0, 0, 2 * r0 + j] for j in range(2 * DMA_GROUP)]
        for j in range(2 * DMA_GROUP):
            row_copy(r0 + j // 2, dst[j]).start(priority=j % 2)

    def wait(g, carry):
        for _ in range(2 * DMA_GROUP):
            row_copy(0, 0).wait()
        return carry

    lax.fori_loop(0, tm // DMA_GROUP, wait, 0)


def _dispatch(dest_smem, h1_rows, xpad0, *, tm):
    n = h1_rows.shape[0] // ROW_CHUNKS
    return pl.pallas_call(
        functools.partial(_dispatch_body, tm=tm),
        grid=(n // tm,),
        in_specs=[
            pl.BlockSpec((1, 1, 2 * tm), lambda i: (i, 0, 0), memory_space=pltpu.SMEM),
            pl.BlockSpec((tm * ROW_CHUNKS, LANES), lambda i: (i, 0)),
            pl.BlockSpec(memory_space=pl.ANY),
        ],
        out_specs=pl.BlockSpec(memory_space=pl.ANY),
        scratch_shapes=[pltpu.SemaphoreType.DMA(())],
        out_shape=jax.ShapeDtypeStruct(xpad0.shape, xpad0.dtype),
        input_output_aliases={2: 0},
        compiler_params=_cparams("arbitrary"),
        name="dispatch",
    )(dest_smem, h1_rows, xpad0)


def _experts_body(blk_e_ref, nused_ref, x_ref, w1_ref, w3_ref, w2_ref, y_ref, w1b, w3b, w2b):
    i = pl.program_id(0)
    prev = blk_e_ref[jnp.maximum(i - 1, 0)]

    @pl.when((i == 0) | (blk_e_ref[i] != prev))
    def _():
        w1b[...] = w1_ref[0].astype(BF16)
        w3b[...] = w3_ref[0].astype(BF16)
        w2b[...] = w2_ref[0].astype(BF16)

    @pl.when(i < nused_ref[0])
    def _():
        xb = _load_row_tiles(x_ref, EXPERT_ROWS)
        a = jnp.dot(xb, w1b[...], preferred_element_type=F32)
        b = jnp.dot(xb, w3b[...], preferred_element_type=F32)
        hb = (a * jax.nn.sigmoid(a)) * b
        _store_row_tiles(y_ref, jnp.dot(hb.astype(BF16), w2b[...], preferred_element_type=F32))

    @pl.when(i >= nused_ref[0])
    def _():
        y_ref[...] = jnp.zeros_like(y_ref)


def _experts(blk_e, n_used, xpad, w1, w3, w2):
    tm = EXPERT_ROWS * ROW_CHUNKS
    return pl.pallas_call(
        _experts_body,
        grid_spec=pltpu.PrefetchScalarGridSpec(
            num_scalar_prefetch=2,
            grid=(xpad.shape[0] // tm,),
            in_specs=[
                pl.BlockSpec((tm, LANES), lambda i, be, nu: (jnp.minimum(i, jnp.maximum(nu[0] - 1, 0)), 0)),
                pl.BlockSpec((1, D_MODEL, D_EXPERT), lambda i, be, nu: (be[i], 0, 0)),
                pl.BlockSpec((1, D_MODEL, D_EXPERT), lambda i, be, nu: (be[i], 0, 0)),
                pl.BlockSpec((1, D_EXPERT, D_MODEL), lambda i, be, nu: (be[i], 0, 0)),
            ],
            out_specs=pl.BlockSpec((tm, LANES), lambda i, be, nu: (i, 0)),
            scratch_shapes=[
                pltpu.VMEM((D_MODEL, D_EXPERT), BF16),
                pltpu.VMEM((D_MODEL, D_EXPERT), BF16),
                pltpu.VMEM((D_EXPERT, D_MODEL), BF16),
            ],
        ),
        out_shape=jax.ShapeDtypeStruct(xpad.shape, xpad.dtype),
        compiler_params=_cparams("arbitrary"),
        name="experts",
    )(blk_e, n_used, xpad, w1, w3, w2)


def _combine_body(dest_ref, dest_next_ref, h1_ref, gate_ref, l2g_ref, l2b_ref, ypad_ref, o_ref, buf, sem, *, tm):
    i = pl.program_id(0)
    half = i % 2

    def row_copy(src_row, r, slot, hf):
        return pltpu.make_async_copy(_row_tile(ypad_ref, src_row), _row_tile(buf.at[2 * hf + slot], r), sem.at[hf])

    def gather(idx_ref, hf):
        for r0 in range(0, tm, DMA_GROUP):
            src = [idx_ref[0, 0, 2 * r0 + j] for j in range(2 * DMA_GROUP)]
            for j in range(2 * DMA_GROUP):
                row_copy(src[j], r0 + j // 2, j % 2, hf).start(priority=j % 2)

    @pl.when(i == 0)
    def _():
        gather(dest_ref, 0)

    @pl.when(i + 1 < pl.num_programs(0))
    def _():
        gather(dest_next_ref, 1 - half)

    def wait(g, carry):
        for j in range(2 * DMA_GROUP):
            row_copy(0, 0, j % 2, half).wait()
        return carry

    lax.fori_loop(0, tm // DMA_GROUP, wait, 0)
    gate = gate_ref[...]
    ffn = (_load_row_tiles(buf.at[2 * half], tm).astype(F32) * gate[:, 0:1]
           + _load_row_tiles(buf.at[2 * half + 1], tm).astype(F32) * gate[:, 1:2])
    o_ref[...] = _layer_norm(DEEPNORM_ALPHA * h1_ref[...] + ffn, l2g_ref[...], l2b_ref[...])


def _combine(dest_smem, h1, gates, l2g, l2b, ypad, *, tm):
    n = h1.shape[0]
    steps = n // tm
    return pl.pallas_call(
        functools.partial(_combine_body, tm=tm),
        grid=(steps,),
        in_specs=[
            pl.BlockSpec((1, 1, 2 * tm), lambda i: (i, 0, 0), memory_space=pltpu.SMEM),
            pl.BlockSpec((1, 1, 2 * tm), lambda i: (jnp.minimum(i + 1, steps - 1), 0, 0), memory_space=pltpu.SMEM),
            pl.BlockSpec((tm, D_MODEL), lambda i: (i, 0)),
            pl.BlockSpec((tm, LANES), lambda i: (i, 0)),
            pl.BlockSpec((1, D_MODEL), lambda i: (0, 0)),
            pl.BlockSpec((1, D_MODEL), lambda i: (0, 0)),
            pl.BlockSpec(memory_space=pl.ANY),
        ],
        out_specs=pl.BlockSpec((tm, D_MODEL), lambda i: (i, 0)),
        scratch_shapes=[
            pltpu.VMEM((4, tm * ROW_CHUNKS, LANES), jnp.uint32),
            pltpu.SemaphoreType.DMA((2,)),
        ],
        out_shape=jax.ShapeDtypeStruct((n, D_MODEL), F32),
        compiler_params=_cparams("arbitrary"),
        name="combine",
    )(dest_smem, dest_smem, h1, gates, l2g, l2b, ypad)


def _row(v, width=None):
    v = v.reshape(1, -1).astype(F32)
    if width is not None and v.shape[1] < width:
        v = jnp.pad(v, ((0, 0), (0, width - v.shape[1])))
    return v


def kernel(x, ln_in_g, ln_in_b, w_in, b_in, gm_ln_g, gm_ln_b, gm_w_s, gm_b_s, dn_conv_w, dn_a_log,
           dn_dt_bias, dn_norm_w, w_pa, w_pb, w_o, ln1_g, ln1_b, w_rg, b_rg, w_re, b_re, w1, w3, w2,
           ln2_g, ln2_b):
    bsz, t, d = x.shape
    n = bsz * t
    x2 = x.reshape(n, d)
    l = 0

    wi, bi = w_in[l], b_in[l]
    c_ab = N_FRONT_BLOCKS * D_MODEL
    c_gate = c_ab + 2 * DN_HEADS
    w_ab = jnp.pad(wi[:, c_ab:c_gate], ((0, 0), (0, LANES - 2 * DN_HEADS))).astype(BF16)
    b_ab = _row(bi[c_ab:c_gate], LANES)
    w_gates = wi[:, c_gate:]
    b_proj = jnp.concatenate([bi[:c_ab], bi[c_gate:]]).reshape(1, PROJ_WIDTH)

    ln_g, ln_b = _row(ln_in_g), _row(ln_in_b)
    h, ab = _ln_ab(x2, ln_g, ln_b, w_ab, b_ab, tm=min(1024, n))
    proj = _inproj(h, wi, w_gates, b_proj, tm=min(2048, n))

    bs_full = jnp.broadcast_to(gm_b_s[l][:, :, None], (GM_GROUPS, GM_BLOCK, LANES))
    ya = _gmlp(proj, _row(gm_ln_g[l]), _row(gm_ln_b[l]), gm_w_s[l], bs_full)

    yb = _deltanet(proj, ab, dn_conv_w[l], _row(dn_a_log[l], LANES), _row(dn_dt_bias[l], LANES),
                   _row(dn_norm_w[l]), bsz=bsz, t=t, nb=min(4, bsz))

    w_r = jnp.concatenate([w_rg[l], w_re[l], jnp.zeros((d, LANES - MOE_GROUPS - N_EXPERTS), F32)], axis=1)
    w_r_hi = w_r.astype(BF16)
    w_r_lo = (w_r - w_r_hi.astype(F32)).astype(BF16)
    b_r = _row(jnp.concatenate([b_rg[l], b_re[l]]), LANES)
    h1, h1_rows, logits = _merge(x2, ya, yb, proj, ln_g, ln_b, w_pa[l].astype(BF16), w_pb[l].astype(BF16),
                        w_o[l].astype(BF16), _row(ln1_g[l]), _row(ln1_b[l]), w_r_hi, w_r_lo, b_r,
                        tm=min(1024, n))

    gates, idx, counts = _route(logits, tm=min(1024, n))

    tm_e = EXPERT_ROWS
    cnt = counts[0, :N_EXPERTS]
    padded = (cnt + tm_e - 1) // tm_e * tm_e
    pad_end = jnp.cumsum(padded)
    pad_start = (pad_end - padded).astype(jnp.int32)
    n_blocks = (2 * n + N_EXPERTS * (tm_e - 1) + tm_e - 1) // tm_e
    blk_start = jnp.arange(n_blocks, dtype=jnp.int32) * tm_e
    blk_e = jnp.minimum(jnp.sum(pad_end[None, :] <= blk_start[:, None], -1), N_EXPERTS - 1).astype(jnp.int32)
    n_used = (pad_end[-1] // tm_e).astype(jnp.int32).reshape(1)

    ps_row = jnp.pad(pad_start, (0, LANES - N_EXPERTS)).reshape(1, LANES)
    dest = _dest(idx, ps_row, tm=min(1024, n))[:, :2]
    tm_d, tm_c = min(1024, n), 256
    xpad0 = jnp.zeros((n_blocks * tm_e * ROW_CHUNKS, LANES), jnp.uint32)
    xpad = _dispatch(dest.reshape(n // tm_d, 1, 2 * tm_d), h1_rows, xpad0, tm=tm_d)
    ypad = _experts(blk_e, n_used, xpad, w1[l], w3[l], w2[l])
    out = _combine(dest.reshape(n // tm_c, 1, 2 * tm_c), h1, gates, _row(ln2_g[l]), _row(ln2_b[l]), ypad,
                   tm=tm_c)
    return out.reshape(bsz, t, d)
```

```python
import functools
import math

import jax
import jax.numpy as jnp
from jax import lax
from jax.experimental import pallas as pl
from jax.experimental.pallas import tpu as pltpu

D_MODEL = 1024
GM_GROUPS = 8
GM_BLOCK = 128
GM_CHUNK = 64
DN_HEADS = 8
DN_DK = 128
DN_CONV = 4
MOE_GROUPS = 4
MOE_EPG = 8
N_EXPERTS = MOE_GROUPS * MOE_EPG
D_EXPERT = D_MODEL // 2
LN_EPS = 1e-5
RMS_EPS = 1e-6
L2_EPS = 1e-6
DEEPNORM_ALPHA = 2.0 ** 0.25

LANES = 128
COL_U, COL_V, COL_Q, COL_K, COL_VV, COL_Z, COL_GA, COL_GB = range(8)
N_PROJ_BLOCKS = 8
N_FRONT_BLOCKS = 6
PROJ_WIDTH = N_PROJ_BLOCKS * D_MODEL

DN_CHUNK = 64
CARRY_ROWS = 16
EXPERT_ROWS = 512
VMEM_LIMIT = 56 * 1024 * 1024
SUBLANES = 8
ROW_WORDS = D_MODEL // 2
ROW_CHUNKS = ROW_WORDS // LANES

F32 = jnp.float32
BF16 = jnp.bfloat16


def _cparams(*sem):
    return pltpu.CompilerParams(dimension_semantics=sem, vmem_limit_bytes=VMEM_LIMIT)


def _store_row_tiles(ref, val):
    m = val.shape[0]
    hi = pltpu.bitcast(val[:, :ROW_WORDS].astype(BF16).astype(F32), jnp.uint32)
    lo = pltpu.bitcast(val[:, ROW_WORDS:].astype(BF16).astype(F32), jnp.uint32)
    words = hi | (lo >> 16)
    for ch in range(ROW_CHUNKS):
        ref[pl.ds(ch, m, stride=ROW_CHUNKS), :] = words[:, ch * LANES:(ch + 1) * LANES]


def _load_row_tiles(ref, m):
    words = jnp.concatenate([ref[pl.ds(ch, m, stride=ROW_CHUNKS), :] for ch in range(ROW_CHUNKS)], axis=1)
    hi = pltpu.bitcast(words & jnp.uint32(0xFFFF0000), F32)
    lo = pltpu.bitcast(words << 16, F32)
    return jnp.concatenate([hi, lo], axis=1).astype(BF16)


def _layer_norm(x, g, b):
    mu = jnp.mean(x, -1, keepdims=True)
    xc = x - mu
    var = jnp.mean(xc * xc, -1, keepdims=True)
    return xc * lax.rsqrt(var + LN_EPS) * g + b


def _dot(a, b):
    return jnp.dot(a.astype(BF16), b.astype(BF16), preferred_element_type=F32)


def _dot_nt(a, b):
    return lax.dot_general(a.astype(BF16), b.astype(BF16), (((1,), (1,)), ((), ())),
                           preferred_element_type=F32)


def _ln_ab_body(x_ref, g_ref, b_ref, wab_ref, bab_ref, h_ref, ab_ref):
    h = _layer_norm(x_ref[...], g_ref[...], b_ref[...]).astype(BF16)
    h_ref[...] = h
    ab_ref[...] = jnp.dot(h, wab_ref[...], preferred_element_type=F32) + bab_ref[...]


def _ln_ab(x2, ln_g, ln_b, w_ab, b_ab, *, tm):
    n = x2.shape[0]
    return pl.pallas_call(
        _ln_ab_body,
        grid=(n // tm,),
        in_specs=[
            pl.BlockSpec((tm, D_MODEL), lambda i: (i, 0)),
            pl.BlockSpec((1, D_MODEL), lambda i: (0, 0)),
            pl.BlockSpec((1, D_MODEL), lambda i: (0, 0)),
            pl.BlockSpec((D_MODEL, LANES), lambda i: (0, 0)),
            pl.BlockSpec((1, LANES), lambda i: (0, 0)),
        ],
        out_specs=[pl.BlockSpec((tm, D_MODEL), lambda i: (i, 0)), pl.BlockSpec((tm, LANES), lambda i: (i, 0))],
        out_shape=[jax.ShapeDtypeStruct((n, D_MODEL), BF16), jax.ShapeDtypeStruct((n, LANES), F32)],
        compiler_params=_cparams("parallel"),
        name="ln_ab",
    )(x2, ln_g, ln_b, w_ab, b_ab)


def _inproj_body(h_ref, wa_ref, wb_ref, bias_ref, o_ref, w_scr):
    j = pl.program_id(0)

    @pl.when((pl.program_id(1) == 0) & (j < N_FRONT_BLOCKS))
    def _():
        w_scr[...] = wa_ref[...].astype(BF16)

    @pl.when((pl.program_id(1) == 0) & (j >= N_FRONT_BLOCKS))
    def _():
        w_scr[...] = wb_ref[...].astype(BF16)

    o_ref[...] = (jnp.dot(h_ref[...], w_scr[...], preferred_element_type=F32) + bias_ref[...]).astype(o_ref.dtype)


def _inproj(h, w_in, w_gates, bias, *, tm=1024):
    n = h.shape[0]
    return pl.pallas_call(
        _inproj_body,
        grid=(N_PROJ_BLOCKS, n // tm),
        in_specs=[
            pl.BlockSpec((tm, D_MODEL), lambda j, i: (i, 0)),
            pl.BlockSpec((D_MODEL, D_MODEL), lambda j, i: (0, jnp.minimum(j, N_FRONT_BLOCKS - 1))),
            pl.BlockSpec((D_MODEL, D_MODEL), lambda j, i: (0, jnp.maximum(j - N_FRONT_BLOCKS, 0))),
            pl.BlockSpec((1, D_MODEL), lambda j, i: (0, j)),
        ],
        out_specs=pl.BlockSpec((tm, D_MODEL), lambda j, i: (i, j)),
        out_shape=jax.ShapeDtypeStruct((n, N_PROJ_BLOCKS * D_MODEL), BF16),
        scratch_shapes=[pltpu.VMEM((D_MODEL, D_MODEL), BF16)],
        compiler_params=_cparams("arbitrary", "arbitrary"),
        name="inproj",
    )(h, w_in, w_gates, bias)


def _gelu(x):
    return 0.5 * x * (1.0 + lax.erf(x * (1.0 / math.sqrt(2.0))))


def _gmlp_body(u_ref, v_ref, lng_ref, lnb_ref, ws_ref, bs_ref, o_ref, *, nblk):
    u = _gelu(u_ref[...].astype(F32))
    v = _layer_norm(_gelu(v_ref[...].astype(F32)), lng_ref[...], lnb_ref[...]).astype(BF16)
    row_chunk = lax.broadcasted_iota(jnp.int32, (GM_BLOCK, GM_BLOCK), 0) // GM_CHUNK
    col_chunk = lax.broadcasted_iota(jnp.int32, (GM_BLOCK, GM_BLOCK), 1) // GM_CHUNK
    causal = col_chunk <= row_chunk
    for g in range(GM_GROUPS):
        cols = slice(g * LANES, (g + 1) * LANES)
        w = jnp.where(causal, ws_ref[g], 0.0).astype(BF16)
        for blk in range(nblk):
            rows = slice(blk * GM_BLOCK, (blk + 1) * GM_BLOCK)
            s = jnp.dot(w, v[rows, cols], preferred_element_type=F32) + bs_ref[g]
            o_ref[rows, cols] = (u[rows, cols] * s).astype(o_ref.dtype)


def _gmlp(proj, ln_g, ln_b, w_s, b_s_full, *, nblk=4):
    n = proj.shape[0]
    rows = nblk * GM_BLOCK
    return pl.pallas_call(
        functools.partial(_gmlp_body, nblk=nblk),
        grid=(n // rows,),
        in_specs=[
            pl.BlockSpec((rows, D_MODEL), lambda i: (i, COL_U)),
            pl.BlockSpec((rows, D_MODEL), lambda i: (i, COL_V)),
            pl.BlockSpec((1, D_MODEL), lambda i: (0, 0)),
            pl.BlockSpec((1, D_MODEL), lambda i: (0, 0)),
            pl.BlockSpec((GM_GROUPS, GM_BLOCK, GM_BLOCK), lambda i: (0, 0, 0)),
            pl.BlockSpec((GM_GROUPS, GM_BLOCK, LANES), lambda i: (0, 0, 0)),
        ],
        out_specs=pl.BlockSpec((rows, D_MODEL), lambda i: (i, 0)),
        out_shape=jax.ShapeDtypeStruct((n, D_MODEL), BF16),
        compiler_params=_cparams("parallel"),
        name="gmlp",
    )(proj, proj, ln_g, ln_b, w_s, b_s_full)


def _cumsum_rows(x):
    n = x.shape[0]
    row = lax.broadcasted_iota(jnp.int32, x.shape, 0)
    shift = 1
    while shift < n:
        x = x + jnp.where(row >= shift, pltpu.roll(x, shift, 0), 0.0)
        shift *= 2
    return x


def _deltanet_body(q_ref, k_ref, v_ref, z_ref, ab_ref, cw_ref, alog_ref, dtb_ref, nw_ref, o_ref,
                   xs_ref, qkv_ref, state_ref, kq_ref, m_ref, r0_ref, qd_ref, kd_ref, cd_ref, zg_ref, *, nb):
    c = DN_CHUNK
    w3 = 3 * D_MODEL
    heads = range(DN_HEADS)
    units = range(nb * DN_HEADS)
    hsl = lambda base, h: slice(base + h * DN_DK, base + (h + 1) * DN_DK)

    @pl.when(pl.program_id(1) == 0)
    def _():
        xs_ref[:, 0:CARRY_ROWS, :] = jnp.zeros((nb, CARRY_ROWS, w3), BF16)
        for ref in (qkv_ref, state_ref, kq_ref, m_ref, r0_ref, qd_ref, kd_ref, cd_ref, zg_ref):
            ref[...] = jnp.zeros_like(ref)

    kq_in = [kq_ref[u] for u in units]
    kq = [lax.dot_general(kq_in[u], kq_in[u], (((1,), (1,)), ((), ())), preferred_element_type=F32)
          for u in units]
    pm = [kq[u] * m_ref[u] for u in units]
    attn = [x[c:, 0:c].astype(BF16) for x in pm]

    lane = lax.broadcasted_iota(jnp.int32, (c, 2 * c), 1)
    right = lane >= c
    eye_right = jnp.where(lane - c == lax.broadcasted_iota(jnp.int32, (c, 2 * c), 0), 1.0, 0.0)
    pa = [x[0:c] + eye_right for x in pm]
    span = 1
    while span < c:
        pab = [x.astype(BF16) for x in pa]
        pa = [jnp.dot(pab[u][:, 0:c], pab[u], preferred_element_type=F32) + jnp.where(right, pa[u], 0.0)
              for u in units]
        span *= 2
    r = [jnp.dot(pa[u].astype(BF16), r0_ref[u], preferred_element_type=F32) for u in units]

    s_old = [state_ref[u] for u in units]
    ws = [jnp.dot(jnp.concatenate([r[u][:, DN_DK:].astype(BF16), qd_ref[u]], axis=0),
                  s_old[u].astype(BF16), preferred_element_type=F32) for u in units]
    v_new = [(r[u][:, 0:DN_DK] - ws[u][0:c]).astype(BF16) for u in units]
    o = [ws[u][c:] + jnp.dot(attn[u], v_new[u], preferred_element_type=F32) for u in units]
    ds = [lax.dot_general(kd_ref[u], v_new[u], (((0,), (0,)), ((), ())), preferred_element_type=F32)
          for u in units]
    for u in units:
        state_ref[u] = s_old[u] * cd_ref[u][0:1, :] + ds[u]
    for b in range(nb):
        for h in heads:
            u = b * DN_HEADS + h
            on = o[u] * lax.rsqrt(jnp.mean(o[u] * o[u], -1, keepdims=True) + RMS_EPS)
            o_ref[b, :, hsl(0, h)] = (on * zg_ref[b, :, hsl(0, h)]).astype(o_ref.dtype)

    ri = lax.broadcasted_iota(jnp.int32, (c, c), 0)
    ci = lax.broadcasted_iota(jnp.int32, (c, c), 1)
    causal = ri >= ci
    strict = ri > ci
    for b in range(nb):
        qkv = qkv_ref[b]
        ab = ab_ref[b]
        xg = ab + dtb_ref[...]
        softplus = jnp.maximum(xg, 0.0) + jnp.log1p(jnp.exp(-jnp.abs(xg)))
        gcum = _cumsum_rows(-jnp.exp(alog_ref[...]) * softplus)
        beta_all = jax.nn.sigmoid(ab)
        gcum_sq = gcum if c == LANES else jnp.concatenate([gcum, jnp.zeros((LANES - c, LANES), F32)], axis=0)
        gcum_t = gcum_sq.T

        z = z_ref[b].astype(F32)
        zg_ref[b] = z * jax.nn.sigmoid(z) * jnp.concatenate([nw_ref[...]] * DN_HEADS, axis=1)

        for h in heads:
            u = b * DN_HEADS + h
            qh, kh, vh = qkv[:, hsl(0, h)], qkv[:, hsl(D_MODEL, h)], qkv[:, hsl(2 * D_MODEL, h)]
            qh = qh * (lax.rsqrt(jnp.sum(qh * qh, -1, keepdims=True) + L2_EPS) * (DN_DK ** -0.5))
            kh = kh * lax.rsqrt(jnp.sum(kh * kh, -1, keepdims=True) + L2_EPS)
            g_col = gcum[:, h:h + 1]
            beta = beta_all[:, DN_HEADS + h:DN_HEADS + h + 1]
            g_last = gcum[c - 1:c, h:h + 1]
            decay = jnp.where(causal, jnp.exp(jnp.where(causal, g_col - gcum_t[h:h + 1, 0:c], 0.0)), 0.0)
            eg = jnp.exp(g_col)
            kq_ref[u, 0:c, :] = kh.astype(BF16)
            kq_ref[u, c:, :] = qh.astype(BF16)
            m_ref[u, 0:c, 0:c] = jnp.where(strict, -beta * decay, 0.0)
            m_ref[u, c:, 0:c] = decay
            r0_ref[u, c:, 0:DN_DK] = (vh * beta).astype(BF16)
            r0_ref[u, c:, DN_DK:] = (kh * (beta * eg)).astype(BF16)
            qd_ref[u] = (qh * eg).astype(BF16)
            kd_ref[u] = (kh * jnp.exp(g_last - g_col)).astype(BF16)
            cd_ref[u] = jnp.broadcast_to(jnp.exp(g_last), (SUBLANES, DN_DK))

    nsh = DN_CONV - 1
    srow = lax.broadcasted_iota(jnp.int32, (nsh * c, CARRY_ROWS + c), 0)
    scol = lax.broadcasted_iota(jnp.int32, (nsh * c, CARRY_ROWS + c), 1)
    shift_mat = jnp.where(scol == srow % c + srow // c + (CARRY_ROWS - nsh), 1.0, 0.0).astype(BF16)
    for b in range(nb):
        xs_ref[b, CARRY_ROWS:, 0:D_MODEL] = q_ref[b]
        xs_ref[b, CARRY_ROWS:, D_MODEL:2 * D_MODEL] = k_ref[b]
        xs_ref[b, CARRY_ROWS:, 2 * D_MODEL:] = v_ref[b]
        window = xs_ref[b]
        shifted = jnp.dot(shift_mat, window, preferred_element_type=F32)
        acc = cw_ref[nsh:DN_CONV, :] * window[CARRY_ROWS:].astype(F32)
        for j in range(nsh):
            acc = acc + cw_ref[j:j + 1, :] * shifted[j * c:(j + 1) * c]
        xs_ref[b, 0:CARRY_ROWS, :] = xs_ref[b, c:c + CARRY_ROWS, :]
        qkv_ref[b] = acc * jax.nn.sigmoid(acc)


def _deltanet(proj, ab, conv_w, alog_row, dtb_row, norm_w, *, bsz, t, nb=4):
    c = DN_CHUNK
    nc = t // c
    nu = nb * DN_HEADS
    proj3 = proj.reshape(bsz, t, PROJ_WIDTH)
    ab3 = ab.reshape(bsz, t, LANES)
    conv_chunk = lambda n: jnp.minimum(n, nc - 1)
    prep_chunk = lambda n: jnp.clip(n - 1, 0, nc - 1)
    out_chunk = lambda n: jnp.maximum(n - 2, 0)
    col = lambda j, chunk: pl.BlockSpec((nb, c, D_MODEL), lambda bp, n: (bp, chunk(n), j))
    yb = pl.pallas_call(
        functools.partial(_deltanet_body, nb=nb),
        grid=(bsz // nb, nc + 2),
        in_specs=[
            col(COL_Q, conv_chunk), col(COL_K, conv_chunk), col(COL_VV, conv_chunk), col(COL_Z, prep_chunk),
            pl.BlockSpec((nb, c, LANES), lambda bp, n: (bp, prep_chunk(n), 0)),
            pl.BlockSpec((DN_CONV, 3 * D_MODEL), lambda bp, n: (0, 0)),
            pl.BlockSpec((1, LANES), lambda bp, n: (0, 0)),
            pl.BlockSpec((1, LANES), lambda bp, n: (0, 0)),
            pl.BlockSpec((1, DN_DK), lambda bp, n: (0, 0)),
        ],
        out_specs=pl.BlockSpec((nb, c, D_MODEL), lambda bp, n: (bp, out_chunk(n), 0)),
        out_shape=jax.ShapeDtypeStruct((bsz, t, D_MODEL), BF16),
        scratch_shapes=[
            pltpu.VMEM((nb, CARRY_ROWS + c, 3 * D_MODEL), BF16),
            pltpu.VMEM((nb, c, 3 * D_MODEL), F32),
            pltpu.VMEM((nu, DN_DK, DN_DK), F32),
            pltpu.VMEM((nu, 2 * c, DN_DK), BF16),
            pltpu.VMEM((nu, 2 * c, 2 * c), F32),
            pltpu.VMEM((nu, 2 * c, 2 * DN_DK), BF16),
            pltpu.VMEM((nu, c, DN_DK), BF16),
            pltpu.VMEM((nu, c, DN_DK), BF16),
            pltpu.VMEM((nu, SUBLANES, DN_DK), F32),
            pltpu.VMEM((nb, c, D_MODEL), F32),
        ],
        compiler_params=_cparams("parallel", "arbitrary"),
        name="deltanet",
    )(proj3, proj3, proj3, proj3, ab3, conv_w, alog_row, dtb_row, norm_w)
    return yb.reshape(bsz * t, D_MODEL)


def _merge_body(x_ref, ya_ref, yb_ref, ga_ref, gb_ref, lng_ref, lnb_ref, wpa_ref, wpb_ref, wo_ref,
                l1g_ref, l1b_ref, wr_ref, wrlo_ref, br_ref, h1_ref, h1_rows_ref, logit_ref):
    h = _layer_norm(x_ref[...], lng_ref[...], lnb_ref[...])
    pa = jnp.dot(ya_ref[...], wpa_ref[...], preferred_element_type=F32)
    pb = jnp.dot(yb_ref[...], wpb_ref[...], preferred_element_type=F32)
    merged = jax.nn.sigmoid(ga_ref[...].astype(F32)) * pa + jax.nn.sigmoid(gb_ref[...].astype(F32)) * pb
    mix = jnp.dot(merged.astype(BF16), wo_ref[...], preferred_element_type=F32)
    h1 = _layer_norm(DEEPNORM_ALPHA * h + mix, l1g_ref[...], l1b_ref[...])
    h1_ref[...] = h1
    _store_row_tiles(h1_rows_ref, h1)
    tm = h1.shape[0]
    hi = h1.astype(BF16)
    lo = (h1 - hi.astype(F32)).astype(BF16)
    hw = jnp.dot(jnp.concatenate([hi, lo], axis=0), wr_ref[...], preferred_element_type=F32)
    logit_ref[...] = (hw[0:tm] + hw[tm:] + jnp.dot(hi, wrlo_ref[...], preferred_element_type=F32)
                      + br_ref[...])


def _merge(x2, ya, yb, proj, ln_g, ln_b, w_pa, w_pb, w_o, l1g, l1b, w_r, w_r_lo, b_r, *, tm=512):
    n = x2.shape[0]
    vec = lambda: pl.BlockSpec((1, D_MODEL), lambda i: (0, 0))
    mat = lambda: pl.BlockSpec((D_MODEL, D_MODEL), lambda i: (0, 0))
    return pl.pallas_call(
        _merge_body,
        grid=(n // tm,),
        in_specs=[
            pl.BlockSpec((tm, D_MODEL), lambda i: (i, 0)),
            pl.BlockSpec((tm, D_MODEL), lambda i: (i, 0)),
            pl.BlockSpec((tm, D_MODEL), lambda i: (i, 0)),
            pl.BlockSpec((tm, D_MODEL), lambda i: (i, COL_GA)),
            pl.BlockSpec((tm, D_MODEL), lambda i: (i, COL_GB)),
            vec(), vec(), mat(), mat(), mat(), vec(), vec(),
            pl.BlockSpec((D_MODEL, LANES), lambda i: (0, 0)),
            pl.BlockSpec((D_MODEL, LANES), lambda i: (0, 0)),
            pl.BlockSpec((1, LANES), lambda i: (0, 0)),
        ],
        out_specs=[
            pl.BlockSpec((tm, D_MODEL), lambda i: (i, 0)),
            pl.BlockSpec((tm * ROW_CHUNKS, LANES), lambda i: (i, 0)),
            pl.BlockSpec((tm, LANES), lambda i: (i, 0)),
        ],
        out_shape=[
            jax.ShapeDtypeStruct((n, D_MODEL), F32),
            jax.ShapeDtypeStruct((n * ROW_CHUNKS, LANES), jnp.uint32),
            jax.ShapeDtypeStruct((n, LANES), F32),
        ],
        compiler_params=_cparams("parallel"),
        name="merge",
    )(x2, ya, yb, proj, proj, ln_g, ln_b, w_pa, w_pb, w_o, l1g, l1b, w_r, w_r_lo, b_r)


def _route_body(logit_ref, gate_ref, idx_ref, cnt_ref, carry_ref, *, tm):
    @pl.when(pl.program_id(0) == 0)
    def _():
        carry_ref[...] = jnp.zeros_like(carry_ref)

    lg = logit_ref[...]
    lane_i = lax.broadcasted_iota(jnp.int32, lg.shape, 1)
    lane = lane_i.astype(F32)
    neg = jnp.float32(-jnp.inf)
    big = jnp.float32(1 << 20)

    is_grp = lane_i < MOE_GROUPS
    gl = jnp.where(is_grp, lg, neg)
    gmax = jnp.max(gl, -1, keepdims=True)
    grp = jnp.min(jnp.where(is_grp & (gl == gmax), lane, big), -1, keepdims=True)
    p_grp = 1.0 / jnp.sum(jnp.where(is_grp, jnp.exp(gl - gmax), 0.0), -1, keepdims=True)

    elane = lane - MOE_GROUPS
    in_grp = (elane >= grp * MOE_EPG) & (elane < (grp + 1) * MOE_EPG)
    el = jnp.where(in_grp, lg, neg)
    m1 = jnp.max(el, -1, keepdims=True)
    e1 = jnp.min(jnp.where(in_grp & (el == m1), elane, big), -1, keepdims=True)
    rest = in_grp & (elane != e1)
    el2 = jnp.where(rest, lg, neg)
    m2 = jnp.max(el2, -1, keepdims=True)
    e2 = jnp.min(jnp.where(rest & (el2 == m2), elane, big), -1, keepdims=True)
    t2 = jnp.exp(m2 - m1)
    g1 = p_grp * (1.0 / (1.0 + t2))
    g2 = p_grp * (t2 / (1.0 + t2))

    oh1 = lane == e1
    oh2 = lane == e2
    onehot = jnp.where(oh1 | oh2, 1.0, 0.0)
    ri = lax.broadcasted_iota(jnp.int32, (tm, tm), 0)
    ci = lax.broadcasted_iota(jnp.int32, (tm, tm), 1)
    tri = jnp.where(ci < ri, 1.0, 0.0).astype(BF16)
    before = jnp.dot(tri, onehot.astype(BF16), preferred_element_type=F32) + carry_ref[0:1, :]
    r1 = jnp.sum(jnp.where(oh1, before, 0.0), -1, keepdims=True)
    r2 = jnp.sum(jnp.where(oh2, before, 0.0), -1, keepdims=True)
    total = carry_ref[0:1, :] + jnp.sum(onehot, 0, keepdims=True)
    carry_ref[...] = jnp.broadcast_to(total, carry_ref.shape)
    cnt_ref[...] = jnp.broadcast_to(total, cnt_ref.shape).astype(jnp.int32)

    gate_ref[...] = jnp.where(lane_i == 0, g1, jnp.where(lane_i == 1, g2, 0.0))
    idx_ref[...] = jnp.where(lane_i == 0, e1, jnp.where(lane_i == 1, e2,
                             jnp.where(lane_i == 2, r1, jnp.where(lane_i == 3, r2, 0.0)))).astype(jnp.int32)


def _route(logits, *, tm=1024):
    n = logits.shape[0]
    return pl.pallas_call(
        functools.partial(_route_body, tm=tm),
        grid=(n // tm,),
        in_specs=[pl.BlockSpec((tm, LANES), lambda i: (i, 0))],
        out_specs=[
            pl.BlockSpec((tm, LANES), lambda i: (i, 0)),
            pl.BlockSpec((tm, LANES), lambda i: (i, 0)),
            pl.BlockSpec((8, LANES), lambda i: (0, 0)),
        ],
        out_shape=[
            jax.ShapeDtypeStruct((n, LANES), F32),
            jax.ShapeDtypeStruct((n, LANES), jnp.int32),
            jax.ShapeDtypeStruct((8, LANES), jnp.int32),
        ],
        scratch_shapes=[pltpu.VMEM((8, LANES), F32)],
        compiler_params=_cparams("arbitrary"),
        name="route",
    )(logits)


def _dest_body(idx_ref, ps_ref, o_ref):
    idx = idx_ref[...]
    lane = lax.broadcasted_iota(jnp.int32, idx.shape, 1)
    ps = ps_ref[...].astype(F32)
    d1 = jnp.sum(jnp.where(lane == idx[:, 0:1], ps, 0.0), -1, keepdims=True).astype(jnp.int32) + idx[:, 2:3]
    d2 = jnp.sum(jnp.where(lane == idx[:, 1:2], ps, 0.0), -1, keepdims=True).astype(jnp.int32) + idx[:, 3:4]
    o_ref[...] = jnp.where(lane == 0, d1, jnp.where(lane == 1, d2, 0))


def _dest(idx, pad_start_row, *, tm=1024):
    n = idx.shape[0]
    return pl.pallas_call(
        _dest_body,
        grid=(n // tm,),
        in_specs=[pl.BlockSpec((tm, LANES), lambda i: (i, 0)), pl.BlockSpec((1, LANES), lambda i: (0, 0))],
        out_specs=pl.BlockSpec((tm, LANES), lambda i: (i, 0)),
        out_shape=jax.ShapeDtypeStruct((n, LANES), jnp.int32),
        compiler_params=_cparams("parallel"),
        name="dest",
    )(idx, pad_start_row)


DMA_GROUP = 8


def _row_tile(ref, row):
    start = row * ROW_CHUNKS
    return ref.at[pl.ds(start if isinstance(row, int) else pl.multiple_of(start, ROW_CHUNKS), ROW_CHUNKS)]


def _dispatch_body(dest_ref, h_ref, xin_ref, xpad_ref, sem, *, tm):
    del xin_ref

    def row_copy(src_row, dst_row):
        return pltpu.make_async_copy(_row_tile(h_ref, src_row), _row_tile(xpad_ref, dst_row), sem)

    for r0 in range(0, tm, DMA_GROUP):
        dst = [dest_ref[0, 0, 2 * r0 + j] for j in range(2 * DMA_GROUP)]
        for j in range(2 * DMA_GROUP):
            row_copy(r0 + j // 2, dst[j]).start(priority=j % 2)

    def wait(g, carry):
        for _ in range(2 * DMA_GROUP):
            row_copy(0, 0).wait()
        return carry

    lax.fori_loop(0, tm // DMA_GROUP, wait, 0)


def _dispatch(dest_smem, h1_rows, xpad0, *, tm):
    n = h1_rows.shape[0] // ROW_CHUNKS
    return pl.pallas_call(
        functools.partial(_dispatch_body, tm=tm),
        grid=(n // tm,),
        in_specs=[
            pl.BlockSpec((1, 1, 2 * tm), lambda i: (i, 0, 0), memory_space=pltpu.SMEM),
            pl.BlockSpec((tm * ROW_CHUNKS, LANES), lambda i: (i, 0)),
            pl.BlockSpec(memory_space=pl.ANY),
        ],
        out_specs=pl.BlockSpec(memory_space=pl.ANY),
        scratch_shapes=[pltpu.SemaphoreType.DMA(())],
        out_shape=jax.ShapeDtypeStruct(xpad0.shape, xpad0.dtype),
        input_output_aliases={2: 0},
        compiler_params=_cparams("arbitrary"),
        name="dispatch",
    )(dest_smem, h1_rows, xpad0)


def _experts_body(blk_e_ref, next_e_ref, slot_ref, nused_ref, x_ref, w1_ref, w3_ref, w2_ref, y_ref,
                  st1, st3, st2, w1b, w3b, w2b, sem):
    i = pl.program_id(0)
    e = blk_e_ref[i]
    first_of_expert = (i == 0) | (e != blk_e_ref[jnp.maximum(i - 1, 0)])

    def weight_copies(expert, slot):
        return [pltpu.make_async_copy(w_ref.at[expert], st.at[slot], sem.at[slot])
                for w_ref, st in ((w1_ref, st1), (w3_ref, st3), (w2_ref, st2))]

    @pl.when(i == 0)
    def _():
        for cp in weight_copies(e, slot_ref[0]):
            cp.start()

    @pl.when(first_of_expert)
    def _():
        slot = slot_ref[i]
        for cp in weight_copies(e, slot):
            cp.wait()
        w1b[...] = st1[slot].astype(BF16)
        w3b[...] = st3[slot].astype(BF16)
        w2b[...] = st2[slot].astype(BF16)

        @pl.when(next_e_ref[i] >= 0)
        def _():
            for cp in weight_copies(next_e_ref[i], 1 - slot):
                cp.start()

    @pl.when(i < nused_ref[0])
    def _():
        xb = _load_row_tiles(x_ref, EXPERT_ROWS)
        a = jnp.dot(xb, w1b[...], preferred_element_type=F32)
        b = jnp.dot(xb, w3b[...], preferred_element_type=F32)
        hb = (a * jax.nn.sigmoid(a)) * b
        _store_row_tiles(y_ref, jnp.dot(hb.astype(BF16), w2b[...], preferred_element_type=F32))

    @pl.when(i >= nused_ref[0])
    def _():
        y_ref[...] = jnp.zeros_like(y_ref)


def _experts(blk_e, next_e, slot, n_used, xpad, w1, w3, w2):
    tm = EXPERT_ROWS * ROW_CHUNKS
    return pl.pallas_call(
        _experts_body,
        grid_spec=pltpu.PrefetchScalarGridSpec(
            num_scalar_prefetch=4,
            grid=(xpad.shape[0] // tm,),
            in_specs=[
                pl.BlockSpec((tm, LANES), lambda i, be, ne, sl, nu: (jnp.minimum(i, jnp.maximum(nu[0] - 1, 0)), 0)),
                pl.BlockSpec(memory_space=pl.ANY),
                pl.BlockSpec(memory_space=pl.ANY),
                pl.BlockSpec(memory_space=pl.ANY),
            ],
            out_specs=pl.BlockSpec((tm, LANES), lambda i, be, ne, sl, nu: (i, 0)),
            scratch_shapes=[
                pltpu.VMEM((2, D_MODEL, D_EXPERT), F32),
                pltpu.VMEM((2, D_MODEL, D_EXPERT), F32),
                pltpu.VMEM((2, D_EXPERT, D_MODEL), F32),
                pltpu.VMEM((D_MODEL, D_EXPERT), BF16),
                pltpu.VMEM((D_MODEL, D_EXPERT), BF16),
                pltpu.VMEM((D_EXPERT, D_MODEL), BF16),
                pltpu.SemaphoreType.DMA((2,)),
            ],
        ),
        out_shape=jax.ShapeDtypeStruct(xpad.shape, xpad.dtype),
        compiler_params=_cparams("arbitrary"),
        name="experts",
    )(blk_e, next_e, slot, n_used, xpad, w1, w3, w2)


def _combine_body(dest_ref, dest_next_ref, h1_ref, gate_ref, l2g_ref, l2b_ref, ypad_ref, o_ref, buf, sem, *, tm):
    i = pl.program_id(0)
    half = i % 2

    def row_copy(src_row, r, slot, hf):
        return pltpu.make_async_copy(_row_tile(ypad_ref, src_row), _row_tile(buf.at[2 * hf + slot], r), sem.at[hf])

    def gather(idx_ref, hf):
        for r0 in range(0, tm, DMA_GROUP):
            src = [idx_ref[0, 0, 2 * r0 + j] for j in range(2 * DMA_GROUP)]
            for j in range(2 * DMA_GROUP):
                row_copy(src[j], r0 + j // 2, j % 2, hf).start(priority=j % 2)

    @pl.when(i == 0)
    def _():
        gather(dest_ref, 0)

    @pl.when(i + 1 < pl.num_programs(0))
    def _():
        gather(dest_next_ref, 1 - half)

    def wait(g, carry):
        for j in range(2 * DMA_GROUP):
            row_copy(0, 0, j % 2, half).wait()
        return carry

    lax.fori_loop(0, tm // DMA_GROUP, wait, 0)
    gate = gate_ref[...]
    ffn = (_load_row_tiles(buf.at[2 * half], tm).astype(F32) * gate[:, 0:1]
           + _load_row_tiles(buf.at[2 * half + 1], tm).astype(F32) * gate[:, 1:2])
    o_ref[...] = _layer_norm(DEEPNORM_ALPHA * h1_ref[...] + ffn, l2g_ref[...], l2b_ref[...])


def _combine(dest_smem, h1, gates, l2g, l2b, ypad, *, tm):
    n = h1.shape[0]
    steps = n // tm
    return pl.pallas_call(
        functools.partial(_combine_body, tm=tm),
        grid=(steps,),
        in_specs=[
            pl.BlockSpec((1, 1, 2 * tm), lambda i: (i, 0, 0), memory_space=pltpu.SMEM),
            pl.BlockSpec((1, 1, 2 * tm), lambda i: (jnp.minimum(i + 1, steps - 1), 0, 0), memory_space=pltpu.SMEM),
            pl.BlockSpec((tm, D_MODEL), lambda i: (i, 0)),
            pl.BlockSpec((tm, LANES), lambda i: (i, 0)),
            pl.BlockSpec((1, D_MODEL), lambda i: (0, 0)),
            pl.BlockSpec((1, D_MODEL), lambda i: (0, 0)),
            pl.BlockSpec(memory_space=pl.ANY),
        ],
        out_specs=pl.BlockSpec((tm, D_MODEL), lambda i: (i, 0)),
        scratch_shapes=[
            pltpu.VMEM((4, tm * ROW_CHUNKS, LANES), jnp.uint32),
            pltpu.SemaphoreType.DMA((2,)),
        ],
        out_shape=jax.ShapeDtypeStruct((n, D_MODEL), F32),
        compiler_params=_cparams("arbitrary"),
        name="combine",
    )(dest_smem, dest_smem, h1, gates, l2g, l2b, ypad)


def _row(v, width=None):
    v = v.reshape(1, -1).astype(F32)
    if width is not None and v.shape[1] < width:
        v = jnp.pad(v, ((0, 0), (0, width - v.shape[1])))
    return v


def kernel(x, ln_in_g, ln_in_b, w_in, b_in, gm_ln_g, gm_ln_b, gm_w_s, gm_b_s, dn_conv_w, dn_a_log,
           dn_dt_bias, dn_norm_w, w_pa, w_pb, w_o, ln1_g, ln1_b, w_rg, b_rg, w_re, b_re, w1, w3, w2,
           ln2_g, ln2_b):
    bsz, t, d = x.shape
    n = bsz * t
    x2 = x.reshape(n, d)
    l = 0

    wi, bi = w_in[l], b_in[l]
    c_ab = N_FRONT_BLOCKS * D_MODEL
    c_gate = c_ab + 2 * DN_HEADS
    w_ab = jnp.pad(wi[:, c_ab:c_gate], ((0, 0), (0, LANES - 2 * DN_HEADS))).astype(BF16)
    b_ab = _row(bi[c_ab:c_gate], LANES)
    w_gates = wi[:, c_gate:]
    b_proj = jnp.concatenate([bi[:c_ab], bi[c_gate:]]).reshape(1, PROJ_WIDTH)

    ln_g, ln_b = _row(ln_in_g), _row(ln_in_b)
    h, ab = _ln_ab(x2, ln_g, ln_b, w_ab, b_ab, tm=min(1024, n))
    proj = _inproj(h, wi, w_gates, b_proj, tm=min(2048, n))

    bs_full = jnp.broadcast_to(gm_b_s[l][:, :, None], (GM_GROUPS, GM_BLOCK, LANES))
    ya = _gmlp(proj, _row(gm_ln_g[l]), _row(gm_ln_b[l]), gm_w_s[l], bs_full)

    yb = _deltanet(proj, ab, dn_conv_w[l], _row(dn_a_log[l], LANES), _row(dn_dt_bias[l], LANES),
                   _row(dn_norm_w[l]), bsz=bsz, t=t, nb=min(4, bsz))

    w_r = jnp.concatenate([w_rg[l], w_re[l], jnp.zeros((d, LANES - MOE_GROUPS - N_EXPERTS), F32)], axis=1)
    w_r_hi = w_r.astype(BF16)
    w_r_lo = (w_r - w_r_hi.astype(F32)).astype(BF16)
    b_r = _row(jnp.concatenate([b_rg[l], b_re[l]]), LANES)
    h1, h1_rows, logits = _merge(x2, ya, yb, proj, ln_g, ln_b, w_pa[l].astype(BF16), w_pb[l].astype(BF16),
                        w_o[l].astype(BF16), _row(ln1_g[l]), _row(ln1_b[l]), w_r_hi, w_r_lo, b_r,
                        tm=min(1024, n))

    gates, idx, counts = _route(logits, tm=min(1024, n))

    tm_e = EXPERT_ROWS
    cnt = counts[0, :N_EXPERTS]
    padded = (cnt + tm_e - 1) // tm_e * tm_e
    pad_end = jnp.cumsum(padded)
    pad_start = (pad_end - padded).astype(jnp.int32)
    n_blocks = (2 * n + N_EXPERTS * (tm_e - 1) + tm_e - 1) // tm_e
    blk_start = jnp.arange(n_blocks, dtype=jnp.int32) * tm_e
    blk_e = jnp.minimum(jnp.sum(pad_end[None, :] <= blk_start[:, None], -1), N_EXPERTS - 1).astype(jnp.int32)
    n_used = (pad_end[-1] // tm_e).astype(jnp.int32).reshape(1)
    blk_ids = jnp.arange(n_blocks, dtype=jnp.int32)
    is_first = jnp.concatenate([jnp.ones((1,), bool), blk_e[1:] != blk_e[:-1]])
    slot = ((jnp.cumsum(is_first) - 1) % 2).astype(jnp.int32)
    later_first = is_first[None, :] & (blk_ids[None, :] > blk_ids[:, None])
    next_first = jnp.min(jnp.where(later_first, blk_ids[None, :], n_blocks), axis=1)
    next_e = jnp.where(next_first < n_blocks, blk_e[jnp.minimum(next_first, n_blocks - 1)], -1).astype(jnp.int32)

    ps_row = jnp.pad(pad_start, (0, LANES - N_EXPERTS)).reshape(1, LANES)
    dest = _dest(idx, ps_row, tm=min(1024, n))[:, :2]
    tm_d, tm_c = min(1024, n), 256
    xpad0 = jnp.zeros((n_blocks * tm_e * ROW_CHUNKS, LANES), jnp.uint32)
    xpad = _dispatch(dest.reshape(n // tm_d, 1, 2 * tm_d), h1_rows, xpad0, tm=tm_d)
    ypad = _experts(blk_e, next_e, slot, n_used, xpad, w1[l], w3[l], w2[l])
    out = _combine(dest.reshape(n // tm_c, 1, 2 * tm_c), h1, gates, _row(ln2_g[l]), _row(ln2_b[l]), ypad,
                   tm=tm_c)
    return out.reshape(bsz, t, d)
```

```python
import functools
import math

import jax
import jax.numpy as jnp
from jax import lax
from jax.experimental import pallas as pl
from jax.experimental.pallas import tpu as pltpu

D_MODEL = 1024
GM_GROUPS = 8
GM_BLOCK = 128
GM_CHUNK = 64
DN_HEADS = 8
DN_DK = 128
DN_CONV = 4
MOE_GROUPS = 4
MOE_EPG = 8
N_EXPERTS = MOE_GROUPS * MOE_EPG
D_EXPERT = D_MODEL // 2
LN_EPS = 1e-5
RMS_EPS = 1e-6
L2_EPS = 1e-6
DEEPNORM_ALPHA = 2.0 ** 0.25

LANES = 128
COL_U, COL_V, COL_Q, COL_K, COL_VV, COL_Z, COL_GA, COL_GB = range(8)
N_PROJ_BLOCKS = 8
N_FRONT_BLOCKS = 6
AB_COLS = 2 * DN_HEADS
PROJ_WIDTH = N_PROJ_BLOCKS * D_MODEL

DN_CHUNK = 64
CARRY_ROWS = 16
EXPERT_ROWS = 512
VMEM_LIMIT = 56 * 1024 * 1024
SUBLANES = 8
ROW_WORDS = D_MODEL // 2
ROW_CHUNKS = ROW_WORDS // LANES

F32 = jnp.float32
BF16 = jnp.bfloat16


def _cparams(*sem):
    return pltpu.CompilerParams(dimension_semantics=sem, vmem_limit_bytes=VMEM_LIMIT)


def _store_row_tiles(ref, val):
    m = val.shape[0]
    hi = pltpu.bitcast(val[:, :ROW_WORDS].astype(BF16).astype(F32), jnp.uint32)
    lo = pltpu.bitcast(val[:, ROW_WORDS:].astype(BF16).astype(F32), jnp.uint32)
    words = hi | (lo >> 16)
    for ch in range(ROW_CHUNKS):
        ref[pl.ds(ch, m, stride=ROW_CHUNKS), :] = words[:, ch * LANES:(ch + 1) * LANES]


def _load_row_tiles(ref, m):
    words = jnp.concatenate([ref[pl.ds(ch, m, stride=ROW_CHUNKS), :] for ch in range(ROW_CHUNKS)], axis=1)
    hi = pltpu.bitcast(words & jnp.uint32(0xFFFF0000), F32)
    lo = pltpu.bitcast(words << 16, F32)
    return jnp.concatenate([hi, lo], axis=1).astype(BF16)


def _layer_norm(x, g, b):
    mu = jnp.mean(x, -1, keepdims=True)
    xc = x - mu
    var = jnp.mean(xc * xc, -1, keepdims=True)
    return xc * lax.rsqrt(var + LN_EPS) * g + b


def _dot(a, b):
    return jnp.dot(a.astype(BF16), b.astype(BF16), preferred_element_type=F32)


def _dot_nt(a, b):
    return lax.dot_general(a.astype(BF16), b.astype(BF16), (((1,), (1,)), ((), ())),
                           preferred_element_type=F32)


def _ln_ab_body(x_ref, g_ref, b_ref, wab_ref, bab_ref, h_ref, ab_ref):
    h = _layer_norm(x_ref[...], g_ref[...], b_ref[...]).astype(BF16)
    h_ref[...] = h
    ab_ref[...] = jnp.dot(h, wab_ref[...], preferred_element_type=F32) + bab_ref[...]


def _ln_ab(x2, ln_g, ln_b, w_ab, b_ab, *, tm):
    n = x2.shape[0]
    return pl.pallas_call(
        _ln_ab_body,
        grid=(n // tm,),
        in_specs=[
            pl.BlockSpec((tm, D_MODEL), lambda i: (i, 0)),
            pl.BlockSpec((1, D_MODEL), lambda i: (0, 0)),
            pl.BlockSpec((1, D_MODEL), lambda i: (0, 0)),
            pl.BlockSpec((D_MODEL, LANES), lambda i: (0, 0)),
            pl.BlockSpec((1, LANES), lambda i: (0, 0)),
        ],
        out_specs=[pl.BlockSpec((tm, D_MODEL), lambda i: (i, 0)), pl.BlockSpec((tm, LANES), lambda i: (i, 0))],
        out_shape=[jax.ShapeDtypeStruct((n, D_MODEL), BF16), jax.ShapeDtypeStruct((n, LANES), F32)],
        compiler_params=_cparams("parallel"),
        name="ln_ab",
    )(x2, ln_g, ln_b, w_ab, b_ab)


def _inproj_body(h_ref, wa_ref, wb_ref, bias_ref, o_ref, w_scr):
    j = pl.program_id(0)

    @pl.when((pl.program_id(1) == 0) & (j < N_FRONT_BLOCKS))
    def _():
        w_scr[...] = wa_ref[...].astype(BF16)

    @pl.when((pl.program_id(1) == 0) & (j >= N_FRONT_BLOCKS))
    def _():
        both = jnp.concatenate([wa_ref[...], wb_ref[...]], axis=1)
        w_scr[...] = both[:, AB_COLS:AB_COLS + D_MODEL].astype(BF16)

    o_ref[...] = (jnp.dot(h_ref[...], w_scr[...], preferred_element_type=F32) + bias_ref[...]).astype(o_ref.dtype)


def _inproj(h, w_in, bias, *, tm=1024):
    n = h.shape[0]
    return pl.pallas_call(
        _inproj_body,
        grid=(N_PROJ_BLOCKS, n // tm),
        in_specs=[
            pl.BlockSpec((tm, D_MODEL), lambda j, i: (i, 0)),
            pl.BlockSpec((D_MODEL, D_MODEL), lambda j, i: (0, j)),
            pl.BlockSpec((D_MODEL, D_MODEL), lambda j, i: (0, j + 1)),
            pl.BlockSpec((1, D_MODEL), lambda j, i: (0, j)),
        ],
        out_specs=pl.BlockSpec((tm, D_MODEL), lambda j, i: (i, j)),
        out_shape=jax.ShapeDtypeStruct((n, N_PROJ_BLOCKS * D_MODEL), BF16),
        scratch_shapes=[pltpu.VMEM((D_MODEL, D_MODEL), BF16)],
        compiler_params=_cparams("arbitrary", "arbitrary"),
        name="inproj",
    )(h, w_in, w_in, bias)


def _gelu(x):
    return 0.5 * x * (1.0 + lax.erf(x * (1.0 / math.sqrt(2.0))))


def _gmlp_body(u_ref, v_ref, lng_ref, lnb_ref, ws_ref, bs_ref, o_ref, *, nblk):
    u = _gelu(u_ref[...].astype(F32))
    v = _layer_norm(_gelu(v_ref[...].astype(F32)), lng_ref[...], lnb_ref[...]).astype(BF16)
    row_chunk = lax.broadcasted_iota(jnp.int32, (GM_BLOCK, GM_BLOCK), 0) // GM_CHUNK
    col_chunk = lax.broadcasted_iota(jnp.int32, (GM_BLOCK, GM_BLOCK), 1) // GM_CHUNK
    causal = col_chunk <= row_chunk
    for g in range(GM_GROUPS):
        cols = slice(g * LANES, (g + 1) * LANES)
        w = jnp.where(causal, ws_ref[g], 0.0).astype(BF16)
        for blk in range(nblk):
            rows = slice(blk * GM_BLOCK, (blk + 1) * GM_BLOCK)
            s = jnp.dot(w, v[rows, cols], preferred_element_type=F32) + bs_ref[g]
            o_ref[rows, cols] = (u[rows, cols] * s).astype(o_ref.dtype)


def _gmlp(proj, ln_g, ln_b, w_s, b_s_full, *, nblk):
    n = proj.shape[0]
    rows = nblk * GM_BLOCK
    return pl.pallas_call(
        functools.partial(_gmlp_body, nblk=nblk),
        grid=(n // rows,),
        in_specs=[
            pl.BlockSpec((rows, D_MODEL), lambda i: (i, COL_U)),
            pl.BlockSpec((rows, D_MODEL), lambda i: (i, COL_V)),
            pl.BlockSpec((1, D_MODEL), lambda i: (0, 0)),
            pl.BlockSpec((1, D_MODEL), lambda i: (0, 0)),
            pl.BlockSpec((GM_GROUPS, GM_BLOCK, GM_BLOCK), lambda i: (0, 0, 0)),
            pl.BlockSpec((GM_GROUPS, GM_BLOCK, LANES), lambda i: (0, 0, 0)),
        ],
        out_specs=pl.BlockSpec((rows, D_MODEL), lambda i: (i, 0)),
        out_shape=jax.ShapeDtypeStruct((n, D_MODEL), BF16),
        compiler_params=_cparams("parallel"),
        name="gmlp",
    )(proj, proj, ln_g, ln_b, w_s, b_s_full)


def _cumsum_rows(x):
    n = x.shape[0]
    row = lax.broadcasted_iota(jnp.int32, x.shape, 0)
    shift = 1
    while shift < n:
        x = x + jnp.where(row >= shift, pltpu.roll(x, shift, 0), 0.0)
        shift *= 2
    return x


def _deltanet_body(q_ref, k_ref, v_ref, z_ref, ab_ref, cw_ref, alog_ref, dtb_ref, nw_ref, o_ref,
                   xs_ref, qkv_ref, state_ref, kq_ref, m_ref, r0_ref, qd_ref, kd_ref, cd_ref, zg_ref, *, nb):
    c = DN_CHUNK
    w3 = 3 * D_MODEL
    heads = range(DN_HEADS)
    units = range(nb * DN_HEADS)
    hsl = lambda base, h: slice(base + h * DN_DK, base + (h + 1) * DN_DK)

    @pl.when(pl.program_id(1) == 0)
    def _():
        xs_ref[:, 0:CARRY_ROWS, :] = jnp.zeros((nb, CARRY_ROWS, w3), BF16)
        for ref in (qkv_ref, state_ref, kq_ref, m_ref, r0_ref, qd_ref, kd_ref, cd_ref, zg_ref):
            ref[...] = jnp.zeros_like(ref)

    kq_in = [kq_ref[u] for u in units]
    kq = [lax.dot_general(kq_in[u], kq_in[u], (((1,), (1,)), ((), ())), preferred_element_type=F32)
          for u in units]
    pm = [kq[u] * m_ref[u] for u in units]
    attn = [x[c:, 0:c].astype(BF16) for x in pm]

    lane = lax.broadcasted_iota(jnp.int32, (c, 2 * c), 1)
    right = lane >= c
    eye_right = jnp.where(lane - c == lax.broadcasted_iota(jnp.int32, (c, 2 * c), 0), 1.0, 0.0)
    pa = [x[0:c] + eye_right for x in pm]
    span = 1
    while span < c:
        pab = [x.astype(BF16) for x in pa]
        pa = [jnp.dot(pab[u][:, 0:c], pab[u], preferred_element_type=F32) + jnp.where(right, pa[u], 0.0)
              for u in units]
        span *= 2
    r = [jnp.dot(pa[u].astype(BF16), r0_ref[u], preferred_element_type=F32) for u in units]

    s_old = [state_ref[u] for u in units]
    ws = [jnp.dot(jnp.concatenate([r[u][:, DN_DK:].astype(BF16), qd_ref[u]], axis=0),
                  s_old[u].astype(BF16), preferred_element_type=F32) for u in units]
    v_new = [(r[u][:, 0:DN_DK] - ws[u][0:c]).astype(BF16) for u in units]
    o = [ws[u][c:] + jnp.dot(attn[u], v_new[u], preferred_element_type=F32) for u in units]
    ds = [lax.dot_general(kd_ref[u], v_new[u], (((0,), (0,)), ((), ())), preferred_element_type=F32)
          for u in units]
    for u in units:
        state_ref[u] = s_old[u] * cd_ref[u][0:1, :] + ds[u]
    for b in range(nb):
        for h in heads:
            u = b * DN_HEADS + h
            on = o[u] * lax.rsqrt(jnp.mean(o[u] * o[u], -1, keepdims=True) + RMS_EPS)
            o_ref[b, :, hsl(0, h)] = (on * zg_ref[b, :, hsl(0, h)]).astype(o_ref.dtype)

    ri = lax.broadcasted_iota(jnp.int32, (c, c), 0)
    ci = lax.broadcasted_iota(jnp.int32, (c, c), 1)
    causal = ri >= ci
    strict = ri > ci
    for b in range(nb):
        qkv = qkv_ref[b]
        ab = ab_ref[b]
        xg = ab + dtb_ref[...]
        softplus = jnp.maximum(xg, 0.0) + jnp.log1p(jnp.exp(-jnp.abs(xg)))
        gcum = _cumsum_rows(-jnp.exp(alog_ref[...]) * softplus)
        beta_all = jax.nn.sigmoid(ab)
        gcum_sq = gcum if c == LANES else jnp.concatenate([gcum, jnp.zeros((LANES - c, LANES), F32)], axis=0)
        gcum_t = gcum_sq.T

        z = z_ref[b].astype(F32)
        zg_ref[b] = z * jax.nn.sigmoid(z) * jnp.concatenate([nw_ref[...]] * DN_HEADS, axis=1)

        for h in heads:
            u = b * DN_HEADS + h
            qh, kh, vh = qkv[:, hsl(0, h)], qkv[:, hsl(D_MODEL, h)], qkv[:, hsl(2 * D_MODEL, h)]
            qh = qh * (lax.rsqrt(jnp.sum(qh * qh, -1, keepdims=True) + L2_EPS) * (DN_DK ** -0.5))
            kh = kh * lax.rsqrt(jnp.sum(kh * kh, -1, keepdims=True) + L2_EPS)
            g_col = gcum[:, h:h + 1]
            beta = beta_all[:, DN_HEADS + h:DN_HEADS + h + 1]
            g_last = gcum[c - 1:c, h:h + 1]
            decay = jnp.where(causal, jnp.exp(jnp.where(causal, g_col - gcum_t[h:h + 1, 0:c], 0.0)), 0.0)
            eg = jnp.exp(g_col)
            kq_ref[u, 0:c, :] = kh.astype(BF16)
            kq_ref[u, c:, :] = qh.astype(BF16)
            m_ref[u, 0:c, 0:c] = jnp.where(strict, -beta * decay, 0.0)
            m_ref[u, c:, 0:c] = decay
            r0_ref[u, c:, 0:DN_DK] = (vh * beta).astype(BF16)
            r0_ref[u, c:, DN_DK:] = (kh * (beta * eg)).astype(BF16)
            qd_ref[u] = (qh * eg).astype(BF16)
            kd_ref[u] = (kh * jnp.exp(g_last - g_col)).astype(BF16)
            cd_ref[u] = jnp.broadcast_to(jnp.exp(g_last), (SUBLANES, DN_DK))

    nsh = DN_CONV - 1
    srow = lax.broadcasted_iota(jnp.int32, (nsh * c, CARRY_ROWS + c), 0)
    scol = lax.broadcasted_iota(jnp.int32, (nsh * c, CARRY_ROWS + c), 1)
    shift_mat = jnp.where(scol == srow % c + srow // c + (CARRY_ROWS - nsh), 1.0, 0.0).astype(BF16)
    for b in range(nb):
        xs_ref[b, CARRY_ROWS:, 0:D_MODEL] = q_ref[b]
        xs_ref[b, CARRY_ROWS:, D_MODEL:2 * D_MODEL] = k_ref[b]
        xs_ref[b, CARRY_ROWS:, 2 * D_MODEL:] = v_ref[b]
        window = xs_ref[b]
        shifted = jnp.dot(shift_mat, window, preferred_element_type=F32)
        acc = cw_ref[nsh:DN_CONV, :] * window[CARRY_ROWS:].astype(F32)
        for j in range(nsh):
            acc = acc + cw_ref[j:j + 1, :] * shifted[j * c:(j + 1) * c]
        xs_ref[b, 0:CARRY_ROWS, :] = xs_ref[b, c:c + CARRY_ROWS, :]
        qkv_ref[b] = acc * jax.nn.sigmoid(acc)


def _deltanet(proj, ab, conv_w, alog_row, dtb_row, norm_w, *, bsz, t, nb=4):
    c = DN_CHUNK
    nc = t // c
    nu = nb * DN_HEADS
    proj3 = proj.reshape(bsz, t, PROJ_WIDTH)
    ab3 = ab.reshape(bsz, t, LANES)
    conv_chunk = lambda n: jnp.minimum(n, nc - 1)
    prep_chunk = lambda n: jnp.clip(n - 1, 0, nc - 1)
    out_chunk = lambda n: jnp.maximum(n - 2, 0)
    col = lambda j, chunk: pl.BlockSpec((nb, c, D_MODEL), lambda bp, n: (bp, chunk(n), j))
    yb = pl.pallas_call(
        functools.partial(_deltanet_body, nb=nb),
        grid=(bsz // nb, nc + 2),
        in_specs=[
            col(COL_Q, conv_chunk), col(COL_K, conv_chunk), col(COL_VV, conv_chunk), col(COL_Z, prep_chunk),
            pl.BlockSpec((nb, c, LANES), lambda bp, n: (bp, prep_chunk(n), 0)),
            pl.BlockSpec((DN_CONV, 3 * D_MODEL), lambda bp, n: (0, 0)),
            pl.BlockSpec((1, LANES), lambda bp, n: (0, 0)),
            pl.BlockSpec((1, LANES), lambda bp, n: (0, 0)),
            pl.BlockSpec((1, DN_DK), lambda bp, n: (0, 0)),
        ],
        out_specs=pl.BlockSpec((nb, c, D_MODEL), lambda bp, n: (bp, out_chunk(n), 0)),
        out_shape=jax.ShapeDtypeStruct((bsz, t, D_MODEL), BF16),
        scratch_shapes=[
            pltpu.VMEM((nb, CARRY_ROWS + c, 3 * D_MODEL), BF16),
            pltpu.VMEM((nb, c, 3 * D_MODEL), F32),
            pltpu.VMEM((nu, DN_DK, DN_DK), F32),
            pltpu.VMEM((nu, 2 * c, DN_DK), BF16),
            pltpu.VMEM((nu, 2 * c, 2 * c), F32),
            pltpu.VMEM((nu, 2 * c, 2 * DN_DK), BF16),
            pltpu.VMEM((nu, c, DN_DK), BF16),
            pltpu.VMEM((nu, c, DN_DK), BF16),
            pltpu.VMEM((nu, SUBLANES, DN_DK), F32),
            pltpu.VMEM((nb, c, D_MODEL), F32),
        ],
        compiler_params=_cparams("parallel", "arbitrary"),
        name="deltanet",
    )(proj3, proj3, proj3, proj3, ab3, conv_w, alog_row, dtb_row, norm_w)
    return yb.reshape(bsz * t, D_MODEL)


def _merge_body(x_ref, ya_ref, yb_ref, ga_ref, gb_ref, lng_ref, lnb_ref, wpa_ref, wpb_ref, wo_ref,
                l1g_ref, l1b_ref, wr_ref, wrlo_ref, br_ref, h1_ref, h1_rows_ref, logit_ref):
    h = _layer_norm(x_ref[...], lng_ref[...], lnb_ref[...])
    pa = jnp.dot(ya_ref[...], wpa_ref[...], preferred_element_type=F32)
    pb = jnp.dot(yb_ref[...], wpb_ref[...], preferred_element_type=F32)
    merged = jax.nn.sigmoid(ga_ref[...].astype(F32)) * pa + jax.nn.sigmoid(gb_ref[...].astype(F32)) * pb
    mix = jnp.dot(merged.astype(BF16), wo_ref[...], preferred_element_type=F32)
    h1 = _layer_norm(DEEPNORM_ALPHA * h + mix, l1g_ref[...], l1b_ref[...])
    h1_ref[...] = h1
    _store_row_tiles(h1_rows_ref, h1)
    tm = h1.shape[0]
    hi = h1.astype(BF16)
    lo = (h1 - hi.astype(F32)).astype(BF16)
    hw = jnp.dot(jnp.concatenate([hi, lo], axis=0), wr_ref[...], preferred_element_type=F32)
    logit_ref[...] = (hw[0:tm] + hw[tm:] + jnp.dot(hi, wrlo_ref[...], preferred_element_type=F32)
                      + br_ref[...])


def _merge(x2, ya, yb, proj, ln_g, ln_b, w_pa, w_pb, w_o, l1g, l1b, w_r, w_r_lo, b_r, *, tm=512):
    n = x2.shape[0]
    vec = lambda: pl.BlockSpec((1, D_MODEL), lambda i: (0, 0))
    mat = lambda: pl.BlockSpec((D_MODEL, D_MODEL), lambda i: (0, 0))
    return pl.pallas_call(
        _merge_body,
        grid=(n // tm,),
        in_specs=[
            pl.BlockSpec((tm, D_MODEL), lambda i: (i, 0)),
            pl.BlockSpec((tm, D_MODEL), lambda i: (i, 0)),
            pl.BlockSpec((tm, D_MODEL), lambda i: (i, 0)),
            pl.BlockSpec((tm, D_MODEL), lambda i: (i, COL_GA)),
            pl.BlockSpec((tm, D_MODEL), lambda i: (i, COL_GB)),
            vec(), vec(), mat(), mat(), mat(), vec(), vec(),
            pl.BlockSpec((D_MODEL, LANES), lambda i: (0, 0)),
            pl.BlockSpec((D_MODEL, LANES), lambda i: (0, 0)),
            pl.BlockSpec((1, LANES), lambda i: (0, 0)),
        ],
        out_specs=[
            pl.BlockSpec((tm, D_MODEL), lambda i: (i, 0)),
            pl.BlockSpec((tm * ROW_CHUNKS, LANES), lambda i: (i, 0)),
            pl.BlockSpec((tm, LANES), lambda i: (i, 0)),
        ],
        out_shape=[
            jax.ShapeDtypeStruct((n, D_MODEL), F32),
            jax.ShapeDtypeStruct((n * ROW_CHUNKS, LANES), jnp.uint32),
            jax.ShapeDtypeStruct((n, LANES), F32),
        ],
        compiler_params=_cparams("parallel"),
        name="merge",
    )(x2, ya, yb, proj, proj, ln_g, ln_b, w_pa, w_pb, w_o, l1g, l1b, w_r, w_r_lo, b_r)


def _route_body(logit_ref, gate_ref, idx_ref, cnt_ref, carry_ref, *, tm):
    @pl.when(pl.program_id(0) == 0)
    def _():
        carry_ref[...] = jnp.zeros_like(carry_ref)

    lg = logit_ref[...]
    lane_i = lax.broadcasted_iota(jnp.int32, lg.shape, 1)
    lane = lane_i.astype(F32)
    neg = jnp.float32(-jnp.inf)
    big = jnp.float32(1 << 20)

    is_grp = lane_i < MOE_GROUPS
    gl = jnp.where(is_grp, lg, neg)
    gmax = jnp.max(gl, -1, keepdims=True)
    grp = jnp.min(jnp.where(is_grp & (gl == gmax), lane, big), -1, keepdims=True)
    p_grp = 1.0 / jnp.sum(jnp.where(is_grp, jnp.exp(gl - gmax), 0.0), -1, keepdims=True)

    elane = lane - MOE_GROUPS
    in_grp = (elane >= grp * MOE_EPG) & (elane < (grp + 1) * MOE_EPG)
    el = jnp.where(in_grp, lg, neg)
    m1 = jnp.max(el, -1, keepdims=True)
    e1 = jnp.min(jnp.where(in_grp & (el == m1), elane, big), -1, keepdims=True)
    rest = in_grp & (elane != e1)
    el2 = jnp.where(rest, lg, neg)
    m2 = jnp.max(el2, -1, keepdims=True)
    e2 = jnp.min(jnp.where(rest & (el2 == m2), elane, big), -1, keepdims=True)
    t2 = jnp.exp(m2 - m1)
    g1 = p_grp * (1.0 / (1.0 + t2))
    g2 = p_grp * (t2 / (1.0 + t2))

    oh1 = lane == e1
    oh2 = lane == e2
    onehot = jnp.where(oh1 | oh2, 1.0, 0.0)
    ri = lax.broadcasted_iota(jnp.int32, (tm, tm), 0)
    ci = lax.broadcasted_iota(jnp.int32, (tm, tm), 1)
    tri = jnp.where(ci < ri, 1.0, 0.0).astype(BF16)
    before = jnp.dot(tri, onehot.astype(BF16), preferred_element_type=F32) + carry_ref[0:1, :]
    r1 = jnp.sum(jnp.where(oh1, before, 0.0), -1, keepdims=True)
    r2 = jnp.sum(jnp.where(oh2, before, 0.0), -1, keepdims=True)
    total = carry_ref[0:1, :] + jnp.sum(onehot, 0, keepdims=True)
    carry_ref[...] = jnp.broadcast_to(total, carry_ref.shape)
    cnt_ref[...] = jnp.broadcast_to(total, cnt_ref.shape).astype(jnp.int32)

    gate_ref[...] = jnp.where(lane_i == 0, g1, jnp.where(lane_i == 1, g2, 0.0))
    idx_ref[...] = jnp.where(lane_i == 0, e1, jnp.where(lane_i == 1, e2,
                             jnp.where(lane_i == 2, r1, jnp.where(lane_i == 3, r2, 0.0)))).astype(jnp.int32)


def _route(logits, *, tm=1024):
    n = logits.shape[0]
    return pl.pallas_call(
        functools.partial(_route_body, tm=tm),
        grid=(n // tm,),
        in_specs=[pl.BlockSpec((tm, LANES), lambda i: (i, 0))],
        out_specs=[
            pl.BlockSpec((tm, LANES), lambda i: (i, 0)),
            pl.BlockSpec((tm, LANES), lambda i: (i, 0)),
            pl.BlockSpec((8, LANES), lambda i: (0, 0)),
        ],
        out_shape=[
            jax.ShapeDtypeStruct((n, LANES), F32),
            jax.ShapeDtypeStruct((n, LANES), jnp.int32),
            jax.ShapeDtypeStruct((8, LANES), jnp.int32),
        ],
        scratch_shapes=[pltpu.VMEM((8, LANES), F32)],
        compiler_params=_cparams("arbitrary"),
        name="route",
    )(logits)


def _dest_body(idx_ref, ps_ref, o_ref):
    idx = idx_ref[...]
    lane = lax.broadcasted_iota(jnp.int32, idx.shape, 1)
    ps = ps_ref[...].astype(F32)
    d1 = jnp.sum(jnp.where(lane == idx[:, 0:1], ps, 0.0), -1, keepdims=True).astype(jnp.int32) + idx[:, 2:3]
    d2 = jnp.sum(jnp.where(lane == idx[:, 1:2], ps, 0.0), -1, keepdims=True).astype(jnp.int32) + idx[:, 3:4]
    o_ref[...] = jnp.where(lane == 0, d1, jnp.where(lane == 1, d2, 0))


def _dest(idx, pad_start_row, *, tm=1024):
    n = idx.shape[0]
    return pl.pallas_call(
        _dest_body,
        grid=(n // tm,),
        in_specs=[pl.BlockSpec((tm, LANES), lambda i: (i, 0)), pl.BlockSpec((1, LANES), lambda i: (0, 0))],
        out_specs=pl.BlockSpec((tm, LANES), lambda i: (i, 0)),
        out_shape=jax.ShapeDtypeStruct((n, LANES), jnp.int32),
        compiler_params=_cparams("parallel"),
        name="dest",
    )(idx, pad_start_row)


DMA_GROUP = 8


def _row_tile(ref, row):
    start = row * ROW_CHUNKS
    return ref.at[pl.ds(start if isinstance(row, int) else pl.multiple_of(start, ROW_CHUNKS), ROW_CHUNKS)]


def _dispatch_body(dest_ref, h_ref, xin_ref, xpad_ref, sem, *, tm):
    del xin_ref

    def row_copy(src_row, dst_row):
        return pltpu.make_async_copy(_row_tile(h_ref, src_row), _row_tile(xpad_ref, dst_row), sem)

    for r0 in range(0, tm, DMA_GROUP):
        dst = [dest_ref[0, 0, 2 * r0 + j] for j in range(2 * DMA_GROUP)]
        for j in range(2 * DMA_GROUP):
            row_copy(r0 + j // 2, dst[j]).start(priority=j % 2)

    def wait(g, carry):
        for _ in range(2 * DMA_GROUP):
            row_copy(0, 0).wait()
        return carry

    lax.fori_loop(0, tm // DMA_GROUP, wait, 0)


def _dispatch(dest_smem, h1_rows, xpad0, *, tm):
    n = h1_rows.shape[0] // ROW_CHUNKS
    return pl.pallas_call(
        functools.partial(_dispatch_body, tm=tm),
        grid=(n // tm,),
        in_specs=[
            pl.BlockSpec((1, 1, 2 * tm), lambda i: (i, 0, 0), memory_space=pltpu.SMEM),
            pl.BlockSpec((tm * ROW_CHUNKS, LANES), lambda i: (i, 0)),
            pl.BlockSpec(memory_space=pl.ANY),
        ],
        out_specs=pl.BlockSpec(memory_space=pl.ANY),
        scratch_shapes=[pltpu.SemaphoreType.DMA(())],
        out_shape=jax.ShapeDtypeStruct(xpad0.shape, xpad0.dtype),
        input_output_aliases={2: 0},
        compiler_params=_cparams("arbitrary"),
        name="dispatch",
    )(dest_smem, h1_rows, xpad0)


def _experts_body(blk_e_ref, next_e_ref, slot_ref, nused_ref, x_ref, w1_ref, w3_ref, w2_ref, y_ref,
                  st1, st3, st2, w1b, w3b, w2b, sem):
    i = pl.program_id(0)
    e = blk_e_ref[i]
    first_of_expert = (i == 0) | (e != blk_e_ref[jnp.maximum(i - 1, 0)])

    def weight_copies(expert, slot):
        return [pltpu.make_async_copy(w_ref.at[expert], st.at[slot], sem.at[slot])
                for w_ref, st in ((w1_ref, st1), (w3_ref, st3), (w2_ref, st2))]

    @pl.when(i == 0)
    def _():
        for cp in weight_copies(e, slot_ref[0]):
            cp.start()

    @pl.when(first_of_expert)
    def _():
        slot = slot_ref[i]
        for cp in weight_copies(e, slot):
            cp.wait()
        w1b[...] = st1[slot].astype(BF16)
        w3b[...] = st3[slot].astype(BF16)
        w2b[...] = st2[slot].astype(BF16)

        @pl.when(next_e_ref[i] >= 0)
        def _():
            for cp in weight_copies(next_e_ref[i], 1 - slot):
                cp.start()

    @pl.when(i < nused_ref[0])
    def _():
        xb = _load_row_tiles(x_ref, EXPERT_ROWS)
        a = jnp.dot(xb, w1b[...], preferred_element_type=F32)
        b = jnp.dot(xb, w3b[...], preferred_element_type=F32)
        hb = (a * jax.nn.sigmoid(a)) * b
        _store_row_tiles(y_ref, jnp.dot(hb.astype(BF16), w2b[...], preferred_element_type=F32))

    @pl.when(i >= nused_ref[0])
    def _():
        y_ref[...] = jnp.zeros_like(y_ref)


def _experts(blk_e, next_e, slot, n_used, xpad, w1, w3, w2):
    tm = EXPERT_ROWS * ROW_CHUNKS
    return pl.pallas_call(
        _experts_body,
        grid_spec=pltpu.PrefetchScalarGridSpec(
            num_scalar_prefetch=4,
            grid=(xpad.shape[0] // tm,),
            in_specs=[
                pl.BlockSpec((tm, LANES), lambda i, be, ne, sl, nu: (jnp.minimum(i, jnp.maximum(nu[0] - 1, 0)), 0)),
                pl.BlockSpec(memory_space=pl.ANY),
                pl.BlockSpec(memory_space=pl.ANY),
                pl.BlockSpec(memory_space=pl.ANY),
            ],
            out_specs=pl.BlockSpec((tm, LANES), lambda i, be, ne, sl, nu: (i, 0)),
            scratch_shapes=[
                pltpu.VMEM((2, D_MODEL, D_EXPERT), F32),
                pltpu.VMEM((2, D_MODEL, D_EXPERT), F32),
                pltpu.VMEM((2, D_EXPERT, D_MODEL), F32),
                pltpu.VMEM((D_MODEL, D_EXPERT), BF16),
                pltpu.VMEM((D_MODEL, D_EXPERT), BF16),
                pltpu.VMEM((D_EXPERT, D_MODEL), BF16),
                pltpu.SemaphoreType.DMA((2,)),
            ],
        ),
        out_shape=jax.ShapeDtypeStruct(xpad.shape, xpad.dtype),
        compiler_params=_cparams("arbitrary"),
        name="experts",
    )(blk_e, next_e, slot, n_used, xpad, w1, w3, w2)


def _combine_body(dest_ref, dest_next_ref, h1_ref, gate_ref, l2g_ref, l2b_ref, ypad_ref, o_ref, buf, sem, *, tm):
    i = pl.program_id(0)
    half = i % 2

    def row_copy(src_row, r, slot, hf):
        return pltpu.make_async_copy(_row_tile(ypad_ref, src_row), _row_tile(buf.at[2 * hf + slot], r), sem.at[hf])

    def gather(idx_ref, hf):
        for r0 in range(0, tm, DMA_GROUP):
            src = [idx_ref[0, 0, 2 * r0 + j] for j in range(2 * DMA_GROUP)]
            for j in range(2 * DMA_GROUP):
                row_copy(src[j], r0 + j // 2, j % 2, hf).start(priority=j % 2)

    @pl.when(i == 0)
    def _():
        gather(dest_ref, 0)

    @pl.when(i + 1 < pl.num_programs(0))
    def _():
        gather(dest_next_ref, 1 - half)

    def wait(g, carry):
        for j in range(2 * DMA_GROUP):
            row_copy(0, 0, j % 2, half).wait()
        return carry

    lax.fori_loop(0, tm // DMA_GROUP, wait, 0)
    gate = gate_ref[...]
    ffn = (_load_row_tiles(buf.at[2 * half], tm).astype(F32) * gate[:, 0:1]
           + _load_row_tiles(buf.at[2 * half + 1], tm).astype(F32) * gate[:, 1:2])
    o_ref[...] = _layer_norm(DEEPNORM_ALPHA * h1_ref[...] + ffn, l2g_ref[...], l2b_ref[...])


def _combine(dest_smem, h1, gates, l2g, l2b, ypad, *, tm):
    n = h1.shape[0]
    steps = n // tm
    return pl.pallas_call(
        functools.partial(_combine_body, tm=tm),
        grid=(steps,),
        in_specs=[
            pl.BlockSpec((1, 1, 2 * tm), lambda i: (i, 0, 0), memory_space=pltpu.SMEM),
            pl.BlockSpec((1, 1, 2 * tm), lambda i: (jnp.minimum(i + 1, steps - 1), 0, 0), memory_space=pltpu.SMEM),
            pl.BlockSpec((tm, D_MODEL), lambda i: (i, 0)),
            pl.BlockSpec((tm, LANES), lambda i: (i, 0)),
            pl.BlockSpec((1, D_MODEL), lambda i: (0, 0)),
            pl.BlockSpec((1, D_MODEL), lambda i: (0, 0)),
            pl.BlockSpec(memory_space=pl.ANY),
        ],
        out_specs=pl.BlockSpec((tm, D_MODEL), lambda i: (i, 0)),
        scratch_shapes=[
            pltpu.VMEM((4, tm * ROW_CHUNKS, LANES), jnp.uint32),
            pltpu.SemaphoreType.DMA((2,)),
        ],
        out_shape=jax.ShapeDtypeStruct((n, D_MODEL), F32),
        compiler_params=_cparams("arbitrary"),
        name="combine",
    )(dest_smem, dest_smem, h1, gates, l2g, l2b, ypad)


def _row(v, width=None):
    v = v.reshape(1, -1).astype(F32)
    if width is not None and v.shape[1] < width:
        v = jnp.pad(v, ((0, 0), (0, width - v.shape[1])))
    return v


def kernel(x, ln_in_g, ln_in_b, w_in, b_in, gm_ln_g, gm_ln_b, gm_w_s, gm_b_s, dn_conv_w, dn_a_log,
           dn_dt_bias, dn_norm_w, w_pa, w_pb, w_o, ln1_g, ln1_b, w_rg, b_rg, w_re, b_re, w1, w3, w2,
           ln2_g, ln2_b):
    bsz, t, d = x.shape
    n = bsz * t
    x2 = x.reshape(n, d)
    l = 0

    wi, bi = w_in[l], b_in[l]
    c_ab = N_FRONT_BLOCKS * D_MODEL
    c_gate = c_ab + 2 * DN_HEADS
    w_ab = jnp.pad(wi[:, c_ab:c_gate], ((0, 0), (0, LANES - 2 * DN_HEADS))).astype(BF16)
    b_ab = _row(bi[c_ab:c_gate], LANES)
    b_proj = jnp.concatenate([bi[:c_ab], bi[c_gate:]]).reshape(1, PROJ_WIDTH)

    ln_g, ln_b = _row(ln_in_g), _row(ln_in_b)
    h, ab = _ln_ab(x2, ln_g, ln_b, w_ab, b_ab, tm=min(2048, n))
    proj = _inproj(h, wi, b_proj, tm=min(2048, n))

    bs_full = jnp.broadcast_to(gm_b_s[l][:, :, None], (GM_GROUPS, GM_BLOCK, LANES))
    ya = _gmlp(proj, _row(gm_ln_g[l]), _row(gm_ln_b[l]), gm_w_s[l], bs_full, nblk=min(8, n // GM_BLOCK))

    yb = _deltanet(proj, ab, dn_conv_w[l], _row(dn_a_log[l], LANES), _row(dn_dt_bias[l], LANES),
                   _row(dn_norm_w[l]), bsz=bsz, t=t, nb=min(4, bsz))

    w_r = jnp.concatenate([w_rg[l], w_re[l], jnp.zeros((d, LANES - MOE_GROUPS - N_EXPERTS), F32)], axis=1)
    w_r_hi = w_r.astype(BF16)
    w_r_lo = (w_r - w_r_hi.astype(F32)).astype(BF16)
    b_r = _row(jnp.concatenate([b_rg[l], b_re[l]]), LANES)
    h1, h1_rows, logits = _merge(x2, ya, yb, proj, ln_g, ln_b, w_pa[l].astype(BF16), w_pb[l].astype(BF16),
                        w_o[l].astype(BF16), _row(ln1_g[l]), _row(ln1_b[l]), w_r_hi, w_r_lo, b_r,
                        tm=min(1024, n))

    gates, idx, counts = _route(logits, tm=min(1024, n))

    tm_e = EXPERT_ROWS
    cnt = counts[0, :N_EXPERTS]
    padded = (cnt + tm_e - 1) // tm_e * tm_e
    pad_end = jnp.cumsum(padded)
    pad_start = (pad_end - padded).astype(jnp.int32)
    n_blocks = (2 * n + N_EXPERTS * (tm_e - 1) + tm_e - 1) // tm_e
    blk_start = jnp.arange(n_blocks, dtype=jnp.int32) * tm_e
    blk_e = jnp.minimum(jnp.sum(pad_end[None, :] <= blk_start[:, None], -1), N_EXPERTS - 1).astype(jnp.int32)
    n_used = (pad_end[-1] // tm_e).astype(jnp.int32).reshape(1)
    blk_ids = jnp.arange(n_blocks, dtype=jnp.int32)
    is_first = jnp.concatenate([jnp.ones((1,), bool), blk_e[1:] != blk_e[:-1]])
    slot = ((jnp.cumsum(is_first) - 1) % 2).astype(jnp.int32)
    later_first = is_first[None, :] & (blk_ids[None, :] > blk_ids[:, None])
    next_first = jnp.min(jnp.where(later_first, blk_ids[None, :], n_blocks), axis=1)
    next_e = jnp.where(next_first < n_blocks, blk_e[jnp.minimum(next_first, n_blocks - 1)], -1).astype(jnp.int32)

    ps_row = jnp.pad(pad_start, (0, LANES - N_EXPERTS)).reshape(1, LANES)
    dest = _dest(idx, ps_row, tm=min(1024, n))[:, :2]
    tm_d, tm_c = min(1024, n), 256
    xpad0 = jnp.zeros((n_blocks * tm_e * ROW_CHUNKS, LANES), jnp.uint32)
    xpad = _dispatch(dest.reshape(n // tm_d, 1, 2 * tm_d), h1_rows, xpad0, tm=tm_d)
    ypad = _experts(blk_e, next_e, slot, n_used, xpad, w1[l], w3[l], w2[l])
    out = _combine(dest.reshape(n // tm_c, 1, 2 * tm_c), h1, gates, _row(ln2_g[l]), _row(ln2_b[l]), ypad,
                   tm=tm_c)
    return out.reshape(bsz, t, d)
```

```python
import functools
import math

import jax
import jax.numpy as jnp
from jax import lax
from jax.experimental import pallas as pl
from jax.experimental.pallas import tpu as pltpu

D_MODEL = 1024
GM_GROUPS = 8
GM_BLOCK = 128
GM_CHUNK = 64
DN_HEADS = 8
DN_DK = 128
DN_CONV = 4
MOE_GROUPS = 4
MOE_EPG = 8
N_EXPERTS = MOE_GROUPS * MOE_EPG
D_EXPERT = D_MODEL // 2
LN_EPS = 1e-5
RMS_EPS = 1e-6
L2_EPS = 1e-6
DEEPNORM_ALPHA = 2.0 ** 0.25

LANES = 128
COL_U, COL_V, COL_Q, COL_K, COL_VV, COL_Z, COL_GA, COL_GB = range(8)
N_PROJ_BLOCKS = 8
N_FRONT_BLOCKS = 6
AB_COLS = 2 * DN_HEADS
PROJ_WIDTH = N_PROJ_BLOCKS * D_MODEL

DN_CHUNK = 64
CARRY_ROWS = 16
EXPERT_ROWS = 512
VMEM_LIMIT = 56 * 1024 * 1024
SUBLANES = 8
ROW_WORDS = D_MODEL // 2
ROW_CHUNKS = ROW_WORDS // LANES

F32 = jnp.float32
BF16 = jnp.bfloat16


def _cparams(*sem):
    return pltpu.CompilerParams(dimension_semantics=sem, vmem_limit_bytes=VMEM_LIMIT)


def _store_row_tiles(ref, val):
    m = val.shape[0]
    hi = pltpu.bitcast(val[:, :ROW_WORDS].astype(BF16).astype(F32), jnp.uint32)
    lo = pltpu.bitcast(val[:, ROW_WORDS:].astype(BF16).astype(F32), jnp.uint32)
    words = hi | (lo >> 16)
    for ch in range(ROW_CHUNKS):
        ref[pl.ds(ch, m, stride=ROW_CHUNKS), :] = words[:, ch * LANES:(ch + 1) * LANES]


def _load_row_tiles(ref, m):
    words = jnp.concatenate([ref[pl.ds(ch, m, stride=ROW_CHUNKS), :] for ch in range(ROW_CHUNKS)], axis=1)
    hi = pltpu.bitcast(words & jnp.uint32(0xFFFF0000), F32)
    lo = pltpu.bitcast(words << 16, F32)
    return jnp.concatenate([hi, lo], axis=1).astype(BF16)


def _layer_norm(x, g, b):
    mu = jnp.mean(x, -1, keepdims=True)
    xc = x - mu
    var = jnp.mean(xc * xc, -1, keepdims=True)
    return xc * lax.rsqrt(var + LN_EPS) * g + b


def _dot(a, b):
    return jnp.dot(a.astype(BF16), b.astype(BF16), preferred_element_type=F32)


def _dot_nt(a, b):
    return lax.dot_general(a.astype(BF16), b.astype(BF16), (((1,), (1,)), ((), ())),
                           preferred_element_type=F32)


def _ln_ab_body(x_ref, g_ref, b_ref, wab_ref, bab_ref, h_ref, ab_ref):
    h = _layer_norm(x_ref[...], g_ref[...], b_ref[...]).astype(BF16)
    h_ref[...] = h
    ab_ref[...] = jnp.dot(h, wab_ref[...], preferred_element_type=F32) + bab_ref[...]


def _ln_ab(x2, ln_g, ln_b, w_ab, b_ab, *, tm):
    n = x2.shape[0]
    return pl.pallas_call(
        _ln_ab_body,
        grid=(n // tm,),
        in_specs=[
            pl.BlockSpec((tm, D_MODEL), lambda i: (i, 0)),
            pl.BlockSpec((1, D_MODEL), lambda i: (0, 0)),
            pl.BlockSpec((1, D_MODEL), lambda i: (0, 0)),
            pl.BlockSpec((D_MODEL, LANES), lambda i: (0, 0)),
            pl.BlockSpec((1, LANES), lambda i: (0, 0)),
        ],
        out_specs=[pl.BlockSpec((tm, D_MODEL), lambda i: (i, 0)), pl.BlockSpec((tm, LANES), lambda i: (i, 0))],
        out_shape=[jax.ShapeDtypeStruct((n, D_MODEL), BF16), jax.ShapeDtypeStruct((n, LANES), F32)],
        compiler_params=_cparams("parallel"),
        name="ln_ab",
    )(x2, ln_g, ln_b, w_ab, b_ab)


def _inproj_body(h_ref, wa_ref, wb_ref, bias_ref, o_ref, w_scr):
    j = pl.program_id(0)

    @pl.when((pl.program_id(1) == 0) & (j < N_FRONT_BLOCKS))
    def _():
        w_scr[...] = wa_ref[...].T.astype(BF16)

    @pl.when((pl.program_id(1) == 0) & (j >= N_FRONT_BLOCKS))
    def _():
        both = jnp.concatenate([wa_ref[...], wb_ref[...]], axis=0)
        w_scr[...] = both[AB_COLS:AB_COLS + D_MODEL].T.astype(BF16)

    o_ref[...] = (jnp.dot(h_ref[...], w_scr[...], preferred_element_type=F32) + bias_ref[...]).astype(o_ref.dtype)


def _inproj(h, w_t, bias, *, tm=1024):
    n = h.shape[0]
    return pl.pallas_call(
        _inproj_body,
        grid=(N_PROJ_BLOCKS, n // tm),
        in_specs=[
            pl.BlockSpec((tm, D_MODEL), lambda j, i: (i, 0)),
            pl.BlockSpec((D_MODEL, D_MODEL), lambda j, i: (j, 0)),
            pl.BlockSpec((D_MODEL, D_MODEL), lambda j, i: (j + 1, 0)),
            pl.BlockSpec((1, D_MODEL), lambda j, i: (0, j)),
        ],
        out_specs=pl.BlockSpec((tm, D_MODEL), lambda j, i: (i, j)),
        out_shape=jax.ShapeDtypeStruct((n, N_PROJ_BLOCKS * D_MODEL), BF16),
        scratch_shapes=[pltpu.VMEM((D_MODEL, D_MODEL), BF16)],
        compiler_params=_cparams("arbitrary", "arbitrary"),
        name="inproj",
    )(h, w_t, w_t, bias)


def _gelu(x):
    return 0.5 * x * (1.0 + lax.erf(x * (1.0 / math.sqrt(2.0))))


def _gmlp_body(u_ref, v_ref, lng_ref, lnb_ref, ws_ref, bs_ref, o_ref, *, nblk):
    u = _gelu(u_ref[...].astype(F32))
    v = _layer_norm(_gelu(v_ref[...].astype(F32)), lng_ref[...], lnb_ref[...]).astype(BF16)
    row_chunk = lax.broadcasted_iota(jnp.int32, (GM_BLOCK, GM_BLOCK), 0) // GM_CHUNK
    col_chunk = lax.broadcasted_iota(jnp.int32, (GM_BLOCK, GM_BLOCK), 1) // GM_CHUNK
    causal = col_chunk <= row_chunk
    for g in range(GM_GROUPS):
        cols = slice(g * LANES, (g + 1) * LANES)
        w = jnp.where(causal, ws_ref[g], 0.0).astype(BF16)
        for blk in range(nblk):
            rows = slice(blk * GM_BLOCK, (blk + 1) * GM_BLOCK)
            s = jnp.dot(w, v[rows, cols], preferred_element_type=F32) + bs_ref[g]
            o_ref[rows, cols] = (u[rows, cols] * s).astype(o_ref.dtype)


def _gmlp(proj, ln_g, ln_b, w_s, b_s_full, *, nblk):
    n = proj.shape[0]
    rows = nblk * GM_BLOCK
    return pl.pallas_call(
        functools.partial(_gmlp_body, nblk=nblk),
        grid=(n // rows,),
        in_specs=[
            pl.BlockSpec((rows, D_MODEL), lambda i: (i, COL_U)),
            pl.BlockSpec((rows, D_MODEL), lambda i: (i, COL_V)),
            pl.BlockSpec((1, D_MODEL), lambda i: (0, 0)),
            pl.BlockSpec((1, D_MODEL), lambda i: (0, 0)),
            pl.BlockSpec((GM_GROUPS, GM_BLOCK, GM_BLOCK), lambda i: (0, 0, 0)),
            pl.BlockSpec((GM_GROUPS, GM_BLOCK, LANES), lambda i: (0, 0, 0)),
        ],
        out_specs=pl.BlockSpec((rows, D_MODEL), lambda i: (i, 0)),
        out_shape=jax.ShapeDtypeStruct((n, D_MODEL), BF16),
        compiler_params=_cparams("parallel"),
        name="gmlp",
    )(proj, proj, ln_g, ln_b, w_s, b_s_full)


def _cumsum_rows(x):
    n = x.shape[0]
    row = lax.broadcasted_iota(jnp.int32, x.shape, 0)
    shift = 1
    while shift < n:
        x = x + jnp.where(row >= shift, pltpu.roll(x, shift, 0), 0.0)
        shift *= 2
    return x


def _deltanet_body(q_ref, k_ref, v_ref, z_ref, ab_ref, cw_ref, alog_ref, dtb_ref, nw_ref, o_ref,
                   xs_ref, qkv_ref, state_ref, kq_ref, m_ref, r0_ref, qd_ref, kd_ref, cd_ref, zg_ref, *, nb):
    c = DN_CHUNK
    w3 = 3 * D_MODEL
    heads = range(DN_HEADS)
    units = range(nb * DN_HEADS)
    hsl = lambda base, h: slice(base + h * DN_DK, base + (h + 1) * DN_DK)

    @pl.when(pl.program_id(1) == 0)
    def _():
        xs_ref[:, 0:CARRY_ROWS, :] = jnp.zeros((nb, CARRY_ROWS, w3), BF16)
        for ref in (qkv_ref, state_ref, kq_ref, m_ref, r0_ref, qd_ref, kd_ref, cd_ref, zg_ref):
            ref[...] = jnp.zeros_like(ref)

    kq_in = [kq_ref[u] for u in units]
    kq = [lax.dot_general(kq_in[u], kq_in[u], (((1,), (1,)), ((), ())), preferred_element_type=F32)
          for u in units]
    pm = [kq[u] * m_ref[u] for u in units]
    attn = [x[c:, 0:c].astype(BF16) for x in pm]

    lane = lax.broadcasted_iota(jnp.int32, (c, 2 * c), 1)
    right = lane >= c
    eye_right = jnp.where(lane - c == lax.broadcasted_iota(jnp.int32, (c, 2 * c), 0), 1.0, 0.0)
    pa = [x[0:c] + eye_right for x in pm]
    span = 1
    while span < c:
        pab = [x.astype(BF16) for x in pa]
        pa = [jnp.dot(pab[u][:, 0:c], pab[u], preferred_element_type=F32) + jnp.where(right, pa[u], 0.0)
              for u in units]
        span *= 2
    r = [jnp.dot(pa[u].astype(BF16), r0_ref[u], preferred_element_type=F32) for u in units]

    s_old = [state_ref[u] for u in units]
    ws = [jnp.dot(jnp.concatenate([r[u][:, DN_DK:].astype(BF16), qd_ref[u]], axis=0),
                  s_old[u].astype(BF16), preferred_element_type=F32) for u in units]
    v_new = [(r[u][:, 0:DN_DK] - ws[u][0:c]).astype(BF16) for u in units]
    o = [ws[u][c:] + jnp.dot(attn[u], v_new[u], preferred_element_type=F32) for u in units]
    ds = [lax.dot_general(kd_ref[u], v_new[u], (((0,), (0,)), ((), ())), preferred_element_type=F32)
          for u in units]
    for u in units:
        state_ref[u] = s_old[u] * cd_ref[u][0:1, :] + ds[u]
    for b in range(nb):
        for h in heads:
            u = b * DN_HEADS + h
            on = o[u] * lax.rsqrt(jnp.mean(o[u] * o[u], -1, keepdims=True) + RMS_EPS)
            o_ref[b, :, hsl(0, h)] = (on * zg_ref[b, :, hsl(0, h)]).astype(o_ref.dtype)

    ri = lax.broadcasted_iota(jnp.int32, (c, c), 0)
    ci = lax.broadcasted_iota(jnp.int32, (c, c), 1)
    causal = ri >= ci
    strict = ri > ci
    for b in range(nb):
        qkv = qkv_ref[b]
        ab = ab_ref[b]
        xg = ab + dtb_ref[...]
        softplus = jnp.maximum(xg, 0.0) + jnp.log1p(jnp.exp(-jnp.abs(xg)))
        gcum = _cumsum_rows(-jnp.exp(alog_ref[...]) * softplus)
        beta_all = jax.nn.sigmoid(ab)
        gcum_sq = gcum if c == LANES else jnp.concatenate([gcum, jnp.zeros((LANES - c, LANES), F32)], axis=0)
        gcum_t = gcum_sq.T

        z = z_ref[b].astype(F32)
        zg_ref[b] = z * jax.nn.sigmoid(z) * jnp.concatenate([nw_ref[...]] * DN_HEADS, axis=1)

        for h in heads:
            u = b * DN_HEADS + h
            qh, kh, vh = qkv[:, hsl(0, h)], qkv[:, hsl(D_MODEL, h)], qkv[:, hsl(2 * D_MODEL, h)]
            qh = qh * (lax.rsqrt(jnp.sum(qh * qh, -1, keepdims=True) + L2_EPS) * (DN_DK ** -0.5))
            kh = kh * lax.rsqrt(jnp.sum(kh * kh, -1, keepdims=True) + L2_EPS)
            g_col = gcum[:, h:h + 1]
            beta = beta_all[:, DN_HEADS + h:DN_HEADS + h + 1]
            g_last = gcum[c - 1:c, h:h + 1]
            decay = jnp.where(causal, jnp.exp(jnp.where(causal, g_col - gcum_t[h:h + 1, 0:c], 0.0)), 0.0)
            eg = jnp.exp(g_col)
            kq_ref[u, 0:c, :] = kh.astype(BF16)
            kq_ref[u, c:, :] = qh.astype(BF16)
            m_ref[u, 0:c, 0:c] = jnp.where(strict, -beta * decay, 0.0)
            m_ref[u, c:, 0:c] = decay
            r0_ref[u, c:, 0:DN_DK] = (vh * beta).astype(BF16)
            r0_ref[u, c:, DN_DK:] = (kh * (beta * eg)).astype(BF16)
            qd_ref[u] = (qh * eg).astype(BF16)
            kd_ref[u] = (kh * jnp.exp(g_last - g_col)).astype(BF16)
            cd_ref[u] = jnp.broadcast_to(jnp.exp(g_last), (SUBLANES, DN_DK))

    nsh = DN_CONV - 1
    srow = lax.broadcasted_iota(jnp.int32, (nsh * c, CARRY_ROWS + c), 0)
    scol = lax.broadcasted_iota(jnp.int32, (nsh * c, CARRY_ROWS + c), 1)
    shift_mat = jnp.where(scol == srow % c + srow // c + (CARRY_ROWS - nsh), 1.0, 0.0).astype(BF16)
    for b in range(nb):
        xs_ref[b, CARRY_ROWS:, 0:D_MODEL] = q_ref[b]
        xs_ref[b, CARRY_ROWS:, D_MODEL:2 * D_MODEL] = k_ref[b]
        xs_ref[b, CARRY_ROWS:, 2 * D_MODEL:] = v_ref[b]
        window = xs_ref[b]
        shifted = jnp.dot(shift_mat, window, preferred_element_type=F32)
        acc = cw_ref[nsh:DN_CONV, :] * window[CARRY_ROWS:].astype(F32)
        for j in range(nsh):
            acc = acc + cw_ref[j:j + 1, :] * shifted[j * c:(j + 1) * c]
        xs_ref[b, 0:CARRY_ROWS, :] = xs_ref[b, c:c + CARRY_ROWS, :]
        qkv_ref[b] = acc * jax.nn.sigmoid(acc)


def _deltanet(proj, ab, conv_w, alog_row, dtb_row, norm_w, *, bsz, t, nb=4):
    c = DN_CHUNK
    nc = t // c
    nu = nb * DN_HEADS
    proj3 = proj.reshape(bsz, t, PROJ_WIDTH)
    ab3 = ab.reshape(bsz, t, LANES)
    conv_chunk = lambda n: jnp.minimum(n, nc - 1)
    prep_chunk = lambda n: jnp.clip(n - 1, 0, nc - 1)
    out_chunk = lambda n: jnp.maximum(n - 2, 0)
    col = lambda j, chunk: pl.BlockSpec((nb, c, D_MODEL), lambda bp, n: (bp, chunk(n), j))
    yb = pl.pallas_call(
        functools.partial(_deltanet_body, nb=nb),
        grid=(bsz // nb, nc + 2),
        in_specs=[
            col(COL_Q, conv_chunk), col(COL_K, conv_chunk), col(COL_VV, conv_chunk), col(COL_Z, prep_chunk),
            pl.BlockSpec((nb, c, LANES), lambda bp, n: (bp, prep_chunk(n), 0)),
            pl.BlockSpec((DN_CONV, 3 * D_MODEL), lambda bp, n: (0, 0)),
            pl.BlockSpec((1, LANES), lambda bp, n: (0, 0)),
            pl.BlockSpec((1, LANES), lambda bp, n: (0, 0)),
            pl.BlockSpec((1, DN_DK), lambda bp, n: (0, 0)),
        ],
        out_specs=pl.BlockSpec((nb, c, D_MODEL), lambda bp, n: (bp, out_chunk(n), 0)),
        out_shape=jax.ShapeDtypeStruct((bsz, t, D_MODEL), BF16),
        scratch_shapes=[
            pltpu.VMEM((nb, CARRY_ROWS + c, 3 * D_MODEL), BF16),
            pltpu.VMEM((nb, c, 3 * D_MODEL), F32),
            pltpu.VMEM((nu, DN_DK, DN_DK), F32),
            pltpu.VMEM((nu, 2 * c, DN_DK), BF16),
            pltpu.VMEM((nu, 2 * c, 2 * c), F32),
            pltpu.VMEM((nu, 2 * c, 2 * DN_DK), BF16),
            pltpu.VMEM((nu, c, DN_DK), BF16),
            pltpu.VMEM((nu, c, DN_DK), BF16),
            pltpu.VMEM((nu, SUBLANES, DN_DK), F32),
            pltpu.VMEM((nb, c, D_MODEL), F32),
        ],
        compiler_params=_cparams("parallel", "arbitrary"),
        name="deltanet",
    )(proj3, proj3, proj3, proj3, ab3, conv_w, alog_row, dtb_row, norm_w)
    return yb.reshape(bsz * t, D_MODEL)


def _merge_body(x_ref, ya_ref, yb_ref, ga_ref, gb_ref, lng_ref, lnb_ref, wpa_ref, wpb_ref, wo_ref,
                l1g_ref, l1b_ref, wr_ref, wrlo_ref, br_ref, h1_ref, h1_rows_ref, logit_ref):
    h = _layer_norm(x_ref[...], lng_ref[...], lnb_ref[...])
    pa = jnp.dot(ya_ref[...], wpa_ref[...], preferred_element_type=F32)
    pb = jnp.dot(yb_ref[...], wpb_ref[...], preferred_element_type=F32)
    merged = jax.nn.sigmoid(ga_ref[...].astype(F32)) * pa + jax.nn.sigmoid(gb_ref[...].astype(F32)) * pb
    mix = jnp.dot(merged.astype(BF16), wo_ref[...], preferred_element_type=F32)
    h1 = _layer_norm(DEEPNORM_ALPHA * h + mix, l1g_ref[...], l1b_ref[...])
    h1_ref[...] = h1
    _store_row_tiles(h1_rows_ref, h1)
    tm = h1.shape[0]
    hi = h1.astype(BF16)
    lo = (h1 - hi.astype(F32)).astype(BF16)
    hw = jnp.dot(jnp.concatenate([hi, lo], axis=0), wr_ref[...], preferred_element_type=F32)
    logit_ref[...] = (hw[0:tm] + hw[tm:] + jnp.dot(hi, wrlo_ref[...], preferred_element_type=F32)
                      + br_ref[...])


def _merge(x2, ya, yb, proj, ln_g, ln_b, w_pa, w_pb, w_o, l1g, l1b, w_r, w_r_lo, b_r, *, tm=512):
    n = x2.shape[0]
    vec = lambda: pl.BlockSpec((1, D_MODEL), lambda i: (0, 0))
    mat = lambda: pl.BlockSpec((D_MODEL, D_MODEL), lambda i: (0, 0))
    return pl.pallas_call(
        _merge_body,
        grid=(n // tm,),
        in_specs=[
            pl.BlockSpec((tm, D_MODEL), lambda i: (i, 0)),
            pl.BlockSpec((tm, D_MODEL), lambda i: (i, 0)),
            pl.BlockSpec((tm, D_MODEL), lambda i: (i, 0)),
            pl.BlockSpec((tm, D_MODEL), lambda i: (i, COL_GA)),
            pl.BlockSpec((tm, D_MODEL), lambda i: (i, COL_GB)),
            vec(), vec(), mat(), mat(), mat(), vec(), vec(),
            pl.BlockSpec((D_MODEL, LANES), lambda i: (0, 0)),
            pl.BlockSpec((D_MODEL, LANES), lambda i: (0, 0)),
            pl.BlockSpec((1, LANES), lambda i: (0, 0)),
        ],
        out_specs=[
            pl.BlockSpec((tm, D_MODEL), lambda i: (i, 0)),
            pl.BlockSpec((tm * ROW_CHUNKS, LANES), lambda i: (i, 0)),
            pl.BlockSpec((tm, LANES), lambda i: (i, 0)),
        ],
        out_shape=[
            jax.ShapeDtypeStruct((n, D_MODEL), F32),
            jax.ShapeDtypeStruct((n * ROW_CHUNKS, LANES), jnp.uint32),
            jax.ShapeDtypeStruct((n, LANES), F32),
        ],
        compiler_params=_cparams("parallel"),
        name="merge",
    )(x2, ya, yb, proj, proj, ln_g, ln_b, w_pa, w_pb, w_o, l1g, l1b, w_r, w_r_lo, b_r)


def _route_body(logit_ref, gate_ref, idx_ref, cnt_ref, carry_ref, *, tm):
    @pl.when(pl.program_id(0) == 0)
    def _():
        carry_ref[...] = jnp.zeros_like(carry_ref)

    lg = logit_ref[...]
    lane_i = lax.broadcasted_iota(jnp.int32, lg.shape, 1)
    lane = lane_i.astype(F32)
    neg = jnp.float32(-jnp.inf)
    big = jnp.float32(1 << 20)

    is_grp = lane_i < MOE_GROUPS
    gl = jnp.where(is_grp, lg, neg)
    gmax = jnp.max(gl, -1, keepdims=True)
    grp = jnp.min(jnp.where(is_grp & (gl == gmax), lane, big), -1, keepdims=True)
    p_grp = 1.0 / jnp.sum(jnp.where(is_grp, jnp.exp(gl - gmax), 0.0), -1, keepdims=True)

    elane = lane - MOE_GROUPS
    in_grp = (elane >= grp * MOE_EPG) & (elane < (grp + 1) * MOE_EPG)
    el = jnp.where(in_grp, lg, neg)
    m1 = jnp.max(el, -1, keepdims=True)
    e1 = jnp.min(jnp.where(in_grp & (el == m1), elane, big), -1, keepdims=True)
    rest = in_grp & (elane != e1)
    el2 = jnp.where(rest, lg, neg)
    m2 = jnp.max(el2, -1, keepdims=True)
    e2 = jnp.min(jnp.where(rest & (el2 == m2), elane, big), -1, keepdims=True)
    t2 = jnp.exp(m2 - m1)
    g1 = p_grp * (1.0 / (1.0 + t2))
    g2 = p_grp * (t2 / (1.0 + t2))

    oh1 = lane == e1
    oh2 = lane == e2
    onehot = jnp.where(oh1 | oh2, 1.0, 0.0)
    ri = lax.broadcasted_iota(jnp.int32, (tm, tm), 0)
    ci = lax.broadcasted_iota(jnp.int32, (tm, tm), 1)
    tri = jnp.where(ci < ri, 1.0, 0.0).astype(BF16)
    before = jnp.dot(tri, onehot.astype(BF16), preferred_element_type=F32) + carry_ref[0:1, :]
    r1 = jnp.sum(jnp.where(oh1, before, 0.0), -1, keepdims=True)
    r2 = jnp.sum(jnp.where(oh2, before, 0.0), -1, keepdims=True)
    total = carry_ref[0:1, :] + jnp.sum(onehot, 0, keepdims=True)
    carry_ref[...] = jnp.broadcast_to(total, carry_ref.shape)
    cnt_ref[...] = jnp.broadcast_to(total, cnt_ref.shape).astype(jnp.int32)

    gate_ref[...] = jnp.where(lane_i == 0, g1, jnp.where(lane_i == 1, g2, 0.0))
    idx_ref[...] = jnp.where(lane_i == 0, e1, jnp.where(lane_i == 1, e2,
                             jnp.where(lane_i == 2, r1, jnp.where(lane_i == 3, r2, 0.0)))).astype(jnp.int32)


def _route(logits, *, tm=1024):
    n = logits.shape[0]
    return pl.pallas_call(
        functools.partial(_route_body, tm=tm),
        grid=(n // tm,),
        in_specs=[pl.BlockSpec((tm, LANES), lambda i: (i, 0))],
        out_specs=[
            pl.BlockSpec((tm, LANES), lambda i: (i, 0)),
            pl.BlockSpec((tm, LANES), lambda i: (i, 0)),
            pl.BlockSpec((8, LANES), lambda i: (0, 0)),
        ],
        out_shape=[
            jax.ShapeDtypeStruct((n, LANES), F32),
            jax.ShapeDtypeStruct((n, LANES), jnp.int32),
            jax.ShapeDtypeStruct((8, LANES), jnp.int32),
        ],
        scratch_shapes=[pltpu.VMEM((8, LANES), F32)],
        compiler_params=_cparams("arbitrary"),
        name="route",
    )(logits)


def _dest_body(idx_ref, ps_ref, o_ref):
    idx = idx_ref[...]
    lane = lax.broadcasted_iota(jnp.int32, idx.shape, 1)
    ps = ps_ref[...].astype(F32)
    d1 = jnp.sum(jnp.where(lane == idx[:, 0:1], ps, 0.0), -1, keepdims=True).astype(jnp.int32) + idx[:, 2:3]
    d2 = jnp.sum(jnp.where(lane == idx[:, 1:2], ps, 0.0), -1, keepdims=True).astype(jnp.int32) + idx[:, 3:4]
    o_ref[...] = jnp.where(lane == 0, d1, jnp.where(lane == 1, d2, 0))


def _dest(idx, pad_start_row, *, tm=1024):
    n = idx.shape[0]
    return pl.pallas_call(
        _dest_body,
        grid=(n // tm,),
        in_specs=[pl.BlockSpec((tm, LANES), lambda i: (i, 0)), pl.BlockSpec((1, LANES), lambda i: (0, 0))],
        out_specs=pl.BlockSpec((tm, LANES), lambda i: (i, 0)),
        out_shape=jax.ShapeDtypeStruct((n, LANES), jnp.int32),
        compiler_params=_cparams("parallel"),
        name="dest",
    )(idx, pad_start_row)


DMA_GROUP = 8


def _row_tile(ref, row):
    start = row * ROW_CHUNKS
    return ref.at[pl.ds(start if isinstance(row, int) else pl.multiple_of(start, ROW_CHUNKS), ROW_CHUNKS)]


def _dispatch_body(dest_ref, h_ref, xin_ref, xpad_ref, sem, *, tm):
    del xin_ref

    def row_copy(src_row, dst_row):
        return pltpu.make_async_copy(_row_tile(h_ref, src_row), _row_tile(xpad_ref, dst_row), sem)

    for r0 in range(0, tm, DMA_GROUP):
        dst = [dest_ref[0, 0, 2 * r0 + j] for j in range(2 * DMA_GROUP)]
        for j in range(2 * DMA_GROUP):
            row_copy(r0 + j // 2, dst[j]).start(priority=j % 2)

    def wait(g, carry):
        for _ in range(2 * DMA_GROUP):
            row_copy(0, 0).wait()
        return carry

    lax.fori_loop(0, tm // DMA_GROUP, wait, 0)


def _dispatch(dest_smem, h1_rows, xpad0, *, tm):
    n = h1_rows.shape[0] // ROW_CHUNKS
    return pl.pallas_call(
        functools.partial(_dispatch_body, tm=tm),
        grid=(n // tm,),
        in_specs=[
            pl.BlockSpec((1, 1, 2 * tm), lambda i: (i, 0, 0), memory_space=pltpu.SMEM),
            pl.BlockSpec((tm * ROW_CHUNKS, LANES), lambda i: (i, 0)),
            pl.BlockSpec(memory_space=pl.ANY),
        ],
        out_specs=pl.BlockSpec(memory_space=pl.ANY),
        scratch_shapes=[pltpu.SemaphoreType.DMA(())],
        out_shape=jax.ShapeDtypeStruct(xpad0.shape, xpad0.dtype),
        input_output_aliases={2: 0},
        compiler_params=_cparams("arbitrary"),
        name="dispatch",
    )(dest_smem, h1_rows, xpad0)


def _experts_body(blk_e_ref, next_e_ref, slot_ref, nused_ref, x_ref, w1_ref, w3_ref, w2_ref, y_ref,
                  st1, st3, st2, w1b, w3b, w2b, sem):
    i = pl.program_id(0)
    e = blk_e_ref[i]
    first_of_expert = (i == 0) | (e != blk_e_ref[jnp.maximum(i - 1, 0)])

    def weight_copies(expert, slot):
        return [pltpu.make_async_copy(w_ref.at[expert], st.at[slot], sem.at[slot])
                for w_ref, st in ((w1_ref, st1), (w3_ref, st3), (w2_ref, st2))]

    @pl.when(i == 0)
    def _():
        for cp in weight_copies(e, slot_ref[0]):
            cp.start()

    @pl.when(first_of_expert)
    def _():
        slot = slot_ref[i]
        for cp in weight_copies(e, slot):
            cp.wait()
        w1b[...] = st1[slot].astype(BF16)
        w3b[...] = st3[slot].astype(BF16)
        w2b[...] = st2[slot].astype(BF16)

        @pl.when(next_e_ref[i] >= 0)
        def _():
            for cp in weight_copies(next_e_ref[i], 1 - slot):
                cp.start()

    @pl.when(i < nused_ref[0])
    def _():
        xb = _load_row_tiles(x_ref, EXPERT_ROWS)
        a = jnp.dot(xb, w1b[...], preferred_element_type=F32)
        b = jnp.dot(xb, w3b[...], preferred_element_type=F32)
        hb = (a * jax.nn.sigmoid(a)) * b
        _store_row_tiles(y_ref, jnp.dot(hb.astype(BF16), w2b[...], preferred_element_type=F32))

    @pl.when(i >= nused_ref[0])
    def _():
        y_ref[...] = jnp.zeros_like(y_ref)


def _experts(blk_e, next_e, slot, n_used, xpad, w1, w3, w2):
    tm = EXPERT_ROWS * ROW_CHUNKS
    return pl.pallas_call(
        _experts_body,
        grid_spec=pltpu.PrefetchScalarGridSpec(
            num_scalar_prefetch=4,
            grid=(xpad.shape[0] // tm,),
            in_specs=[
                pl.BlockSpec((tm, LANES), lambda i, be, ne, sl, nu: (jnp.minimum(i, jnp.maximum(nu[0] - 1, 0)), 0)),
                pl.BlockSpec(memory_space=pl.ANY),
                pl.BlockSpec(memory_space=pl.ANY),
                pl.BlockSpec(memory_space=pl.ANY),
            ],
            out_specs=pl.BlockSpec((tm, LANES), lambda i, be, ne, sl, nu: (i, 0)),
            scratch_shapes=[
                pltpu.VMEM((2, D_MODEL, D_EXPERT), F32),
                pltpu.VMEM((2, D_MODEL, D_EXPERT), F32),
                pltpu.VMEM((2, D_EXPERT, D_MODEL), F32),
                pltpu.VMEM((D_MODEL, D_EXPERT), BF16),
                pltpu.VMEM((D_MODEL, D_EXPERT), BF16),
                pltpu.VMEM((D_EXPERT, D_MODEL), BF16),
                pltpu.SemaphoreType.DMA((2,)),
            ],
        ),
        out_shape=jax.ShapeDtypeStruct(xpad.shape, xpad.dtype),
        compiler_params=_cparams("arbitrary"),
        name="experts",
    )(blk_e, next_e, slot, n_used, xpad, w1, w3, w2)


def _combine_body(dest_ref, dest_next_ref, h1_ref, gate_ref, l2g_ref, l2b_ref, ypad_ref, o_ref, buf, sem, *, tm):
    i = pl.program_id(0)
    half = i % 2

    def row_copy(src_row, r, slot, hf):
        return pltpu.make_async_copy(_row_tile(ypad_ref, src_row), _row_tile(buf.at[2 * hf + slot], r), sem.at[hf])

    def gather(idx_ref, hf):
        for r0 in range(0, tm, DMA_GROUP):
            src = [idx_ref[0, 0, 2 * r0 + j] for j in range(2 * DMA_GROUP)]
            for j in range(2 * DMA_GROUP):
                row_copy(src[j], r0 + j // 2, j % 2, hf).start(priority=j % 2)

    @pl.when(i == 0)
    def _():
        gather(dest_ref, 0)

    @pl.when(i + 1 < pl.num_programs(0))
    def _():
        gather(dest_next_ref, 1 - half)

    def wait(g, carry):
        for j in range(2 * DMA_GROUP):
            row_copy(0, 0, j % 2, half).wait()
        return carry

    lax.fori_loop(0, tm // DMA_GROUP, wait, 0)
    gate = gate_ref[...]
    ffn = (_load_row_tiles(buf.at[2 * half], tm).astype(F32) * gate[:, 0:1]
           + _load_row_tiles(buf.at[2 * half + 1], tm).astype(F32) * gate[:, 1:2])
    o_ref[...] = _layer_norm(DEEPNORM_ALPHA * h1_ref[...] + ffn, l2g_ref[...], l2b_ref[...])


def _combine(dest_smem, h1, gates, l2g, l2b, ypad, *, tm):
    n = h1.shape[0]
    steps = n // tm
    return pl.pallas_call(
        functools.partial(_combine_body, tm=tm),
        grid=(steps,),
        in_specs=[
            pl.BlockSpec((1, 1, 2 * tm), lambda i: (i, 0, 0), memory_space=pltpu.SMEM),
            pl.BlockSpec((1, 1, 2 * tm), lambda i: (jnp.minimum(i + 1, steps - 1), 0, 0), memory_space=pltpu.SMEM),
            pl.BlockSpec((tm, D_MODEL), lambda i: (i, 0)),
            pl.BlockSpec((tm, LANES), lambda i: (i, 0)),
            pl.BlockSpec((1, D_MODEL), lambda i: (0, 0)),
            pl.BlockSpec((1, D_MODEL), lambda i: (0, 0)),
            pl.BlockSpec(memory_space=pl.ANY),
        ],
        out_specs=pl.BlockSpec((tm, D_MODEL), lambda i: (i, 0)),
        scratch_shapes=[
            pltpu.VMEM((4, tm * ROW_CHUNKS, LANES), jnp.uint32),
            pltpu.SemaphoreType.DMA((2,)),
        ],
        out_shape=jax.ShapeDtypeStruct((n, D_MODEL), F32),
        compiler_params=_cparams("arbitrary"),
        name="combine",
    )(dest_smem, dest_smem, h1, gates, l2g, l2b, ypad)


def _row(v, width=None):
    v = v.reshape(1, -1).astype(F32)
    if width is not None and v.shape[1] < width:
        v = jnp.pad(v, ((0, 0), (0, width - v.shape[1])))
    return v


def kernel(x, ln_in_g, ln_in_b, w_in, b_in, gm_ln_g, gm_ln_b, gm_w_s, gm_b_s, dn_conv_w, dn_a_log,
           dn_dt_bias, dn_norm_w, w_pa, w_pb, w_o, ln1_g, ln1_b, w_rg, b_rg, w_re, b_re, w1, w3, w2,
           ln2_g, ln2_b):
    bsz, t, d = x.shape
    n = bsz * t
    x2 = x.reshape(n, d)
    l = 0

    w_t, bi = jnp.swapaxes(w_in, 1, 2)[l], b_in[l]
    c_ab = N_FRONT_BLOCKS * D_MODEL
    c_gate = c_ab + AB_COLS
    w_ab = jnp.pad(w_t[c_ab:c_gate].T, ((0, 0), (0, LANES - AB_COLS))).astype(BF16)
    b_ab = _row(bi[c_ab:c_gate], LANES)
    b_proj = jnp.concatenate([bi[:c_ab], bi[c_gate:]]).reshape(1, PROJ_WIDTH)

    ln_g, ln_b = _row(ln_in_g), _row(ln_in_b)
    h, ab = _ln_ab(x2, ln_g, ln_b, w_ab, b_ab, tm=min(2048, n))
    proj = _inproj(h, w_t, b_proj, tm=min(2048, n))

    bs_full = jnp.broadcast_to(gm_b_s[l][:, :, None], (GM_GROUPS, GM_BLOCK, LANES))
    ya = _gmlp(proj, _row(gm_ln_g[l]), _row(gm_ln_b[l]), gm_w_s[l], bs_full, nblk=min(8, n // GM_BLOCK))

    yb = _deltanet(proj, ab, dn_conv_w[l], _row(dn_a_log[l], LANES), _row(dn_dt_bias[l], LANES),
                   _row(dn_norm_w[l]), bsz=bsz, t=t, nb=min(4, bsz))

    w_r = jnp.concatenate([w_rg[l], w_re[l], jnp.zeros((d, LANES - MOE_GROUPS - N_EXPERTS), F32)], axis=1)
    w_r_hi = w_r.astype(BF16)
    w_r_lo = (w_r - w_r_hi.astype(F32)).astype(BF16)
    b_r = _row(jnp.concatenate([b_rg[l], b_re[l]]), LANES)
    h1, h1_rows, logits = _merge(x2, ya, yb, proj, ln_g, ln_b, w_pa[l].astype(BF16), w_pb[l].astype(BF16),
                        w_o[l].astype(BF16), _row(ln1_g[l]), _row(ln1_b[l]), w_r_hi, w_r_lo, b_r,
                        tm=min(1024, n))

    gates, idx, counts = _route(logits, tm=min(1024, n))

    tm_e = EXPERT_ROWS
    cnt = counts[0, :N_EXPERTS]
    padded = (cnt + tm_e - 1) // tm_e * tm_e
    pad_end = jnp.cumsum(padded)
    pad_start = (pad_end - padded).astype(jnp.int32)
    n_blocks = (2 * n + N_EXPERTS * (tm_e - 1) + tm_e - 1) // tm_e
    blk_start = jnp.arange(n_blocks, dtype=jnp.int32) * tm_e
    blk_e = jnp.minimum(jnp.sum(pad_end[None, :] <= blk_start[:, None], -1), N_EXPERTS - 1).astype(jnp.int32)
    n_used = (pad_end[-1] // tm_e).astype(jnp.int32).reshape(1)
    blk_ids = jnp.arange(n_blocks, dtype=jnp.int32)
    is_first = jnp.concatenate([jnp.ones((1,), bool), blk_e[1:] != blk_e[:-1]])
    slot = ((jnp.cumsum(is_first) - 1) % 2).astype(jnp.int32)
    later_first = is_first[None, :] & (blk_ids[None, :] > blk_ids[:, None])
    next_first = jnp.min(jnp.where(later_first, blk_ids[None, :], n_blocks), axis=1)
    next_e = jnp.where(next_first < n_blocks, blk_e[jnp.minimum(next_first, n_blocks - 1)], -1).astype(jnp.int32)

    ps_row = jnp.pad(pad_start, (0, LANES - N_EXPERTS)).reshape(1, LANES)
    dest = _dest(idx, ps_row, tm=min(1024, n))[:, :2]
    tm_d, tm_c = min(1024, n), 256
    xpad0 = jnp.zeros((n_blocks * tm_e * ROW_CHUNKS, LANES), jnp.uint32)
    xpad = _dispatch(dest.reshape(n // tm_d, 1, 2 * tm_d), h1_rows, xpad0, tm=tm_d)
    ypad = _experts(blk_e, next_e, slot, n_used, xpad, w1[l], w3[l], w2[l])
    out = _combine(dest.reshape(n // tm_c, 1, 2 * tm_c), h1, gates, _row(ln2_g[l]), _row(ln2_b[l]), ypad,
                   tm=tm_c)
    return out.reshape(bsz, t, d)
```

```python
import functools
import math

import jax
import jax.numpy as jnp
from jax import lax
from jax.experimental import pallas as pl
from jax.experimental.pallas import tpu as pltpu

D_MODEL = 1024
GM_GROUPS = 8
GM_BLOCK = 128
GM_CHUNK = 64
DN_HEADS = 8
DN_DK = 128
DN_CONV = 4
MOE_GROUPS = 4
MOE_EPG = 8
N_EXPERTS = MOE_GROUPS * MOE_EPG
D_EXPERT = D_MODEL // 2
LN_EPS = 1e-5
RMS_EPS = 1e-6
L2_EPS = 1e-6
DEEPNORM_ALPHA = 2.0 ** 0.25

LANES = 128
COL_U, COL_V, COL_Q, COL_K, COL_VV, COL_Z, COL_GA, COL_GB = range(8)
N_PROJ_BLOCKS = 8
N_FRONT_BLOCKS = 6
AB_COLS = 2 * DN_HEADS
PROJ_WIDTH = N_PROJ_BLOCKS * D_MODEL

DN_CHUNK = 64
CARRY_ROWS = 16
EXPERT_ROWS = 512
VMEM_LIMIT = 56 * 1024 * 1024
SUBLANES = 8
ROW_WORDS = D_MODEL // 2
ROW_CHUNKS = ROW_WORDS // LANES

F32 = jnp.float32
BF16 = jnp.bfloat16


def _cparams(*sem):
    return pltpu.CompilerParams(dimension_semantics=sem, vmem_limit_bytes=VMEM_LIMIT)


def _store_row_tiles(ref, val):
    m = val.shape[0]
    hi = pltpu.bitcast(val[:, :ROW_WORDS].astype(BF16).astype(F32), jnp.uint32)
    lo = pltpu.bitcast(val[:, ROW_WORDS:].astype(BF16).astype(F32), jnp.uint32)
    words = hi | (lo >> 16)
    for ch in range(ROW_CHUNKS):
        ref[pl.ds(ch, m, stride=ROW_CHUNKS), :] = words[:, ch * LANES:(ch + 1) * LANES]


def _load_row_tiles(ref, m):
    words = jnp.concatenate([ref[pl.ds(ch, m, stride=ROW_CHUNKS), :] for ch in range(ROW_CHUNKS)], axis=1)
    hi = pltpu.bitcast(words & jnp.uint32(0xFFFF0000), F32)
    lo = pltpu.bitcast(words << 16, F32)
    return jnp.concatenate([hi, lo], axis=1).astype(BF16)


def _layer_norm(x, g, b):
    mu = jnp.mean(x, -1, keepdims=True)
    xc = x - mu
    var = jnp.mean(xc * xc, -1, keepdims=True)
    return xc * lax.rsqrt(var + LN_EPS) * g + b


def _dot(a, b):
    return jnp.dot(a.astype(BF16), b.astype(BF16), preferred_element_type=F32)


def _dot_nt(a, b):
    return lax.dot_general(a.astype(BF16), b.astype(BF16), (((1,), (1,)), ((), ())),
                           preferred_element_type=F32)


def _ln_ab_body(x_ref, g_ref, b_ref, wab_ref, bab_ref, h_ref, ab_ref):
    h = _layer_norm(x_ref[...], g_ref[...], b_ref[...]).astype(BF16)
    h_ref[...] = h
    ab_ref[...] = jnp.dot(h, wab_ref[...], preferred_element_type=F32) + bab_ref[...]


def _ln_ab(x2, ln_g, ln_b, w_ab, b_ab, *, tm):
    n = x2.shape[0]
    return pl.pallas_call(
        _ln_ab_body,
        grid=(n // tm,),
        in_specs=[
            pl.BlockSpec((tm, D_MODEL), lambda i: (i, 0)),
            pl.BlockSpec((1, D_MODEL), lambda i: (0, 0)),
            pl.BlockSpec((1, D_MODEL), lambda i: (0, 0)),
            pl.BlockSpec((D_MODEL, LANES), lambda i: (0, 0)),
            pl.BlockSpec((1, LANES), lambda i: (0, 0)),
        ],
        out_specs=[pl.BlockSpec((tm, D_MODEL), lambda i: (i, 0)), pl.BlockSpec((tm, LANES), lambda i: (i, 0))],
        out_shape=[jax.ShapeDtypeStruct((n, D_MODEL), BF16), jax.ShapeDtypeStruct((n, LANES), F32)],
        compiler_params=_cparams("parallel"),
        name="ln_ab",
    )(x2, ln_g, ln_b, w_ab, b_ab)


def _inproj_body(h_ref, wa_ref, wb_ref, bias_ref, o_ref, w_scr):
    j = pl.program_id(0)

    @pl.when((pl.program_id(1) == 0) & (j < N_FRONT_BLOCKS))
    def _():
        w_scr[...] = wa_ref[...].T.astype(BF16)

    @pl.when((pl.program_id(1) == 0) & (j >= N_FRONT_BLOCKS))
    def _():
        both = jnp.concatenate([wa_ref[...], wb_ref[...]], axis=0)
        w_scr[...] = both[AB_COLS:AB_COLS + D_MODEL].T.astype(BF16)

    o_ref[...] = (jnp.dot(h_ref[...], w_scr[...], preferred_element_type=F32) + bias_ref[...]).astype(o_ref.dtype)


def _inproj(h, w_t, bias, *, tm=1024):
    n = h.shape[0]
    return pl.pallas_call(
        _inproj_body,
        grid=(N_PROJ_BLOCKS, n // tm),
        in_specs=[
            pl.BlockSpec((tm, D_MODEL), lambda j, i: (i, 0)),
            pl.BlockSpec((D_MODEL, D_MODEL), lambda j, i: (j, 0)),
            pl.BlockSpec((D_MODEL, D_MODEL), lambda j, i: (jnp.maximum(j, N_FRONT_BLOCKS) + 1, 0)),
            pl.BlockSpec((1, D_MODEL), lambda j, i: (0, j)),
        ],
        out_specs=pl.BlockSpec((tm, D_MODEL), lambda j, i: (i, j)),
        out_shape=jax.ShapeDtypeStruct((n, N_PROJ_BLOCKS * D_MODEL), BF16),
        scratch_shapes=[pltpu.VMEM((D_MODEL, D_MODEL), BF16)],
        compiler_params=_cparams("arbitrary", "arbitrary"),
        name="inproj",
    )(h, w_t, w_t, bias)


def _gelu(x):
    return 0.5 * x * (1.0 + lax.erf(x * (1.0 / math.sqrt(2.0))))


def _gmlp_body(u_ref, v_ref, lng_ref, lnb_ref, ws_ref, bs_ref, o_ref, *, nblk):
    u = _gelu(u_ref[...].astype(F32))
    v = _layer_norm(_gelu(v_ref[...].astype(F32)), lng_ref[...], lnb_ref[...]).astype(BF16)
    row_chunk = lax.broadcasted_iota(jnp.int32, (GM_BLOCK, GM_BLOCK), 0) // GM_CHUNK
    col_chunk = lax.broadcasted_iota(jnp.int32, (GM_BLOCK, GM_BLOCK), 1) // GM_CHUNK
    causal = col_chunk <= row_chunk
    for g in range(GM_GROUPS):
        cols = slice(g * LANES, (g + 1) * LANES)
        w = jnp.where(causal, ws_ref[g], 0.0).astype(BF16)
        for blk in range(nblk):
            rows = slice(blk * GM_BLOCK, (blk + 1) * GM_BLOCK)
            s = jnp.dot(w, v[rows, cols], preferred_element_type=F32) + bs_ref[g]
            o_ref[rows, cols] = (u[rows, cols] * s).astype(o_ref.dtype)


def _gmlp(proj, ln_g, ln_b, w_s, b_s_full, *, nblk):
    n = proj.shape[0]
    rows = nblk * GM_BLOCK
    return pl.pallas_call(
        functools.partial(_gmlp_body, nblk=nblk),
        grid=(n // rows,),
        in_specs=[
            pl.BlockSpec((rows, D_MODEL), lambda i: (i, COL_U)),
            pl.BlockSpec((rows, D_MODEL), lambda i: (i, COL_V)),
            pl.BlockSpec((1, D_MODEL), lambda i: (0, 0)),
            pl.BlockSpec((1, D_MODEL), lambda i: (0, 0)),
            pl.BlockSpec((GM_GROUPS, GM_BLOCK, GM_BLOCK), lambda i: (0, 0, 0)),
            pl.BlockSpec((GM_GROUPS, GM_BLOCK, LANES), lambda i: (0, 0, 0)),
        ],
        out_specs=pl.BlockSpec((rows, D_MODEL), lambda i: (i, 0)),
        out_shape=jax.ShapeDtypeStruct((n, D_MODEL), BF16),
        compiler_params=_cparams("parallel"),
        name="gmlp",
    )(proj, proj, ln_g, ln_b, w_s, b_s_full)


def _cumsum_rows(x):
    n = x.shape[0]
    row = lax.broadcasted_iota(jnp.int32, x.shape, 0)
    shift = 1
    while shift < n:
        x = x + jnp.where(row >= shift, pltpu.roll(x, shift, 0), 0.0)
        shift *= 2
    return x


def _deltanet_body(q_ref, k_ref, v_ref, z_ref, ab_ref, cw_ref, alog_ref, dtb_ref, nw_ref, o_ref,
                   xs_ref, qkv_ref, state_ref, kq_ref, m_ref, r0_ref, qd_ref, kd_ref, cd_ref, zg_ref, *, nb):
    c = DN_CHUNK
    w3 = 3 * D_MODEL
    heads = range(DN_HEADS)
    units = range(nb * DN_HEADS)
    hsl = lambda base, h: slice(base + h * DN_DK, base + (h + 1) * DN_DK)

    @pl.when(pl.program_id(1) == 0)
    def _():
        xs_ref[:, 0:CARRY_ROWS, :] = jnp.zeros((nb, CARRY_ROWS, w3), BF16)
        for ref in (qkv_ref, state_ref, kq_ref, m_ref, r0_ref, qd_ref, kd_ref, cd_ref, zg_ref):
            ref[...] = jnp.zeros_like(ref)

    kq_in = [kq_ref[u] for u in units]
    kq = [lax.dot_general(kq_in[u], kq_in[u], (((1,), (1,)), ((), ())), preferred_element_type=F32)
          for u in units]
    pm = [kq[u] * m_ref[u] for u in units]
    attn = [x[c:, 0:c].astype(BF16) for x in pm]

    lane = lax.broadcasted_iota(jnp.int32, (c, 2 * c), 1)
    right = lane >= c
    eye_right = jnp.where(lane - c == lax.broadcasted_iota(jnp.int32, (c, 2 * c), 0), 1.0, 0.0)
    pa = [x[0:c] + eye_right for x in pm]
    span = 1
    while span < c:
        pab = [x.astype(BF16) for x in pa]
        pa = [jnp.dot(pab[u][:, 0:c], pab[u], preferred_element_type=F32) + jnp.where(right, pa[u], 0.0)
              for u in units]
        span *= 2
    r = [jnp.dot(pa[u].astype(BF16), r0_ref[u], preferred_element_type=F32) for u in units]

    s_old = [state_ref[u] for u in units]
    ws = [jnp.dot(jnp.concatenate([r[u][:, DN_DK:].astype(BF16), qd_ref[u]], axis=0),
                  s_old[u].astype(BF16), preferred_element_type=F32) for u in units]
    v_new = [(r[u][:, 0:DN_DK] - ws[u][0:c]).astype(BF16) for u in units]
    o = [ws[u][c:] + jnp.dot(attn[u], v_new[u], preferred_element_type=F32) for u in units]
    ds = [lax.dot_general(kd_ref[u], v_new[u], (((0,), (0,)), ((), ())), preferred_element_type=F32)
          for u in units]
    for u in units:
        state_ref[u] = s_old[u] * cd_ref[u][0:1, :] + ds[u]
    for b in range(nb):
        for h in heads:
            u = b * DN_HEADS + h
            on = o[u] * lax.rsqrt(jnp.mean(o[u] * o[u], -1, keepdims=True) + RMS_EPS)
            o_ref[b, :, hsl(0, h)] = (on * zg_ref[b, :, hsl(0, h)]).astype(o_ref.dtype)

    ri = lax.broadcasted_iota(jnp.int32, (c, c), 0)
    ci = lax.broadcasted_iota(jnp.int32, (c, c), 1)
    causal = ri >= ci
    strict = ri > ci
    for b in range(nb):
        qkv = qkv_ref[b]
        ab = ab_ref[b]
        xg = ab + dtb_ref[...]
        softplus = jnp.maximum(xg, 0.0) + jnp.log1p(jnp.exp(-jnp.abs(xg)))
        gcum = _cumsum_rows(-jnp.exp(alog_ref[...]) * softplus)
        beta_all = jax.nn.sigmoid(ab)
        gcum_sq = gcum if c == LANES else jnp.concatenate([gcum, jnp.zeros((LANES - c, LANES), F32)], axis=0)
        gcum_t = gcum_sq.T

        z = z_ref[b].astype(F32)
        zg_ref[b] = z * jax.nn.sigmoid(z) * jnp.concatenate([nw_ref[...]] * DN_HEADS, axis=1)

        for h in heads:
            u = b * DN_HEADS + h
            qh, kh, vh = qkv[:, hsl(0, h)], qkv[:, hsl(D_MODEL, h)], qkv[:, hsl(2 * D_MODEL, h)]
            qh = qh * (lax.rsqrt(jnp.sum(qh * qh, -1, keepdims=True) + L2_EPS) * (DN_DK ** -0.5))
            kh = kh * lax.rsqrt(jnp.sum(kh * kh, -1, keepdims=True) + L2_EPS)
            g_col = gcum[:, h:h + 1]
            beta = beta_all[:, DN_HEADS + h:DN_HEADS + h + 1]
            g_last = gcum[c - 1:c, h:h + 1]
            decay = jnp.where(causal, jnp.exp(jnp.where(causal, g_col - gcum_t[h:h + 1, 0:c], 0.0)), 0.0)
            eg = jnp.exp(g_col)
            kq_ref[u, 0:c, :] = kh.astype(BF16)
            kq_ref[u, c:, :] = qh.astype(BF16)
            m_ref[u, 0:c, 0:c] = jnp.where(strict, -beta * decay, 0.0)
            m_ref[u, c:, 0:c] = decay
            r0_ref[u, c:, 0:DN_DK] = (vh * beta).astype(BF16)
            r0_ref[u, c:, DN_DK:] = (kh * (beta * eg)).astype(BF16)
            qd_ref[u] = (qh * eg).astype(BF16)
            kd_ref[u] = (kh * jnp.exp(g_last - g_col)).astype(BF16)
            cd_ref[u] = jnp.broadcast_to(jnp.exp(g_last), (SUBLANES, DN_DK))

    nsh = DN_CONV - 1
    srow = lax.broadcasted_iota(jnp.int32, (nsh * c, CARRY_ROWS + c), 0)
    scol = lax.broadcasted_iota(jnp.int32, (nsh * c, CARRY_ROWS + c), 1)
    shift_mat = jnp.where(scol == srow % c + srow // c + (CARRY_ROWS - nsh), 1.0, 0.0).astype(BF16)
    for b in range(nb):
        xs_ref[b, CARRY_ROWS:, 0:D_MODEL] = q_ref[b]
        xs_ref[b, CARRY_ROWS:, D_MODEL:2 * D_MODEL] = k_ref[b]
        xs_ref[b, CARRY_ROWS:, 2 * D_MODEL:] = v_ref[b]
        window = xs_ref[b]
        shifted = jnp.dot(shift_mat, window, preferred_element_type=F32)
        acc = cw_ref[nsh:DN_CONV, :] * window[CARRY_ROWS:].astype(F32)
        for j in range(nsh):
            acc = acc + cw_ref[j:j + 1, :] * shifted[j * c:(j + 1) * c]
        xs_ref[b, 0:CARRY_ROWS, :] = xs_ref[b, c:c + CARRY_ROWS, :]
        qkv_ref[b] = acc * jax.nn.sigmoid(acc)


def _deltanet(proj, ab, conv_w, alog_row, dtb_row, norm_w, *, bsz, t, nb=4):
    c = DN_CHUNK
    nc = t // c
    nu = nb * DN_HEADS
    proj3 = proj.reshape(bsz, t, PROJ_WIDTH)
    ab3 = ab.reshape(bsz, t, LANES)
    conv_chunk = lambda n: jnp.minimum(n, nc - 1)
    prep_chunk = lambda n: jnp.clip(n - 1, 0, nc - 1)
    out_chunk = lambda n: jnp.maximum(n - 2, 0)
    col = lambda j, chunk: pl.BlockSpec((nb, c, D_MODEL), lambda bp, n: (bp, chunk(n), j))
    yb = pl.pallas_call(
        functools.partial(_deltanet_body, nb=nb),
        grid=(bsz // nb, nc + 2),
        in_specs=[
            col(COL_Q, conv_chunk), col(COL_K, conv_chunk), col(COL_VV, conv_chunk), col(COL_Z, prep_chunk),
            pl.BlockSpec((nb, c, LANES), lambda bp, n: (bp, prep_chunk(n), 0)),
            pl.BlockSpec((DN_CONV, 3 * D_MODEL), lambda bp, n: (0, 0)),
            pl.BlockSpec((1, LANES), lambda bp, n: (0, 0)),
            pl.BlockSpec((1, LANES), lambda bp, n: (0, 0)),
            pl.BlockSpec((1, DN_DK), lambda bp, n: (0, 0)),
        ],
        out_specs=pl.BlockSpec((nb, c, D_MODEL), lambda bp, n: (bp, out_chunk(n), 0)),
        out_shape=jax.ShapeDtypeStruct((bsz, t, D_MODEL), BF16),
        scratch_shapes=[
            pltpu.VMEM((nb, CARRY_ROWS + c, 3 * D_MODEL), BF16),
            pltpu.VMEM((nb, c, 3 * D_MODEL), F32),
            pltpu.VMEM((nu, DN_DK, DN_DK), F32),
            pltpu.VMEM((nu, 2 * c, DN_DK), BF16),
            pltpu.VMEM((nu, 2 * c, 2 * c), F32),
            pltpu.VMEM((nu, 2 * c, 2 * DN_DK), BF16),
            pltpu.VMEM((nu, c, DN_DK), BF16),
            pltpu.VMEM((nu, c, DN_DK), BF16),
            pltpu.VMEM((nu, SUBLANES, DN_DK), F32),
            pltpu.VMEM((nb, c, D_MODEL), F32),
        ],
        compiler_params=_cparams("parallel", "arbitrary"),
        name="deltanet",
    )(proj3, proj3, proj3, proj3, ab3, conv_w, alog_row, dtb_row, norm_w)
    return yb.reshape(bsz * t, D_MODEL)


def _merge_body(x_ref, ya_ref, yb_ref, ga_ref, gb_ref, lng_ref, lnb_ref, wpa_ref, wpb_ref, wo_ref,
                l1g_ref, l1b_ref, wr_ref, wrlo_ref, br_ref, h1_ref, h1_rows_ref, logit_ref):
    h = _layer_norm(x_ref[...], lng_ref[...], lnb_ref[...])
    pa = jnp.dot(ya_ref[...], wpa_ref[...], preferred_element_type=F32)
    pb = jnp.dot(yb_ref[...], wpb_ref[...], preferred_element_type=F32)
    merged = jax.nn.sigmoid(ga_ref[...].astype(F32)) * pa + jax.nn.sigmoid(gb_ref[...].astype(F32)) * pb
    mix = jnp.dot(merged.astype(BF16), wo_ref[...], preferred_element_type=F32)
    h1 = _layer_norm(DEEPNORM_ALPHA * h + mix, l1g_ref[...], l1b_ref[...])
    h1_ref[...] = h1
    _store_row_tiles(h1_rows_ref, h1)
    tm = h1.shape[0]
    hi = h1.astype(BF16)
    lo = (h1 - hi.astype(F32)).astype(BF16)
    hw = jnp.dot(jnp.concatenate([hi, lo], axis=0), wr_ref[...], preferred_element_type=F32)
    logit_ref[...] = (hw[0:tm] + hw[tm:] + jnp.dot(hi, wrlo_ref[...], preferred_element_type=F32)
                      + br_ref[...])


def _merge(x2, ya, yb, proj, ln_g, ln_b, w_pa, w_pb, w_o, l1g, l1b, w_r, w_r_lo, b_r, *, tm=512):
    n = x2.shape[0]
    vec = lambda: pl.BlockSpec((1, D_MODEL), lambda i: (0, 0))
    mat = lambda: pl.BlockSpec((D_MODEL, D_MODEL), lambda i: (0, 0))
    return pl.pallas_call(
        _merge_body,
        grid=(n // tm,),
        in_specs=[
            pl.BlockSpec((tm, D_MODEL), lambda i: (i, 0)),
            pl.BlockSpec((tm, D_MODEL), lambda i: (i, 0)),
            pl.BlockSpec((tm, D_MODEL), lambda i: (i, 0)),
            pl.BlockSpec((tm, D_MODEL), lambda i: (i, COL_GA)),
            pl.BlockSpec((tm, D_MODEL), lambda i: (i, COL_GB)),
            vec(), vec(), mat(), mat(), mat(), vec(), vec(),
            pl.BlockSpec((D_MODEL, LANES), lambda i: (0, 0)),
            pl.BlockSpec((D_MODEL, LANES), lambda i: (0, 0)),
            pl.BlockSpec((1, LANES), lambda i: (0, 0)),
        ],
        out_specs=[
            pl.BlockSpec((tm, D_MODEL), lambda i: (i, 0)),
            pl.BlockSpec((tm * ROW_CHUNKS, LANES), lambda i: (i, 0)),
            pl.BlockSpec((tm, LANES), lambda i: (i, 0)),
        ],
        out_shape=[
            jax.ShapeDtypeStruct((n, D_MODEL), F32),
            jax.ShapeDtypeStruct((n * ROW_CHUNKS, LANES), jnp.uint32),
            jax.ShapeDtypeStruct((n, LANES), F32),
        ],
        compiler_params=_cparams("parallel"),
        name="merge",
    )(x2, ya, yb, proj, proj, ln_g, ln_b, w_pa, w_pb, w_o, l1g, l1b, w_r, w_r_lo, b_r)


def _route_body(logit_ref, gate_ref, idx_ref, cnt_ref, carry_ref, *, tm):
    @pl.when(pl.program_id(0) == 0)
    def _():
        carry_ref[...] = jnp.zeros_like(carry_ref)

    lg = logit_ref[...]
    lane_i = lax.broadcasted_iota(jnp.int32, lg.shape, 1)
    lane = lane_i.astype(F32)
    neg = jnp.float32(-jnp.inf)
    big = jnp.float32(1 << 20)

    is_grp = lane_i < MOE_GROUPS
    gl = jnp.where(is_grp, lg, neg)
    gmax = jnp.max(gl, -1, keepdims=True)
    grp = jnp.min(jnp.where(is_grp & (gl == gmax), lane, big), -1, keepdims=True)
    p_grp = 1.0 / jnp.sum(jnp.where(is_grp, jnp.exp(gl - gmax), 0.0), -1, keepdims=True)

    elane = lane - MOE_GROUPS
    in_grp = (elane >= grp * MOE_EPG) & (elane < (grp + 1) * MOE_EPG)
    el = jnp.where(in_grp, lg, neg)
    m1 = jnp.max(el, -1, keepdims=True)
    e1 = jnp.min(jnp.where(in_grp & (el == m1), elane, big), -1, keepdims=True)
    rest = in_grp & (elane != e1)
    el2 = jnp.where(rest, lg, neg)
    m2 = jnp.max(el2, -1, keepdims=True)
    e2 = jnp.min(jnp.where(rest & (el2 == m2), elane, big), -1, keepdims=True)
    t2 = jnp.exp(m2 - m1)
    g1 = p_grp * (1.0 / (1.0 + t2))
    g2 = p_grp * (t2 / (1.0 + t2))

    oh1 = lane == e1
    oh2 = lane == e2
    onehot = jnp.where(oh1 | oh2, 1.0, 0.0)
    ri = lax.broadcasted_iota(jnp.int32, (tm, tm), 0)
    ci = lax.broadcasted_iota(jnp.int32, (tm, tm), 1)
    tri = jnp.where(ci < ri, 1.0, 0.0).astype(BF16)
    before = jnp.dot(tri, onehot.astype(BF16), preferred_element_type=F32) + carry_ref[0:1, :]
    r1 = jnp.sum(jnp.where(oh1, before, 0.0), -1, keepdims=True)
    r2 = jnp.sum(jnp.where(oh2, before, 0.0), -1, keepdims=True)
    total = carry_ref[0:1, :] + jnp.sum(onehot, 0, keepdims=True)
    carry_ref[...] = jnp.broadcast_to(total, carry_ref.shape)
    cnt_ref[...] = jnp.broadcast_to(total, cnt_ref.shape).astype(jnp.int32)

    gate_ref[...] = jnp.where(lane_i == 0, g1, jnp.where(lane_i == 1, g2, 0.0))
    idx_ref[...] = jnp.where(lane_i == 0, e1, jnp.where(lane_i == 1, e2,
                             jnp.where(lane_i == 2, r1, jnp.where(lane_i == 3, r2, 0.0)))).astype(jnp.int32)


def _route(logits, *, tm=1024):
    n = logits.shape[0]
    return pl.pallas_call(
        functools.partial(_route_body, tm=tm),
        grid=(n // tm,),
        in_specs=[pl.BlockSpec((tm, LANES), lambda i: (i, 0))],
        out_specs=[
            pl.BlockSpec((tm, LANES), lambda i: (i, 0)),
            pl.BlockSpec((tm, LANES), lambda i: (i, 0)),
            pl.BlockSpec((8, LANES), lambda i: (0, 0)),
        ],
        out_shape=[
            jax.ShapeDtypeStruct((n, LANES), F32),
            jax.ShapeDtypeStruct((n, LANES), jnp.int32),
            jax.ShapeDtypeStruct((8, LANES), jnp.int32),
        ],
        scratch_shapes=[pltpu.VMEM((8, LANES), F32)],
        compiler_params=_cparams("arbitrary"),
        name="route",
    )(logits)


def _dest_body(idx_ref, ps_ref, o_ref):
    idx = idx_ref[...]
    lane = lax.broadcasted_iota(jnp.int32, idx.shape, 1)
    ps = ps_ref[...].astype(F32)
    d1 = jnp.sum(jnp.where(lane == idx[:, 0:1], ps, 0.0), -1, keepdims=True).astype(jnp.int32) + idx[:, 2:3]
    d2 = jnp.sum(jnp.where(lane == idx[:, 1:2], ps, 0.0), -1, keepdims=True).astype(jnp.int32) + idx[:, 3:4]
    o_ref[...] = jnp.where(lane == 0, d1, jnp.where(lane == 1, d2, 0))


def _dest(idx, pad_start_row, *, tm=1024):
    n = idx.shape[0]
    return pl.pallas_call(
        _dest_body,
        grid=(n // tm,),
        in_specs=[pl.BlockSpec((tm, LANES), lambda i: (i, 0)), pl.BlockSpec((1, LANES), lambda i: (0, 0))],
        out_specs=pl.BlockSpec((tm, LANES), lambda i: (i, 0)),
        out_shape=jax.ShapeDtypeStruct((n, LANES), jnp.int32),
        compiler_params=_cparams("parallel"),
        name="dest",
    )(idx, pad_start_row)


DMA_GROUP = 8


def _row_tile(ref, row):
    start = row * ROW_CHUNKS
    return ref.at[pl.ds(start if isinstance(row, int) else pl.multiple_of(start, ROW_CHUNKS), ROW_CHUNKS)]


def _dispatch_body(dest_ref, h_ref, xin_ref, xpad_ref, sem, *, tm):
    del xin_ref

    def row_copy(src_row, dst_row):
        return pltpu.make_async_copy(_row_tile(h_ref, src_row), _row_tile(xpad_ref, dst_row), sem)

    for r0 in range(0, tm, DMA_GROUP):
        dst = [dest_ref[0, 0, 2 * r0 + j] for j in range(2 * DMA_GROUP)]
        for j in range(2 * DMA_GROUP):
            row_copy(r0 + j // 2, dst[j]).start(priority=j % 2)

    def wait(g, carry):
        for _ in range(2 * DMA_GROUP):
            row_copy(0, 0).wait()
        return carry

    lax.fori_loop(0, tm // DMA_GROUP, wait, 0)


def _dispatch(dest_smem, h1_rows, xpad0, *, tm):
    n = h1_rows.shape[0] // ROW_CHUNKS
    return pl.pallas_call(
        functools.partial(_dispatch_body, tm=tm),
        grid=(n // tm,),
        in_specs=[
            pl.BlockSpec((1, 1, 2 * tm), lambda i: (i, 0, 0), memory_space=pltpu.SMEM),
            pl.BlockSpec((tm * ROW_CHUNKS, LANES), lambda i: (i, 0)),
            pl.BlockSpec(memory_space=pl.ANY),
        ],
        out_specs=pl.BlockSpec(memory_space=pl.ANY),
        scratch_shapes=[pltpu.SemaphoreType.DMA(())],
        out_shape=jax.ShapeDtypeStruct(xpad0.shape, xpad0.dtype),
        input_output_aliases={2: 0},
        compiler_params=_cparams("arbitrary"),
        name="dispatch",
    )(dest_smem, h1_rows, xpad0)


def _experts_body(blk_e_ref, next_e_ref, slot_ref, nused_ref, x_ref, w1_ref, w3_ref, w2_ref, y_ref,
                  st1, st3, st2, w1b, w3b, w2b, sem):
    i = pl.program_id(0)
    e = blk_e_ref[i]
    first_of_expert = (i == 0) | (e != blk_e_ref[jnp.maximum(i - 1, 0)])

    def weight_copies(expert, slot):
        return [pltpu.make_async_copy(w_ref.at[expert], st.at[slot], sem.at[slot])
                for w_ref, st in ((w1_ref, st1), (w3_ref, st3), (w2_ref, st2))]

    @pl.when(i == 0)
    def _():
        for cp in weight_copies(e, slot_ref[0]):
            cp.start()

    @pl.when(first_of_expert)
    def _():
        slot = slot_ref[i]
        for cp in weight_copies(e, slot):
            cp.wait()
        w1b[...] = st1[slot].astype(BF16)
        w3b[...] = st3[slot].astype(BF16)
        w2b[...] = st2[slot].astype(BF16)

        @pl.when(next_e_ref[i] >= 0)
        def _():
            for cp in weight_copies(next_e_ref[i], 1 - slot):
                cp.start()

    @pl.when(i < nused_ref[0])
    def _():
        xb = _load_row_tiles(x_ref, EXPERT_ROWS)
        a = jnp.dot(xb, w1b[...], preferred_element_type=F32)
        b = jnp.dot(xb, w3b[...], preferred_element_type=F32)
        hb = (a * jax.nn.sigmoid(a)) * b
        _store_row_tiles(y_ref, jnp.dot(hb.astype(BF16), w2b[...], preferred_element_type=F32))

    @pl.when(i >= nused_ref[0])
    def _():
        y_ref[...] = jnp.zeros_like(y_ref)


def _experts(blk_e, next_e, slot, n_used, xpad, w1, w3, w2):
    tm = EXPERT_ROWS * ROW_CHUNKS
    return pl.pallas_call(
        _experts_body,
        grid_spec=pltpu.PrefetchScalarGridSpec(
            num_scalar_prefetch=4,
            grid=(xpad.shape[0] // tm,),
            in_specs=[
                pl.BlockSpec((tm, LANES), lambda i, be, ne, sl, nu: (jnp.minimum(i, jnp.maximum(nu[0] - 1, 0)), 0)),
                pl.BlockSpec(memory_space=pl.ANY),
                pl.BlockSpec(memory_space=pl.ANY),
                pl.BlockSpec(memory_space=pl.ANY),
            ],
            out_specs=pl.BlockSpec((tm, LANES), lambda i, be, ne, sl, nu: (i, 0)),
            scratch_shapes=[
                pltpu.VMEM((2, D_MODEL, D_EXPERT), F32),
                pltpu.VMEM((2, D_MODEL, D_EXPERT), F32),
                pltpu.VMEM((2, D_EXPERT, D_MODEL), F32),
                pltpu.VMEM((D_MODEL, D_EXPERT), BF16),
                pltpu.VMEM((D_MODEL, D_EXPERT), BF16),
                pltpu.VMEM((D_EXPERT, D_MODEL), BF16),
                pltpu.SemaphoreType.DMA((2,)),
            ],
        ),
        out_shape=jax.ShapeDtypeStruct(xpad.shape, xpad.dtype),
        compiler_params=_cparams("arbitrary"),
        name="experts",
    )(blk_e, next_e, slot, n_used, xpad, w1, w3, w2)


def _combine_body(dest_ref, dest_next_ref, h1_ref, gate_ref, l2g_ref, l2b_ref, ypad_ref, o_ref, buf, sem, *, tm):
    i = pl.program_id(0)
    half = i % 2

    def row_copy(src_row, r, slot, hf):
        return pltpu.make_async_copy(_row_tile(ypad_ref, src_row), _row_tile(buf.at[2 * hf + slot], r), sem.at[hf])

    def gather(idx_ref, hf):
        for r0 in range(0, tm, DMA_GROUP):
            src = [idx_ref[0, 0, 2 * r0 + j] for j in range(2 * DMA_GROUP)]
            for j in range(2 * DMA_GROUP):
                row_copy(src[j], r0 + j // 2, j % 2, hf).start(priority=j % 2)

    def drain(hf):
        def wait(g, carry):
            for j in range(2 * DMA_GROUP):
                row_copy(0, 0, j % 2, hf).wait()
            return carry
        lax.fori_loop(0, tm // DMA_GROUP, wait, 0)

    @pl.when(i == 0)
    def _():
        gather(dest_ref, 0)

    drain(half)
    gather(dest_next_ref, 1 - half)
    gate = gate_ref[...]
    ffn = (_load_row_tiles(buf.at[2 * half], tm).astype(F32) * gate[:, 0:1]
           + _load_row_tiles(buf.at[2 * half + 1], tm).astype(F32) * gate[:, 1:2])
    o_ref[...] = _layer_norm(DEEPNORM_ALPHA * h1_ref[...] + ffn, l2g_ref[...], l2b_ref[...])

    @pl.when(i + 1 == pl.num_programs(0))
    def _():
        drain(1 - half)


def _combine(dest_smem, h1, gates, l2g, l2b, ypad, *, tm):
    n = h1.shape[0]
    steps = n // tm
    return pl.pallas_call(
        functools.partial(_combine_body, tm=tm),
        grid=(steps,),
        in_specs=[
            pl.BlockSpec((1, 1, 2 * tm), lambda i: (i, 0, 0), memory_space=pltpu.SMEM),
            pl.BlockSpec((1, 1, 2 * tm), lambda i: (jnp.minimum(i + 1, steps - 1), 0, 0), memory_space=pltpu.SMEM),
            pl.BlockSpec((tm, D_MODEL), lambda i: (i, 0)),
            pl.BlockSpec((tm, LANES), lambda i: (i, 0)),
            pl.BlockSpec((1, D_MODEL), lambda i: (0, 0)),
            pl.BlockSpec((1, D_MODEL), lambda i: (0, 0)),
            pl.BlockSpec(memory_space=pl.ANY),
        ],
        out_specs=pl.BlockSpec((tm, D_MODEL), lambda i: (i, 0)),
        scratch_shapes=[
            pltpu.VMEM((4, tm * ROW_CHUNKS, LANES), jnp.uint32),
            pltpu.SemaphoreType.DMA((2,)),
        ],
        out_shape=jax.ShapeDtypeStruct((n, D_MODEL), F32),
        compiler_params=_cparams("arbitrary"),
        name="combine",
    )(dest_smem, dest_smem, h1, gates, l2g, l2b, ypad)


def _row(v, width=None):
    v = v.reshape(1, -1).astype(F32)
    if width is not None and v.shape[1] < width:
        v = jnp.pad(v, ((0, 0), (0, width - v.shape[1])))
    return v


def kernel(x, ln_in_g, ln_in_b, w_in, b_in, gm_ln_g, gm_ln_b, gm_w_s, gm_b_s, dn_conv_w, dn_a_log,
           dn_dt_bias, dn_norm_w, w_pa, w_pb, w_o, ln1_g, ln1_b, w_rg, b_rg, w_re, b_re, w1, w3, w2,
           ln2_g, ln2_b):
    bsz, t, d = x.shape
    n = bsz * t
    x2 = x.reshape(n, d)
    l = 0

    w_t, bi = jnp.swapaxes(w_in, 1, 2)[l], b_in[l]
    c_ab = N_FRONT_BLOCKS * D_MODEL
    c_gate = c_ab + AB_COLS
    w_ab = jnp.pad(w_t[c_ab:c_gate].T, ((0, 0), (0, LANES - AB_COLS))).astype(BF16)
    b_ab = _row(bi[c_ab:c_gate], LANES)
    b_proj = jnp.concatenate([bi[:c_ab], bi[c_gate:]]).reshape(1, PROJ_WIDTH)

    ln_g, ln_b = _row(ln_in_g), _row(ln_in_b)
    h, ab = _ln_ab(x2, ln_g, ln_b, w_ab, b_ab, tm=min(2048, n))
    proj = _inproj(h, w_t, b_proj, tm=min(2048, n))

    bs_full = jnp.broadcast_to(gm_b_s[l][:, :, None], (GM_GROUPS, GM_BLOCK, LANES))
    ya = _gmlp(proj, _row(gm_ln_g[l]), _row(gm_ln_b[l]), gm_w_s[l], bs_full, nblk=min(8, n // GM_BLOCK))

    yb = _deltanet(proj, ab, dn_conv_w[l], _row(dn_a_log[l], LANES), _row(dn_dt_bias[l], LANES),
                   _row(dn_norm_w[l]), bsz=bsz, t=t, nb=min(4, bsz))

    w_r = jnp.concatenate([w_rg[l], w_re[l], jnp.zeros((d, LANES - MOE_GROUPS - N_EXPERTS), F32)], axis=1)
    w_r_hi = w_r.astype(BF16)
    w_r_lo = (w_r - w_r_hi.astype(F32)).astype(BF16)
    b_r = _row(jnp.concatenate([b_rg[l], b_re[l]]), LANES)
    h1, h1_rows, logits = _merge(x2, ya, yb, proj, ln_g, ln_b, w_pa[l].astype(BF16), w_pb[l].astype(BF16),
                        w_o[l].astype(BF16), _row(ln1_g[l]), _row(ln1_b[l]), w_r_hi, w_r_lo, b_r,
                        tm=min(1024, n))

    gates, idx, counts = _route(logits, tm=min(1024, n))

    tm_e = EXPERT_ROWS
    cnt = counts[0, :N_EXPERTS]
    padded = (cnt + tm_e - 1) // tm_e * tm_e
    pad_end = jnp.cumsum(padded)
    pad_start = (pad_end - padded).astype(jnp.int32)
    n_blocks = (2 * n + N_EXPERTS * (tm_e - 1) + tm_e - 1) // tm_e
    blk_start = jnp.arange(n_blocks, dtype=jnp.int32) * tm_e
    blk_e = jnp.minimum(jnp.sum(pad_end[None, :] <= blk_start[:, None], -1), N_EXPERTS - 1).astype(jnp.int32)
    n_used = (pad_end[-1] // tm_e).astype(jnp.int32).reshape(1)
    blk_ids = jnp.arange(n_blocks, dtype=jnp.int32)
    is_first = jnp.concatenate([jnp.ones((1,), bool), blk_e[1:] != blk_e[:-1]])
    slot = ((jnp.cumsum(is_first) - 1) % 2).astype(jnp.int32)
    later_first = is_first[None, :] & (blk_ids[None, :] > blk_ids[:, None])
    next_first = jnp.min(jnp.where(later_first, blk_ids[None, :], n_blocks), axis=1)
    next_e = jnp.where(next_first < n_blocks, blk_e[jnp.minimum(next_first, n_blocks - 1)], -1).astype(jnp.int32)

    ps_row = jnp.pad(pad_start, (0, LANES - N_EXPERTS)).reshape(1, LANES)
    dest = _dest(idx, ps_row, tm=min(1024, n))[:, :2]
    tm_d, tm_c = min(1024, n), 256
    xpad0 = jnp.zeros((n_blocks * tm_e * ROW_CHUNKS, LANES), jnp.uint32)
    xpad = _dispatch(dest.reshape(n // tm_d, 1, 2 * tm_d), h1_rows, xpad0, tm=tm_d)
    ypad = _experts(blk_e, next_e, slot, n_used, xpad, w1[l], w3[l], w2[l])
    out = _combine(dest.reshape(n // tm_c, 1, 2 * tm_c), h1, gates, _row(ln2_g[l]), _row(ln2_b[l]), ypad,
                   tm=tm_c)
    return out.reshape(bsz, t, d)
```

```python
import functools
import math

import jax
import jax.numpy as jnp
from jax import lax
from jax.experimental import pallas as pl
from jax.experimental.pallas import tpu as pltpu

D_MODEL = 1024
GM_GROUPS = 8
GM_BLOCK = 128
GM_CHUNK = 64
DN_HEADS = 8
DN_DK = 128
DN_CONV = 4
MOE_GROUPS = 4
MOE_EPG = 8
N_EXPERTS = MOE_GROUPS * MOE_EPG
D_EXPERT = D_MODEL // 2
LN_EPS = 1e-5
RMS_EPS = 1e-6
L2_EPS = 1e-6
DEEPNORM_ALPHA = 2.0 ** 0.25

LANES = 128
COL_U, COL_V, COL_Q, COL_K, COL_VV, COL_Z, COL_GA, COL_GB = range(8)
N_PROJ_BLOCKS = 8
N_FRONT_BLOCKS = 6
AB_COLS = 2 * DN_HEADS
PROJ_WIDTH = N_PROJ_BLOCKS * D_MODEL

DN_CHUNK = 64
CARRY_ROWS = 16
EXPERT_ROWS = 512
VMEM_LIMIT = 56 * 1024 * 1024
SUBLANES = 8
ROW_WORDS = D_MODEL // 2
ROW_CHUNKS = ROW_WORDS // LANES

F32 = jnp.float32
BF16 = jnp.bfloat16


def _cparams(*sem):
    return pltpu.CompilerParams(dimension_semantics=sem, vmem_limit_bytes=VMEM_LIMIT)


def _store_row_tiles(ref, val):
    m = val.shape[0]
    hi = pltpu.bitcast(val[:, :ROW_WORDS].astype(BF16).astype(F32), jnp.uint32)
    lo = pltpu.bitcast(val[:, ROW_WORDS:].astype(BF16).astype(F32), jnp.uint32)
    words = hi | (lo >> 16)
    for ch in range(ROW_CHUNKS):
        ref[pl.ds(ch, m, stride=ROW_CHUNKS), :] = words[:, ch * LANES:(ch + 1) * LANES]


def _load_row_tiles(ref, m):
    words = jnp.concatenate([ref[pl.ds(ch, m, stride=ROW_CHUNKS), :] for ch in range(ROW_CHUNKS)], axis=1)
    hi = pltpu.bitcast(words & jnp.uint32(0xFFFF0000), F32)
    lo = pltpu.bitcast(words << 16, F32)
    return jnp.concatenate([hi, lo], axis=1).astype(BF16)


def _layer_norm(x, g, b):
    mu = jnp.mean(x, -1, keepdims=True)
    xc = x - mu
    var = jnp.mean(xc * xc, -1, keepdims=True)
    return xc * lax.rsqrt(var + LN_EPS) * g + b


def _dot(a, b):
    return jnp.dot(a.astype(BF16), b.astype(BF16), preferred_element_type=F32)


def _dot_nt(a, b):
    return lax.dot_general(a.astype(BF16), b.astype(BF16), (((1,), (1,)), ((), ())),
                           preferred_element_type=F32)


def _ln_ab_body(x_ref, g_ref, b_ref, wab_ref, bab_ref, h_ref, ab_ref):
    h = _layer_norm(x_ref[...], g_ref[...], b_ref[...]).astype(BF16)
    h_ref[...] = h
    ab_ref[...] = jnp.dot(h, wab_ref[...], preferred_element_type=F32) + bab_ref[...]


def _ln_ab(x2, ln_g, ln_b, w_ab, b_ab, *, tm):
    n = x2.shape[0]
    return pl.pallas_call(
        _ln_ab_body,
        grid=(n // tm,),
        in_specs=[
            pl.BlockSpec((tm, D_MODEL), lambda i: (i, 0)),
            pl.BlockSpec((1, D_MODEL), lambda i: (0, 0)),
            pl.BlockSpec((1, D_MODEL), lambda i: (0, 0)),
            pl.BlockSpec((D_MODEL, LANES), lambda i: (0, 0)),
            pl.BlockSpec((1, LANES), lambda i: (0, 0)),
        ],
        out_specs=[pl.BlockSpec((tm, D_MODEL), lambda i: (i, 0)), pl.BlockSpec((tm, LANES), lambda i: (i, 0))],
        out_shape=[jax.ShapeDtypeStruct((n, D_MODEL), BF16), jax.ShapeDtypeStruct((n, LANES), F32)],
        compiler_params=_cparams("parallel"),
        name="ln_ab",
    )(x2, ln_g, ln_b, w_ab, b_ab)


def _inproj_body(h_ref, wa_ref, wb_ref, bias_ref, o_ref, w_scr):
    j = pl.program_id(0)

    @pl.when((pl.program_id(1) == 0) & (j < N_FRONT_BLOCKS))
    def _():
        w_scr[...] = wa_ref[...].T.astype(BF16)

    @pl.when((pl.program_id(1) == 0) & (j >= N_FRONT_BLOCKS))
    def _():
        both = jnp.concatenate([wa_ref[...], wb_ref[...]], axis=0)
        w_scr[...] = both[AB_COLS:AB_COLS + D_MODEL].T.astype(BF16)

    o_ref[...] = (jnp.dot(h_ref[...], w_scr[...], preferred_element_type=F32) + bias_ref[...]).astype(o_ref.dtype)


def _inproj(h, w_t, bias, *, tm=1024):
    n = h.shape[0]
    return pl.pallas_call(
        _inproj_body,
        grid=(N_PROJ_BLOCKS, n // tm),
        in_specs=[
            pl.BlockSpec((tm, D_MODEL), lambda j, i: (i, 0)),
            pl.BlockSpec((D_MODEL, D_MODEL), lambda j, i: (j, 0)),
            pl.BlockSpec((D_MODEL, D_MODEL), lambda j, i: (jnp.maximum(j, N_FRONT_BLOCKS) + 1, 0)),
            pl.BlockSpec((1, D_MODEL), lambda j, i: (0, j)),
        ],
        out_specs=pl.BlockSpec((tm, D_MODEL), lambda j, i: (i, j)),
        out_shape=jax.ShapeDtypeStruct((n, N_PROJ_BLOCKS * D_MODEL), BF16),
        scratch_shapes=[pltpu.VMEM((D_MODEL, D_MODEL), BF16)],
        compiler_params=_cparams("arbitrary", "arbitrary"),
        name="inproj",
    )(h, w_t, w_t, bias)


def _gelu(x):
    return 0.5 * x * (1.0 + lax.erf(x * (1.0 / math.sqrt(2.0))))


def _gmlp_body(u_ref, v_ref, lng_ref, lnb_ref, ws_ref, bs_ref, o_ref, *, nblk):
    u = _gelu(u_ref[...].astype(F32))
    v = _layer_norm(_gelu(v_ref[...].astype(F32)), lng_ref[...], lnb_ref[...]).astype(BF16)
    row_chunk = lax.broadcasted_iota(jnp.int32, (GM_BLOCK, GM_BLOCK), 0) // GM_CHUNK
    col_chunk = lax.broadcasted_iota(jnp.int32, (GM_BLOCK, GM_BLOCK), 1) // GM_CHUNK
    causal = col_chunk <= row_chunk
    for g in range(GM_GROUPS):
        cols = slice(g * LANES, (g + 1) * LANES)
        w = jnp.where(causal, ws_ref[g], 0.0).astype(BF16)
        for blk in range(nblk):
            rows = slice(blk * GM_BLOCK, (blk + 1) * GM_BLOCK)
            s = jnp.dot(w, v[rows, cols], preferred_element_type=F32) + bs_ref[g]
            o_ref[rows, cols] = (u[rows, cols] * s).astype(o_ref.dtype)


def _gmlp(proj, ln_g, ln_b, w_s, b_s_full, *, nblk):
    n = proj.shape[0]
    rows = nblk * GM_BLOCK
    return pl.pallas_call(
        functools.partial(_gmlp_body, nblk=nblk),
        grid=(n // rows,),
        in_specs=[
            pl.BlockSpec((rows, D_MODEL), lambda i: (i, COL_U)),
            pl.BlockSpec((rows, D_MODEL), lambda i: (i, COL_V)),
            pl.BlockSpec((1, D_MODEL), lambda i: (0, 0)),
            pl.BlockSpec((1, D_MODEL), lambda i: (0, 0)),
            pl.BlockSpec((GM_GROUPS, GM_BLOCK, GM_BLOCK), lambda i: (0, 0, 0)),
            pl.BlockSpec((GM_GROUPS, GM_BLOCK, LANES), lambda i: (0, 0, 0)),
        ],
        out_specs=pl.BlockSpec((rows, D_MODEL), lambda i: (i, 0)),
        out_shape=jax.ShapeDtypeStruct((n, D_MODEL), BF16),
        compiler_params=_cparams("parallel"),
        name="gmlp",
    )(proj, proj, ln_g, ln_b, w_s, b_s_full)


def _cumsum_rows(x):
    n = x.shape[0]
    row = lax.broadcasted_iota(jnp.int32, x.shape, 0)
    shift = 1
    while shift < n:
        x = x + jnp.where(row >= shift, pltpu.roll(x, shift, 0), 0.0)
        shift *= 2
    return x


def _deltanet_body(q_ref, k_ref, v_ref, z_ref, ab_ref, cw_ref, alog_ref, dtb_ref, nw_ref, o_ref,
                   xs_ref, qkv_ref, state_ref, kq_ref, m_ref, r0_ref, qd_ref, kd_ref, cd_ref, zg_ref, *, nb):
    c = DN_CHUNK
    w3 = 3 * D_MODEL
    heads = range(DN_HEADS)
    units = range(nb * DN_HEADS)
    hsl = lambda base, h: slice(base + h * DN_DK, base + (h + 1) * DN_DK)

    @pl.when(pl.program_id(1) == 0)
    def _():
        xs_ref[:, 0:CARRY_ROWS, :] = jnp.zeros((nb, CARRY_ROWS, w3), BF16)
        for ref in (qkv_ref, state_ref, kq_ref, m_ref, r0_ref, qd_ref, kd_ref, cd_ref, zg_ref):
            ref[...] = jnp.zeros_like(ref)

    kq_in = [kq_ref[u] for u in units]
    kq = [lax.dot_general(kq_in[u], kq_in[u], (((1,), (1,)), ((), ())), preferred_element_type=F32)
          for u in units]
    pm = [kq[u] * m_ref[u] for u in units]
    attn = [x[c:, 0:c].astype(BF16) for x in pm]

    lane = lax.broadcasted_iota(jnp.int32, (c, 2 * c), 1)
    right = lane >= c
    eye_right = jnp.where(lane - c == lax.broadcasted_iota(jnp.int32, (c, 2 * c), 0), 1.0, 0.0)
    pa = [x[0:c] + eye_right for x in pm]
    span = 1
    while span < c:
        pab = [x.astype(BF16) for x in pa]
        pa = [jnp.dot(pab[u][:, 0:c], pab[u], preferred_element_type=F32) + jnp.where(right, pa[u], 0.0)
              for u in units]
        span *= 2
    r = [jnp.dot(pa[u].astype(BF16), r0_ref[u], preferred_element_type=F32) for u in units]

    s_old = [state_ref[u] for u in units]
    ws = [jnp.dot(jnp.concatenate([r[u][:, DN_DK:].astype(BF16), qd_ref[u]], axis=0),
                  s_old[u].astype(BF16), preferred_element_type=F32) for u in units]
    v_new = [(r[u][:, 0:DN_DK] - ws[u][0:c]).astype(BF16) for u in units]
    o = [ws[u][c:] + jnp.dot(attn[u], v_new[u], preferred_element_type=F32) for u in units]
    ds = [lax.dot_general(kd_ref[u], v_new[u], (((0,), (0,)), ((), ())), preferred_element_type=F32)
          for u in units]
    for u in units:
        state_ref[u] = s_old[u] * cd_ref[u][0:1, :] + ds[u]
    for b in range(nb):
        for h in heads:
            u = b * DN_HEADS + h
            on = o[u] * lax.rsqrt(jnp.mean(o[u] * o[u], -1, keepdims=True) + RMS_EPS)
            o_ref[b, :, hsl(0, h)] = (on * zg_ref[b, :, hsl(0, h)]).astype(o_ref.dtype)

    ri = lax.broadcasted_iota(jnp.int32, (c, c), 0)
    ci = lax.broadcasted_iota(jnp.int32, (c, c), 1)
    causal = ri >= ci
    strict = ri > ci
    for b in range(nb):
        qkv = qkv_ref[b]
        ab = ab_ref[b]
        xg = ab + dtb_ref[...]
        softplus = jnp.maximum(xg, 0.0) + jnp.log1p(jnp.exp(-jnp.abs(xg)))
        gcum = _cumsum_rows(-jnp.exp(alog_ref[...]) * softplus)
        beta_all = jax.nn.sigmoid(ab)
        gcum_sq = gcum if c == LANES else jnp.concatenate([gcum, jnp.zeros((LANES - c, LANES), F32)], axis=0)
        gcum_t = gcum_sq.T

        z = z_ref[b].astype(F32)
        zg_ref[b] = z * jax.nn.sigmoid(z) * jnp.concatenate([nw_ref[...]] * DN_HEADS, axis=1)

        for h in heads:
            u = b * DN_HEADS + h
            qh, kh, vh = qkv[:, hsl(0, h)], qkv[:, hsl(D_MODEL, h)], qkv[:, hsl(2 * D_MODEL, h)]
            qh = qh * (lax.rsqrt(jnp.sum(qh * qh, -1, keepdims=True) + L2_EPS) * (DN_DK ** -0.5))
            kh = kh * lax.rsqrt(jnp.sum(kh * kh, -1, keepdims=True) + L2_EPS)
            g_col = gcum[:, h:h + 1]
            beta = beta_all[:, DN_HEADS + h:DN_HEADS + h + 1]
            g_last = gcum[c - 1:c, h:h + 1]
            decay = jnp.where(causal, jnp.exp(jnp.where(causal, g_col - gcum_t[h:h + 1, 0:c], 0.0)), 0.0)
            eg = jnp.exp(g_col)
            kq_ref[u, 0:c, :] = kh.astype(BF16)
            kq_ref[u, c:, :] = qh.astype(BF16)
            m_ref[u, 0:c, 0:c] = jnp.where(strict, -beta * decay, 0.0)
            m_ref[u, c:, 0:c] = decay
            r0_ref[u, c:, 0:DN_DK] = (vh * beta).astype(BF16)
            r0_ref[u, c:, DN_DK:] = (kh * (beta * eg)).astype(BF16)
            qd_ref[u] = (qh * eg).astype(BF16)
            kd_ref[u] = (kh * jnp.exp(g_last - g_col)).astype(BF16)
            cd_ref[u] = jnp.broadcast_to(jnp.exp(g_last), (SUBLANES, DN_DK))

    nsh = DN_CONV - 1
    srow = lax.broadcasted_iota(jnp.int32, (nsh * c, CARRY_ROWS + c), 0)
    scol = lax.broadcasted_iota(jnp.int32, (nsh * c, CARRY_ROWS + c), 1)
    shift_mat = jnp.where(scol == srow % c + srow // c + (CARRY_ROWS - nsh), 1.0, 0.0).astype(BF16)
    for b in range(nb):
        xs_ref[b, CARRY_ROWS:, 0:D_MODEL] = q_ref[b]
        xs_ref[b, CARRY_ROWS:, D_MODEL:2 * D_MODEL] = k_ref[b]
        xs_ref[b, CARRY_ROWS:, 2 * D_MODEL:] = v_ref[b]
        window = xs_ref[b]
        shifted = jnp.dot(shift_mat, window, preferred_element_type=F32)
        acc = cw_ref[nsh:DN_CONV, :] * window[CARRY_ROWS:].astype(F32)
        for j in range(nsh):
            acc = acc + cw_ref[j:j + 1, :] * shifted[j * c:(j + 1) * c]
        xs_ref[b, 0:CARRY_ROWS, :] = xs_ref[b, c:c + CARRY_ROWS, :]
        qkv_ref[b] = acc * jax.nn.sigmoid(acc)


def _deltanet(proj, ab, conv_w, alog_row, dtb_row, norm_w, *, bsz, t, nb=4):
    c = DN_CHUNK
    nc = t // c
    nu = nb * DN_HEADS
    proj3 = proj.reshape(bsz, t, PROJ_WIDTH)
    ab3 = ab.reshape(bsz, t, LANES)
    conv_chunk = lambda n: jnp.minimum(n, nc - 1)
    prep_chunk = lambda n: jnp.clip(n - 1, 0, nc - 1)
    out_chunk = lambda n: jnp.maximum(n - 2, 0)
    col = lambda j, chunk: pl.BlockSpec((nb, c, D_MODEL), lambda bp, n: (bp, chunk(n), j))
    yb = pl.pallas_call(
        functools.partial(_deltanet_body, nb=nb),
        grid=(bsz // nb, nc + 2),
        in_specs=[
            col(COL_Q, conv_chunk), col(COL_K, conv_chunk), col(COL_VV, conv_chunk), col(COL_Z, prep_chunk),
            pl.BlockSpec((nb, c, LANES), lambda bp, n: (bp, prep_chunk(n), 0)),
            pl.BlockSpec((DN_CONV, 3 * D_MODEL), lambda bp, n: (0, 0)),
            pl.BlockSpec((1, LANES), lambda bp, n: (0, 0)),
            pl.BlockSpec((1, LANES), lambda bp, n: (0, 0)),
            pl.BlockSpec((1, DN_DK), lambda bp, n: (0, 0)),
        ],
        out_specs=pl.BlockSpec((nb, c, D_MODEL), lambda bp, n: (bp, out_chunk(n), 0)),
        out_shape=jax.ShapeDtypeStruct((bsz, t, D_MODEL), BF16),
        scratch_shapes=[
            pltpu.VMEM((nb, CARRY_ROWS + c, 3 * D_MODEL), BF16),
            pltpu.VMEM((nb, c, 3 * D_MODEL), F32),
            pltpu.VMEM((nu, DN_DK, DN_DK), F32),
            pltpu.VMEM((nu, 2 * c, DN_DK), BF16),
            pltpu.VMEM((nu, 2 * c, 2 * c), F32),
            pltpu.VMEM((nu, 2 * c, 2 * DN_DK), BF16),
            pltpu.VMEM((nu, c, DN_DK), BF16),
            pltpu.VMEM((nu, c, DN_DK), BF16),
            pltpu.VMEM((nu, SUBLANES, DN_DK), F32),
            pltpu.VMEM((nb, c, D_MODEL), F32),
        ],
        compiler_params=_cparams("parallel", "arbitrary"),
        name="deltanet",
    )(proj3, proj3, proj3, proj3, ab3, conv_w, alog_row, dtb_row, norm_w)
    return yb.reshape(bsz * t, D_MODEL)


def _merge_body(x_ref, ya_ref, yb_ref, ga_ref, gb_ref, lng_ref, lnb_ref, wpa_ref, wpb_ref, wo_ref,
                l1g_ref, l1b_ref, wr_ref, wrlo_ref, br_ref, h1_ref, h1_rows_ref, logit_ref):
    h = _layer_norm(x_ref[...], lng_ref[...], lnb_ref[...])
    pa = jnp.dot(ya_ref[...], wpa_ref[...], preferred_element_type=F32)
    pb = jnp.dot(yb_ref[...], wpb_ref[...], preferred_element_type=F32)
    merged = jax.nn.sigmoid(ga_ref[...].astype(F32)) * pa + jax.nn.sigmoid(gb_ref[...].astype(F32)) * pb
    mix = jnp.dot(merged.astype(BF16), wo_ref[...], preferred_element_type=F32)
    h1 = _layer_norm(DEEPNORM_ALPHA * h + mix, l1g_ref[...], l1b_ref[...])
    h1_ref[...] = h1
    _store_row_tiles(h1_rows_ref, h1)
    tm = h1.shape[0]
    hi = h1.astype(BF16)
    lo = (h1 - hi.astype(F32)).astype(BF16)
    hw = jnp.dot(jnp.concatenate([hi, lo], axis=0), wr_ref[...], preferred_element_type=F32)
    logit_ref[...] = (hw[0:tm] + hw[tm:] + jnp.dot(hi, wrlo_ref[...], preferred_element_type=F32)
                      + br_ref[...])


def _merge(x2, ya, yb, proj, ln_g, ln_b, w_pa, w_pb, w_o, l1g, l1b, w_r, w_r_lo, b_r, *, tm=512):
    n = x2.shape[0]
    vec = lambda: pl.BlockSpec((1, D_MODEL), lambda i: (0, 0))
    mat = lambda: pl.BlockSpec((D_MODEL, D_MODEL), lambda i: (0, 0))
    return pl.pallas_call(
        _merge_body,
        grid=(n // tm,),
        in_specs=[
            pl.BlockSpec((tm, D_MODEL), lambda i: (i, 0)),
            pl.BlockSpec((tm, D_MODEL), lambda i: (i, 0)),
            pl.BlockSpec((tm, D_MODEL), lambda i: (i, 0)),
            pl.BlockSpec((tm, D_MODEL), lambda i: (i, COL_GA)),
            pl.BlockSpec((tm, D_MODEL), lambda i: (i, COL_GB)),
            vec(), vec(), mat(), mat(), mat(), vec(), vec(),
            pl.BlockSpec((D_MODEL, LANES), lambda i: (0, 0)),
            pl.BlockSpec((D_MODEL, LANES), lambda i: (0, 0)),
            pl.BlockSpec((1, LANES), lambda i: (0, 0)),
        ],
        out_specs=[
            pl.BlockSpec((tm, D_MODEL), lambda i: (i, 0)),
            pl.BlockSpec((tm * ROW_CHUNKS, LANES), lambda i: (i, 0)),
            pl.BlockSpec((tm, LANES), lambda i: (i, 0)),
        ],
        out_shape=[
            jax.ShapeDtypeStruct((n, D_MODEL), F32),
            jax.ShapeDtypeStruct((n * ROW_CHUNKS, LANES), jnp.uint32),
            jax.ShapeDtypeStruct((n, LANES), F32),
        ],
        compiler_params=_cparams("parallel"),
        name="merge",
    )(x2, ya, yb, proj, proj, ln_g, ln_b, w_pa, w_pb, w_o, l1g, l1b, w_r, w_r_lo, b_r)


def _route_body(logit_ref, gate_ref, idx_ref, cnt_ref, carry_ref, *, tm):
    @pl.when(pl.program_id(0) == 0)
    def _():
        carry_ref[...] = jnp.zeros_like(carry_ref)

    lg = logit_ref[...]
    lane_i = lax.broadcasted_iota(jnp.int32, lg.shape, 1)
    lane = lane_i.astype(F32)
    neg = jnp.float32(-jnp.inf)
    big = jnp.float32(1 << 20)

    is_grp = lane_i < MOE_GROUPS
    gl = jnp.where(is_grp, lg, neg)
    gmax = jnp.max(gl, -1, keepdims=True)
    grp = jnp.min(jnp.where(is_grp & (gl == gmax), lane, big), -1, keepdims=True)
    p_grp = 1.0 / jnp.sum(jnp.where(is_grp, jnp.exp(gl - gmax), 0.0), -1, keepdims=True)

    elane = lane - MOE_GROUPS
    in_grp = (elane >= grp * MOE_EPG) & (elane < (grp + 1) * MOE_EPG)
    el = jnp.where(in_grp, lg, neg)
    m1 = jnp.max(el, -1, keepdims=True)
    e1 = jnp.min(jnp.where(in_grp & (el == m1), elane, big), -1, keepdims=True)
    rest = in_grp & (elane != e1)
    el2 = jnp.where(rest, lg, neg)
    m2 = jnp.max(el2, -1, keepdims=True)
    e2 = jnp.min(jnp.where(rest & (el2 == m2), elane, big), -1, keepdims=True)
    t2 = jnp.exp(m2 - m1)
    g1 = p_grp * (1.0 / (1.0 + t2))
    g2 = p_grp * (t2 / (1.0 + t2))

    oh1 = lane == e1
    oh2 = lane == e2
    onehot = jnp.where(oh1 | oh2, 1.0, 0.0)
    ri = lax.broadcasted_iota(jnp.int32, (tm, tm), 0)
    ci = lax.broadcasted_iota(jnp.int32, (tm, tm), 1)
    tri = jnp.where(ci < ri, 1.0, 0.0).astype(BF16)
    before = jnp.dot(tri, onehot.astype(BF16), preferred_element_type=F32) + carry_ref[0:1, :]
    r1 = jnp.sum(jnp.where(oh1, before, 0.0), -1, keepdims=True)
    r2 = jnp.sum(jnp.where(oh2, before, 0.0), -1, keepdims=True)
    total = carry_ref[0:1, :] + jnp.sum(onehot, 0, keepdims=True)
    carry_ref[...] = jnp.broadcast_to(total, carry_ref.shape)
    cnt_ref[...] = jnp.broadcast_to(total, cnt_ref.shape).astype(jnp.int32)

    gate_ref[...] = jnp.where(lane_i == 0, g1, jnp.where(lane_i == 1, g2, 0.0))
    idx_ref[...] = jnp.where(lane_i == 0, e1, jnp.where(lane_i == 1, e2,
                             jnp.where(lane_i == 2, r1, jnp.where(lane_i == 3, r2, 0.0)))).astype(jnp.int32)


def _route(logits, *, tm=1024):
    n = logits.shape[0]
    return pl.pallas_call(
        functools.partial(_route_body, tm=tm),
        grid=(n // tm,),
        in_specs=[pl.BlockSpec((tm, LANES), lambda i: (i, 0))],
        out_specs=[
            pl.BlockSpec((tm, LANES), lambda i: (i, 0)),
            pl.BlockSpec((tm, LANES), lambda i: (i, 0)),
            pl.BlockSpec((8, LANES), lambda i: (0, 0)),
        ],
        out_shape=[
            jax.ShapeDtypeStruct((n, LANES), F32),
            jax.ShapeDtypeStruct((n, LANES), jnp.int32),
            jax.ShapeDtypeStruct((8, LANES), jnp.int32),
        ],
        scratch_shapes=[pltpu.VMEM((8, LANES), F32)],
        compiler_params=_cparams("arbitrary"),
        name="route",
    )(logits)


def _dest_body(idx_ref, ps_ref, o_ref):
    idx = idx_ref[...]
    lane = lax.broadcasted_iota(jnp.int32, idx.shape, 1)
    ps = ps_ref[...].astype(F32)
    d1 = jnp.sum(jnp.where(lane == idx[:, 0:1], ps, 0.0), -1, keepdims=True).astype(jnp.int32) + idx[:, 2:3]
    d2 = jnp.sum(jnp.where(lane == idx[:, 1:2], ps, 0.0), -1, keepdims=True).astype(jnp.int32) + idx[:, 3:4]
    o_ref[...] = jnp.where(lane == 0, d1, jnp.where(lane == 1, d2, 0))


def _dest(idx, pad_start_row, *, tm=1024):
    n = idx.shape[0]
    return pl.pallas_call(
        _dest_body,
        grid=(n // tm,),
        in_specs=[pl.BlockSpec((tm, LANES), lambda i: (i, 0)), pl.BlockSpec((1, LANES), lambda i: (0, 0))],
        out_specs=pl.BlockSpec((tm, LANES), lambda i: (i, 0)),
        out_shape=jax.ShapeDtypeStruct((n, LANES), jnp.int32),
        compiler_params=_cparams("parallel"),
        name="dest",
    )(idx, pad_start_row)


DMA_GROUP = 8


def _row_tile(ref, row):
    start = row * ROW_CHUNKS
    return ref.at[pl.ds(start if isinstance(row, int) else pl.multiple_of(start, ROW_CHUNKS), ROW_CHUNKS)]


def _dispatch_body(dest_ref, h_ref, xin_ref, xpad_ref, sem, *, tm):
    del xin_ref

    def row_copy(src_row, dst_row):
        return pltpu.make_async_copy(_row_tile(h_ref, src_row), _row_tile(xpad_ref, dst_row), sem)

    for r0 in range(0, tm, DMA_GROUP):
        dst = [dest_ref[0, 0, 2 * r0 + j] for j in range(2 * DMA_GROUP)]
        for j in range(2 * DMA_GROUP):
            row_copy(r0 + j // 2, dst[j]).start(priority=j % 2)

    def wait(g, carry):
        for _ in range(2 * DMA_GROUP):
            row_copy(0, 0).wait()
        return carry

    lax.fori_loop(0, tm // DMA_GROUP, wait, 0)


def _dispatch(dest_smem, h1_rows, xpad0, *, tm):
    n = h1_rows.shape[0] // ROW_CHUNKS
    return pl.pallas_call(
        functools.partial(_dispatch_body, tm=tm),
        grid=(n // tm,),
        in_specs=[
            pl.BlockSpec((1, 1, 2 * tm), lambda i: (i, 0, 0), memory_space=pltpu.SMEM),
            pl.BlockSpec((tm * ROW_CHUNKS, LANES), lambda i: (i, 0)),
            pl.BlockSpec(memory_space=pl.ANY),
        ],
        out_specs=pl.BlockSpec(memory_space=pl.ANY),
        scratch_shapes=[pltpu.SemaphoreType.DMA(())],
        out_shape=jax.ShapeDtypeStruct(xpad0.shape, xpad0.dtype),
        input_output_aliases={2: 0},
        compiler_params=_cparams("arbitrary"),
        name="dispatch",
    )(dest_smem, h1_rows, xpad0)


def _experts_body(blk_e_ref, next_e_ref, slot_ref, nused_ref, x_ref, w1_ref, w3_ref, w2_ref, y_ref,
                  st1, st3, st2, w1b, w3b, w2b, sem):
    i = pl.program_id(0)
    e = blk_e_ref[i]
    first_of_expert = (i == 0) | (e != blk_e_ref[jnp.maximum(i - 1, 0)])

    def weight_copies(expert, slot):
        return [pltpu.make_async_copy(w_ref.at[expert], st.at[slot], sem.at[slot])
                for w_ref, st in ((w1_ref, st1), (w3_ref, st3), (w2_ref, st2))]

    @pl.when(i == 0)
    def _():
        for cp in weight_copies(e, slot_ref[0]):
            cp.start()

    @pl.when(first_of_expert)
    def _():
        slot = slot_ref[i]
        for cp in weight_copies(e, slot):
            cp.wait()
        w1b[...] = st1[slot].astype(BF16)
        w3b[...] = st3[slot].astype(BF16)
        w2b[...] = st2[slot].astype(BF16)

        @pl.when(next_e_ref[i] >= 0)
        def _():
            for cp in weight_copies(next_e_ref[i], 1 - slot):
                cp.start()

    @pl.when(i < nused_ref[0])
    def _():
        xb = _load_row_tiles(x_ref, EXPERT_ROWS)
        a = jnp.dot(xb, w1b[...], preferred_element_type=F32)
        b = jnp.dot(xb, w3b[...], preferred_element_type=F32)
        hb = (a * jax.nn.sigmoid(a)) * b
        _store_row_tiles(y_ref, jnp.dot(hb.astype(BF16), w2b[...], preferred_element_type=F32))

    @pl.when(i >= nused_ref[0])
    def _():
        y_ref[...] = jnp.zeros_like(y_ref)


def _experts(blk_e, next_e, slot, n_used, xpad, w1, w3, w2):
    tm = EXPERT_ROWS * ROW_CHUNKS
    return pl.pallas_call(
        _experts_body,
        grid_spec=pltpu.PrefetchScalarGridSpec(
            num_scalar_prefetch=4,
            grid=(xpad.shape[0] // tm,),
            in_specs=[
                pl.BlockSpec((tm, LANES), lambda i, be, ne, sl, nu: (jnp.minimum(i, jnp.maximum(nu[0] - 1, 0)), 0)),
                pl.BlockSpec(memory_space=pl.ANY),
                pl.BlockSpec(memory_space=pl.ANY),
                pl.BlockSpec(memory_space=pl.ANY),
            ],
            out_specs=pl.BlockSpec((tm, LANES), lambda i, be, ne, sl, nu: (i, 0)),
            scratch_shapes=[
                pltpu.VMEM((2, D_MODEL, D_EXPERT), F32),
                pltpu.VMEM((2, D_MODEL, D_EXPERT), F32),
                pltpu.VMEM((2, D_EXPERT, D_MODEL), F32),
                pltpu.VMEM((D_MODEL, D_EXPERT), BF16),
                pltpu.VMEM((D_MODEL, D_EXPERT), BF16),
                pltpu.VMEM((D_EXPERT, D_MODEL), BF16),
                pltpu.SemaphoreType.DMA((2,)),
            ],
        ),
        out_shape=jax.ShapeDtypeStruct(xpad.shape, xpad.dtype),
        compiler_params=_cparams("arbitrary"),
        name="experts",
    )(blk_e, next_e, slot, n_used, xpad, w1, w3, w2)


def _combine_body(dest_ref, dest_next_ref, h1_ref, gate_ref, l2g_ref, l2b_ref, ypad_ref, o_ref, buf, sem, *, tm):
    i = pl.program_id(0)
    half = i % 2

    def row_copy(src_row, r, slot, hf):
        return pltpu.make_async_copy(_row_tile(ypad_ref, src_row), _row_tile(buf.at[2 * hf + slot], r), sem.at[hf])

    def gather(idx_ref, hf):
        for r0 in range(0, tm, DMA_GROUP):
            src = [idx_ref[0, 0, 2 * r0 + j] for j in range(2 * DMA_GROUP)]
            for j in range(2 * DMA_GROUP):
                row_copy(src[j], r0 + j // 2, j % 2, hf).start(priority=j % 2)

    @pl.when(i == 0)
    def _():
        gather(dest_ref, 0)

    @pl.when(i + 1 < pl.num_programs(0))
    def _():
        gather(dest_next_ref, 1 - half)

    def wait(g, carry):
        for j in range(2 * DMA_GROUP):
            row_copy(0, 0, j % 2, half).wait()
        return carry

    lax.fori_loop(0, tm // DMA_GROUP, wait, 0)
    gate = gate_ref[...]
    ffn = (_load_row_tiles(buf.at[2 * half], tm).astype(F32) * gate[:, 0:1]
           + _load_row_tiles(buf.at[2 * half + 1], tm).astype(F32) * gate[:, 1:2])
    o_ref[...] = _layer_norm(DEEPNORM_ALPHA * h1_ref[...] + ffn, l2g_ref[...], l2b_ref[...])


def _combine(dest_smem, h1, gates, l2g, l2b, ypad, *, tm):
    n = h1.shape[0]
    steps = n // tm
    return pl.pallas_call(
        functools.partial(_combine_body, tm=tm),
        grid=(steps,),
        in_specs=[
            pl.BlockSpec((1, 1, 2 * tm), lambda i: (i, 0, 0), memory_space=pltpu.SMEM),
            pl.BlockSpec((1, 1, 2 * tm), lambda i: (jnp.minimum(i + 1, steps - 1), 0, 0), memory_space=pltpu.SMEM),
            pl.BlockSpec((tm, D_MODEL), lambda i: (i, 0)),
            pl.BlockSpec((tm, LANES), lambda i: (i, 0)),
            pl.BlockSpec((1, D_MODEL), lambda i: (0, 0)),
            pl.BlockSpec((1, D_MODEL), lambda i: (0, 0)),
            pl.BlockSpec(memory_space=pl.ANY),
        ],
        out_specs=pl.BlockSpec((tm, D_MODEL), lambda i: (i, 0)),
        scratch_shapes=[
            pltpu.VMEM((4, tm * ROW_CHUNKS, LANES), jnp.uint32),
            pltpu.SemaphoreType.DMA((2,)),
        ],
        out_shape=jax.ShapeDtypeStruct((n, D_MODEL), F32),
        compiler_params=_cparams("arbitrary"),
        name="combine",
    )(dest_smem, dest_smem, h1, gates, l2g, l2b, ypad)


def _row(v, width=None):
    v = v.reshape(1, -1).astype(F32)
    if width is not None and v.shape[1] < width:
        v = jnp.pad(v, ((0, 0), (0, width - v.shape[1])))
    return v


def kernel(x, ln_in_g, ln_in_b, w_in, b_in, gm_ln_g, gm_ln_b, gm_w_s, gm_b_s, dn_conv_w, dn_a_log,
           dn_dt_bias, dn_norm_w, w_pa, w_pb, w_o, ln1_g, ln1_b, w_rg, b_rg, w_re, b_re, w1, w3, w2,
           ln2_g, ln2_b):
    bsz, t, d = x.shape
    n = bsz * t
    x2 = x.reshape(n, d)
    l = 0

    w_t, bi = jnp.swapaxes(w_in, 1, 2)[l], b_in[l]
    c_ab = N_FRONT_BLOCKS * D_MODEL
    c_gate = c_ab + AB_COLS
    w_ab = jnp.pad(w_t[c_ab:c_gate].T, ((0, 0), (0, LANES - AB_COLS))).astype(BF16)
    b_ab = _row(bi[c_ab:c_gate], LANES)
    b_proj = jnp.concatenate([bi[:c_ab], bi[c_gate:]]).reshape(1, PROJ_WIDTH)

    ln_g, ln_b = _row(ln_in_g), _row(ln_in_b)
    h, ab = _ln_ab(x2, ln_g, ln_b, w_ab, b_ab, tm=min(2048, n))
    proj = _inproj(h, w_t, b_proj, tm=min(2048, n))

    bs_full = jnp.broadcast_to(gm_b_s[l][:, :, None], (GM_GROUPS, GM_BLOCK, LANES))
    ya = _gmlp(proj, _row(gm_ln_g[l]), _row(gm_ln_b[l]), gm_w_s[l], bs_full, nblk=min(8, n // GM_BLOCK))

    yb = _deltanet(proj, ab, dn_conv_w[l], _row(dn_a_log[l], LANES), _row(dn_dt_bias[l], LANES),
                   _row(dn_norm_w[l]), bsz=bsz, t=t, nb=min(4, bsz))

    w_r = jnp.concatenate([w_rg[l], w_re[l], jnp.zeros((d, LANES - MOE_GROUPS - N_EXPERTS), F32)], axis=1)
    w_r_hi = w_r.astype(BF16)
    w_r_lo = (w_r - w_r_hi.astype(F32)).astype(BF16)
    b_r = _row(jnp.concatenate([b_rg[l], b_re[l]]), LANES)
    h1, h1_rows, logits = _merge(x2, ya, yb, proj, ln_g, ln_b, w_pa[l].astype(BF16), w_pb[l].astype(BF16),
                        w_o[l].astype(BF16), _row(ln1_g[l]), _row(ln1_b[l]), w_r_hi, w_r_lo, b_r,
                        tm=min(1024, n))

    gates, idx, counts = _route(logits, tm=min(1024, n))

    tm_e = EXPERT_ROWS
    cnt = counts[0, :N_EXPERTS]
    padded = (cnt + tm_e - 1) // tm_e * tm_e
    pad_end = jnp.cumsum(padded)
    pad_start = (pad_end - padded).astype(jnp.int32)
    n_blocks = (2 * n + N_EXPERTS * (tm_e - 1) + tm_e - 1) // tm_e
    blk_start = jnp.arange(n_blocks, dtype=jnp.int32) * tm_e
    blk_e = jnp.minimum(jnp.sum(pad_end[None, :] <= blk_start[:, None], -1), N_EXPERTS - 1).astype(jnp.int32)
    n_used = (pad_end[-1] // tm_e).astype(jnp.int32).reshape(1)
    blk_ids = jnp.arange(n_blocks, dtype=jnp.int32)
    is_first = jnp.concatenate([jnp.ones((1,), bool), blk_e[1:] != blk_e[:-1]])
    slot = ((jnp.cumsum(is_first) - 1) % 2).astype(jnp.int32)
    later_first = is_first[None, :] & (blk_ids[None, :] > blk_ids[:, None])
    next_first = jnp.min(jnp.where(later_first, blk_ids[None, :], n_blocks), axis=1)
    next_e = jnp.where(next_first < n_blocks, blk_e[jnp.minimum(next_first, n_blocks - 1)], -1).astype(jnp.int32)

    ps_row = jnp.pad(pad_start, (0, LANES - N_EXPERTS)).reshape(1, LANES)
    dest = _dest(idx, ps_row, tm=min(1024, n))[:, :2]
    tm_d, tm_c = min(1024, n), 256
    xpad0 = jnp.zeros((n_blocks * tm_e * ROW_CHUNKS, LANES), jnp.uint32)
    xpad = _dispatch(dest.reshape(n // tm_d, 1, 2 * tm_d), h1_rows, xpad0, tm=tm_d)
    ypad = _experts(blk_e, next_e, slot, n_used, xpad, w1[l], w3[l], w2[l])
    out = _combine(dest.reshape(n // tm_c, 1, 2 * tm_c), h1, gates, _row(ln2_g[l]), _row(ln2_b[l]), ypad,
                   tm=tm_c)
    return out.reshape(bsz, t, d)
```

```python
import functools
import math

import jax
import jax.numpy as jnp
from jax import lax
from jax.experimental import pallas as pl
from jax.experimental.pallas import tpu as pltpu

D_MODEL = 1024
GM_GROUPS = 8
GM_BLOCK = 128
GM_CHUNK = 64
DN_HEADS = 8
DN_DK = 128
DN_CONV = 4
MOE_GROUPS = 4
MOE_EPG = 8
N_EXPERTS = MOE_GROUPS * MOE_EPG
D_EXPERT = D_MODEL // 2
LN_EPS = 1e-5
RMS_EPS = 1e-6
L2_EPS = 1e-6
DEEPNORM_ALPHA = 2.0 ** 0.25

LANES = 128
COL_U, COL_V, COL_Q, COL_K, COL_VV, COL_Z, COL_GA, COL_GB = range(8)
N_PROJ_BLOCKS = 8
N_FRONT_BLOCKS = 6
AB_COLS = 2 * DN_HEADS
PROJ_WIDTH = N_PROJ_BLOCKS * D_MODEL

DN_CHUNK = 64
CARRY_ROWS = 16
EXPERT_ROWS = 512
VMEM_LIMIT = 56 * 1024 * 1024
SUBLANES = 8
ROW_WORDS = D_MODEL // 2
ROW_CHUNKS = ROW_WORDS // LANES

F32 = jnp.float32
BF16 = jnp.bfloat16


def _cparams(*sem):
    return pltpu.CompilerParams(dimension_semantics=sem, vmem_limit_bytes=VMEM_LIMIT)


def _store_row_tiles(ref, val):
    m = val.shape[0]
    hi = pltpu.bitcast(val[:, :ROW_WORDS].astype(BF16).astype(F32), jnp.uint32)
    lo = pltpu.bitcast(val[:, ROW_WORDS:].astype(BF16).astype(F32), jnp.uint32)
    words = hi | (lo >> 16)
    for ch in range(ROW_CHUNKS):
        ref[pl.ds(ch, m, stride=ROW_CHUNKS), :] = words[:, ch * LANES:(ch + 1) * LANES]


def _load_row_tiles(ref, m):
    words = jnp.concatenate([ref[pl.ds(ch, m, stride=ROW_CHUNKS), :] for ch in range(ROW_CHUNKS)], axis=1)
    hi = pltpu.bitcast(words & jnp.uint32(0xFFFF0000), F32)
    lo = pltpu.bitcast(words << 16, F32)
    return jnp.concatenate([hi, lo], axis=1).astype(BF16)


def _layer_norm(x, g, b):
    mu = jnp.mean(x, -1, keepdims=True)
    xc = x - mu
    var = jnp.mean(xc * xc, -1, keepdims=True)
    return xc * lax.rsqrt(var + LN_EPS) * g + b


def _dot(a, b):
    return jnp.dot(a.astype(BF16), b.astype(BF16), preferred_element_type=F32)


def _dot_nt(a, b):
    return lax.dot_general(a.astype(BF16), b.astype(BF16), (((1,), (1,)), ((), ())),
                           preferred_element_type=F32)


def _ln_ab_body(x_ref, g_ref, b_ref, wab_ref, bab_ref, h_ref, ab_ref):
    h = _layer_norm(x_ref[...], g_ref[...], b_ref[...]).astype(BF16)
    h_ref[...] = h
    ab_ref[...] = jnp.dot(h, wab_ref[...], preferred_element_type=F32) + bab_ref[...]


def _ln_ab(x2, ln_g, ln_b, w_ab, b_ab, *, tm):
    n = x2.shape[0]
    return pl.pallas_call(
        _ln_ab_body,
        grid=(n // tm,),
        in_specs=[
            pl.BlockSpec((tm, D_MODEL), lambda i: (i, 0)),
            pl.BlockSpec((1, D_MODEL), lambda i: (0, 0)),
            pl.BlockSpec((1, D_MODEL), lambda i: (0, 0)),
            pl.BlockSpec((D_MODEL, LANES), lambda i: (0, 0)),
            pl.BlockSpec((1, LANES), lambda i: (0, 0)),
        ],
        out_specs=[pl.BlockSpec((tm, D_MODEL), lambda i: (i, 0)), pl.BlockSpec((tm, LANES), lambda i: (i, 0))],
        out_shape=[jax.ShapeDtypeStruct((n, D_MODEL), BF16), jax.ShapeDtypeStruct((n, LANES), F32)],
        compiler_params=_cparams("parallel"),
        name="ln_ab",
    )(x2, ln_g, ln_b, w_ab, b_ab)


def _inproj_body(h_ref, wa_ref, wb_ref, bias_ref, o_ref, w_scr):
    j = pl.program_id(0)

    @pl.when((pl.program_id(1) == 0) & (j < N_FRONT_BLOCKS))
    def _():
        w_scr[...] = wa_ref[...].T.astype(BF16)

    @pl.when((pl.program_id(1) == 0) & (j >= N_FRONT_BLOCKS))
    def _():
        both = jnp.concatenate([wa_ref[...], wb_ref[...]], axis=0)
        w_scr[...] = both[AB_COLS:AB_COLS + D_MODEL].T.astype(BF16)

    o_ref[...] = (jnp.dot(h_ref[...], w_scr[...], preferred_element_type=F32) + bias_ref[...]).astype(o_ref.dtype)


def _inproj(h, w_t, bias, *, tm=1024):
    n = h.shape[0]
    return pl.pallas_call(
        _inproj_body,
        grid=(N_PROJ_BLOCKS, n // tm),
        in_specs=[
            pl.BlockSpec((tm, D_MODEL), lambda j, i: (i, 0)),
            pl.BlockSpec((D_MODEL, D_MODEL), lambda j, i: (j, 0)),
            pl.BlockSpec((D_MODEL, D_MODEL), lambda j, i: (jnp.maximum(j, N_FRONT_BLOCKS) + 1, 0)),
            pl.BlockSpec((1, D_MODEL), lambda j, i: (0, j)),
        ],
        out_specs=pl.BlockSpec((tm, D_MODEL), lambda j, i: (i, j)),
        out_shape=jax.ShapeDtypeStruct((n, N_PROJ_BLOCKS * D_MODEL), BF16),
        scratch_shapes=[pltpu.VMEM((D_MODEL, D_MODEL), BF16)],
        compiler_params=_cparams("arbitrary", "arbitrary"),
        name="inproj",
    )(h, w_t, w_t, bias)


def _gelu(x):
    return 0.5 * x * (1.0 + lax.erf(x * (1.0 / math.sqrt(2.0))))


def _gmlp_body(u_ref, v_ref, lng_ref, lnb_ref, ws_ref, bs_ref, o_ref, *, nblk):
    u = _gelu(u_ref[...].astype(F32))
    v = _layer_norm(_gelu(v_ref[...].astype(F32)), lng_ref[...], lnb_ref[...]).astype(BF16)
    row_chunk = lax.broadcasted_iota(jnp.int32, (GM_BLOCK, GM_BLOCK), 0) // GM_CHUNK
    col_chunk = lax.broadcasted_iota(jnp.int32, (GM_BLOCK, GM_BLOCK), 1) // GM_CHUNK
    causal = col_chunk <= row_chunk
    for g in range(GM_GROUPS):
        cols = slice(g * LANES, (g + 1) * LANES)
        w = jnp.where(causal, ws_ref[g], 0.0).astype(BF16)
        for blk in range(nblk):
            rows = slice(blk * GM_BLOCK, (blk + 1) * GM_BLOCK)
            s = jnp.dot(w, v[rows, cols], preferred_element_type=F32) + bs_ref[g]
            o_ref[rows, cols] = (u[rows, cols] * s).astype(o_ref.dtype)


def _gmlp(proj, ln_g, ln_b, w_s, b_s_full, *, nblk):
    n = proj.shape[0]
    rows = nblk * GM_BLOCK
    return pl.pallas_call(
        functools.partial(_gmlp_body, nblk=nblk),
        grid=(n // rows,),
        in_specs=[
            pl.BlockSpec((rows, D_MODEL), lambda i: (i, COL_U)),
            pl.BlockSpec((rows, D_MODEL), lambda i: (i, COL_V)),
            pl.BlockSpec((1, D_MODEL), lambda i: (0, 0)),
            pl.BlockSpec((1, D_MODEL), lambda i: (0, 0)),
            pl.BlockSpec((GM_GROUPS, GM_BLOCK, GM_BLOCK), lambda i: (0, 0, 0)),
            pl.BlockSpec((GM_GROUPS, GM_BLOCK, LANES), lambda i: (0, 0, 0)),
        ],
        out_specs=pl.BlockSpec((rows, D_MODEL), lambda i: (i, 0)),
        out_shape=jax.ShapeDtypeStruct((n, D_MODEL), BF16),
        compiler_params=_cparams("parallel"),
        name="gmlp",
    )(proj, proj, ln_g, ln_b, w_s, b_s_full)


def _cumsum_rows(x):
    n = x.shape[0]
    row = lax.broadcasted_iota(jnp.int32, x.shape, 0)
    shift = 1
    while shift < n:
        x = x + jnp.where(row >= shift, pltpu.roll(x, shift, 0), 0.0)
        shift *= 2
    return x


def _deltanet_body(q_ref, k_ref, v_ref, z_ref, ab_ref, cw_ref, alog_ref, dtb_ref, nw_ref, o_ref,
                   xs_ref, qkv_ref, state_ref, kq_ref, m_ref, r0_ref, qd_ref, kd_ref, cd_ref, zg_ref, *, nb):
    c = DN_CHUNK
    w3 = 3 * D_MODEL
    heads = range(DN_HEADS)
    units = range(nb * DN_HEADS)
    hsl = lambda base, h: slice(base + h * DN_DK, base + (h + 1) * DN_DK)

    @pl.when(pl.program_id(1) == 0)
    def _():
        xs_ref[:, 0:CARRY_ROWS, :] = jnp.zeros((nb, CARRY_ROWS, w3), BF16)
        for ref in (qkv_ref, state_ref, kq_ref, m_ref, r0_ref, qd_ref, kd_ref, cd_ref, zg_ref):
            ref[...] = jnp.zeros_like(ref)

    kq_in = [kq_ref[u] for u in units]
    kq = [lax.dot_general(kq_in[u], kq_in[u], (((1,), (1,)), ((), ())), preferred_element_type=F32)
          for u in units]
    pm = [kq[u] * m_ref[u] for u in units]
    attn = [x[c:, 0:c].astype(BF16) for x in pm]

    lane = lax.broadcasted_iota(jnp.int32, (c, 2 * c), 1)
    right = lane >= c
    eye_right = jnp.where(lane - c == lax.broadcasted_iota(jnp.int32, (c, 2 * c), 0), 1.0, 0.0)
    pa = [x[0:c] + eye_right for x in pm]
    span = 1
    while span < c:
        pab = [x.astype(BF16) for x in pa]
        pa = [jnp.dot(pab[u][:, 0:c], pab[u], preferred_element_type=F32) + jnp.where(right, pa[u], 0.0)
              for u in units]
        span *= 2
    r = [jnp.dot(pa[u].astype(BF16), r0_ref[u], preferred_element_type=F32) for u in units]

    s_old = [state_ref[u] for u in units]
    ws = [jnp.dot(jnp.concatenate([r[u][:, DN_DK:].astype(BF16), qd_ref[u]], axis=0),
                  s_old[u].astype(BF16), preferred_element_type=F32) for u in units]
    v_new = [(r[u][:, 0:DN_DK] - ws[u][0:c]).astype(BF16) for u in units]
    o = [ws[u][c:] + jnp.dot(attn[u], v_new[u], preferred_element_type=F32) for u in units]
    ds = [lax.dot_general(kd_ref[u], v_new[u], (((0,), (0,)), ((), ())), preferred_element_type=F32)
          for u in units]
    for u in units:
        state_ref[u] = s_old[u] * cd_ref[u][0:1, :] + ds[u]
    for b in range(nb):
        for h in heads:
            u = b * DN_HEADS + h
            on = o[u] * lax.rsqrt(jnp.mean(o[u] * o[u], -1, keepdims=True) + RMS_EPS)
            o_ref[b, :, hsl(0, h)] = (on * zg_ref[b, :, hsl(0, h)]).astype(o_ref.dtype)

    ri = lax.broadcasted_iota(jnp.int32, (c, c), 0)
    ci = lax.broadcasted_iota(jnp.int32, (c, c), 1)
    causal = ri >= ci
    strict = ri > ci
    for b in range(nb):
        qkv = qkv_ref[b]
        ab = ab_ref[b]
        xg = ab + dtb_ref[...]
        softplus = jnp.maximum(xg, 0.0) + jnp.log1p(jnp.exp(-jnp.abs(xg)))
        gcum = _cumsum_rows(-jnp.exp(alog_ref[...]) * softplus)
        beta_all = jax.nn.sigmoid(ab)
        gcum_sq = gcum if c == LANES else jnp.concatenate([gcum, jnp.zeros((LANES - c, LANES), F32)], axis=0)
        gcum_t = gcum_sq.T

        z = z_ref[b].astype(F32)
        zg_ref[b] = z * jax.nn.sigmoid(z) * jnp.concatenate([nw_ref[...]] * DN_HEADS, axis=1)

        for h in heads:
            u = b * DN_HEADS + h
            qh, kh, vh = qkv[:, hsl(0, h)], qkv[:, hsl(D_MODEL, h)], qkv[:, hsl(2 * D_MODEL, h)]
            qh = qh * (lax.rsqrt(jnp.sum(qh * qh, -1, keepdims=True) + L2_EPS) * (DN_DK ** -0.5))
            kh = kh * lax.rsqrt(jnp.sum(kh * kh, -1, keepdims=True) + L2_EPS)
            g_col = gcum[:, h:h + 1]
            beta = beta_all[:, DN_HEADS + h:DN_HEADS + h + 1]
            g_last = gcum[c - 1:c, h:h + 1]
            decay = jnp.where(causal, jnp.exp(jnp.where(causal, g_col - gcum_t[h:h + 1, 0:c], 0.0)), 0.0)
            eg = jnp.exp(g_col)
            kq_ref[u, 0:c, :] = kh.astype(BF16)
            kq_ref[u, c:, :] = qh.astype(BF16)
            m_ref[u, 0:c, 0:c] = jnp.where(strict, -beta * decay, 0.0)
            m_ref[u, c:, 0:c] = decay
            r0_ref[u, c:, 0:DN_DK] = (vh * beta).astype(BF16)
            r0_ref[u, c:, DN_DK:] = (kh * (beta * eg)).astype(BF16)
            qd_ref[u] = (qh * eg).astype(BF16)
            kd_ref[u] = (kh * jnp.exp(g_last - g_col)).astype(BF16)
            cd_ref[u] = jnp.broadcast_to(jnp.exp(g_last), (SUBLANES, DN_DK))

    nsh = DN_CONV - 1
    srow = lax.broadcasted_iota(jnp.int32, (nsh * c, CARRY_ROWS + c), 0)
    scol = lax.broadcasted_iota(jnp.int32, (nsh * c, CARRY_ROWS + c), 1)
    shift_mat = jnp.where(scol == srow % c + srow // c + (CARRY_ROWS - nsh), 1.0, 0.0).astype(BF16)
    for b in range(nb):
        xs_ref[b, CARRY_ROWS:, 0:D_MODEL] = q_ref[b]
        xs_ref[b, CARRY_ROWS:, D_MODEL:2 * D_MODEL] = k_ref[b]
        xs_ref[b, CARRY_ROWS:, 2 * D_MODEL:] = v_ref[b]
        window = xs_ref[b]
        shifted = jnp.dot(shift_mat, window, preferred_element_type=F32)
        acc = cw_ref[nsh:DN_CONV, :] * window[CARRY_ROWS:].astype(F32)
        for j in range(nsh):
            acc = acc + cw_ref[j:j + 1, :] * shifted[j * c:(j + 1) * c]
        xs_ref[b, 0:CARRY_ROWS, :] = xs_ref[b, c:c + CARRY_ROWS, :]
        qkv_ref[b] = acc * jax.nn.sigmoid(acc)


def _deltanet(proj, ab, conv_w, alog_row, dtb_row, norm_w, *, bsz, t, nb=4):
    c = DN_CHUNK
    nc = t // c
    nu = nb * DN_HEADS
    proj3 = proj.reshape(bsz, t, PROJ_WIDTH)
    ab3 = ab.reshape(bsz, t, LANES)
    conv_chunk = lambda n: jnp.minimum(n, nc - 1)
    prep_chunk = lambda n: jnp.clip(n - 1, 0, nc - 1)
    out_chunk = lambda n: jnp.maximum(n - 2, 0)
    col = lambda j, chunk: pl.BlockSpec((nb, c, D_MODEL), lambda bp, n: (bp, chunk(n), j))
    yb = pl.pallas_call(
        functools.partial(_deltanet_body, nb=nb),
        grid=(bsz // nb, nc + 2),
        in_specs=[
            col(COL_Q, conv_chunk), col(COL_K, conv_chunk), col(COL_VV, conv_chunk), col(COL_Z, prep_chunk),
            pl.BlockSpec((nb, c, LANES), lambda bp, n: (bp, prep_chunk(n), 0)),
            pl.BlockSpec((DN_CONV, 3 * D_MODEL), lambda bp, n: (0, 0)),
            pl.BlockSpec((1, LANES), lambda bp, n: (0, 0)),
            pl.BlockSpec((1, LANES), lambda bp, n: (0, 0)),
            pl.BlockSpec((1, DN_DK), lambda bp, n: (0, 0)),
        ],
        out_specs=pl.BlockSpec((nb, c, D_MODEL), lambda bp, n: (bp, out_chunk(n), 0)),
        out_shape=jax.ShapeDtypeStruct((bsz, t, D_MODEL), BF16),
        scratch_shapes=[
            pltpu.VMEM((nb, CARRY_ROWS + c, 3 * D_MODEL), BF16),
            pltpu.VMEM((nb, c, 3 * D_MODEL), F32),
            pltpu.VMEM((nu, DN_DK, DN_DK), F32),
            pltpu.VMEM((nu, 2 * c, DN_DK), BF16),
            pltpu.VMEM((nu, 2 * c, 2 * c), F32),
            pltpu.VMEM((nu, 2 * c, 2 * DN_DK), BF16),
            pltpu.VMEM((nu, c, DN_DK), BF16),
            pltpu.VMEM((nu, c, DN_DK), BF16),
            pltpu.VMEM((nu, SUBLANES, DN_DK), F32),
            pltpu.VMEM((nb, c, D_MODEL), F32),
        ],
        compiler_params=_cparams("parallel", "arbitrary"),
        name="deltanet",
    )(proj3, proj3, proj3, proj3, ab3, conv_w, alog_row, dtb_row, norm_w)
    return yb.reshape(bsz * t, D_MODEL)


def _merge_body(x_ref, ya_ref, yb_ref, ga_ref, gb_ref, lng_ref, lnb_ref, wpa_ref, wpb_ref, wo_ref,
                l1g_ref, l1b_ref, wr_ref, wrlo_ref, br_ref, h1_ref, h1_rows_ref, logit_ref):
    h = _layer_norm(x_ref[...], lng_ref[...], lnb_ref[...])
    pa = jnp.dot(ya_ref[...], wpa_ref[...], preferred_element_type=F32)
    pb = jnp.dot(yb_ref[...], wpb_ref[...], preferred_element_type=F32)
    merged = jax.nn.sigmoid(ga_ref[...].astype(F32)) * pa + jax.nn.sigmoid(gb_ref[...].astype(F32)) * pb
    mix = jnp.dot(merged.astype(BF16), wo_ref[...], preferred_element_type=F32)
    h1 = _layer_norm(DEEPNORM_ALPHA * h + mix, l1g_ref[...], l1b_ref[...])
    h1_ref[...] = h1
    _store_row_tiles(h1_rows_ref, h1)
    tm = h1.shape[0]
    hi = h1.astype(BF16)
    lo = (h1 - hi.astype(F32)).astype(BF16)
    hw = jnp.dot(jnp.concatenate([hi, lo], axis=0), wr_ref[...], preferred_element_type=F32)
    logit_ref[...] = (hw[0:tm] + hw[tm:] + jnp.dot(hi, wrlo_ref[...], preferred_element_type=F32)
                      + br_ref[...])


def _merge(x2, ya, yb, proj, ln_g, ln_b, w_pa, w_pb, w_o, l1g, l1b, w_r, w_r_lo, b_r, *, tm=512):
    n = x2.shape[0]
    vec = lambda: pl.BlockSpec((1, D_MODEL), lambda i: (0, 0))
    mat = lambda: pl.BlockSpec((D_MODEL, D_MODEL), lambda i: (0, 0))
    return pl.pallas_call(
        _merge_body,
        grid=(n // tm,),
        in_specs=[
            pl.BlockSpec((tm, D_MODEL), lambda i: (i, 0)),
            pl.BlockSpec((tm, D_MODEL), lambda i: (i, 0)),
            pl.BlockSpec((tm, D_MODEL), lambda i: (i, 0)),
            pl.BlockSpec((tm, D_MODEL), lambda i: (i, COL_GA)),
            pl.BlockSpec((tm, D_MODEL), lambda i: (i, COL_GB)),
            vec(), vec(), mat(), mat(), mat(), vec(), vec(),
            pl.BlockSpec((D_MODEL, LANES), lambda i: (0, 0)),
            pl.BlockSpec((D_MODEL, LANES), lambda i: (0, 0)),
            pl.BlockSpec((1, LANES), lambda i: (0, 0)),
        ],
        out_specs=[
            pl.BlockSpec((tm, D_MODEL), lambda i: (i, 0)),
            pl.BlockSpec((tm * ROW_CHUNKS, LANES), lambda i: (i, 0)),
            pl.BlockSpec((tm, LANES), lambda i: (i, 0)),
        ],
        out_shape=[
            jax.ShapeDtypeStruct((n, D_MODEL), F32),
            jax.ShapeDtypeStruct((n * ROW_CHUNKS, LANES), jnp.uint32),
            jax.ShapeDtypeStruct((n, LANES), F32),
        ],
        compiler_params=_cparams("parallel"),
        name="merge",
    )(x2, ya, yb, proj, proj, ln_g, ln_b, w_pa, w_pb, w_o, l1g, l1b, w_r, w_r_lo, b_r)


def _route_body(logit_ref, gate_ref, idx_ref, cnt_ref, carry_ref, *, tm):
    @pl.when(pl.program_id(0) == 0)
    def _():
        carry_ref[...] = jnp.zeros_like(carry_ref)

    lg = logit_ref[...]
    lane_i = lax.broadcasted_iota(jnp.int32, lg.shape, 1)
    lane = lane_i.astype(F32)
    neg = jnp.float32(-jnp.inf)
    big = jnp.float32(1 << 20)

    is_grp = lane_i < MOE_GROUPS
    gl = jnp.where(is_grp, lg, neg)
    gmax = jnp.max(gl, -1, keepdims=True)
    grp = jnp.min(jnp.where(is_grp & (gl == gmax), lane, big), -1, keepdims=True)
    p_grp = 1.0 / jnp.sum(jnp.where(is_grp, jnp.exp(gl - gmax), 0.0), -1, keepdims=True)

    elane = lane - MOE_GROUPS
    in_grp = (elane >= grp * MOE_EPG) & (elane < (grp + 1) * MOE_EPG)
    el = jnp.where(in_grp, lg, neg)
    m1 = jnp.max(el, -1, keepdims=True)
    e1 = jnp.min(jnp.where(in_grp & (el == m1), elane, big), -1, keepdims=True)
    rest = in_grp & (elane != e1)
    el2 = jnp.where(rest, lg, neg)
    m2 = jnp.max(el2, -1, keepdims=True)
    e2 = jnp.min(jnp.where(rest & (el2 == m2), elane, big), -1, keepdims=True)
    t2 = jnp.exp(m2 - m1)
    g1 = p_grp * (1.0 / (1.0 + t2))
    g2 = p_grp * (t2 / (1.0 + t2))

    oh1 = lane == e1
    oh2 = lane == e2
    onehot = jnp.where(oh1 | oh2, 1.0, 0.0)
    ri = lax.broadcasted_iota(jnp.int32, (tm, tm), 0)
    ci = lax.broadcasted_iota(jnp.int32, (tm, tm), 1)
    tri = jnp.where(ci < ri, 1.0, 0.0).astype(BF16)
    before = jnp.dot(tri, onehot.astype(BF16), preferred_element_type=F32) + carry_ref[0:1, :]
    r1 = jnp.sum(jnp.where(oh1, before, 0.0), -1, keepdims=True)
    r2 = jnp.sum(jnp.where(oh2, before, 0.0), -1, keepdims=True)
    total = carry_ref[0:1, :] + jnp.sum(onehot, 0, keepdims=True)
    carry_ref[...] = jnp.broadcast_to(total, carry_ref.shape)
    cnt_ref[...] = jnp.broadcast_to(total, cnt_ref.shape).astype(jnp.int32)

    gate_ref[...] = jnp.where(lane_i == 0, g1, jnp.where(lane_i == 1, g2, 0.0))
    idx_ref[...] = jnp.where(lane_i == 0, e1, jnp.where(lane_i == 1, e2,
                             jnp.where(lane_i == 2, r1, jnp.where(lane_i == 3, r2, 0.0)))).astype(jnp.int32)


def _route(logits, *, tm=1024):
    n = logits.shape[0]
    return pl.pallas_call(
        functools.partial(_route_body, tm=tm),
        grid=(n // tm,),
        in_specs=[pl.BlockSpec((tm, LANES), lambda i: (i, 0))],
        out_specs=[
            pl.BlockSpec((tm, LANES), lambda i: (i, 0)),
            pl.BlockSpec((tm, LANES), lambda i: (i, 0)),
            pl.BlockSpec((8, LANES), lambda i: (0, 0)),
        ],
        out_shape=[
            jax.ShapeDtypeStruct((n, LANES), F32),
            jax.ShapeDtypeStruct((n, LANES), jnp.int32),
            jax.ShapeDtypeStruct((8, LANES), jnp.int32),
        ],
        scratch_shapes=[pltpu.VMEM((8, LANES), F32)],
        compiler_params=_cparams("arbitrary"),
        name="route",
    )(logits)


def _dest_body(idx_ref, ps_ref, o_ref):
    idx = idx_ref[...]
    lane = lax.broadcasted_iota(jnp.int32, idx.shape, 1)
    ps = ps_ref[...].astype(F32)
    d1 = jnp.sum(jnp.where(lane == idx[:, 0:1], ps, 0.0), -1, keepdims=True).astype(jnp.int32) + idx[:, 2:3]
    d2 = jnp.sum(jnp.where(lane == idx[:, 1:2], ps, 0.0), -1, keepdims=True).astype(jnp.int32) + idx[:, 3:4]
    o_ref[...] = jnp.where(lane == 0, d1, jnp.where(lane == 1, d2, 0))


def _dest(idx, pad_start_row, *, tm=1024):
    n = idx.shape[0]
    return pl.pallas_call(
        _dest_body,
        grid=(n // tm,),
        in_specs=[pl.BlockSpec((tm, LANES), lambda i: (i, 0)), pl.BlockSpec((1, LANES), lambda i: (0, 0))],
        out_specs=pl.BlockSpec((tm, LANES), lambda i: (i, 0)),
        out_shape=jax.ShapeDtypeStruct((n, LANES), jnp.int32),
        compiler_params=_cparams("parallel"),
        name="dest",
    )(idx, pad_start_row)


DMA_GROUP = 8


def _row_tile(ref, row):
    start = row * ROW_CHUNKS
    return ref.at[pl.ds(start if isinstance(row, int) else pl.multiple_of(start, ROW_CHUNKS), ROW_CHUNKS)]


def _dispatch_body(dest_ref, h_ref, xin_ref, xpad_ref, sem, *, tm):
    del xin_ref

    def row_copy(src_row, dst_row):
        return pltpu.make_async_copy(_row_tile(h_ref, src_row), _row_tile(xpad_ref, dst_row), sem)

    for r0 in range(0, tm, DMA_GROUP):
        dst = [dest_ref[0, 0, 2 * r0 + j] for j in range(2 * DMA_GROUP)]
        for j in range(2 * DMA_GROUP):
            row_copy(r0 + j // 2, dst[j]).start(priority=j % 2)

    def wait(g, carry):
        for _ in range(2 * DMA_GROUP):
            row_copy(0, 0).wait()
        return carry

    lax.fori_loop(0, tm // DMA_GROUP, wait, 0)


def _dispatch(dest_smem, h1_rows, xpad0, *, tm):
    n = h1_rows.shape[0] // ROW_CHUNKS
    return pl.pallas_call(
        functools.partial(_dispatch_body, tm=tm),
        grid=(n // tm,),
        in_specs=[
            pl.BlockSpec((1, 1, 2 * tm), lambda i: (i, 0, 0), memory_space=pltpu.SMEM),
            pl.BlockSpec((tm * ROW_CHUNKS, LANES), lambda i: (i, 0)),
            pl.BlockSpec(memory_space=pl.ANY),
        ],
        out_specs=pl.BlockSpec(memory_space=pl.ANY),
        scratch_shapes=[pltpu.SemaphoreType.DMA(())],
        out_shape=jax.ShapeDtypeStruct(xpad0.shape, xpad0.dtype),
        input_output_aliases={2: 0},
        compiler_params=_cparams("arbitrary"),
        name="dispatch",
    )(dest_smem, h1_rows, xpad0)


def _experts_body(blk_e_ref, next_e_ref, slot_ref, nused_ref, x_ref, w1_ref, w3_ref, w2_ref, y_ref,
                  st1, st3, st2, w1b, w3b, w2b, sem):
    i = pl.program_id(0)
    e = blk_e_ref[i]
    first_of_expert = (i == 0) | (e != blk_e_ref[jnp.maximum(i - 1, 0)])

    def weight_copies(expert, slot):
        return [pltpu.make_async_copy(w_ref.at[expert], st.at[slot], sem.at[slot])
                for w_ref, st in ((w1_ref, st1), (w3_ref, st3), (w2_ref, st2))]

    @pl.when(i == 0)
    def _():
        for cp in weight_copies(e, slot_ref[0]):
            cp.start()

    @pl.when(first_of_expert)
    def _():
        slot = slot_ref[i]
        for cp in weight_copies(e, slot):
            cp.wait()
        w1b[...] = st1[slot].astype(BF16)
        w3b[...] = st3[slot].astype(BF16)
        w2b[...] = st2[slot].astype(BF16)

        @pl.when(next_e_ref[i] >= 0)
        def _():
            for cp in weight_copies(next_e_ref[i], 1 - slot):
                cp.start()

    @pl.when(i < nused_ref[0])
    def _():
        xb = _load_row_tiles(x_ref, EXPERT_ROWS)
        a = jnp.dot(xb, w1b[...], preferred_element_type=F32)
        b = jnp.dot(xb, w3b[...], preferred_element_type=F32)
        hb = (a * jax.nn.sigmoid(a)) * b
        _store_row_tiles(y_ref, jnp.dot(hb.astype(BF16), w2b[...], preferred_element_type=F32))

    @pl.when(i >= nused_ref[0])
    def _():
        y_ref[...] = jnp.zeros_like(y_ref)


def _experts(blk_e, next_e, slot, n_used, xpad, w1, w3, w2):
    tm = EXPERT_ROWS * ROW_CHUNKS
    return pl.pallas_call(
        _experts_body,
        grid_spec=pltpu.PrefetchScalarGridSpec(
            num_scalar_prefetch=4,
            grid=(xpad.shape[0] // tm,),
            in_specs=[
                pl.BlockSpec((tm, LANES), lambda i, be, ne, sl, nu: (jnp.minimum(i, jnp.maximum(nu[0] - 1, 0)), 0)),
                pl.BlockSpec(memory_space=pl.ANY),
                pl.BlockSpec(memory_space=pl.ANY),
                pl.BlockSpec(memory_space=pl.ANY),
            ],
            out_specs=pl.BlockSpec((tm, LANES), lambda i, be, ne, sl, nu: (i, 0)),
            scratch_shapes=[
                pltpu.VMEM((2, D_MODEL, D_EXPERT), F32),
                pltpu.VMEM((2, D_MODEL, D_EXPERT), F32),
                pltpu.VMEM((2, D_EXPERT, D_MODEL), F32),
                pltpu.VMEM((D_MODEL, D_EXPERT), BF16),
                pltpu.VMEM((D_MODEL, D_EXPERT), BF16),
                pltpu.VMEM((D_EXPERT, D_MODEL), BF16),
                pltpu.SemaphoreType.DMA((2,)),
            ],
        ),
        out_shape=jax.ShapeDtypeStruct(xpad.shape, xpad.dtype),
        compiler_params=_cparams("arbitrary"),
        name="experts",
    )(blk_e, next_e, slot, n_used, xpad, w1, w3, w2)


def _combine_body(dest_ref, dest_next_ref, h1_ref, gate_ref, l2g_ref, l2b_ref, ypad_ref, o_ref, buf, sem, *, tm):
    i = pl.program_id(0)
    half = i % 2

    def row_copy(src_row, r, slot, hf):
        return pltpu.make_async_copy(_row_tile(ypad_ref, src_row), _row_tile(buf.at[2 * hf + slot], r), sem.at[hf])

    def gather(idx_ref, hf):
        for r0 in range(0, tm, DMA_GROUP):
            src = [idx_ref[0, 0, 2 * r0 + j] for j in range(2 * DMA_GROUP)]
            for j in range(2 * DMA_GROUP):
                row_copy(src[j], r0 + j // 2, j % 2, hf).start(priority=j % 2)

    @pl.when(i == 0)
    def _():
        gather(dest_ref, 0)

    @pl.when(i + 1 < pl.num_programs(0))
    def _():
        gather(dest_next_ref, 1 - half)

    def wait(g, carry):
        for j in range(2 * DMA_GROUP):
            row_copy(0, 0, j % 2, half).wait()
        return carry

    lax.fori_loop(0, tm // DMA_GROUP, wait, 0)
    gate = gate_ref[...]
    ffn = (_load_row_tiles(buf.at[2 * half], tm).astype(F32) * gate[:, 0:1]
           + _load_row_tiles(buf.at[2 * half + 1], tm).astype(F32) * gate[:, 1:2])
    o_ref[...] = _layer_norm(DEEPNORM_ALPHA * h1_ref[...] + ffn, l2g_ref[...], l2b_ref[...])


def _combine(dest_smem, h1, gates, l2g, l2b, ypad, *, tm):
    n = h1.shape[0]
    steps = n // tm
    return pl.pallas_call(
        functools.partial(_combine_body, tm=tm),
        grid=(steps,),
        in_specs=[
            pl.BlockSpec((1, 1, 2 * tm), lambda i: (i, 0, 0), memory_space=pltpu.SMEM),
            pl.BlockSpec((1, 1, 2 * tm), lambda i: (jnp.minimum(i + 1, steps - 1), 0, 0), memory_space=pltpu.SMEM),
            pl.BlockSpec((tm, D_MODEL), lambda i: (i, 0)),
            pl.BlockSpec((tm, LANES), lambda i: (i, 0)),
            pl.BlockSpec((1, D_MODEL), lambda i: (0, 0)),
            pl.BlockSpec((1, D_MODEL), lambda i: (0, 0)),
            pl.BlockSpec(memory_space=pl.ANY),
        ],
        out_specs=pl.BlockSpec((tm, D_MODEL), lambda i: (i, 0)),
        scratch_shapes=[
            pltpu.VMEM((4, tm * ROW_CHUNKS, LANES), jnp.uint32),
            pltpu.SemaphoreType.DMA((2,)),
        ],
        out_shape=jax.ShapeDtypeStruct((n, D_MODEL), F32),
        compiler_params=_cparams("arbitrary"),
        name="combine",
    )(dest_smem, dest_smem, h1, gates, l2g, l2b, ypad)


def _row(v, width=None):
    v = v.reshape(1, -1).astype(F32)
    if width is not None and v.shape[1] < width:
        v = jnp.pad(v, ((0, 0), (0, width - v.shape[1])))
    return v


def kernel(x, ln_in_g, ln_in_b, w_in, b_in, gm_ln_g, gm_ln_b, gm_w_s, gm_b_s, dn_conv_w, dn_a_log,
           dn_dt_bias, dn_norm_w, w_pa, w_pb, w_o, ln1_g, ln1_b, w_rg, b_rg, w_re, b_re, w1, w3, w2,
           ln2_g, ln2_b):
    bsz, t, d = x.shape
    n = bsz * t
    x2 = x.reshape(n, d)
    l = 0

    w_t, bi = jnp.swapaxes(w_in, 1, 2)[l], b_in[l]
    c_ab = N_FRONT_BLOCKS * D_MODEL
    c_gate = c_ab + AB_COLS
    w_ab = jnp.pad(w_t[c_ab:c_gate].T, ((0, 0), (0, LANES - AB_COLS))).astype(BF16)
    b_ab = _row(bi[c_ab:c_gate], LANES)
    b_proj = jnp.concatenate([bi[:c_ab], bi[c_gate:]]).reshape(1, PROJ_WIDTH)

    ln_g, ln_b = _row(ln_in_g), _row(ln_in_b)
    h, ab = _ln_ab(x2, ln_g, ln_b, w_ab, b_ab, tm=min(2048, n))
    proj = _inproj(h, w_t, b_proj, tm=min(2048, n))

    bs_full = jnp.broadcast_to(gm_b_s[l][:, :, None], (GM_GROUPS, GM_BLOCK, LANES))
    ya = _gmlp(proj, _row(gm_ln_g[l]), _row(gm_ln_b[l]), gm_w_s[l], bs_full, nblk=min(8, n // GM_BLOCK))

    yb = _deltanet(proj, ab, dn_conv_w[l], _row(dn_a_log[l], LANES), _row(dn_dt_bias[l], LANES),
                   _row(dn_norm_w[l]), bsz=bsz, t=t, nb=min(8, bsz))

    w_r = jnp.concatenate([w_rg[l], w_re[l], jnp.zeros((d, LANES - MOE_GROUPS - N_EXPERTS), F32)], axis=1)
    w_r_hi = w_r.astype(BF16)
    w_r_lo = (w_r - w_r_hi.astype(F32)).astype(BF16)
    b_r = _row(jnp.concatenate([b_rg[l], b_re[l]]), LANES)
    h1, h1_rows, logits = _merge(x2, ya, yb, proj, ln_g, ln_b, w_pa[l].astype(BF16), w_pb[l].astype(BF16),
                        w_o[l].astype(BF16), _row(ln1_g[l]), _row(ln1_b[l]), w_r_hi, w_r_lo, b_r,
                        tm=min(1024, n))

    gates, idx, counts = _route(logits, tm=min(1024, n))

    tm_e = EXPERT_ROWS
    cnt = counts[0, :N_EXPERTS]
    padded = (cnt + tm_e - 1) // tm_e * tm_e
    pad_end = jnp.cumsum(padded)
    pad_start = (pad_end - padded).astype(jnp.int32)
    n_blocks = (2 * n + N_EXPERTS * (tm_e - 1) + tm_e - 1) // tm_e
    blk_start = jnp.arange(n_blocks, dtype=jnp.int32) * tm_e
    blk_e = jnp.minimum(jnp.sum(pad_end[None, :] <= blk_start[:, None], -1), N_EXPERTS - 1).astype(jnp.int32)
    n_used = (pad_end[-1] // tm_e).astype(jnp.int32).reshape(1)
    blk_ids = jnp.arange(n_blocks, dtype=jnp.int32)
    is_first = jnp.concatenate([jnp.ones((1,), bool), blk_e[1:] != blk_e[:-1]])
    slot = ((jnp.cumsum(is_first) - 1) % 2).astype(jnp.int32)
    later_first = is_first[None, :] & (blk_ids[None, :] > blk_ids[:, None])
    next_first = jnp.min(jnp.where(later_first, blk_ids[None, :], n_blocks), axis=1)
    next_e = jnp.where(next_first < n_blocks, blk_e[jnp.minimum(next_first, n_blocks - 1)], -1).astype(jnp.int32)

    ps_row = jnp.pad(pad_start, (0, LANES - N_EXPERTS)).reshape(1, LANES)
    dest = _dest(idx, ps_row, tm=min(1024, n))[:, :2]
    tm_d, tm_c = min(1024, n), 256
    xpad0 = jnp.zeros((n_blocks * tm_e * ROW_CHUNKS, LANES), jnp.uint32)
    xpad = _dispatch(dest.reshape(n // tm_d, 1, 2 * tm_d), h1_rows, xpad0, tm=tm_d)
    ypad = _experts(blk_e, next_e, slot, n_used, xpad, w1[l], w3[l], w2[l])
    out = _combine(dest.reshape(n // tm_c, 1, 2 * tm_c), h1, gates, _row(ln2_g[l]), _row(ln2_b[l]), ypad,
                   tm=tm_c)
    return out.reshape(bsz, t, d)
```
